```python
import math
import jax, jax.numpy as jnp
from jax import lax
import numpy as np

D_MODEL = 2048
BATCH = 2
SEQ = 4096
DEPTH = 2
DEC_BATCH = 16
DEC_SEQ = 64
PAST_LEN = 1024

CHUNK = 64
Q_BLOCK = 128
RMS_EPS = 1e-6
N_MIXERS = 2
N_A_LAYERS = (DEPTH + 1) // 2
N_B_LAYERS = DEPTH // 2
A_HEADS = 16
A_HEAD_DIM = 128
A_QD = A_HEADS * A_HEAD_DIM
IDX_HEADS = 16
IDX_DIM = 64
TOPK_MAX = 256
IDX_W_SCALE = float((IDX_HEADS * IDX_DIM) ** -0.5)
A_SPLITS = (A_QD, 2 * A_QD, 3 * A_QD, 3 * A_QD + IDX_HEADS * IDX_DIM, 3 * A_QD + IDX_HEADS * IDX_DIM + IDX_DIM)
A_PROJ = 3 * A_QD + IDX_HEADS * IDX_DIM + IDX_DIM + IDX_HEADS
NUM_BUCKETS = 32
MAX_DISTANCE = 128
B_HEADS = 16
Q_LORA = 512
KV_LORA = 512
QK_NOPE = 128
QK_ROPE = 64
V_DIM = 128
ROPE_THETA = 10000.0
B_SCALE = float((QK_NOPE + QK_ROPE) ** -0.5)
B_SPLITS = (Q_LORA, Q_LORA + KV_LORA)
B_PROJ = Q_LORA + KV_LORA + QK_ROPE
N_GROUPS = 8
EXPERTS_PER_GROUP = 8
N_EXPERTS = N_GROUPS * EXPERTS_PER_GROUP
TOP_K_IN_GROUP = 2
D_EXPERT = 512
MOE_BLOCK = 128

kernel_name = 'hybrid_dsa_mla_hmoe_stream_step'


def rmsnorm(x, g):
    xf = x.astype(jnp.float32)
    y = xf * lax.rsqrt(jnp.mean(xf * xf, axis=-1, keepdims=True) + RMS_EPS)
    return (y * g.astype(jnp.float32)).astype(x.dtype)


def rel_bucket(rel):
    nb = NUM_BUCKETS // 2
    max_exact = nb // 2
    n = jnp.abs(rel)
    log_ratio = jnp.log(jnp.maximum(n, 1).astype(jnp.float32) / max_exact) / math.log(MAX_DISTANCE / max_exact)
    large = jnp.minimum(max_exact + (log_ratio * (nb - max_exact)).astype(jnp.int32), nb - 1)
    return jnp.where(rel > 0, nb, 0) + jnp.where(n < max_exact, n, large)


def rope(x, pos):
    half = x.shape[-1] // 2
    inv = ROPE_THETA ** (-jnp.arange(half, dtype=jnp.float32) / half)
    ang = pos.astype(jnp.float32)[:, None] * inv[None, :]
    cos = jnp.cos(ang)[None, :, None, :]
    sin = jnp.sin(ang)[None, :, None, :]
    xf = x.astype(jnp.float32)
    x1, x2 = xf[..., :half], xf[..., half:]
    return jnp.concatenate([x1 * cos - x2 * sin, x1 * sin + x2 * cos], axis=-1).astype(x.dtype)


def dsa_attend(q, qi, wi, k, v, ki, rel_bias, pos0):
    n_seq, s_q, n_h, d_h = q.shape
    n_keys = k.shape[1]
    topk = min(TOPK_MAX, n_keys // 4)
    qb = min(Q_BLOCK, s_q)
    n_blk = s_q // qb
    key_chunk = jnp.arange(n_keys, dtype=jnp.int32) // CHUNK
    t_pos = (pos0 + jnp.arange(s_q, dtype=jnp.int32)).reshape(n_blk, qb)
    scale = A_HEAD_DIM ** -0.5

    def per_seq(seq_args):
        q_s, qi_s, wi_s, k_s, v_s, ki_s = seq_args

        def per_block(blk_args):
            q_b, qi_b, wi_b, t_b = blk_args
            s_idx = jnp.einsum('qhd,sd->qhs', qi_b, ki_s, preferred_element_type=jnp.float32)
            s_idx = jnp.einsum('qhs,qh->qs', jax.nn.relu(s_idx), wi_b.astype(jnp.float32))
            q_chunk = (t_b // CHUNK)[:, None]
            s_idx = jnp.where(key_chunk[None, :] <= q_chunk, s_idx, -jnp.inf)
            _, sel = lax.top_k(s_idx, topk)
            k_sel = k_s[sel]
            v_sel = v_s[sel]
            logits = jnp.einsum('qhd,qkhd->qhk', q_b, k_sel, preferred_element_type=jnp.float32) * scale
            bias = rel_bias[rel_bucket(sel - t_b[:, None])].astype(jnp.float32)
            logits = logits + jnp.swapaxes(bias, 1, 2)
            valid = (sel // CHUNK) <= q_chunk
            logits = jnp.where(valid[:, None, :], logits, -jnp.inf)
            p = jax.nn.softmax(logits, axis=-1).astype(v_s.dtype)
            return jnp.einsum('qhk,qkhd->qhd', p, v_sel)

        out = lax.map(per_block, (q_s.reshape(n_blk, qb, n_h, d_h),
                                  qi_s.reshape(n_blk, qb, IDX_HEADS, IDX_DIM),
                                  wi_s.reshape(n_blk, qb, IDX_HEADS), t_pos))
        return out.reshape(s_q, n_h, d_h)

    return lax.map(per_seq, (q, qi, wi, k, v, ki))


def mixer_a(h, w_in, w_out, rel_bias, past_k, past_v, past_ki, pos0):
    n_seq, s, _ = h.shape
    q, k, v, qi, ki, wi = jnp.split(h @ w_in, A_SPLITS, axis=-1)
    q = q.reshape(n_seq, s, A_HEADS, A_HEAD_DIM)
    k = k.reshape(n_seq, s, A_HEADS, A_HEAD_DIM)
    v = v.reshape(n_seq, s, A_HEADS, A_HEAD_DIM)
    qi = qi.reshape(n_seq, s, IDX_HEADS, IDX_DIM)
    wi = wi * IDX_W_SCALE
    if past_k is None:
        k_all, v_all, ki_all = k, v, ki
    else:
        k_all = jnp.concatenate([past_k, k], axis=1)
        v_all = jnp.concatenate([past_v, v], axis=1)
        ki_all = jnp.concatenate([past_ki, ki], axis=1)
    out = dsa_attend(q, qi, wi, k_all, v_all, ki_all, rel_bias, pos0)
    return out.reshape(n_seq, s, A_QD) @ w_out, k, v, ki


def mixer_b(h, w_in, g_q, g_kv, w_uq, w_uk, w_uv, w_out, past_ckv, past_kr, pos0):
    n_seq, s, _ = h.shape
    c_q, c_kv, k_r = jnp.split(h @ w_in, B_SPLITS, axis=-1)
    c_q = rmsnorm(c_q, g_q)
    c_kv = rmsnorm(c_kv, g_kv)
    pos = pos0 + jnp.arange(s, dtype=jnp.int32)
    q = (c_q @ w_uq).reshape(n_seq, s, B_HEADS, QK_NOPE + QK_ROPE)
    q_nope = q[..., :QK_NOPE]
    q_rope = rope(q[..., QK_NOPE:], pos)
    k_r = rope(k_r[:, :, None, :], pos)[:, :, 0, :]
    if past_ckv is None:
        ckv_all, kr_all = c_kv, k_r
    else:
        ckv_all = jnp.concatenate([past_ckv, c_kv], axis=1)
        kr_all = jnp.concatenate([past_kr, k_r], axis=1)
    n_keys = ckv_all.shape[1]
    k_nope = (ckv_all @ w_uk).reshape(n_seq, n_keys, B_HEADS, QK_NOPE)
    v_all = (ckv_all @ w_uv).reshape(n_seq, n_keys, B_HEADS, V_DIM)
    qb = min(Q_BLOCK, s)
    n_blk = s // qb
    key_chunk = jnp.arange(n_keys, dtype=jnp.int32) // CHUNK

    def per_block(blk_args):
        qn_b, qr_b, t_b = blk_args
        logits = (jnp.einsum('nqhd,nkhd->nhqk', qn_b, k_nope, preferred_element_type=jnp.float32)
                  + jnp.einsum('nqhr,nkr->nhqk', qr_b, kr_all, preferred_element_type=jnp.float32)) * B_SCALE
        mask = key_chunk[None, :] <= (t_b // CHUNK)[:, None]
        logits = jnp.where(mask[None, None], logits, -jnp.inf)
        p = jax.nn.softmax(logits, axis=-1).astype(v_all.dtype)
        return jnp.einsum('nhqk,nkhd->nqhd', p, v_all)

    blocks = lambda a: jnp.swapaxes(a.reshape(n_seq, n_blk, qb, *a.shape[2:]), 0, 1)
    out = lax.map(per_block, (blocks(q_nope), blocks(q_rope), pos.reshape(n_blk, qb)))
    out = jnp.swapaxes(out, 0, 1).reshape(n_seq, s, B_HEADS * V_DIM)
    return out @ w_out, c_kv, k_r


def expert_dispatch(xf, expert_idx, gates, w_gate, w_up, w_down):
    n_tok, d = xf.shape
    n_assign = n_tok * TOP_K_IN_GROUP
    flat_e = expert_idx.reshape(n_assign)
    flat_t = jnp.repeat(jnp.arange(n_tok, dtype=jnp.int32), TOP_K_IN_GROUP)
    flat_g = gates.reshape(n_assign)
    order = jnp.argsort(flat_e)
    e_sorted = flat_e[order]
    counts = jnp.bincount(flat_e, length=N_EXPERTS)
    padded = (counts + MOE_BLOCK - 1) // MOE_BLOCK * MOE_BLOCK
    pad_end = jnp.cumsum(padded)
    pad_start = pad_end - padded
    cnt_start = jnp.cumsum(counts) - counts
    slot = pad_start[e_sorted] + jnp.arange(n_assign, dtype=jnp.int32) - cnt_start[e_sorted]
    n_blocks = -(-n_assign // MOE_BLOCK) + N_EXPERTS
    n_slots = n_blocks * MOE_BLOCK
    slot_tok = jnp.zeros((n_slots,), jnp.int32).at[slot].set(flat_t[order])
    slot_gate = jnp.zeros((n_slots,), jnp.float32).at[slot].set(flat_g[order])
    block_expert = jnp.minimum(jnp.searchsorted(pad_end, jnp.arange(n_blocks) * MOE_BLOCK, side='right'), N_EXPERTS - 1)
    xb = xf[slot_tok].reshape(n_blocks, MOE_BLOCK, d)

    def run(args):
        x_blk, e = args
        return (jax.nn.silu(x_blk @ w_gate[e]) * (x_blk @ w_up[e])) @ w_down[e]

    yb = lax.map(run, (xb, block_expert)).reshape(n_slots, d)
    yb = yb * slot_gate[:, None].astype(yb.dtype)
    return jnp.zeros_like(xf).at[slot_tok].add(yb)


def hier_moe(h, w_grp, b_grp, w_rtr, b_rtr, w_gate, w_up, w_down):
    n_seq, s, d = h.shape
    n_tok = n_seq * s
    xf = h.reshape(n_tok, d)
    g_logits = (xf @ w_grp).astype(jnp.float32) + b_grp.astype(jnp.float32)
    g_prob = jax.nn.softmax(g_logits, axis=-1)
    g_sel = jnp.argmax(g_logits, axis=-1).astype(jnp.int32)
    g_w = jnp.take_along_axis(g_prob, g_sel[:, None], axis=-1)
    e_logits = ((xf @ w_rtr).astype(jnp.float32) + b_rtr.astype(jnp.float32)).reshape(n_tok, N_GROUPS, EXPERTS_PER_GROUP)
    e_logits = jnp.take_along_axis(e_logits, g_sel[:, None, None], axis=1)[:, 0]
    top_p, top_i = lax.top_k(jax.nn.softmax(e_logits, axis=-1), TOP_K_IN_GROUP)
    gates = g_w * (top_p / jnp.sum(top_p, axis=-1, keepdims=True))
    expert_idx = g_sel[:, None] * EXPERTS_PER_GROUP + top_i.astype(jnp.int32)
    return expert_dispatch(xf, expert_idx, gates, w_gate, w_up, w_down).reshape(n_seq, s, d)


def setup_inputs(seed: int = 0) -> dict:
    key = jax.random.key(seed)
    ks = iter(jax.random.split(key, 40))
    f32 = jnp.float32
    nrm = lambda shape, sc=1.0: jax.random.normal(next(ks), shape, f32) * sc
    w = lambda shape, fan_in: nrm(shape, fan_in ** -0.5)
    inp = {}
    inp['x_prompt'] = nrm((BATCH, SEQ, D_MODEL))
    inp['x_sample'] = nrm((DEC_BATCH, DEC_SEQ, D_MODEL))
    inp['cache_a_k'] = nrm((N_A_LAYERS, DEC_BATCH, PAST_LEN, A_HEADS, A_HEAD_DIM))
    inp['cache_a_v'] = nrm((N_A_LAYERS, DEC_BATCH, PAST_LEN, A_HEADS, A_HEAD_DIM))
    inp['cache_a_kidx'] = nrm((N_A_LAYERS, DEC_BATCH, PAST_LEN, IDX_DIM))
    inp['cache_b_ckv'] = nrm((N_B_LAYERS, DEC_BATCH, PAST_LEN, KV_LORA))
    inp['cache_b_krope'] = nrm((N_B_LAYERS, DEC_BATCH, PAST_LEN, QK_ROPE))
    inp['norm_mix'] = 1.0 + nrm((DEPTH, D_MODEL), 0.1)
    inp['norm_ffn'] = 1.0 + nrm((DEPTH, D_MODEL), 0.1)
    inp['norm_final'] = 1.0 + nrm((D_MODEL,), 0.1)
    inp['rel_bias'] = nrm((NUM_BUCKETS, A_HEADS), 0.5)
    inp['a_w_in'] = w((N_A_LAYERS, D_MODEL, A_PROJ), D_MODEL)
    inp['a_w_out'] = w((N_A_LAYERS, A_QD, D_MODEL), A_QD)
    inp['b_w_in'] = w((N_B_LAYERS, D_MODEL, B_PROJ), D_MODEL)
    inp['b_norm_q'] = 1.0 + nrm((N_B_LAYERS, Q_LORA), 0.1)
    inp['b_norm_kv'] = 1.0 + nrm((N_B_LAYERS, KV_LORA), 0.1)
    inp['b_w_uq'] = w((N_B_LAYERS, Q_LORA, B_HEADS * (QK_NOPE + QK_ROPE)), Q_LORA)
    inp['b_w_uk'] = w((N_B_LAYERS, KV_LORA, B_HEADS * QK_NOPE), KV_LORA)
    inp['b_w_uv'] = w((N_B_LAYERS, KV_LORA, B_HEADS * V_DIM), KV_LORA)
    inp['b_w_out'] = w((N_B_LAYERS, B_HEADS * V_DIM, D_MODEL), B_HEADS * V_DIM)
    inp['moe_w_grp'] = w((DEPTH, D_MODEL, N_GROUPS), D_MODEL)
    inp['moe_b_grp'] = nrm((DEPTH, N_GROUPS), 0.01)
    inp['moe_w_rtr'] = w((DEPTH, D_MODEL, N_EXPERTS), D_MODEL)
    inp['moe_b_rtr'] = nrm((DEPTH, N_EXPERTS), 0.01)
    inp['moe_w_gate'] = w((DEPTH, N_EXPERTS, D_MODEL, D_EXPERT), D_MODEL)
    inp['moe_w_up'] = w((DEPTH, N_EXPERTS, D_MODEL, D_EXPERT), D_MODEL)
    inp['moe_w_down'] = w((DEPTH, N_EXPERTS, D_EXPERT, D_MODEL), D_EXPERT)
    return inp


def reference(x_prompt, x_sample, cache_a_k, cache_a_v, cache_a_kidx, cache_b_ckv, cache_b_krope,
              norm_mix, norm_ffn, norm_final, rel_bias, a_w_in, a_w_out,
              b_w_in, b_norm_q, b_norm_kv, b_w_uq, b_w_uk, b_w_uv, b_w_out,
              moe_w_grp, moe_b_grp, moe_w_rtr, moe_b_rtr, moe_w_gate, moe_w_up, moe_w_down):
    hp, hs = x_prompt, x_sample
    a_rows_p, a_rows_s, b_rows_p, b_rows_s = [], [], [], []
    for i in range(DEPTH):
        u_p = rmsnorm(hp, norm_mix[i])
        u_s = rmsnorm(hs, norm_mix[i])
        j = i // N_MIXERS
        if i % N_MIXERS == 0:
            m_p, *r_p = mixer_a(u_p, a_w_in[j], a_w_out[j], rel_bias, None, None, None, 0)
            m_s, *r_s = mixer_a(u_s, a_w_in[j], a_w_out[j], rel_bias,
                                cache_a_k[j], cache_a_v[j], cache_a_kidx[j], PAST_LEN)
            a_rows_p.append(r_p)
            a_rows_s.append(r_s)
        else:
            b_args = (b_w_in[j], b_norm_q[j], b_norm_kv[j], b_w_uq[j], b_w_uk[j], b_w_uv[j], b_w_out[j])
            m_p, *r_p = mixer_b(u_p, *b_args, None, None, 0)
            m_s, *r_s = mixer_b(u_s, *b_args, cache_b_ckv[j], cache_b_krope[j], PAST_LEN)
            b_rows_p.append(r_p)
            b_rows_s.append(r_s)
        hp = hp + m_p
        hs = hs + m_s
        moe_args = (moe_w_grp[i], moe_b_grp[i], moe_w_rtr[i], moe_b_rtr[i], moe_w_gate[i], moe_w_up[i], moe_w_down[i])
        hp = hp + hier_moe(rmsnorm(hp, norm_ffn[i]), *moe_args)
        hs = hs + hier_moe(rmsnorm(hs, norm_ffn[i]), *moe_args)
    y_prompt = rmsnorm(hp, norm_final)
    y_sample = rmsnorm(hs, norm_final)
    stack = lambda rows, n: jnp.stack([r[n] for r in rows], axis=0)
    new_a_k_prompt = stack(a_rows_p, 0)
    new_a_v_prompt = stack(a_rows_p, 1)
    new_a_kidx_prompt = stack(a_rows_p, 2)
    new_b_ckv_prompt = stack(b_rows_p, 0)
    new_b_krope_prompt = stack(b_rows_p, 1)
    new_a_k_sample = stack(a_rows_s, 0)
    new_a_v_sample = stack(a_rows_s, 1)
    new_a_kidx_sample = stack(a_rows_s, 2)
    new_b_ckv_sample = stack(b_rows_s, 0)
    new_b_krope_sample = stack(b_rows_s, 1)
    return (y_prompt, y_sample, new_a_k_prompt, new_a_v_prompt, new_a_kidx_prompt, new_b_ckv_prompt, new_b_krope_prompt,
            new_a_k_sample, new_a_v_sample, new_a_kidx_sample, new_b_ckv_sample, new_b_krope_sample)
```

```python
import functools
import math

import jax
import jax.numpy as jnp
from jax import lax
from jax.experimental import pallas as pl
from jax.experimental.pallas import tpu as pltpu

F32 = jnp.float32
BF16 = jnp.bfloat16
I32 = jnp.int32

RMS_EPS = 1e-6
CHUNK = 64
NEG = -1e30
INT_MIN = -2 ** 31
INT_MAX = 2 ** 31 - 1

LANES = 128
KEY_BLOCK = 128
ATT_TILE = 256
VMEM_LIMIT = 56 * 1024 * 1024

A_HEADS = 16
A_HEAD_DIM = 128
IDX_HEADS = 16
IDX_DIM = 64
TOPK_MAX = 256
IDX_W_SCALE = float((IDX_HEADS * IDX_DIM) ** -0.5)
NUM_BUCKETS = 32
MAX_DISTANCE = 128
B_HEADS = 16
Q_LORA = 512
KV_LORA = 512
QK_NOPE = 128
QK_ROPE = 64
V_DIM = 128
ROPE_THETA = 10000.0
B_SCALE = float((QK_NOPE + QK_ROPE) ** -0.5)
MLA_QK_PAD = 256
N_GROUPS = 8
EXPERTS_PER_GROUP = 8
N_EXPERTS = 64
D_EXPERT = 512
MOE_BM = 128


def _tile(n, pref):
    for c in range(pref, 0, -LANES):
        if n % c == 0:
            return c
    raise ValueError(f"no 128-multiple tile divides {n}")


def _cparams(sem, vmem=None):
    return pltpu.CompilerParams(dimension_semantics=sem, vmem_limit_bytes=vmem)


def _dot(a, b):
    return jnp.dot(a, b, preferred_element_type=F32)


def _dot_nt(a, b):
    return lax.dot_general(a, b, (((1,), (1,)), ((), ())), preferred_element_type=F32)


def _rms(x, g):
    ms = jnp.mean(x * x, axis=-1, keepdims=True)
    return x * lax.rsqrt(ms + RMS_EPS) * g


def _rmsnorm_body(x_ref, g_ref, o_ref):
    o_ref[...] = _rms(x_ref[...], g_ref[...]).astype(o_ref.dtype)


def rmsnorm(x, g, tm, out_dtype=BF16):
    t, d = x.shape
    return pl.pallas_call(
        _rmsnorm_body,
        grid=(t // tm,),
        in_specs=[pl.BlockSpec((tm, d), lambda i: (i, 0)), pl.BlockSpec((1, d), lambda i: (0, 0))],
        out_specs=pl.BlockSpec((tm, d), lambda i: (i, 0)),
        out_shape=jax.ShapeDtypeStruct((t, d), out_dtype),
        compiler_params=_cparams(("parallel",)),
        name="rmsnorm",
    )(x, g.reshape(1, d))


def _matmul_body(a_ref, w_ref, *rest):
    outs, wb = rest[:-1], rest[-1]

    @pl.when(pl.program_id(1) == 0)
    def _():
        wb[...] = w_ref[...].astype(BF16)

    r = _dot(a_ref[...], wb[...])
    for o in outs:
        o[...] = r.astype(o.dtype)


def matmul(a, w, col0, ncols, out_dtypes, tm, tn, name):
    t, k = a.shape
    assert ncols % tn == 0 and col0 % tn == 0 and t % tm == 0
    cb = col0 // tn
    return pl.pallas_call(
        _matmul_body,
        grid=(ncols // tn, t // tm),
        in_specs=[pl.BlockSpec((tm, k), lambda j, i: (i, 0)),
                  pl.BlockSpec((k, tn), lambda j, i: (0, j + cb))],
        out_specs=[pl.BlockSpec((tm, tn), lambda j, i: (i, j)) for _ in out_dtypes],
        out_shape=[jax.ShapeDtypeStruct((t, ncols), dt) for dt in out_dtypes],
        scratch_shapes=[pltpu.VMEM((k, tn), BF16)],
        compiler_params=_cparams(("arbitrary", "arbitrary"), VMEM_LIMIT),
        name=name,
    )(a, w)


def _float_sort_key(x):
    bits = lax.bitcast_convert_type(x, I32)
    return bits ^ ((bits >> 31) & INT_MAX)


def _select_body(qi_ref, wi_ref, kc_ref, mask_ref, key_scr, jm_scr, *, bq, sk, sk_real, tk, pos0, topk):
    b = pl.program_id(1)
    t0 = pos0 + b * bq
    kmax = jnp.minimum(sk_real, ((t0 + bq - 1) // CHUNK + 1) * CHUNK)
    nkt = (kmax + tk - 1) // tk
    nch = tk // LANES

    w = wi_ref[...] * IDX_W_SCALE
    wb = [jnp.broadcast_to(w[:, h:h + 1], (bq, LANES)) for h in range(IDX_HEADS)]
    row_pos = t0 + lax.broadcasted_iota(I32, (bq, LANES), 0)
    lim = jnp.minimum((row_pos // CHUNK + 1) * CHUNK, sk_real)
    lane = lax.broadcasted_iota(I32, (bq, LANES), 1)

    def score_tile(j, carry):
        off = pl.multiple_of(j * tk, tk)
        kc = kc_ref[pl.ds(off, tk), :].astype(BF16)
        ka, kb = kc[:, :LANES], kc[:, LANES:]
        accs = [jnp.zeros((bq, LANES), F32) for _ in range(nch)]
        for g in range(IDX_HEADS // 2):
            qg = qi_ref[:, g * LANES:(g + 1) * LANES]
            sa = _dot_nt(qg, ka)
            sb = _dot_nt(qg, kb)
            for c in range(nch):
                sl = slice(c * LANES, (c + 1) * LANES)
                accs[c] = accs[c] + wb[2 * g] * jnp.maximum(sa[:, sl], 0.0) \
                    + wb[2 * g + 1] * jnp.maximum(sb[:, sl], 0.0)
        for c in range(nch):
            kpos = off + c * LANES + lane
            sc = jnp.where(kpos < lim, accs[c], -jnp.inf)
            key_scr[:, pl.ds(off + c * LANES, LANES)] = _float_sort_key(sc)
        return carry

    lax.fori_loop(0, nkt, score_tile, 0)

    def count(indicator):
        def tile(j, cnt):
            off = pl.multiple_of(j * tk, tk)
            for c in range(nch):
                kt = key_scr[:, pl.ds(off + c * LANES, LANES)]
                cnt = cnt + indicator(kt, off + c * LANES + lane)
            return cnt
        cnt = lax.fori_loop(0, nkt, tile, jnp.zeros((bq, LANES), F32))
        return jnp.sum(cnt, axis=1, keepdims=True)

    def bit_step(i, pfx_u):
        bit = lax.shift_left(jnp.int32(1), 31 - i)
        cand_u = pfx_u | bit
        cand_s = cand_u ^ INT_MIN
        total = count(lambda kt, kp: jnp.where(kt >= cand_s, 1.0, 0.0))
        return jnp.where(total >= topk, cand_u, pfx_u)

    pfx = lax.fori_loop(0, 32, bit_step, jnp.zeros((bq, LANES), I32))
    thr = pfx ^ INT_MIN

    n_gt = count(lambda kt, kp: jnp.where(kt > thr, 1.0, 0.0))
    n_ge = count(lambda kt, kp: jnp.where(kt >= thr, 1.0, 0.0))
    quota = topk - n_gt
    jm_scr[...] = jnp.full((bq, LANES), INT_MAX, I32)
    any_excess = jnp.max(jnp.where(n_ge > topk, 1.0, 0.0)) > 0.0

    @pl.when(any_excess)
    def _():
        nbits = max(1, int(sk - 1).bit_length())

        def idx_step(i, ans):
            cand = ans | lax.shift_left(jnp.int32(1), nbits - 1 - i)
            below = count(lambda kt, kp: jnp.where(kt == thr, jnp.where(kp < cand, 1.0, 0.0), 0.0))
            return jnp.where(below < quota, cand, ans)

        jm_scr[...] = lax.fori_loop(0, nbits, idx_step, jnp.zeros((bq, LANES), I32))

    jm = jm_scr[...]

    def write_tile(j, carry):
        off = pl.multiple_of(j * tk, tk)
        for c in range(nch):
            kt = key_scr[:, pl.ds(off + c * LANES, LANES)]
            kpos = off + c * LANES + lane
            v = jnp.where(kt > thr, 0.0, jnp.where(kt == thr, jnp.where(kpos <= jm, 0.0, NEG), NEG))
            v = jnp.where(kpos < lim, v, NEG)
            mask_ref[:, pl.ds(off + c * LANES, LANES)] = v.astype(mask_ref.dtype)
        return carry

    lax.fori_loop(0, nkt, write_tile, 0)

    def fill_tile(j, carry):
        off = pl.multiple_of(j * tk, tk)
        mask_ref[:, pl.ds(off, tk)] = jnp.full((bq, tk), NEG, mask_ref.dtype)
        return carry

    lax.fori_loop(nkt, sk // tk, fill_tile, 0)


def dsa_select(qi, wi_arr, wi_blk, kc, *, n_seq, sq, bq, sk, sk_real, tk, pos0, q_row0):
    nqb = sq // bq
    qb0 = q_row0 // bq
    assert q_row0 % bq == 0 and sk % tk == 0 and tk >= TOPK_MAX
    topk = min(TOPK_MAX, sk_real // 4)
    body = functools.partial(_select_body, bq=bq, sk=sk, sk_real=sk_real, tk=tk, pos0=pos0, topk=topk)
    return pl.pallas_call(
        body,
        grid=(n_seq, nqb),
        in_specs=[pl.BlockSpec((bq, IDX_HEADS * IDX_DIM), lambda s, b: (qb0 + s * nqb + b, 0)),
                  pl.BlockSpec((bq, LANES), lambda s, b: (qb0 + s * nqb + b, wi_blk)),
                  pl.BlockSpec((sk, 2 * LANES), lambda s, b: (s, 0))],
        out_specs=pl.BlockSpec((bq, sk), lambda s, b: (s * nqb + b, 0)),
        out_shape=jax.ShapeDtypeStruct((n_seq * sq, sk), BF16),
        scratch_shapes=[pltpu.VMEM((bq, sk), I32), pltpu.VMEM((bq, LANES), I32)],
        compiler_params=_cparams(("parallel", "arbitrary"), VMEM_LIMIT),
        name="dsa_select",
    )(qi, wi_arr, kc)


def _rel_tables_body(bias_ref, tab_ref):
    i = lax.broadcasted_iota(I32, (KEY_BLOCK, KEY_BLOCK), 0)
    j = lax.broadcasted_iota(I32, (KEY_BLOCK, KEY_BLOCK), 1)
    nb = NUM_BUCKETS // 2
    max_exact = nb // 2
    edges = [12, 16, 23, 32, 46, 64, 91]
    for d in range(4):
        rel = j - i + (d - 2) * KEY_BLOCK
        n = jnp.abs(rel)
        large = jnp.full_like(n, max_exact)
        for e in edges:
            large = large + jnp.where(n >= e, 1, 0)
        bucket = jnp.where(rel > 0, nb, 0) + jnp.where(n < max_exact, n, large)
        for h in range(A_HEADS):
            acc = jnp.zeros((KEY_BLOCK, KEY_BLOCK), F32)
            for k in range(NUM_BUCKETS):
                acc = jnp.where(bucket == k, bias_ref[k, h], acc)
            tab_ref[h, d] = acc


def rel_bias_tables(rel_bias):
    return pl.pallas_call(
        _rel_tables_body,
        in_specs=[pl.BlockSpec(memory_space=pltpu.SMEM)],
        out_shape=jax.ShapeDtypeStruct((A_HEADS, 4, KEY_BLOCK, KEY_BLOCK), F32),
        name="rel_bias_tables",
    )(rel_bias)


def _causal_tables_body(tab_ref):
    i = lax.broadcasted_iota(I32, (KEY_BLOCK, KEY_BLOCK), 0)
    j = lax.broadcasted_iota(I32, (KEY_BLOCK, KEY_BLOCK), 1)
    zero = jnp.zeros((KEY_BLOCK, KEY_BLOCK), F32)
    tab_ref[0, 0] = zero
    tab_ref[0, 1] = zero
    tab_ref[0, 2] = jnp.where(j // CHUNK <= i // CHUNK, 0.0, NEG)
    tab_ref[0, 3] = jnp.full((KEY_BLOCK, KEY_BLOCK), NEG, F32)


def causal_tables():
    return pl.pallas_call(
        _causal_tables_body,
        out_shape=jax.ShapeDtypeStruct((1, 4, KEY_BLOCK, KEY_BLOCK), F32),
        name="causal_tables",
    )()


def _attn_body(*refs, n_seg, has_mask, per_head_tab, bq, hg, dq, dv, scale, pos0, seg_tiles, dyn_sk):
    q_ref = refs[0]
    kv = refs[1:1 + 2 * n_seg]
    pos = 1 + 2 * n_seg
    tab_ref = refs[pos]
    pos += 1
    mask_ref = refs[pos] if has_mask else None
    pos += 1 if has_mask else 0
    out_ref = refs[pos]
    m_scr, l_scr, acc_scr = refs[pos + 1:pos + 4]

    b = pl.program_id(2)
    t0 = pos0 + b * bq
    qblk = t0 // KEY_BLOCK

    m_scr[...] = jnp.full(m_scr.shape, NEG, F32)
    l_scr[...] = jnp.zeros(l_scr.shape, F32)
    acc_scr[...] = jnp.zeros(acc_scr.shape, F32)

    def process(k_t, v_t, key0, width):
        nch = max(1, width // KEY_BLOCK)
        cw = min(width, KEY_BLOCK)
        ds = [jnp.clip(key0 // KEY_BLOCK + c - qblk + 2, 0, 3) for c in range(nch)]
        if has_mask:
            if isinstance(key0, int):
                mk = mask_ref[:, key0:key0 + width].astype(F32)
            else:
                mk = mask_ref[:, pl.ds(pl.multiple_of(key0, width), width)].astype(F32)
        for h in range(hg):
            hh = h if per_head_tab else 0
            s = _dot_nt(q_ref[:, h * dq:(h + 1) * dq], k_t[:, h * dq:(h + 1) * dq]) * scale
            bias = [tab_ref[hh, ds[c], :bq, :cw] for c in range(nch)]
            s = s + (jnp.concatenate(bias, axis=1) if nch > 1 else bias[0])
            if has_mask:
                s = s + mk
            m_old = m_scr[h]
            m_new = jnp.maximum(m_old, jnp.max(s, axis=1, keepdims=True))
            alpha = jnp.exp(m_old - m_new)
            if cw == LANES:
                p = [jnp.exp(s[:, c * LANES:(c + 1) * LANES] - m_new) for c in range(nch)]
                lsum = p[0]
                for c in range(1, nch):
                    lsum = lsum + p[c]
                pfull = jnp.concatenate(p, axis=1) if nch > 1 else p[0]
                l_scr[h] = alpha * l_scr[h] + lsum
            else:
                pfull = jnp.exp(s - m_new[:, :cw])
                l_scr[h] = alpha * l_scr[h] + jnp.concatenate(
                    [pfull, jnp.zeros((bq, LANES - cw), F32)], axis=1)
            m_scr[h] = m_new
            acc_scr[h] = alpha * acc_scr[h] + _dot(pfull.astype(BF16), v_t[:, h * dv:(h + 1) * dv])

    if dyn_sk is not None:
        k_ref, v_ref = kv
        kmax = jnp.minimum(dyn_sk, ((t0 + bq - 1) // CHUNK + 1) * CHUNK)
        nkt = (kmax + ATT_TILE - 1) // ATT_TILE

        def tile(j, carry):
            off = pl.multiple_of(j * ATT_TILE, ATT_TILE)
            process(k_ref[pl.ds(off, ATT_TILE), :].astype(BF16), v_ref[pl.ds(off, ATT_TILE), :].astype(BF16),
                    off, ATT_TILE)
            return carry

        lax.fori_loop(0, nkt, tile, 0)
    else:
        for (si, row0, key0, width) in seg_tiles:
            process(kv[2 * si][row0:row0 + width, :].astype(BF16),
                    kv[2 * si + 1][row0:row0 + width, :].astype(BF16), key0, width)

    for h in range(hg):
        l_row = jnp.sum(l_scr[h], axis=1, keepdims=True)
        out_ref[:, h * dv:(h + 1) * dv] = (acc_scr[h] / l_row).astype(out_ref.dtype)


def attention(q, segs, tab, mask, *, n_seq, sq, bq, q_row0, q_col0, n_heads, hg, dq, dv, scale, pos0,
              seg_tiles=None, dyn_sk=None, name="attention"):
    nqb = sq // bq
    ng = n_heads // hg
    qb0 = q_row0 // bq
    assert q_row0 % bq == 0 and q_col0 % (hg * dq) == 0 and pos0 % KEY_BLOCK == 0
    assert bq == KEY_BLOCK or nqb == 1
    qc0 = q_col0 // (hg * dq)
    in_specs = [pl.BlockSpec((bq, hg * dq), lambda s, g, b: (qb0 + s * nqb + b, qc0 + g))]
    args = [q]
    for (k, v, rows, kc0, vc0) in segs:
        assert kc0 % (hg * dq) == 0 and vc0 % (hg * dv) == 0
        in_specs.append(pl.BlockSpec((rows, hg * dq), lambda s, g, b, c=kc0 // (hg * dq): (s, c + g)))
        in_specs.append(pl.BlockSpec((rows, hg * dv), lambda s, g, b, c=vc0 // (hg * dv): (s, c + g)))
        args += [k, v]
    per_head_tab = tab.shape[0] > 1
    if per_head_tab:
        in_specs.append(pl.BlockSpec((hg, 4, KEY_BLOCK, KEY_BLOCK), lambda s, g, b: (g, 0, 0, 0)))
    else:
        in_specs.append(pl.BlockSpec((1, 4, KEY_BLOCK, KEY_BLOCK), lambda s, g, b: (0, 0, 0, 0)))
    args.append(tab)
    if mask is not None:
        in_specs.append(pl.BlockSpec((bq, mask.shape[1]), lambda s, g, b: (s * nqb + b, 0)))
        args.append(mask)
    body = functools.partial(_attn_body, n_seg=len(segs), has_mask=mask is not None, per_head_tab=per_head_tab,
                             bq=bq, hg=hg, dq=dq, dv=dv, scale=scale, pos0=pos0, seg_tiles=seg_tiles,
                             dyn_sk=dyn_sk)
    return pl.pallas_call(
        body,
        grid=(n_seq, ng, nqb),
        in_specs=in_specs,
        out_specs=pl.BlockSpec((bq, hg * dv), lambda s, g, b: (s * nqb + b, g)),
        out_shape=jax.ShapeDtypeStruct((n_seq * sq, n_heads * dv), BF16),
        scratch_shapes=[pltpu.VMEM((hg, bq, LANES), F32), pltpu.VMEM((hg, bq, LANES), F32),
                        pltpu.VMEM((hg, bq, dv), F32)],
        compiler_params=_cparams(("parallel", "parallel", "arbitrary"), VMEM_LIMIT),
        name=name,
    )(*args)


def _outproj_body(a_ref, w_ref, x_ref, g_ref, wr_ref, br_ref, h_ref, xn_ref, lg_ref, hrow, *, tn):
    j = pl.program_id(1)
    r = x_ref[...] + _dot(a_ref[...], w_ref[0].astype(BF16))
    h_ref[...] = r
    hrow[:, pl.ds(pl.multiple_of(j * tn, tn), tn)] = r

    @pl.when(j == pl.num_programs(1) - 1)
    def _():
        xn = _rms(hrow[...], g_ref[...])
        xn_ref[...] = xn
        lg_ref[...] = _dot(xn.astype(BF16), wr_ref[...].astype(BF16)) + br_ref[...]


def outproj_norm_router(a, w_out, layer, x_res, g_ffn, w_r, b_r, tm, tn):
    t, k = a.shape
    d = w_out.shape[-1]
    return pl.pallas_call(
        functools.partial(_outproj_body, tn=tn),
        grid=(t // tm, d // tn),
        in_specs=[pl.BlockSpec((tm, k), lambda i, j: (i, 0)),
                  pl.BlockSpec((1, k, tn), lambda i, j: (layer, 0, j)),
                  pl.BlockSpec((tm, tn), lambda i, j: (i, j)),
                  pl.BlockSpec((1, d), lambda i, j: (0, 0)),
                  pl.BlockSpec((d, LANES), lambda i, j: (0, 0)),
                  pl.BlockSpec((1, LANES), lambda i, j: (0, 0))],
        out_specs=[pl.BlockSpec((tm, tn), lambda i, j: (i, j)),
                   pl.BlockSpec((tm, d), lambda i, j: (i, 0)),
                   pl.BlockSpec((tm, LANES), lambda i, j: (i, 0))],
        out_shape=[jax.ShapeDtypeStruct((t, d), F32), jax.ShapeDtypeStruct((t, d), F32),
                   jax.ShapeDtypeStruct((t, LANES), F32)],
        scratch_shapes=[pltpu.VMEM((tm, d), F32)],
        compiler_params=_cparams(("parallel", "arbitrary"), VMEM_LIMIT),
        name="outproj_norm_router",
    )(a, w_out, x_res, g_ffn.reshape(1, d), w_r, b_r)


def _routing_body(lg_ref, out_ref, cnt_ref, carry, *, tm):
    i = pl.program_id(0)

    @pl.when(i == 0)
    def _():
        carry[...] = jnp.zeros(carry.shape, F32)

    x = lg_ref[...]
    lane = lax.broadcasted_iota(I32, (tm, LANES), 1)
    neg_inf = -jnp.inf

    def rmax(v):
        return jnp.max(v, axis=1, keepdims=True)

    def rmin(v):
        return jnp.min(v, axis=1, keepdims=True)

    def rsum(v):
        return jnp.sum(v, axis=1, keepdims=True)

    gm = lane < N_GROUPS
    gmax = rmax(jnp.where(gm, x, neg_inf))
    gsel = rmin(jnp.where(gm, jnp.where(x == gmax, lane, LANES), LANES))
    gsum = rsum(jnp.where(gm, jnp.exp(x - gmax), 0.0))
    g_w = 1.0 / gsum
    lo = N_GROUPS + gsel * EXPERTS_PER_GROUP
    em = jnp.logical_and(lane >= lo, lane < lo + EXPERTS_PER_GROUP)
    emax = rmax(jnp.where(em, x, neg_inf))
    ee = jnp.where(em, jnp.exp(x - emax), 0.0)
    p = ee / rsum(ee)
    p1 = rmax(jnp.where(em, p, -1.0))
    i1 = rmin(jnp.where(em, jnp.where(p == p1, lane, LANES), LANES))
    em2 = jnp.logical_and(em, lane != i1)
    p2 = rmax(jnp.where(em2, p, -1.0))
    i2 = rmin(jnp.where(em2, jnp.where(p == p2, lane, LANES), LANES))
    den = p1 + p2
    g1 = g_w * (p1 / den)
    g2 = g_w * (p2 / den)

    oh1 = jnp.where(lane == i1, 1.0, 0.0)
    oh2 = jnp.where(lane == i2, 1.0, 0.0)
    oh = oh1 + oh2
    r = lax.broadcasted_iota(I32, (tm, tm), 0)
    c = lax.broadcasted_iota(I32, (tm, tm), 1)
    lower = jnp.where(c < r, 1.0, 0.0).astype(BF16)
    before = _dot(lower, oh.astype(BF16)) + carry[...]
    rank1 = rsum(oh1 * before)
    rank2 = rsum(oh2 * before)
    carry[...] = carry[...] + jnp.sum(oh, axis=0, keepdims=True)

    e1 = (i1 - N_GROUPS).astype(F32)
    e2 = (i2 - N_GROUPS).astype(F32)
    vals = [e1, e2, rank1, rank2, g1, g2]
    out = jnp.zeros((tm, LANES), F32)
    for k, v in enumerate(vals):
        out = jnp.where(lane == k, v, out)
    out_ref[...] = out
    cnt_ref[...] = carry[...]


def moe_routing(logits, tm):
    t = logits.shape[0]
    return pl.pallas_call(
        functools.partial(_routing_body, tm=tm),
        grid=(t // tm,),
        in_specs=[pl.BlockSpec((tm, LANES), lambda i: (i, 0))],
        out_specs=[pl.BlockSpec((tm, LANES), lambda i: (i, 0)), pl.BlockSpec((1, LANES), lambda i: (0, 0))],
        out_shape=[jax.ShapeDtypeStruct((t, LANES), F32), jax.ShapeDtypeStruct((1, LANES), F32)],
        scratch_shapes=[pltpu.VMEM((1, LANES), F32)],
        compiler_params=_cparams(("arbitrary",)),
        name="moe_routing",
    )(logits)


def _experts_body(be_ref, nv_ref, tok_ref, x_hbm, wg_ref, wu_ref, wd_ref, ys_ref,
                  buf, sem, wgb, wub, wdb, *, bm):
    b = pl.program_id(0)
    nb = pl.num_programs(0)

    def row_copy(tok, r, s):
        return pltpu.make_async_copy(x_hbm.at[pl.ds(tok, 1), :], buf.at[s, pl.ds(r, 1), :], sem.at[s])

    def issue(blk, s):
        n = nv_ref[blk]

        @pl.when(n < bm)
        def _():
            buf[s] = jnp.zeros(buf.shape[1:], F32)

        def one(r, carry):
            row_copy(tok_ref[blk * bm + r], r, s).start()
            return carry

        lax.fori_loop(0, n, one, 0)

    @pl.when(b == 0)
    def _():
        issue(0, 0)

    @pl.when(b + 1 < nb)
    def _():
        issue(b + 1, (b + 1) % 2)

    slot = b % 2
    n = nv_ref[b]

    def wait_one(r, carry):
        row_copy(0, r, slot).wait()
        return carry

    lax.fori_loop(0, n, wait_one, 0)

    @pl.when(n > 0)
    def _():
        prev = be_ref[jnp.maximum(b - 1, 0)]

        @pl.when(jnp.logical_or(b == 0, be_ref[b] != prev))
        def _():
            wgb[...] = wg_ref[0, 0].astype(BF16)
            wub[...] = wu_ref[0, 0].astype(BF16)
            wdb[...] = wd_ref[0, 0].astype(BF16)

        x = buf[slot].astype(BF16)
        g = _dot(x, wgb[...])
        u = _dot(x, wub[...])
        a = (g * (1.0 / (1.0 + jnp.exp(-g)))) * u
        ys_ref[...] = _dot(a.astype(BF16), wdb[...])

    @pl.when(n == 0)
    def _():
        ys_ref[...] = jnp.zeros(ys_ref.shape, F32)


def moe_experts(xn, block_expert, n_valid, slot_tok, w_gate, w_up, w_down, layer, bm):
    t, d = xn.shape
    nb = block_expert.shape[0]
    de = w_gate.shape[-1]
    grid_spec = pltpu.PrefetchScalarGridSpec(
        num_scalar_prefetch=3,
        grid=(nb,),
        in_specs=[pl.BlockSpec(memory_space=pl.ANY),
                  pl.BlockSpec((1, 1, d, de), lambda b, be, nv, tk: (layer, be[b], 0, 0)),
                  pl.BlockSpec((1, 1, d, de), lambda b, be, nv, tk: (layer, be[b], 0, 0)),
                  pl.BlockSpec((1, 1, de, d), lambda b, be, nv, tk: (layer, be[b], 0, 0))],
        out_specs=pl.BlockSpec((bm, d), lambda b, be, nv, tk: (b, 0)),
        scratch_shapes=[pltpu.VMEM((2, bm, d), F32), pltpu.SemaphoreType.DMA((2,)),
                        pltpu.VMEM((d, de), BF16), pltpu.VMEM((d, de), BF16), pltpu.VMEM((de, d), BF16)],
    )
    return pl.pallas_call(
        functools.partial(_experts_body, bm=bm),
        grid_spec=grid_spec,
        out_shape=jax.ShapeDtypeStruct((nb * bm, d), F32),
        compiler_params=_cparams(("arbitrary",), VMEM_LIMIT),
        name="moe_experts",
    )(block_expert, n_valid, slot_tok, xn, w_gate, w_up, w_down)


def _combine_body(slot_ref, ys_hbm, h_ref, rt_ref, g_ref, h2_ref, u_ref, buf, sem, *, tm):
    i = pl.program_id(0)
    ni = pl.num_programs(0)

    def row_copy(src_row, k, r, s):
        return pltpu.make_async_copy(ys_hbm.at[pl.ds(src_row, 1), :], buf.at[s, k, pl.ds(r, 1), :], sem.at[s])

    def issue(tile, s):
        def one(r, carry):
            base = (tile * tm + r) * 2
            row_copy(slot_ref[base], 0, r, s).start()
            row_copy(slot_ref[base + 1], 1, r, s).start()
            return carry

        lax.fori_loop(0, tm, one, 0)

    @pl.when(i == 0)
    def _():
        issue(0, 0)

    @pl.when(i + 1 < ni)
    def _():
        issue(i + 1, (i + 1) % 2)

    slot = i % 2

    def wait_one(r, carry):
        row_copy(0, 0, r, slot).wait()
        row_copy(0, 1, r, slot).wait()
        return carry

    lax.fori_loop(0, tm, wait_one, 0)

    rt = rt_ref[...]
    g1 = rt[:, 4:5]
    g2 = rt[:, 5:6]
    h2 = h_ref[...] + (buf[slot, 0] * g1 + buf[slot, 1] * g2)
    h2_ref[...] = h2
    u_ref[...] = _rms(h2, g_ref[...]).astype(u_ref.dtype)


def moe_combine(ys, slots, h1, route, g_next, tm, u_dtype):
    t, d = h1.shape
    grid_spec = pltpu.PrefetchScalarGridSpec(
        num_scalar_prefetch=1,
        grid=(t // tm,),
        in_specs=[pl.BlockSpec(memory_space=pl.ANY),
                  pl.BlockSpec((tm, d), lambda i, sl: (i, 0)),
                  pl.BlockSpec((tm, LANES), lambda i, sl: (i, 0)),
                  pl.BlockSpec((1, d), lambda i, sl: (0, 0))],
        out_specs=[pl.BlockSpec((tm, d), lambda i, sl: (i, 0)), pl.BlockSpec((tm, d), lambda i, sl: (i, 0))],
        scratch_shapes=[pltpu.VMEM((2, 2, tm, d), F32), pltpu.SemaphoreType.DMA((2,))],
    )
    return pl.pallas_call(
        functools.partial(_combine_body, tm=tm),
        grid_spec=grid_spec,
        out_shape=[jax.ShapeDtypeStruct((t, d), F32), jax.ShapeDtypeStruct((t, d), u_dtype)],
        compiler_params=_cparams(("arbitrary",), VMEM_LIMIT),
        name="moe_combine",
    )(slots, ys, h1, route, g_next.reshape(1, d))


def hier_moe_layer(h1, xn, logits, w_gate, w_up, w_down, layer, g_next, u_dtype):
    t = h1.shape[0]
    bm = MOE_BM
    route, counts = moe_routing(logits, _tile(t, 256))
    e = route[:, 0:2].astype(I32)
    rank = route[:, 2:4].astype(I32)
    counts = counts[0, N_GROUPS:N_GROUPS + N_EXPERTS].astype(I32)
    padded = (counts + bm - 1) // bm * bm
    pad_end = jnp.cumsum(padded)
    pad_start = pad_end - padded
    slots = (pad_start[e] + rank).reshape(-1)
    nb = (2 * t) // bm + N_EXPERTS
    slot_tok = jnp.zeros((nb * bm,), I32).at[slots].set(jnp.repeat(jnp.arange(t, dtype=I32), 2))
    blk0 = jnp.arange(nb, dtype=I32) * bm
    block_expert = jnp.minimum(jnp.searchsorted(pad_end, blk0, side="right"), N_EXPERTS - 1).astype(I32)
    n_valid = jnp.clip(counts[block_expert] - (blk0 - pad_start[block_expert]), 0, bm).astype(I32)
    ys = moe_experts(xn, block_expert, n_valid, slot_tok, w_gate, w_up, w_down, layer, bm)
    return moe_combine(ys, slots, h1, route, g_next, _tile(t, 256), u_dtype)


def _swap_halves(x):
    lane = lax.broadcasted_iota(I32, x.shape, 1)
    return jnp.where(lane < QK_ROPE // 2, pltpu.roll(x, LANES - QK_ROPE // 2, 1), pltpu.roll(x, QK_ROPE // 2, 1))


def _mla_prep_body(p_ref, gq_ref, gkv_ref, cos_ref, sin_ref, cq_ref, ckv_ref, ckvb_ref, kr_ref):
    p = p_ref[...]
    cq_ref[...] = _rms(p[:, :Q_LORA], gq_ref[...]).astype(cq_ref.dtype)
    ckv = _rms(p[:, Q_LORA:Q_LORA + KV_LORA], gkv_ref[...])
    ckv_ref[...] = ckv
    ckvb_ref[...] = ckv.astype(ckvb_ref.dtype)
    kr = p[:, Q_LORA + KV_LORA:]
    kr_ref[...] = kr * cos_ref[...] + _swap_halves(kr) * sin_ref[...]


def mla_prep(proj, g_q, g_kv, cos, sin, tm):
    t = proj.shape[0]
    return pl.pallas_call(
        _mla_prep_body,
        grid=(t // tm,),
        in_specs=[pl.BlockSpec((tm, proj.shape[1]), lambda i: (i, 0)),
                  pl.BlockSpec((1, Q_LORA), lambda i: (0, 0)),
                  pl.BlockSpec((1, KV_LORA), lambda i: (0, 0)),
                  pl.BlockSpec((tm, LANES), lambda i: (i, 0)),
                  pl.BlockSpec((tm, LANES), lambda i: (i, 0))],
        out_specs=[pl.BlockSpec((tm, Q_LORA), lambda i: (i, 0)),
                   pl.BlockSpec((tm, KV_LORA), lambda i: (i, 0)),
                   pl.BlockSpec((tm, KV_LORA), lambda i: (i, 0)),
                   pl.BlockSpec((tm, LANES), lambda i: (i, 0))],
        out_shape=[jax.ShapeDtypeStruct((t, Q_LORA), BF16), jax.ShapeDtypeStruct((t, KV_LORA), F32),
                   jax.ShapeDtypeStruct((t, KV_LORA), BF16), jax.ShapeDtypeStruct((t, LANES), F32)],
        compiler_params=_cparams(("parallel",)),
        name="mla_prep",
    )(proj, g_q.reshape(1, -1), g_kv.reshape(1, -1), cos, sin)


def _mla_q_body(cq_ref, w_ref, cos_ref, sin_ref, q_ref, wb):
    @pl.when(pl.program_id(0) == 0)
    def _():
        wb[...] = w_ref[...].astype(BF16)

    cq = cq_ref[...]
    cos = cos_ref[...]
    sin = sin_ref[...]
    for h in range(B_HEADS):
        y = _dot(cq, wb[:, h * MLA_QK_PAD:(h + 1) * MLA_QK_PAD])
        xr = y[:, LANES:]
        q_ref[:, h * MLA_QK_PAD:h * MLA_QK_PAD + LANES] = y[:, :LANES].astype(q_ref.dtype)
        q_ref[:, h * MLA_QK_PAD + LANES:(h + 1) * MLA_QK_PAD] = \
            (xr * cos + _swap_halves(xr) * sin).astype(q_ref.dtype)


def mla_q_up(cq, w_uq_pad, cos, sin, tm):
    t = cq.shape[0]
    n = B_HEADS * MLA_QK_PAD
    return pl.pallas_call(
        _mla_q_body,
        grid=(t // tm,),
        in_specs=[pl.BlockSpec((tm, Q_LORA), lambda i: (i, 0)),
                  pl.BlockSpec((Q_LORA, n), lambda i: (0, 0)),
                  pl.BlockSpec((tm, LANES), lambda i: (i, 0)),
                  pl.BlockSpec((tm, LANES), lambda i: (i, 0))],
        out_specs=pl.BlockSpec((tm, n), lambda i: (i, 0)),
        out_shape=jax.ShapeDtypeStruct((t, n), BF16),
        scratch_shapes=[pltpu.VMEM((Q_LORA, n), BF16)],
        compiler_params=_cparams(("arbitrary",), VMEM_LIMIT),
        name="mla_q_up",
    )(cq, w_uq_pad, cos, sin)


def _mla_kv_body(c_ref, kr_ref, wk_ref, wv_ref, k_ref, v_ref, wkb, wvb):
    @pl.when(pl.program_id(0) == 0)
    def _():
        wkb[...] = wk_ref[0].astype(BF16)
        wvb[...] = wv_ref[0].astype(BF16)

    c = c_ref[...].astype(BF16)
    kr = kr_ref[...].astype(k_ref.dtype)
    kn = _dot(c, wkb[...])
    for h in range(B_HEADS):
        k_ref[:, h * MLA_QK_PAD:h * MLA_QK_PAD + LANES] = kn[:, h * QK_NOPE:(h + 1) * QK_NOPE].astype(k_ref.dtype)
        k_ref[:, h * MLA_QK_PAD + LANES:(h + 1) * MLA_QK_PAD] = kr
    v_ref[...] = _dot(c, wvb[...]).astype(v_ref.dtype)


def mla_kv_up(ckv, kr, w_uk, w_uv, layer, tm):
    t = ckv.shape[0]
    nk = B_HEADS * MLA_QK_PAD
    nv = B_HEADS * V_DIM
    return pl.pallas_call(
        _mla_kv_body,
        grid=(t // tm,),
        in_specs=[pl.BlockSpec((tm, KV_LORA), lambda i: (i, 0)),
                  pl.BlockSpec((tm, LANES), lambda i: (i, 0)),
                  pl.BlockSpec((1, KV_LORA, B_HEADS * QK_NOPE), lambda i: (layer, 0, 0)),
                  pl.BlockSpec((1, KV_LORA, nv), lambda i: (layer, 0, 0))],
        out_specs=[pl.BlockSpec((tm, nk), lambda i: (i, 0)), pl.BlockSpec((tm, nv), lambda i: (i, 0))],
        out_shape=[jax.ShapeDtypeStruct((t, nk), BF16), jax.ShapeDtypeStruct((t, nv), BF16)],
        scratch_shapes=[pltpu.VMEM((KV_LORA, B_HEADS * QK_NOPE), BF16), pltpu.VMEM((KV_LORA, nv), BF16)],
        compiler_params=_cparams(("arbitrary",), VMEM_LIMIT),
        name="mla_kv_up",
    )(ckv, kr, w_uk, w_uv)


def kernel(x_prompt, x_sample, cache_a_k, cache_a_v, cache_a_kidx, cache_b_ckv, cache_b_krope, norm_mix, norm_ffn, norm_final, rel_bias, a_w_in, a_w_out, b_w_in, b_norm_q, b_norm_kv, b_w_uq, b_w_uk, b_w_uv, b_w_out, moe_w_grp, moe_b_grp, moe_w_rtr, moe_b_rtr, moe_w_gate, moe_w_up, moe_w_down):
    n_p, s_p, d = x_prompt.shape
    n_s, s_s, _ = x_sample.shape
    past = cache_a_k.shape[2]
    tp = n_p * s_p
    ts = n_s * s_s
    t = tp + ts
    a_qd = A_HEADS * A_HEAD_DIM
    tm = _tile(t, 1024)
    tm2 = _tile(t, 512)

    x = jnp.concatenate([x_prompt.reshape(tp, d), x_sample.reshape(ts, d)], axis=0)

    def router_params(i):
        w_r = jnp.concatenate([moe_w_grp[i], moe_w_rtr[i],
                               jnp.zeros((d, LANES - N_GROUPS - N_EXPERTS), F32)], axis=1)
        b_r = jnp.concatenate([moe_b_grp[i], moe_b_rtr[i],
                               jnp.zeros((LANES - N_GROUPS - N_EXPERTS,), F32)]).reshape(1, LANES)
        return w_r, b_r

    u0 = rmsnorm(x, norm_mix[0], tm)
    w_a = a_w_in[0]
    (q_b,) = matmul(u0, w_a, 0, a_qd, [BF16], tm, 512, "a_proj_q")
    k_f, k_b = matmul(u0, w_a, a_qd, a_qd, [F32, BF16], tm, 512, "a_proj_k")
    v_f, v_b = matmul(u0, w_a, 2 * a_qd, a_qd, [F32, BF16], tm, 512, "a_proj_v")
    (qi_b,) = matmul(u0, w_a, 3 * a_qd, IDX_HEADS * IDX_DIM, [BF16], tm, 512, "a_proj_qi")
    c_ki = 3 * a_qd + IDX_HEADS * IDX_DIM
    w_ki = w_a[:, c_ki:c_ki + IDX_DIM]
    w_wi = w_a[:, c_ki + IDX_DIM:c_ki + IDX_DIM + IDX_HEADS]
    zk = jnp.zeros((d, IDX_DIM), F32)
    w_tail = jnp.concatenate([w_ki, zk, zk, w_ki, w_wi, jnp.zeros((d, LANES - IDX_HEADS), F32)], axis=1)
    (tail,) = matmul(u0, w_tail, 0, 3 * LANES, [F32], tm, 3 * LANES, "a_proj_tail")
    kidx = tail[:, :IDX_DIM]

    bias_tab = rel_bias_tables(rel_bias)
    a_scale = A_HEAD_DIM ** -0.5
    hg = 4
    mask_p = dsa_select(qi_b, tail, 2, tail, n_seq=n_p, sq=s_p, bq=128, sk=s_p, sk_real=s_p, tk=512,
                        pos0=0, q_row0=0)
    att_p = attention(q_b, [(k_b, v_b, s_p, 0, 0)], bias_tab, mask_p, n_seq=n_p, sq=s_p, bq=128, q_row0=0,
                      q_col0=0, n_heads=A_HEADS, hg=hg, dq=A_HEAD_DIM, dv=A_HEAD_DIM, scale=a_scale, pos0=0,
                      dyn_sk=s_p, name="dsa_attention_prompt")
    sk_s = past + s_s
    sk_pad = (sk_s + ATT_TILE - 1) // ATT_TILE * ATT_TILE
    ki_past = cache_a_kidx[0]
    zp = jnp.zeros_like(ki_past)
    kc_past = jnp.concatenate([ki_past, zp, zp, ki_past], axis=-1)
    kc_new = tail[tp:, :2 * LANES].reshape(n_s, s_s, 2 * LANES)
    kc_s = jnp.concatenate([kc_past, kc_new, jnp.zeros((n_s, sk_pad - sk_s, 2 * LANES), F32)], axis=1)
    mask_s = dsa_select(qi_b, tail, 2, kc_s.reshape(n_s * sk_pad, 2 * LANES), n_seq=n_s, sq=s_s, bq=s_s,
                        sk=sk_pad, sk_real=sk_s, tk=ATT_TILE, pos0=past, q_row0=tp)
    tiles_s = [(0, r, r, ATT_TILE) for r in range(0, past, ATT_TILE)] + [(1, 0, past, s_s)]
    k_new = k_b[tp:].reshape(n_s * s_s, a_qd)
    v_new = v_b[tp:].reshape(n_s * s_s, a_qd)
    att_s = attention(q_b, [(cache_a_k[0].reshape(n_s * past, a_qd), cache_a_v[0].reshape(n_s * past, a_qd),
                             past, 0, 0), (k_new, v_new, s_s, 0, 0)],
                      bias_tab, mask_s, n_seq=n_s, sq=s_s, bq=s_s, q_row0=tp, q_col0=0, n_heads=A_HEADS, hg=hg,
                      dq=A_HEAD_DIM, dv=A_HEAD_DIM, scale=a_scale, pos0=past, seg_tiles=tiles_s,
                      name="dsa_attention_sample")
    att0 = jnp.concatenate([att_p, att_s], axis=0)

    w_r0, b_r0 = router_params(0)
    h1, xn0, lg0 = outproj_norm_router(att0, a_w_out, 0, x, norm_ffn[0], w_r0, b_r0, tm2, 512)
    h2, u1 = hier_moe_layer(h1, xn0, lg0, moe_w_gate, moe_w_up, moe_w_down, 0, norm_mix[1], BF16)

    n_in = Q_LORA + KV_LORA + QK_ROPE
    w_b = jnp.concatenate([b_w_in[0], jnp.zeros((d, LANES - QK_ROPE), F32)], axis=1)
    (proj,) = matmul(u1, w_b, 0, n_in + LANES - QK_ROPE, [F32], tm, 384, "b_proj")
    half = QK_ROPE // 2
    inv = ROPE_THETA ** (-jnp.arange(half, dtype=F32) / half)
    pos_all = jnp.concatenate([jnp.tile(jnp.arange(s_p, dtype=I32), n_p),
                               jnp.tile(past + jnp.arange(s_s, dtype=I32), n_s)])
    ang = pos_all.astype(F32)[:, None] * inv[None, :]
    zl = jnp.zeros((t, LANES - QK_ROPE), F32)
    cos_t = jnp.concatenate([jnp.cos(ang), jnp.cos(ang), zl], axis=1)
    sin_t = jnp.concatenate([-jnp.sin(ang), jnp.sin(ang), zl], axis=1)
    cq_b, ckv_f, ckv_b, kr_f = mla_prep(proj, b_norm_q[0], b_norm_kv[0], cos_t, sin_t, tm)
    w_uq_pad = jnp.pad(b_w_uq[0].reshape(Q_LORA, B_HEADS, QK_NOPE + QK_ROPE),
                       ((0, 0), (0, 0), (0, MLA_QK_PAD - QK_NOPE - QK_ROPE))).reshape(Q_LORA, B_HEADS * MLA_QK_PAD)
    qm = mla_q_up(cq_b, w_uq_pad, cos_t, sin_t, tm2)
    km_p, vm_p = mla_kv_up(ckv_b[:tp], kr_f[:tp], b_w_uk, b_w_uv, 0, _tile(tp, 512))
    ctab = causal_tables()
    hgb = 4
    matt_p = attention(qm, [(km_p, vm_p, s_p, 0, 0)], ctab, None, n_seq=n_p, sq=s_p, bq=128, q_row0=0, q_col0=0,
                       n_heads=B_HEADS, hg=hgb, dq=MLA_QK_PAD, dv=V_DIM, scale=B_SCALE, pos0=0, dyn_sk=s_p,
                       name="mla_attention_prompt")
    ckv_all = jnp.concatenate([cache_b_ckv[0], ckv_f[tp:].reshape(n_s, s_s, KV_LORA)], axis=1)
    kr_past = jnp.concatenate([cache_b_krope[0], jnp.zeros((n_s, past, LANES - QK_ROPE), F32)], axis=-1)
    kr_all = jnp.concatenate([kr_past, kr_f[tp:].reshape(n_s, s_s, LANES)], axis=1)
    km_s, vm_s = mla_kv_up(ckv_all.reshape(n_s * sk_s, KV_LORA), kr_all.reshape(n_s * sk_s, LANES),
                           b_w_uk, b_w_uv, 0, sk_s)
    tiles_m = [(0, r, r, ATT_TILE) for r in range(0, past, ATT_TILE)] + [(0, past, past, s_s)]
    matt_s = attention(qm, [(km_s, vm_s, sk_s, 0, 0)], ctab, None, n_seq=n_s, sq=s_s, bq=s_s, q_row0=tp, q_col0=0,
                       n_heads=B_HEADS, hg=hgb, dq=MLA_QK_PAD, dv=V_DIM, scale=B_SCALE, pos0=past,
                       seg_tiles=tiles_m, name="mla_attention_sample")
    att1 = jnp.concatenate([matt_p, matt_s], axis=0)

    w_r1, b_r1 = router_params(1)
    h3, xn1, lg1 = outproj_norm_router(att1, b_w_out, 0, h2, norm_ffn[1], w_r1, b_r1, tm2, 512)
    _, y = hier_moe_layer(h3, xn1, lg1, moe_w_gate, moe_w_up, moe_w_down, 1, norm_final, F32)

    y_prompt = y[:tp].reshape(n_p, s_p, d)
    y_sample = y[tp:].reshape(n_s, s_s, d)

    def heads(a, n, s):
        return a.reshape(1, n, s, A_HEADS, A_HEAD_DIM)

    kr_out = kr_f[:, :QK_ROPE]
    return (y_prompt, y_sample,
            heads(k_f[:tp], n_p, s_p), heads(v_f[:tp], n_p, s_p), kidx[:tp].reshape(1, n_p, s_p, IDX_DIM),
            ckv_f[:tp].reshape(1, n_p, s_p, KV_LORA), kr_out[:tp].reshape(1, n_p, s_p, QK_ROPE),
            heads(k_f[tp:], n_s, s_s), heads(v_f[tp:], n_s, s_s), kidx[tp:].reshape(1, n_s, s_s, IDX_DIM),
            ckv_f[tp:].reshape(1, n_s, s_s, KV_LORA), kr_out[tp:].reshape(1, n_s, s_s, QK_ROPE))
```

```python
import functools
import math

import jax
import jax.numpy as jnp
from jax import lax
from jax.experimental import pallas as pl
from jax.experimental.pallas import tpu as pltpu

F32 = jnp.float32
BF16 = jnp.bfloat16
I32 = jnp.int32

RMS_EPS = 1e-6
CHUNK = 64
NEG = -1e30
INT_MIN = -2 ** 31
INT_MAX = 2 ** 31 - 1

LANES = 128
KEY_BLOCK = 128
ATT_TILE = 256
VMEM_LIMIT = 56 * 1024 * 1024

A_HEADS = 16
A_HEAD_DIM = 128
IDX_HEADS = 16
IDX_DIM = 64
TOPK_MAX = 256
IDX_W_SCALE = float((IDX_HEADS * IDX_DIM) ** -0.5)
NUM_BUCKETS = 32
MAX_DISTANCE = 128
B_HEADS = 16
Q_LORA = 512
KV_LORA = 512
QK_NOPE = 128
QK_ROPE = 64
V_DIM = 128
ROPE_THETA = 10000.0
B_SCALE = float((QK_NOPE + QK_ROPE) ** -0.5)
MLA_QK_PAD = 256
N_GROUPS = 8
EXPERTS_PER_GROUP = 8
N_EXPERTS = 64
D_EXPERT = 512
MOE_BM = 128


def _tile(n, pref):
    for c in range(pref, 0, -LANES):
        if n % c == 0:
            return c
    raise ValueError(f"no 128-multiple tile divides {n}")


def _cparams(sem, vmem=None):
    return pltpu.CompilerParams(dimension_semantics=sem, vmem_limit_bytes=vmem)


def _dot(a, b):
    return jnp.dot(a, b, preferred_element_type=F32)


def _dot_nt(a, b):
    return lax.dot_general(a, b, (((1,), (1,)), ((), ())), preferred_element_type=F32)


def _rms(x, g):
    ms = jnp.mean(x * x, axis=-1, keepdims=True)
    return x * lax.rsqrt(ms + RMS_EPS) * g


def _rmsnorm_body(x_ref, g_ref, o_ref):
    o_ref[...] = _rms(x_ref[...], g_ref[...]).astype(o_ref.dtype)


def rmsnorm(x, g, tm, out_dtype=BF16):
    t, d = x.shape
    return pl.pallas_call(
        _rmsnorm_body,
        grid=(t // tm,),
        in_specs=[pl.BlockSpec((tm, d), lambda i: (i, 0)), pl.BlockSpec((1, d), lambda i: (0, 0))],
        out_specs=pl.BlockSpec((tm, d), lambda i: (i, 0)),
        out_shape=jax.ShapeDtypeStruct((t, d), out_dtype),
        compiler_params=_cparams(("parallel",)),
        name="rmsnorm",
    )(x, g.reshape(1, d))


def _matmul_body(a_ref, w_ref, *rest):
    outs, wb = rest[:-1], rest[-1]

    @pl.when(pl.program_id(1) == 0)
    def _():
        wb[...] = w_ref[...].astype(BF16)

    r = _dot(a_ref[...], wb[...])
    for o in outs:
        o[...] = r.astype(o.dtype)


def matmul(a, w, col0, ncols, out_dtypes, tm, tn, name):
    t, k = a.shape
    assert ncols % tn == 0 and col0 % tn == 0 and t % tm == 0
    cb = col0 // tn
    return pl.pallas_call(
        _matmul_body,
        grid=(ncols // tn, t // tm),
        in_specs=[pl.BlockSpec((tm, k), lambda j, i: (i, 0)),
                  pl.BlockSpec((k, tn), lambda j, i: (0, j + cb))],
        out_specs=[pl.BlockSpec((tm, tn), lambda j, i: (i, j)) for _ in out_dtypes],
        out_shape=[jax.ShapeDtypeStruct((t, ncols), dt) for dt in out_dtypes],
        scratch_shapes=[pltpu.VMEM((k, tn), BF16)],
        compiler_params=_cparams(("arbitrary", "arbitrary"), VMEM_LIMIT),
        name=name,
    )(a, w)


def _float_sort_key(x):
    bits = lax.bitcast_convert_type(x, I32)
    return bits ^ ((bits >> 31) & INT_MAX)


def _select_body(qi_ref, wi_ref, kc_ref, mask_ref, key_scr, jm_scr, *, bq, sk, sk_real, tk, pos0, topk):
    b = pl.program_id(1)
    t0 = pos0 + b * bq
    kmax = jnp.minimum(sk_real, ((t0 + bq - 1) // CHUNK + 1) * CHUNK)
    nkt = (kmax + tk - 1) // tk
    nch = tk // LANES

    w = wi_ref[...] * IDX_W_SCALE
    wb = [jnp.broadcast_to(w[:, h:h + 1], (bq, LANES)) for h in range(IDX_HEADS)]
    row_pos = t0 + lax.broadcasted_iota(I32, (bq, LANES), 0)
    lim = jnp.minimum((row_pos // CHUNK + 1) * CHUNK, sk_real)
    lane = lax.broadcasted_iota(I32, (bq, LANES), 1)

    def score_tile(j, carry):
        off = pl.multiple_of(j * tk, tk)
        kc = kc_ref[pl.ds(off, tk), :].astype(BF16)
        ka, kb = kc[:, :LANES], kc[:, LANES:]
        accs = [jnp.zeros((bq, LANES), F32) for _ in range(nch)]
        for g in range(IDX_HEADS // 2):
            qg = qi_ref[:, g * LANES:(g + 1) * LANES]
            sa = _dot_nt(qg, ka)
            sb = _dot_nt(qg, kb)
            for c in range(nch):
                sl = slice(c * LANES, (c + 1) * LANES)
                accs[c] = accs[c] + wb[2 * g] * jnp.maximum(sa[:, sl], 0.0) \
                    + wb[2 * g + 1] * jnp.maximum(sb[:, sl], 0.0)
        for c in range(nch):
            kpos = off + c * LANES + lane
            sc = jnp.where(kpos < lim, accs[c], -jnp.inf)
            key_scr[:, pl.ds(off + c * LANES, LANES)] = _float_sort_key(sc)
        return carry

    lax.fori_loop(0, nkt, score_tile, 0)

    def count(indicator):
        def tile(j, cnt):
            off = pl.multiple_of(j * tk, tk)
            for c in range(nch):
                kt = key_scr[:, pl.ds(off + c * LANES, LANES)]
                cnt = cnt + indicator(kt, off + c * LANES + lane)
            return cnt
        cnt = lax.fori_loop(0, nkt, tile, jnp.zeros((bq, LANES), F32))
        return jnp.sum(cnt, axis=1, keepdims=True)

    def bit_step(i, pfx_u):
        bit = lax.shift_left(jnp.int32(1), 31 - i)
        cand_u = pfx_u | bit
        cand_s = cand_u ^ INT_MIN
        total = count(lambda kt, kp: jnp.where(kt >= cand_s, 1.0, 0.0))
        return jnp.where(total >= topk, cand_u, pfx_u)

    pfx = lax.fori_loop(0, 32, bit_step, jnp.zeros((bq, LANES), I32))
    thr = pfx ^ INT_MIN

    n_gt = count(lambda kt, kp: jnp.where(kt > thr, 1.0, 0.0))
    n_ge = count(lambda kt, kp: jnp.where(kt >= thr, 1.0, 0.0))
    quota = topk - n_gt
    jm_scr[...] = jnp.full((bq, LANES), INT_MAX, I32)
    any_excess = jnp.max(jnp.where(n_ge > topk, 1.0, 0.0)) > 0.0

    @pl.when(any_excess)
    def _():
        nbits = max(1, int(sk - 1).bit_length())

        def idx_step(i, ans):
            cand = ans | lax.shift_left(jnp.int32(1), nbits - 1 - i)
            below = count(lambda kt, kp: jnp.where(kt == thr, jnp.where(kp < cand, 1.0, 0.0), 0.0))
            return jnp.where(below < quota, cand, ans)

        jm_scr[...] = lax.fori_loop(0, nbits, idx_step, jnp.zeros((bq, LANES), I32))

    jm = jm_scr[...]

    def write_tile(j, carry):
        off = pl.multiple_of(j * tk, tk)
        for c in range(nch):
            kt = key_scr[:, pl.ds(off + c * LANES, LANES)]
            kpos = off + c * LANES + lane
            v = jnp.where(kt > thr, 0.0, jnp.where(kt == thr, jnp.where(kpos <= jm, 0.0, NEG), NEG))
            v = jnp.where(kpos < lim, v, NEG)
            mask_ref[:, pl.ds(off + c * LANES, LANES)] = v.astype(mask_ref.dtype)
        return carry

    lax.fori_loop(0, nkt, write_tile, 0)

    def fill_tile(j, carry):
        off = pl.multiple_of(j * tk, tk)
        mask_ref[:, pl.ds(off, tk)] = jnp.full((bq, tk), NEG, mask_ref.dtype)
        return carry

    lax.fori_loop(nkt, sk // tk, fill_tile, 0)


def dsa_select(qi, wi_arr, wi_blk, kc, *, n_seq, sq, bq, sk, sk_real, tk, pos0, q_row0):
    nqb = sq // bq
    qb0 = q_row0 // bq
    assert q_row0 % bq == 0 and sk % tk == 0 and tk >= TOPK_MAX
    topk = min(TOPK_MAX, sk_real // 4)
    body = functools.partial(_select_body, bq=bq, sk=sk, sk_real=sk_real, tk=tk, pos0=pos0, topk=topk)
    return pl.pallas_call(
        body,
        grid=(n_seq, nqb),
        in_specs=[pl.BlockSpec((bq, IDX_HEADS * IDX_DIM), lambda s, b: (qb0 + s * nqb + b, 0)),
                  pl.BlockSpec((bq, LANES), lambda s, b: (qb0 + s * nqb + b, wi_blk)),
                  pl.BlockSpec((sk, 2 * LANES), lambda s, b: (s, 0))],
        out_specs=pl.BlockSpec((bq, sk), lambda s, b: (s * nqb + b, 0)),
        out_shape=jax.ShapeDtypeStruct((n_seq * sq, sk), BF16),
        scratch_shapes=[pltpu.VMEM((bq, sk), I32), pltpu.VMEM((bq, LANES), I32)],
        compiler_params=_cparams(("parallel", "arbitrary"), VMEM_LIMIT),
        name="dsa_select",
    )(qi, wi_arr, kc)


def _rel_tables_body(bias_ref, tab_ref):
    i = lax.broadcasted_iota(I32, (KEY_BLOCK, KEY_BLOCK), 0)
    j = lax.broadcasted_iota(I32, (KEY_BLOCK, KEY_BLOCK), 1)
    nb = NUM_BUCKETS // 2
    max_exact = nb // 2
    edges = [12, 16, 23, 32, 46, 64, 91]
    for d in range(4):
        rel = j - i + (d - 2) * KEY_BLOCK
        n = jnp.abs(rel)
        large = jnp.full_like(n, max_exact)
        for e in edges:
            large = large + jnp.where(n >= e, 1, 0)
        bucket = jnp.where(rel > 0, nb, 0) + jnp.where(n < max_exact, n, large)
        for h in range(A_HEADS):
            acc = jnp.zeros((KEY_BLOCK, KEY_BLOCK), F32)
            for k in range(NUM_BUCKETS):
                acc = jnp.where(bucket == k, bias_ref[k, h], acc)
            tab_ref[h, d] = acc


def rel_bias_tables(rel_bias):
    return pl.pallas_call(
        _rel_tables_body,
        in_specs=[pl.BlockSpec(memory_space=pltpu.SMEM)],
        out_shape=jax.ShapeDtypeStruct((A_HEADS, 4, KEY_BLOCK, KEY_BLOCK), F32),
        name="rel_bias_tables",
    )(rel_bias)


def _causal_tables_body(tab_ref):
    i = lax.broadcasted_iota(I32, (KEY_BLOCK, KEY_BLOCK), 0)
    j = lax.broadcasted_iota(I32, (KEY_BLOCK, KEY_BLOCK), 1)
    zero = jnp.zeros((KEY_BLOCK, KEY_BLOCK), F32)
    tab_ref[0, 0] = zero
    tab_ref[0, 1] = zero
    tab_ref[0, 2] = jnp.where(j // CHUNK <= i // CHUNK, 0.0, NEG)
    tab_ref[0, 3] = jnp.full((KEY_BLOCK, KEY_BLOCK), NEG, F32)


def causal_tables():
    return pl.pallas_call(
        _causal_tables_body,
        out_shape=jax.ShapeDtypeStruct((1, 4, KEY_BLOCK, KEY_BLOCK), F32),
        name="causal_tables",
    )()


def _attn_pipe_body(*refs, has_mask, per_head_tab, bq, hg, dq, dv, scale, pos0, sk):
    q_ref, k_ref, v_ref, tab_ref = refs[:4]
    mask_ref = refs[4] if has_mask else None
    out_ref = refs[5] if has_mask else refs[4]
    s_scr, p_scr, m_scr, l_scr, acc_scr = refs[-5:]

    b = pl.program_id(2)
    t0 = pos0 + b * bq
    qblk = t0 // KEY_BLOCK
    kmax = jnp.minimum(sk, ((t0 + bq - 1) // CHUNK + 1) * CHUNK)
    nkt = (kmax + ATT_TILE - 1) // ATT_TILE
    nch = ATT_TILE // LANES

    m_scr[...] = jnp.full(m_scr.shape, NEG, F32)
    l_scr[...] = jnp.zeros(l_scr.shape, F32)
    acc_scr[...] = jnp.zeros(acc_scr.shape, F32)

    def logits_tile(j, carry):
        off = pl.multiple_of(j * ATT_TILE, ATT_TILE)
        kb0 = off // KEY_BLOCK
        ds = [jnp.clip(kb0 + c - qblk + 2, 0, 3) for c in range(nch)]
        if has_mask:
            mk = mask_ref[:, pl.ds(off, ATT_TILE)].astype(F32)
        for h in range(hg):
            s = _dot_nt(q_ref[:, h * dq:(h + 1) * dq], k_ref[pl.ds(off, ATT_TILE), h * dq:(h + 1) * dq]) * scale
            hh = h if per_head_tab else 0
            s = s + jnp.concatenate([tab_ref[hh, ds[c]] for c in range(nch)], axis=1)
            if has_mask:
                s = s + mk
            s_scr[h, :, pl.ds(off, ATT_TILE)] = s
            mvec = m_scr[h]
            for c in range(nch):
                mvec = jnp.maximum(mvec, s[:, c * LANES:(c + 1) * LANES])
            m_scr[h] = mvec
        return carry

    lax.fori_loop(0, nkt, logits_tile, 0)
    for h in range(hg):
        m_scr[h] = jnp.broadcast_to(jnp.max(m_scr[h], axis=1, keepdims=True), (bq, LANES))

    def exp_tile(j, carry):
        off = pl.multiple_of(j * ATT_TILE, ATT_TILE)
        for h in range(hg):
            s = s_scr[h, :, pl.ds(off, ATT_TILE)]
            m = m_scr[h]
            p = [jnp.exp(s[:, c * LANES:(c + 1) * LANES] - m) for c in range(nch)]
            lvec = l_scr[h]
            for c in range(nch):
                lvec = lvec + p[c]
            l_scr[h] = lvec
            p_scr[h, :, pl.ds(off, ATT_TILE)] = jnp.concatenate(p, axis=1).astype(BF16)
        return carry

    lax.fori_loop(0, nkt, exp_tile, 0)

    def pv_tile(j, carry):
        off = pl.multiple_of(j * ATT_TILE, ATT_TILE)
        for h in range(hg):
            acc_scr[h] = acc_scr[h] + _dot(p_scr[h, :, pl.ds(off, ATT_TILE)],
                                           v_ref[pl.ds(off, ATT_TILE), h * dv:(h + 1) * dv])
        return carry

    lax.fori_loop(0, nkt, pv_tile, 0)
    for h in range(hg):
        l_row = jnp.sum(l_scr[h], axis=1, keepdims=True)
        out_ref[:, h * dv:(h + 1) * dv] = (acc_scr[h] / l_row).astype(out_ref.dtype)


def _attn_body(*refs, n_seg, has_mask, per_head_tab, bq, hg, dq, dv, scale, pos0, seg_tiles):
    q_ref = refs[0]
    kv = refs[1:1 + 2 * n_seg]
    pos = 1 + 2 * n_seg
    tab_ref = refs[pos]
    pos += 1
    mask_ref = refs[pos] if has_mask else None
    pos += 1 if has_mask else 0
    out_ref = refs[pos]
    m_scr, l_scr, acc_scr = refs[pos + 1:pos + 4]

    b = pl.program_id(2)
    t0 = pos0 + b * bq
    qblk = t0 // KEY_BLOCK

    m_scr[...] = jnp.full(m_scr.shape, NEG, F32)
    l_scr[...] = jnp.zeros(l_scr.shape, F32)
    acc_scr[...] = jnp.zeros(acc_scr.shape, F32)

    def process(k_t, v_t, key0, width):
        nch = max(1, width // KEY_BLOCK)
        cw = min(width, KEY_BLOCK)
        ds = [jnp.clip(key0 // KEY_BLOCK + c - qblk + 2, 0, 3) for c in range(nch)]
        if has_mask:
            if isinstance(key0, int):
                mk = mask_ref[:, key0:key0 + width].astype(F32)
            else:
                mk = mask_ref[:, pl.ds(pl.multiple_of(key0, width), width)].astype(F32)
        for h in range(hg):
            hh = h if per_head_tab else 0
            s = _dot_nt(q_ref[:, h * dq:(h + 1) * dq], k_t[:, h * dq:(h + 1) * dq]) * scale
            bias = [tab_ref[hh, ds[c], :bq, :cw] for c in range(nch)]
            s = s + (jnp.concatenate(bias, axis=1) if nch > 1 else bias[0])
            if has_mask:
                s = s + mk
            m_old = m_scr[h]
            m_new = jnp.maximum(m_old, jnp.max(s, axis=1, keepdims=True))
            alpha = jnp.exp(m_old - m_new)
            if cw == LANES:
                p = [jnp.exp(s[:, c * LANES:(c + 1) * LANES] - m_new) for c in range(nch)]
                lsum = p[0]
                for c in range(1, nch):
                    lsum = lsum + p[c]
                pfull = jnp.concatenate(p, axis=1) if nch > 1 else p[0]
                l_scr[h] = alpha * l_scr[h] + lsum
            else:
                pfull = jnp.exp(s - m_new[:, :cw])
                l_scr[h] = alpha * l_scr[h] + jnp.concatenate(
                    [pfull, jnp.zeros((bq, LANES - cw), F32)], axis=1)
            m_scr[h] = m_new
            acc_scr[h] = alpha * acc_scr[h] + _dot(pfull.astype(BF16), v_t[:, h * dv:(h + 1) * dv])

    for (si, row0, key0, width) in seg_tiles:
        process(kv[2 * si][row0:row0 + width, :].astype(BF16),
                kv[2 * si + 1][row0:row0 + width, :].astype(BF16), key0, width)

    for h in range(hg):
        l_row = jnp.sum(l_scr[h], axis=1, keepdims=True)
        out_ref[:, h * dv:(h + 1) * dv] = (acc_scr[h] / l_row).astype(out_ref.dtype)


def attention(q, segs, tab, mask, *, n_seq, sq, bq, q_row0, q_col0, n_heads, hg, dq, dv, scale, pos0,
              seg_tiles=None, dyn_sk=None, name="attention"):
    nqb = sq // bq
    ng = n_heads // hg
    qb0 = q_row0 // bq
    assert q_row0 % bq == 0 and q_col0 % (hg * dq) == 0 and pos0 % KEY_BLOCK == 0
    assert bq == KEY_BLOCK or nqb == 1
    qc0 = q_col0 // (hg * dq)
    in_specs = [pl.BlockSpec((bq, hg * dq), lambda s, g, b: (qb0 + s * nqb + b, qc0 + g))]
    args = [q]
    for (k, v, rows, kc0, vc0) in segs:
        assert kc0 % (hg * dq) == 0 and vc0 % (hg * dv) == 0
        mode = dict(pipeline_mode=pl.Buffered(1)) if dyn_sk is not None else {}
        in_specs.append(pl.BlockSpec((rows, hg * dq), lambda s, g, b, c=kc0 // (hg * dq): (s, c + g), **mode))
        in_specs.append(pl.BlockSpec((rows, hg * dv), lambda s, g, b, c=vc0 // (hg * dv): (s, c + g), **mode))
        args += [k, v]
    per_head_tab = tab.shape[0] > 1
    if per_head_tab:
        in_specs.append(pl.BlockSpec((hg, 4, KEY_BLOCK, KEY_BLOCK), lambda s, g, b: (g, 0, 0, 0)))
    else:
        in_specs.append(pl.BlockSpec((1, 4, KEY_BLOCK, KEY_BLOCK), lambda s, g, b: (0, 0, 0, 0)))
    args.append(tab)
    if mask is not None:
        in_specs.append(pl.BlockSpec((bq, mask.shape[1]), lambda s, g, b: (s * nqb + b, 0)))
        args.append(mask)
    if dyn_sk is not None:
        assert len(segs) == 1 and dyn_sk % ATT_TILE == 0 and bq == KEY_BLOCK
        body = functools.partial(_attn_pipe_body, has_mask=mask is not None, per_head_tab=per_head_tab,
                                 bq=bq, hg=hg, dq=dq, dv=dv, scale=scale, pos0=pos0, sk=dyn_sk)
        scratch = [pltpu.VMEM((hg, bq, dyn_sk), F32), pltpu.VMEM((hg, bq, dyn_sk), BF16),
                   pltpu.VMEM((hg, bq, LANES), F32), pltpu.VMEM((hg, bq, LANES), F32),
                   pltpu.VMEM((hg, bq, dv), F32)]
    else:
        body = functools.partial(_attn_body, n_seg=len(segs), has_mask=mask is not None,
                                 per_head_tab=per_head_tab, bq=bq, hg=hg, dq=dq, dv=dv, scale=scale, pos0=pos0,
                                 seg_tiles=seg_tiles)
        scratch = [pltpu.VMEM((hg, bq, LANES), F32), pltpu.VMEM((hg, bq, LANES), F32),
                   pltpu.VMEM((hg, bq, dv), F32)]
    return pl.pallas_call(
        body,
        grid=(n_seq, ng, nqb),
        in_specs=in_specs,
        out_specs=pl.BlockSpec((bq, hg * dv), lambda s, g, b: (s * nqb + b, g)),
        out_shape=jax.ShapeDtypeStruct((n_seq * sq, n_heads * dv), BF16),
        scratch_shapes=scratch,
        compiler_params=_cparams(("parallel", "parallel", "arbitrary"), VMEM_LIMIT),
        name=name,
    )(*args)


def _outproj_body(a_ref, w_ref, x_ref, g_ref, wr_ref, br_ref, h_ref, xn_ref, lg_ref, hrow, *, tn):
    j = pl.program_id(1)
    r = x_ref[...] + _dot(a_ref[...], w_ref[0].astype(BF16))
    h_ref[...] = r
    hrow[:, pl.ds(pl.multiple_of(j * tn, tn), tn)] = r

    @pl.when(j == pl.num_programs(1) - 1)
    def _():
        xn = _rms(hrow[...], g_ref[...])
        xn_ref[...] = xn
        lg_ref[...] = _dot(xn.astype(BF16), wr_ref[...].astype(BF16)) + br_ref[...]


def outproj_norm_router(a, w_out, layer, x_res, g_ffn, w_r, b_r, tm, tn):
    t, k = a.shape
    d = w_out.shape[-1]
    return pl.pallas_call(
        functools.partial(_outproj_body, tn=tn),
        grid=(t // tm, d // tn),
        in_specs=[pl.BlockSpec((tm, k), lambda i, j: (i, 0)),
                  pl.BlockSpec((1, k, tn), lambda i, j: (layer, 0, j)),
                  pl.BlockSpec((tm, tn), lambda i, j: (i, j)),
                  pl.BlockSpec((1, d), lambda i, j: (0, 0)),
                  pl.BlockSpec((d, LANES), lambda i, j: (0, 0)),
                  pl.BlockSpec((1, LANES), lambda i, j: (0, 0))],
        out_specs=[pl.BlockSpec((tm, tn), lambda i, j: (i, j)),
                   pl.BlockSpec((tm, d), lambda i, j: (i, 0)),
                   pl.BlockSpec((tm, LANES), lambda i, j: (i, 0))],
        out_shape=[jax.ShapeDtypeStruct((t, d), F32), jax.ShapeDtypeStruct((t, d), F32),
                   jax.ShapeDtypeStruct((t, LANES), F32)],
        scratch_shapes=[pltpu.VMEM((tm, d), F32)],
        compiler_params=_cparams(("parallel", "arbitrary"), VMEM_LIMIT),
        name="outproj_norm_router",
    )(a, w_out, x_res, g_ffn.reshape(1, d), w_r, b_r)


def _routing_body(lg_ref, out_ref, cnt_ref, carry, *, tm):
    i = pl.program_id(0)

    @pl.when(i == 0)
    def _():
        carry[...] = jnp.zeros(carry.shape, F32)

    x = lg_ref[...]
    lane = lax.broadcasted_iota(I32, (tm, LANES), 1)
    neg_inf = -jnp.inf

    def rmax(v):
        return jnp.max(v, axis=1, keepdims=True)

    def rmin(v):
        return jnp.min(v, axis=1, keepdims=True)

    def rsum(v):
        return jnp.sum(v, axis=1, keepdims=True)

    gm = lane < N_GROUPS
    gmax = rmax(jnp.where(gm, x, neg_inf))
    gsel = rmin(jnp.where(gm, jnp.where(x == gmax, lane, LANES), LANES))
    gsum = rsum(jnp.where(gm, jnp.exp(x - gmax), 0.0))
    g_w = 1.0 / gsum
    lo = N_GROUPS + gsel * EXPERTS_PER_GROUP
    em = jnp.logical_and(lane >= lo, lane < lo + EXPERTS_PER_GROUP)
    emax = rmax(jnp.where(em, x, neg_inf))
    ee = jnp.where(em, jnp.exp(x - emax), 0.0)
    p = ee / rsum(ee)
    p1 = rmax(jnp.where(em, p, -1.0))
    i1 = rmin(jnp.where(em, jnp.where(p == p1, lane, LANES), LANES))
    em2 = jnp.logical_and(em, lane != i1)
    p2 = rmax(jnp.where(em2, p, -1.0))
    i2 = rmin(jnp.where(em2, jnp.where(p == p2, lane, LANES), LANES))
    den = p1 + p2
    g1 = g_w * (p1 / den)
    g2 = g_w * (p2 / den)

    oh1 = jnp.where(lane == i1, 1.0, 0.0)
    oh2 = jnp.where(lane == i2, 1.0, 0.0)
    oh = oh1 + oh2
    r = lax.broadcasted_iota(I32, (tm, tm), 0)
    c = lax.broadcasted_iota(I32, (tm, tm), 1)
    lower = jnp.where(c < r, 1.0, 0.0).astype(BF16)
    before = _dot(lower, oh.astype(BF16)) + carry[...]
    rank1 = rsum(oh1 * before)
    rank2 = rsum(oh2 * before)
    carry[...] = carry[...] + jnp.sum(oh, axis=0, keepdims=True)

    e1 = (i1 - N_GROUPS).astype(F32)
    e2 = (i2 - N_GROUPS).astype(F32)
    vals = [e1, e2, rank1, rank2, g1, g2]
    out = jnp.zeros((tm, LANES), F32)
    for k, v in enumerate(vals):
        out = jnp.where(lane == k, v, out)
    out_ref[...] = out
    cnt_ref[...] = carry[...]


def moe_routing(logits, tm):
    t = logits.shape[0]
    return pl.pallas_call(
        functools.partial(_routing_body, tm=tm),
        grid=(t // tm,),
        in_specs=[pl.BlockSpec((tm, LANES), lambda i: (i, 0))],
        out_specs=[pl.BlockSpec((tm, LANES), lambda i: (i, 0)), pl.BlockSpec((1, LANES), lambda i: (0, 0))],
        out_shape=[jax.ShapeDtypeStruct((t, LANES), F32), jax.ShapeDtypeStruct((1, LANES), F32)],
        scratch_shapes=[pltpu.VMEM((1, LANES), F32)],
        compiler_params=_cparams(("arbitrary",)),
        name="moe_routing",
    )(logits)


def _experts_body(be_ref, nv_ref, tok_ref, x_hbm, wg_ref, wu_ref, wd_ref, ys_ref,
                  buf, sem, wgb, wub, wdb, *, bm):
    b = pl.program_id(0)
    nb = pl.num_programs(0)

    def row_copy(tok, r, s):
        return pltpu.make_async_copy(x_hbm.at[pl.ds(tok, 1), :], buf.at[s, pl.ds(r, 1), :], sem.at[s])

    def issue(blk, s):
        n = nv_ref[blk]

        @pl.when(n < bm)
        def _():
            buf[s] = jnp.zeros(buf.shape[1:], F32)

        def one(r, carry):
            row_copy(tok_ref[blk * bm + r], r, s).start()
            return carry

        lax.fori_loop(0, n, one, 0)

    @pl.when(b == 0)
    def _():
        issue(0, 0)

    @pl.when(b + 1 < nb)
    def _():
        issue(b + 1, (b + 1) % 2)

    slot = b % 2
    n = nv_ref[b]

    def wait_one(r, carry):
        row_copy(0, r, slot).wait()
        return carry

    lax.fori_loop(0, n, wait_one, 0)

    @pl.when(n > 0)
    def _():
        prev = be_ref[jnp.maximum(b - 1, 0)]

        @pl.when(jnp.logical_or(b == 0, be_ref[b] != prev))
        def _():
            wgb[...] = wg_ref[0, 0].astype(BF16)
            wub[...] = wu_ref[0, 0].astype(BF16)
            wdb[...] = wd_ref[0, 0].astype(BF16)

        x = buf[slot].astype(BF16)
        g = _dot(x, wgb[...])
        u = _dot(x, wub[...])
        a = (g * (1.0 / (1.0 + jnp.exp(-g)))) * u
        ys_ref[...] = _dot(a.astype(BF16), wdb[...])

    @pl.when(n == 0)
    def _():
        ys_ref[...] = jnp.zeros(ys_ref.shape, F32)


def moe_experts(xn, block_expert, n_valid, slot_tok, w_gate, w_up, w_down, layer, bm):
    t, d = xn.shape
    nb = block_expert.shape[0]
    de = w_gate.shape[-1]
    grid_spec = pltpu.PrefetchScalarGridSpec(
        num_scalar_prefetch=3,
        grid=(nb,),
        in_specs=[pl.BlockSpec(memory_space=pl.ANY),
                  pl.BlockSpec((1, 1, d, de), lambda b, be, nv, tk: (layer, be[b], 0, 0)),
                  pl.BlockSpec((1, 1, d, de), lambda b, be, nv, tk: (layer, be[b], 0, 0)),
                  pl.BlockSpec((1, 1, de, d), lambda b, be, nv, tk: (layer, be[b], 0, 0))],
        out_specs=pl.BlockSpec((bm, d), lambda b, be, nv, tk: (b, 0)),
        scratch_shapes=[pltpu.VMEM((2, bm, d), F32), pltpu.SemaphoreType.DMA((2,)),
                        pltpu.VMEM((d, de), BF16), pltpu.VMEM((d, de), BF16), pltpu.VMEM((de, d), BF16)],
    )
    return pl.pallas_call(
        functools.partial(_experts_body, bm=bm),
        grid_spec=grid_spec,
        out_shape=jax.ShapeDtypeStruct((nb * bm, d), F32),
        compiler_params=_cparams(("arbitrary",), VMEM_LIMIT),
        name="moe_experts",
    )(block_expert, n_valid, slot_tok, xn, w_gate, w_up, w_down)


def _combine_body(slot_ref, ys_hbm, h_ref, rt_ref, g_ref, h2_ref, u_ref, buf, sem, *, tm):
    i = pl.program_id(0)
    ni = pl.num_programs(0)

    def row_copy(src_row, k, r, s):
        return pltpu.make_async_copy(ys_hbm.at[pl.ds(src_row, 1), :], buf.at[s, k, pl.ds(r, 1), :], sem.at[s])

    def issue(tile, s):
        def one(r, carry):
            base = (tile * tm + r) * 2
            row_copy(slot_ref[base], 0, r, s).start()
            row_copy(slot_ref[base + 1], 1, r, s).start()
            return carry

        lax.fori_loop(0, tm, one, 0)

    @pl.when(i == 0)
    def _():
        issue(0, 0)

    @pl.when(i + 1 < ni)
    def _():
        issue(i + 1, (i + 1) % 2)

    slot = i % 2

    def wait_one(r, carry):
        row_copy(0, 0, r, slot).wait()
        row_copy(0, 1, r, slot).wait()
        return carry

    lax.fori_loop(0, tm, wait_one, 0)

    rt = rt_ref[...]
    g1 = rt[:, 4:5]
    g2 = rt[:, 5:6]
    h2 = h_ref[...] + (buf[slot, 0] * g1 + buf[slot, 1] * g2)
    h2_ref[...] = h2
    u_ref[...] = _rms(h2, g_ref[...]).astype(u_ref.dtype)


def moe_combine(ys, slots, h1, route, g_next, tm, u_dtype):
    t, d = h1.shape
    grid_spec = pltpu.PrefetchScalarGridSpec(
        num_scalar_prefetch=1,
        grid=(t // tm,),
        in_specs=[pl.BlockSpec(memory_space=pl.ANY),
                  pl.BlockSpec((tm, d), lambda i, sl: (i, 0)),
                  pl.BlockSpec((tm, LANES), lambda i, sl: (i, 0)),
                  pl.BlockSpec((1, d), lambda i, sl: (0, 0))],
        out_specs=[pl.BlockSpec((tm, d), lambda i, sl: (i, 0)), pl.BlockSpec((tm, d), lambda i, sl: (i, 0))],
        scratch_shapes=[pltpu.VMEM((2, 2, tm, d), F32), pltpu.SemaphoreType.DMA((2,))],
    )
    return pl.pallas_call(
        functools.partial(_combine_body, tm=tm),
        grid_spec=grid_spec,
        out_shape=[jax.ShapeDtypeStruct((t, d), F32), jax.ShapeDtypeStruct((t, d), u_dtype)],
        compiler_params=_cparams(("arbitrary",), VMEM_LIMIT),
        name="moe_combine",
    )(slots, ys, h1, route, g_next.reshape(1, d))


def hier_moe_layer(h1, xn, logits, w_gate, w_up, w_down, layer, g_next, u_dtype):
    t = h1.shape[0]
    bm = MOE_BM
    route, counts = moe_routing(logits, _tile(t, 256))
    e = route[:, 0:2].astype(I32)
    rank = route[:, 2:4].astype(I32)
    counts = counts[0, N_GROUPS:N_GROUPS + N_EXPERTS].astype(I32)
    padded = (counts + bm - 1) // bm * bm
    pad_end = jnp.cumsum(padded)
    pad_start = pad_end - padded
    slots = (pad_start[e] + rank).reshape(-1)
    nb = (2 * t) // bm + N_EXPERTS
    slot_tok = jnp.zeros((nb * bm,), I32).at[slots].set(jnp.repeat(jnp.arange(t, dtype=I32), 2))
    blk0 = jnp.arange(nb, dtype=I32) * bm
    block_expert = jnp.minimum(jnp.searchsorted(pad_end, blk0, side="right"), N_EXPERTS - 1).astype(I32)
    n_valid = jnp.clip(counts[block_expert] - (blk0 - pad_start[block_expert]), 0, bm).astype(I32)
    ys = moe_experts(xn, block_expert, n_valid, slot_tok, w_gate, w_up, w_down, layer, bm)
    return moe_combine(ys, slots, h1, route, g_next, _tile(t, 256), u_dtype)


def _swap_halves(x):
    lane = lax.broadcasted_iota(I32, x.shape, 1)
    return jnp.where(lane < QK_ROPE // 2, pltpu.roll(x, LANES - QK_ROPE // 2, 1), pltpu.roll(x, QK_ROPE // 2, 1))


def _mla_prep_body(p_ref, gq_ref, gkv_ref, cos_ref, sin_ref, cq_ref, ckv_ref, ckvb_ref, kr_ref):
    p = p_ref[...]
    cq_ref[...] = _rms(p[:, :Q_LORA], gq_ref[...]).astype(cq_ref.dtype)
    ckv = _rms(p[:, Q_LORA:Q_LORA + KV_LORA], gkv_ref[...])
    ckv_ref[...] = ckv
    ckvb_ref[...] = ckv.astype(ckvb_ref.dtype)
    kr = p[:, Q_LORA + KV_LORA:]
    kr_ref[...] = kr * cos_ref[...] + _swap_halves(kr) * sin_ref[...]


def mla_prep(proj, g_q, g_kv, cos, sin, tm):
    t = proj.shape[0]
    return pl.pallas_call(
        _mla_prep_body,
        grid=(t // tm,),
        in_specs=[pl.BlockSpec((tm, proj.shape[1]), lambda i: (i, 0)),
                  pl.BlockSpec((1, Q_LORA), lambda i: (0, 0)),
                  pl.BlockSpec((1, KV_LORA), lambda i: (0, 0)),
                  pl.BlockSpec((tm, LANES), lambda i: (i, 0)),
                  pl.BlockSpec((tm, LANES), lambda i: (i, 0))],
        out_specs=[pl.BlockSpec((tm, Q_LORA), lambda i: (i, 0)),
                   pl.BlockSpec((tm, KV_LORA), lambda i: (i, 0)),
                   pl.BlockSpec((tm, KV_LORA), lambda i: (i, 0)),
                   pl.BlockSpec((tm, LANES), lambda i: (i, 0))],
        out_shape=[jax.ShapeDtypeStruct((t, Q_LORA), BF16), jax.ShapeDtypeStruct((t, KV_LORA), F32),
                   jax.ShapeDtypeStruct((t, KV_LORA), BF16), jax.ShapeDtypeStruct((t, LANES), F32)],
        compiler_params=_cparams(("parallel",)),
        name="mla_prep",
    )(proj, g_q.reshape(1, -1), g_kv.reshape(1, -1), cos, sin)


def _mla_q_body(cq_ref, w_ref, cos_ref, sin_ref, q_ref, wb):
    @pl.when(pl.program_id(0) == 0)
    def _():
        wb[...] = w_ref[...].astype(BF16)

    cq = cq_ref[...]
    cos = cos_ref[...]
    sin = sin_ref[...]
    for h in range(B_HEADS):
        y = _dot(cq, wb[:, h * MLA_QK_PAD:(h + 1) * MLA_QK_PAD])
        xr = y[:, LANES:]
        q_ref[:, h * MLA_QK_PAD:h * MLA_QK_PAD + LANES] = y[:, :LANES].astype(q_ref.dtype)
        q_ref[:, h * MLA_QK_PAD + LANES:(h + 1) * MLA_QK_PAD] = \
            (xr * cos + _swap_halves(xr) * sin).astype(q_ref.dtype)


def mla_q_up(cq, w_uq_pad, cos, sin, tm):
    t = cq.shape[0]
    n = B_HEADS * MLA_QK_PAD
    return pl.pallas_call(
        _mla_q_body,
        grid=(t // tm,),
        in_specs=[pl.BlockSpec((tm, Q_LORA), lambda i: (i, 0)),
                  pl.BlockSpec((Q_LORA, n), lambda i: (0, 0)),
                  pl.BlockSpec((tm, LANES), lambda i: (i, 0)),
                  pl.BlockSpec((tm, LANES), lambda i: (i, 0))],
        out_specs=pl.BlockSpec((tm, n), lambda i: (i, 0)),
        out_shape=jax.ShapeDtypeStruct((t, n), BF16),
        scratch_shapes=[pltpu.VMEM((Q_LORA, n), BF16)],
        compiler_params=_cparams(("arbitrary",), VMEM_LIMIT),
        name="mla_q_up",
    )(cq, w_uq_pad, cos, sin)


def _mla_kv_body(c_ref, kr_ref, wk_ref, wv_ref, k_ref, v_ref, wkb, wvb):
    @pl.when(pl.program_id(0) == 0)
    def _():
        wkb[...] = wk_ref[0].astype(BF16)
        wvb[...] = wv_ref[0].astype(BF16)

    c = c_ref[...].astype(BF16)
    kr = kr_ref[...].astype(k_ref.dtype)
    kn = _dot(c, wkb[...])
    for h in range(B_HEADS):
        k_ref[:, h * MLA_QK_PAD:h * MLA_QK_PAD + LANES] = kn[:, h * QK_NOPE:(h + 1) * QK_NOPE].astype(k_ref.dtype)
        k_ref[:, h * MLA_QK_PAD + LANES:(h + 1) * MLA_QK_PAD] = kr
    v_ref[...] = _dot(c, wvb[...]).astype(v_ref.dtype)


def mla_kv_up(ckv, kr, w_uk, w_uv, layer, tm):
    t = ckv.shape[0]
    nk = B_HEADS * MLA_QK_PAD
    nv = B_HEADS * V_DIM
    return pl.pallas_call(
        _mla_kv_body,
        grid=(t // tm,),
        in_specs=[pl.BlockSpec((tm, KV_LORA), lambda i: (i, 0)),
                  pl.BlockSpec((tm, LANES), lambda i: (i, 0)),
                  pl.BlockSpec((1, KV_LORA, B_HEADS * QK_NOPE), lambda i: (layer, 0, 0)),
                  pl.BlockSpec((1, KV_LORA, nv), lambda i: (layer, 0, 0))],
        out_specs=[pl.BlockSpec((tm, nk), lambda i: (i, 0)), pl.BlockSpec((tm, nv), lambda i: (i, 0))],
        out_shape=[jax.ShapeDtypeStruct((t, nk), BF16), jax.ShapeDtypeStruct((t, nv), BF16)],
        scratch_shapes=[pltpu.VMEM((KV_LORA, B_HEADS * QK_NOPE), BF16), pltpu.VMEM((KV_LORA, nv), BF16)],
        compiler_params=_cparams(("arbitrary",), VMEM_LIMIT),
        name="mla_kv_up",
    )(ckv, kr, w_uk, w_uv)


def kernel(x_prompt, x_sample, cache_a_k, cache_a_v, cache_a_kidx, cache_b_ckv, cache_b_krope, norm_mix, norm_ffn, norm_final, rel_bias, a_w_in, a_w_out, b_w_in, b_norm_q, b_norm_kv, b_w_uq, b_w_uk, b_w_uv, b_w_out, moe_w_grp, moe_b_grp, moe_w_rtr, moe_b_rtr, moe_w_gate, moe_w_up, moe_w_down):
    n_p, s_p, d = x_prompt.shape
    n_s, s_s, _ = x_sample.shape
    past = cache_a_k.shape[2]
    tp = n_p * s_p
    ts = n_s * s_s
    t = tp + ts
    a_qd = A_HEADS * A_HEAD_DIM
    tm = _tile(t, 1024)
    tm2 = _tile(t, 512)

    x = jnp.concatenate([x_prompt.reshape(tp, d), x_sample.reshape(ts, d)], axis=0)

    def router_params(i):
        w_r = jnp.concatenate([moe_w_grp[i], moe_w_rtr[i],
                               jnp.zeros((d, LANES - N_GROUPS - N_EXPERTS), F32)], axis=1)
        b_r = jnp.concatenate([moe_b_grp[i], moe_b_rtr[i],
                               jnp.zeros((LANES - N_GROUPS - N_EXPERTS,), F32)]).reshape(1, LANES)
        return w_r, b_r

    u0 = rmsnorm(x, norm_mix[0], tm)
    w_a = a_w_in[0]
    (q_b,) = matmul(u0, w_a, 0, a_qd, [BF16], tm, 512, "a_proj_q")
    k_f, k_b = matmul(u0, w_a, a_qd, a_qd, [F32, BF16], tm, 512, "a_proj_k")
    v_f, v_b = matmul(u0, w_a, 2 * a_qd, a_qd, [F32, BF16], tm, 512, "a_proj_v")
    (qi_b,) = matmul(u0, w_a, 3 * a_qd, IDX_HEADS * IDX_DIM, [BF16], tm, 512, "a_proj_qi")
    c_ki = 3 * a_qd + IDX_HEADS * IDX_DIM
    w_ki = w_a[:, c_ki:c_ki + IDX_DIM]
    w_wi = w_a[:, c_ki + IDX_DIM:c_ki + IDX_DIM + IDX_HEADS]
    zk = jnp.zeros((d, IDX_DIM), F32)
    w_tail = jnp.concatenate([w_ki, zk, zk, w_ki, w_wi, jnp.zeros((d, LANES - IDX_HEADS), F32)], axis=1)
    (tail,) = matmul(u0, w_tail, 0, 3 * LANES, [F32], tm, 3 * LANES, "a_proj_tail")
    kidx = tail[:, :IDX_DIM]

    bias_tab = rel_bias_tables(rel_bias)
    a_scale = A_HEAD_DIM ** -0.5
    hg = 4
    mask_p = dsa_select(qi_b, tail, 2, tail, n_seq=n_p, sq=s_p, bq=128, sk=s_p, sk_real=s_p, tk=512,
                        pos0=0, q_row0=0)
    att_p = attention(q_b, [(k_b, v_b, s_p, 0, 0)], bias_tab, mask_p, n_seq=n_p, sq=s_p, bq=128, q_row0=0,
                      q_col0=0, n_heads=A_HEADS, hg=8, dq=A_HEAD_DIM, dv=A_HEAD_DIM, scale=a_scale, pos0=0,
                      dyn_sk=s_p, name="dsa_attention_prompt")
    sk_s = past + s_s
    sk_pad = (sk_s + ATT_TILE - 1) // ATT_TILE * ATT_TILE
    ki_past = cache_a_kidx[0]
    zp = jnp.zeros_like(ki_past)
    kc_past = jnp.concatenate([ki_past, zp, zp, ki_past], axis=-1)
    kc_new = tail[tp:, :2 * LANES].reshape(n_s, s_s, 2 * LANES)
    kc_s = jnp.concatenate([kc_past, kc_new, jnp.zeros((n_s, sk_pad - sk_s, 2 * LANES), F32)], axis=1)
    mask_s = dsa_select(qi_b, tail, 2, kc_s.reshape(n_s * sk_pad, 2 * LANES), n_seq=n_s, sq=s_s, bq=s_s,
                        sk=sk_pad, sk_real=sk_s, tk=ATT_TILE, pos0=past, q_row0=tp)
    tiles_s = [(0, r, r, ATT_TILE) for r in range(0, past, ATT_TILE)] + [(1, 0, past, s_s)]
    k_new = k_b[tp:].reshape(n_s * s_s, a_qd)
    v_new = v_b[tp:].reshape(n_s * s_s, a_qd)
    att_s = attention(q_b, [(cache_a_k[0].reshape(n_s * past, a_qd), cache_a_v[0].reshape(n_s * past, a_qd),
                             past, 0, 0), (k_new, v_new, s_s, 0, 0)],
                      bias_tab, mask_s, n_seq=n_s, sq=s_s, bq=s_s, q_row0=tp, q_col0=0, n_heads=A_HEADS, hg=hg,
                      dq=A_HEAD_DIM, dv=A_HEAD_DIM, scale=a_scale, pos0=past, seg_tiles=tiles_s,
                      name="dsa_attention_sample")
    att0 = jnp.concatenate([att_p, att_s], axis=0)

    w_r0, b_r0 = router_params(0)
    h1, xn0, lg0 = outproj_norm_router(att0, a_w_out, 0, x, norm_ffn[0], w_r0, b_r0, tm2, 512)
    h2, u1 = hier_moe_layer(h1, xn0, lg0, moe_w_gate, moe_w_up, moe_w_down, 0, norm_mix[1], BF16)

    n_in = Q_LORA + KV_LORA + QK_ROPE
    w_b = jnp.concatenate([b_w_in[0], jnp.zeros((d, LANES - QK_ROPE), F32)], axis=1)
    (proj,) = matmul(u1, w_b, 0, n_in + LANES - QK_ROPE, [F32], tm, 384, "b_proj")
    half = QK_ROPE // 2
    inv = ROPE_THETA ** (-jnp.arange(half, dtype=F32) / half)
    pos_all = jnp.concatenate([jnp.tile(jnp.arange(s_p, dtype=I32), n_p),
                               jnp.tile(past + jnp.arange(s_s, dtype=I32), n_s)])
    ang = pos_all.astype(F32)[:, None] * inv[None, :]
    zl = jnp.zeros((t, LANES - QK_ROPE), F32)
    cos_t = jnp.concatenate([jnp.cos(ang), jnp.cos(ang), zl], axis=1)
    sin_t = jnp.concatenate([-jnp.sin(ang), jnp.sin(ang), zl], axis=1)
    cq_b, ckv_f, ckv_b, kr_f = mla_prep(proj, b_norm_q[0], b_norm_kv[0], cos_t, sin_t, tm)
    w_uq_pad = jnp.pad(b_w_uq[0].reshape(Q_LORA, B_HEADS, QK_NOPE + QK_ROPE),
                       ((0, 0), (0, 0), (0, MLA_QK_PAD - QK_NOPE - QK_ROPE))).reshape(Q_LORA, B_HEADS * MLA_QK_PAD)
    qm = mla_q_up(cq_b, w_uq_pad, cos_t, sin_t, tm2)
    km_p, vm_p = mla_kv_up(ckv_b[:tp], kr_f[:tp], b_w_uk, b_w_uv, 0, _tile(tp, 512))
    ctab = causal_tables()
    hgb = 4
    matt_p = attention(qm, [(km_p, vm_p, s_p, 0, 0)], ctab, None, n_seq=n_p, sq=s_p, bq=128, q_row0=0, q_col0=0,
                       n_heads=B_HEADS, hg=8, dq=MLA_QK_PAD, dv=V_DIM, scale=B_SCALE, pos0=0, dyn_sk=s_p,
                       name="mla_attention_prompt")
    ckv_all = jnp.concatenate([cache_b_ckv[0], ckv_f[tp:].reshape(n_s, s_s, KV_LORA)], axis=1)
    kr_past = jnp.concatenate([cache_b_krope[0], jnp.zeros((n_s, past, LANES - QK_ROPE), F32)], axis=-1)
    kr_all = jnp.concatenate([kr_past, kr_f[tp:].reshape(n_s, s_s, LANES)], axis=1)
    km_s, vm_s = mla_kv_up(ckv_all.reshape(n_s * sk_s, KV_LORA), kr_all.reshape(n_s * sk_s, LANES),
                           b_w_uk, b_w_uv, 0, sk_s)
    tiles_m = [(0, r, r, ATT_TILE) for r in range(0, past, ATT_TILE)] + [(0, past, past, s_s)]
    matt_s = attention(qm, [(km_s, vm_s, sk_s, 0, 0)], ctab, None, n_seq=n_s, sq=s_s, bq=s_s, q_row0=tp, q_col0=0,
                       n_heads=B_HEADS, hg=hgb, dq=MLA_QK_PAD, dv=V_DIM, scale=B_SCALE, pos0=past,
                       seg_tiles=tiles_m, name="mla_attention_sample")
    att1 = jnp.concatenate([matt_p, matt_s], axis=0)

    w_r1, b_r1 = router_params(1)
    h3, xn1, lg1 = outproj_norm_router(att1, b_w_out, 0, h2, norm_ffn[1], w_r1, b_r1, tm2, 512)
    _, y = hier_moe_layer(h3, xn1, lg1, moe_w_gate, moe_w_up, moe_w_down, 1, norm_final, F32)

    y_prompt = y[:tp].reshape(n_p, s_p, d)
    y_sample = y[tp:].reshape(n_s, s_s, d)

    def heads(a, n, s):
        return a.reshape(1, n, s, A_HEADS, A_HEAD_DIM)

    kr_out = kr_f[:, :QK_ROPE]
    return (y_prompt, y_sample,
            heads(k_f[:tp], n_p, s_p), heads(v_f[:tp], n_p, s_p), kidx[:tp].reshape(1, n_p, s_p, IDX_DIM),
            ckv_f[:tp].reshape(1, n_p, s_p, KV_LORA), kr_out[:tp].reshape(1, n_p, s_p, QK_ROPE),
            heads(k_f[tp:], n_s, s_s), heads(v_f[tp:], n_s, s_s), kidx[tp:].reshape(1, n_s, s_s, IDX_DIM),
            ckv_f[tp:].reshape(1, n_s, s_s, KV_LORA), kr_out[tp:].reshape(1, n_s, s_s, QK_ROPE))
```

```python
import functools
import math

import jax
import jax.numpy as jnp
from jax import lax
from jax.experimental import pallas as pl
from jax.experimental.pallas import tpu as pltpu

F32 = jnp.float32
BF16 = jnp.bfloat16
I32 = jnp.int32

RMS_EPS = 1e-6
CHUNK = 64
NEG = -1e30
INT_MIN = -2 ** 31
INT_MAX = 2 ** 31 - 1

LANES = 128
KEY_BLOCK = 128
ATT_TILE = 256
VMEM_LIMIT = 56 * 1024 * 1024

A_HEADS = 16
A_HEAD_DIM = 128
IDX_HEADS = 16
IDX_DIM = 64
TOPK_MAX = 256
IDX_W_SCALE = float((IDX_HEADS * IDX_DIM) ** -0.5)
NUM_BUCKETS = 32
MAX_DISTANCE = 128
B_HEADS = 16
Q_LORA = 512
KV_LORA = 512
QK_NOPE = 128
QK_ROPE = 64
V_DIM = 128
ROPE_THETA = 10000.0
B_SCALE = float((QK_NOPE + QK_ROPE) ** -0.5)
MLA_QK_PAD = 256
N_GROUPS = 8
EXPERTS_PER_GROUP = 8
N_EXPERTS = 64
D_EXPERT = 512
MOE_BM = 128


def _tile(n, pref):
    for c in range(pref, 0, -LANES):
        if n % c == 0:
            return c
    raise ValueError(f"no 128-multiple tile divides {n}")


def _cparams(sem, vmem=None):
    return pltpu.CompilerParams(dimension_semantics=sem, vmem_limit_bytes=vmem)


def _dot(a, b):
    return jnp.dot(a, b, preferred_element_type=F32)


def _dot_nt(a, b):
    return lax.dot_general(a, b, (((1,), (1,)), ((), ())), preferred_element_type=F32)


def _rms(x, g):
    ms = jnp.mean(x * x, axis=-1, keepdims=True)
    return x * lax.rsqrt(ms + RMS_EPS) * g


def _rmsnorm_body(x_ref, g_ref, o_ref):
    o_ref[...] = _rms(x_ref[...], g_ref[...]).astype(o_ref.dtype)


def rmsnorm(x, g, tm, out_dtype=BF16):
    t, d = x.shape
    return pl.pallas_call(
        _rmsnorm_body,
        grid=(t // tm,),
        in_specs=[pl.BlockSpec((tm, d), lambda i: (i, 0)), pl.BlockSpec((1, d), lambda i: (0, 0))],
        out_specs=pl.BlockSpec((tm, d), lambda i: (i, 0)),
        out_shape=jax.ShapeDtypeStruct((t, d), out_dtype),
        compiler_params=_cparams(("parallel",)),
        name="rmsnorm",
    )(x, g.reshape(1, d))


def _matmul_body(a_ref, w_ref, *rest):
    outs, wb = rest[:-1], rest[-1]

    @pl.when(pl.program_id(1) == 0)
    def _():
        wb[...] = w_ref[...].astype(BF16)

    r = _dot(a_ref[...], wb[...])
    for o in outs:
        o[...] = r.astype(o.dtype)


def matmul(a, w, col0, ncols, out_dtypes, tm, tn, name, row0=0, nrows=None):
    k = a.shape[1]
    t = a.shape[0] if nrows is None else nrows
    assert ncols % tn == 0 and col0 % tn == 0 and t % tm == 0 and row0 % tm == 0
    cb = col0 // tn
    rb = row0 // tm
    return pl.pallas_call(
        _matmul_body,
        grid=(ncols // tn, t // tm),
        in_specs=[pl.BlockSpec((tm, k), lambda j, i: (i + rb, 0)),
                  pl.BlockSpec((k, tn), lambda j, i: (0, j + cb))],
        out_specs=[pl.BlockSpec((tm, tn), lambda j, i: (i, j)) for _ in out_dtypes],
        out_shape=[jax.ShapeDtypeStruct((t, ncols), dt) for dt in out_dtypes],
        scratch_shapes=[pltpu.VMEM((k, tn), BF16)],
        compiler_params=_cparams(("arbitrary", "arbitrary"), VMEM_LIMIT),
        name=name,
    )(a, w)


def _float_sort_key(x):
    bits = lax.bitcast_convert_type(x, I32)
    return bits ^ ((bits >> 31) & INT_MAX)


def _select_body(qi_ref, wi_ref, kc_ref, mask_ref, key_scr, jm_scr, *, bq, sk, sk_real, tk, pos0, topk):
    b = pl.program_id(1)
    t0 = pos0 + b * bq
    kmax = jnp.minimum(sk_real, ((t0 + bq - 1) // CHUNK + 1) * CHUNK)
    nkt = (kmax + tk - 1) // tk
    nch = tk // LANES

    w = wi_ref[...] * IDX_W_SCALE
    wb = [jnp.broadcast_to(w[:, h:h + 1], (bq, LANES)) for h in range(IDX_HEADS)]
    row_pos = t0 + lax.broadcasted_iota(I32, (bq, LANES), 0)
    lim = jnp.minimum((row_pos // CHUNK + 1) * CHUNK, sk_real)
    lane = lax.broadcasted_iota(I32, (bq, LANES), 1)

    def score_tile(j, carry):
        off = pl.multiple_of(j * tk, tk)
        kc = kc_ref[pl.ds(off, tk), :].astype(BF16)
        ka, kb = kc[:, :LANES], kc[:, LANES:]
        accs = [jnp.zeros((bq, LANES), F32) for _ in range(nch)]
        for g in range(IDX_HEADS // 2):
            qg = qi_ref[:, g * LANES:(g + 1) * LANES]
            sa = _dot_nt(qg, ka)
            sb = _dot_nt(qg, kb)
            for c in range(nch):
                sl = slice(c * LANES, (c + 1) * LANES)
                accs[c] = accs[c] + wb[2 * g] * jnp.maximum(sa[:, sl], 0.0) \
                    + wb[2 * g + 1] * jnp.maximum(sb[:, sl], 0.0)
        for c in range(nch):
            kpos = off + c * LANES + lane
            sc = jnp.where(kpos < lim, accs[c], -jnp.inf)
            key_scr[:, pl.ds(off + c * LANES, LANES)] = _float_sort_key(sc)
        return carry

    lax.fori_loop(0, nkt, score_tile, 0)

    def count(indicator):
        def tile(j, cnt):
            off = pl.multiple_of(j * tk, tk)
            for c in range(nch):
                kt = key_scr[:, pl.ds(off + c * LANES, LANES)]
                cnt = cnt + indicator(kt, off + c * LANES + lane)
            return cnt
        cnt = lax.fori_loop(0, nkt, tile, jnp.zeros((bq, LANES), F32))
        return jnp.sum(cnt, axis=1, keepdims=True)

    def bit_step(i, pfx_u):
        bit = lax.shift_left(jnp.int32(1), 31 - i)
        cand_u = pfx_u | bit
        cand_s = cand_u ^ INT_MIN
        total = count(lambda kt, kp: jnp.where(kt >= cand_s, 1.0, 0.0))
        return jnp.where(total >= topk, cand_u, pfx_u)

    pfx = lax.fori_loop(0, 32, bit_step, jnp.zeros((bq, LANES), I32))
    thr = pfx ^ INT_MIN

    n_gt = count(lambda kt, kp: jnp.where(kt > thr, 1.0, 0.0))
    n_ge = count(lambda kt, kp: jnp.where(kt >= thr, 1.0, 0.0))
    quota = topk - n_gt
    jm_scr[...] = jnp.full((bq, LANES), INT_MAX, I32)
    any_excess = jnp.max(jnp.where(n_ge > topk, 1.0, 0.0)) > 0.0

    @pl.when(any_excess)
    def _():
        nbits = max(1, int(sk - 1).bit_length())

        def idx_step(i, ans):
            cand = ans | lax.shift_left(jnp.int32(1), nbits - 1 - i)
            below = count(lambda kt, kp: jnp.where(kt == thr, jnp.where(kp < cand, 1.0, 0.0), 0.0))
            return jnp.where(below < quota, cand, ans)

        jm_scr[...] = lax.fori_loop(0, nbits, idx_step, jnp.zeros((bq, LANES), I32))

    jm = jm_scr[...]

    def write_tile(j, carry):
        off = pl.multiple_of(j * tk, tk)
        for c in range(nch):
            kt = key_scr[:, pl.ds(off + c * LANES, LANES)]
            kpos = off + c * LANES + lane
            v = jnp.where(kt > thr, 0.0, jnp.where(kt == thr, jnp.where(kpos <= jm, 0.0, NEG), NEG))
            v = jnp.where(kpos < lim, v, NEG)
            mask_ref[:, pl.ds(off + c * LANES, LANES)] = v.astype(mask_ref.dtype)
        return carry

    lax.fori_loop(0, nkt, write_tile, 0)

    def fill_tile(j, carry):
        off = pl.multiple_of(j * tk, tk)
        mask_ref[:, pl.ds(off, tk)] = jnp.full((bq, tk), NEG, mask_ref.dtype)
        return carry

    lax.fori_loop(nkt, sk // tk, fill_tile, 0)


def dsa_select(qi, wi_arr, wi_blk, kc, *, n_seq, sq, bq, sk, sk_real, tk, pos0, q_row0):
    nqb = sq // bq
    qb0 = q_row0 // bq
    assert q_row0 % bq == 0 and sk % tk == 0 and tk >= TOPK_MAX
    topk = min(TOPK_MAX, sk_real // 4)
    body = functools.partial(_select_body, bq=bq, sk=sk, sk_real=sk_real, tk=tk, pos0=pos0, topk=topk)
    return pl.pallas_call(
        body,
        grid=(n_seq, nqb),
        in_specs=[pl.BlockSpec((bq, IDX_HEADS * IDX_DIM), lambda s, b: (qb0 + s * nqb + b, 0)),
                  pl.BlockSpec((bq, LANES), lambda s, b: (qb0 + s * nqb + b, wi_blk)),
                  pl.BlockSpec((sk, 2 * LANES), lambda s, b: (s, 0))],
        out_specs=pl.BlockSpec((bq, sk), lambda s, b: (s * nqb + b, 0)),
        out_shape=jax.ShapeDtypeStruct((n_seq * sq, sk), BF16),
        scratch_shapes=[pltpu.VMEM((bq, sk), I32), pltpu.VMEM((bq, LANES), I32)],
        compiler_params=_cparams(("parallel", "arbitrary"), VMEM_LIMIT),
        name="dsa_select",
    )(qi, wi_arr, kc)


def _rel_tables_body(bias_ref, tab_ref):
    i = lax.broadcasted_iota(I32, (KEY_BLOCK, KEY_BLOCK), 0)
    j = lax.broadcasted_iota(I32, (KEY_BLOCK, KEY_BLOCK), 1)
    nb = NUM_BUCKETS // 2
    max_exact = nb // 2
    edges = [12, 16, 23, 32, 46, 64, 91]
    for d in range(4):
        rel = j - i + (d - 2) * KEY_BLOCK
        n = jnp.abs(rel)
        large = jnp.full_like(n, max_exact)
        for e in edges:
            large = large + jnp.where(n >= e, 1, 0)
        bucket = jnp.where(rel > 0, nb, 0) + jnp.where(n < max_exact, n, large)
        for h in range(A_HEADS):
            acc = jnp.zeros((KEY_BLOCK, KEY_BLOCK), F32)
            for k in range(NUM_BUCKETS):
                acc = jnp.where(bucket == k, bias_ref[k, h], acc)
            tab_ref[h, d] = acc


def rel_bias_tables(rel_bias):
    return pl.pallas_call(
        _rel_tables_body,
        in_specs=[pl.BlockSpec(memory_space=pltpu.SMEM)],
        out_shape=jax.ShapeDtypeStruct((A_HEADS, 4, KEY_BLOCK, KEY_BLOCK), F32),
        name="rel_bias_tables",
    )(rel_bias)


def _causal_tables_body(tab_ref):
    i = lax.broadcasted_iota(I32, (KEY_BLOCK, KEY_BLOCK), 0)
    j = lax.broadcasted_iota(I32, (KEY_BLOCK, KEY_BLOCK), 1)
    zero = jnp.zeros((KEY_BLOCK, KEY_BLOCK), F32)
    tab_ref[0, 0] = zero
    tab_ref[0, 1] = zero
    tab_ref[0, 2] = jnp.where(j // CHUNK <= i // CHUNK, 0.0, NEG)
    tab_ref[0, 3] = jnp.full((KEY_BLOCK, KEY_BLOCK), NEG, F32)


def causal_tables():
    return pl.pallas_call(
        _causal_tables_body,
        out_shape=jax.ShapeDtypeStruct((1, 4, KEY_BLOCK, KEY_BLOCK), F32),
        name="causal_tables",
    )()


def _attn_pipe_body(*refs, has_mask, per_head_tab, bq, hg, dq, dv, scale, pos0, sk):
    q_ref, k_ref, v_ref, tab_ref = refs[:4]
    mask_ref = refs[4] if has_mask else None
    out_ref = refs[5] if has_mask else refs[4]
    s_scr, p_scr, m_scr, l_scr, acc_scr = refs[-5:]

    b = pl.program_id(2)
    t0 = pos0 + b * bq
    qblk = t0 // KEY_BLOCK
    kmax = jnp.minimum(sk, ((t0 + bq - 1) // CHUNK + 1) * CHUNK)
    nkt = (kmax + ATT_TILE - 1) // ATT_TILE
    nch = ATT_TILE // LANES

    m_scr[...] = jnp.full(m_scr.shape, NEG, F32)
    l_scr[...] = jnp.zeros(l_scr.shape, F32)
    acc_scr[...] = jnp.zeros(acc_scr.shape, F32)

    def logits_tile(j, carry):
        off = pl.multiple_of(j * ATT_TILE, ATT_TILE)
        kb0 = off // KEY_BLOCK
        ds = [jnp.clip(kb0 + c - qblk + 2, 0, 3) for c in range(nch)]
        if has_mask:
            mk = mask_ref[:, pl.ds(off, ATT_TILE)].astype(F32)
        for h in range(hg):
            s = _dot_nt(q_ref[:, h * dq:(h + 1) * dq], k_ref[pl.ds(off, ATT_TILE), h * dq:(h + 1) * dq]) * scale
            hh = h if per_head_tab else 0
            s = s + jnp.concatenate([tab_ref[hh, ds[c]] for c in range(nch)], axis=1)
            if has_mask:
                s = s + mk
            s_scr[h, :, pl.ds(off, ATT_TILE)] = s
            mvec = m_scr[h]
            for c in range(nch):
                mvec = jnp.maximum(mvec, s[:, c * LANES:(c + 1) * LANES])
            m_scr[h] = mvec
        return carry

    lax.fori_loop(0, nkt, logits_tile, 0)
    for h in range(hg):
        m_scr[h] = jnp.broadcast_to(jnp.max(m_scr[h], axis=1, keepdims=True), (bq, LANES))

    def exp_tile(j, carry):
        off = pl.multiple_of(j * ATT_TILE, ATT_TILE)
        for h in range(hg):
            s = s_scr[h, :, pl.ds(off, ATT_TILE)]
            m = m_scr[h]
            p = [jnp.exp(s[:, c * LANES:(c + 1) * LANES] - m) for c in range(nch)]
            lvec = l_scr[h]
            for c in range(nch):
                lvec = lvec + p[c]
            l_scr[h] = lvec
            p_scr[h, :, pl.ds(off, ATT_TILE)] = jnp.concatenate(p, axis=1).astype(BF16)
        return carry

    lax.fori_loop(0, nkt, exp_tile, 0)

    def pv_tile(j, carry):
        off = pl.multiple_of(j * ATT_TILE, ATT_TILE)
        for h in range(hg):
            acc_scr[h] = acc_scr[h] + _dot(p_scr[h, :, pl.ds(off, ATT_TILE)],
                                           v_ref[pl.ds(off, ATT_TILE), h * dv:(h + 1) * dv])
        return carry

    lax.fori_loop(0, nkt, pv_tile, 0)
    for h in range(hg):
        l_row = jnp.sum(l_scr[h], axis=1, keepdims=True)
        out_ref[:, h * dv:(h + 1) * dv] = (acc_scr[h] / l_row).astype(out_ref.dtype)


def _attn_body(*refs, n_seg, has_mask, per_head_tab, bq, hg, dq, dv, scale, pos0, seg_tiles):
    q_ref = refs[0]
    kv = refs[1:1 + 2 * n_seg]
    pos = 1 + 2 * n_seg
    tab_ref = refs[pos]
    pos += 1
    mask_ref = refs[pos] if has_mask else None
    pos += 1 if has_mask else 0
    out_ref = refs[pos]
    s_scr, p_scr = refs[pos + 1:pos + 3]

    b = pl.program_id(2)
    t0 = pos0 + b * bq
    qblk = t0 // KEY_BLOCK

    def lane_pad(x, fill):
        w = x.shape[1]
        return x if w == LANES else jnp.concatenate([x, jnp.full((bq, LANES - w), fill, F32)], axis=1)

    mvec = [jnp.full((bq, LANES), NEG, F32) for _ in range(hg)]
    for (si, row0, key0, width) in seg_tiles:
        nch = max(1, width // KEY_BLOCK)
        cw = min(width, KEY_BLOCK)
        ds = [jnp.clip(key0 // KEY_BLOCK + c - qblk + 2, 0, 3) for c in range(nch)]
        k_t = kv[2 * si][row0:row0 + width, :].astype(BF16)
        if has_mask:
            mk = mask_ref[:, key0:key0 + width].astype(F32)
        for h in range(hg):
            hh = h if per_head_tab else 0
            s = _dot_nt(q_ref[:, h * dq:(h + 1) * dq], k_t[:, h * dq:(h + 1) * dq]) * scale
            bias = [tab_ref[hh, ds[c], :bq, :cw] for c in range(nch)]
            s = s + (jnp.concatenate(bias, axis=1) if nch > 1 else bias[0])
            if has_mask:
                s = s + mk
            s_scr[h, :, key0:key0 + width] = s
            for c in range(nch):
                mvec[h] = jnp.maximum(mvec[h], lane_pad(s[:, c * cw:(c + 1) * cw], NEG))

    m = [jnp.broadcast_to(jnp.max(mvec[h], axis=1, keepdims=True), (bq, LANES)) for h in range(hg)]
    lvec = [jnp.zeros((bq, LANES), F32) for _ in range(hg)]
    for (si, row0, key0, width) in seg_tiles:
        nch = max(1, width // KEY_BLOCK)
        cw = min(width, KEY_BLOCK)
        for h in range(hg):
            s = s_scr[h, :, key0:key0 + width]
            p = [jnp.exp(s[:, c * cw:(c + 1) * cw] - m[h][:, :cw]) for c in range(nch)]
            for c in range(nch):
                lvec[h] = lvec[h] + lane_pad(p[c], 0.0)
            p_scr[h, :, key0:key0 + width] = (jnp.concatenate(p, axis=1) if nch > 1 else p[0]).astype(BF16)

    acc = [jnp.zeros((bq, dv), F32) for _ in range(hg)]
    for (si, row0, key0, width) in seg_tiles:
        v_t = kv[2 * si + 1][row0:row0 + width, :].astype(BF16)
        for h in range(hg):
            acc[h] = acc[h] + _dot(p_scr[h, :, key0:key0 + width], v_t[:, h * dv:(h + 1) * dv])

    for h in range(hg):
        l_row = jnp.sum(lvec[h], axis=1, keepdims=True)
        out_ref[:, h * dv:(h + 1) * dv] = (acc[h] / l_row).astype(out_ref.dtype)


def attention(q, segs, tab, mask, *, n_seq, sq, bq, q_row0, q_col0, n_heads, hg, dq, dv, scale, pos0,
              seg_tiles=None, dyn_sk=None, name="attention"):
    nqb = sq // bq
    ng = n_heads // hg
    qb0 = q_row0 // bq
    assert q_row0 % bq == 0 and q_col0 % (hg * dq) == 0 and pos0 % KEY_BLOCK == 0
    assert bq == KEY_BLOCK or nqb == 1
    qc0 = q_col0 // (hg * dq)
    in_specs = [pl.BlockSpec((bq, hg * dq), lambda s, g, b: (qb0 + s * nqb + b, qc0 + g))]
    args = [q]
    for (k, v, rows, kc0, vc0) in segs:
        assert kc0 % (hg * dq) == 0 and vc0 % (hg * dv) == 0
        mode = dict(pipeline_mode=pl.Buffered(1)) if dyn_sk is not None else {}
        in_specs.append(pl.BlockSpec((rows, hg * dq), lambda s, g, b, c=kc0 // (hg * dq): (s, c + g), **mode))
        in_specs.append(pl.BlockSpec((rows, hg * dv), lambda s, g, b, c=vc0 // (hg * dv): (s, c + g), **mode))
        args += [k, v]
    per_head_tab = tab.shape[0] > 1
    if per_head_tab:
        in_specs.append(pl.BlockSpec((hg, 4, KEY_BLOCK, KEY_BLOCK), lambda s, g, b: (g, 0, 0, 0)))
    else:
        in_specs.append(pl.BlockSpec((1, 4, KEY_BLOCK, KEY_BLOCK), lambda s, g, b: (0, 0, 0, 0)))
    args.append(tab)
    if mask is not None:
        in_specs.append(pl.BlockSpec((bq, mask.shape[1]), lambda s, g, b: (s * nqb + b, 0)))
        args.append(mask)
    if dyn_sk is not None:
        assert len(segs) == 1 and dyn_sk % ATT_TILE == 0 and bq == KEY_BLOCK
        body = functools.partial(_attn_pipe_body, has_mask=mask is not None, per_head_tab=per_head_tab,
                                 bq=bq, hg=hg, dq=dq, dv=dv, scale=scale, pos0=pos0, sk=dyn_sk)
        scratch = [pltpu.VMEM((hg, bq, dyn_sk), F32), pltpu.VMEM((hg, bq, dyn_sk), BF16),
                   pltpu.VMEM((hg, bq, LANES), F32), pltpu.VMEM((hg, bq, LANES), F32),
                   pltpu.VMEM((hg, bq, dv), F32)]
    else:
        body = functools.partial(_attn_body, n_seg=len(segs), has_mask=mask is not None,
                                 per_head_tab=per_head_tab, bq=bq, hg=hg, dq=dq, dv=dv, scale=scale, pos0=pos0,
                                 seg_tiles=seg_tiles)
        sk_tot = max(key0 + width for (_, _, key0, width) in seg_tiles)
        sk_tot = (sk_tot + LANES - 1) // LANES * LANES
        scratch = [pltpu.VMEM((hg, bq, sk_tot), F32), pltpu.VMEM((hg, bq, sk_tot), BF16)]
    return pl.pallas_call(
        body,
        grid=(n_seq, ng, nqb),
        in_specs=in_specs,
        out_specs=pl.BlockSpec((bq, hg * dv), lambda s, g, b: (s * nqb + b, g)),
        out_shape=jax.ShapeDtypeStruct((n_seq * sq, n_heads * dv), BF16),
        scratch_shapes=scratch,
        compiler_params=_cparams(("parallel", "parallel", "arbitrary"), VMEM_LIMIT),
        name=name,
    )(*args)


def _outproj_body(a_ref, w_ref, x_ref, g_ref, wr_ref, br_ref, h_ref, xn_ref, lg_ref, hrow, *, tn):
    j = pl.program_id(1)
    r = x_ref[...] + _dot(a_ref[...], w_ref[0].astype(BF16))
    h_ref[...] = r
    hrow[:, pl.ds(pl.multiple_of(j * tn, tn), tn)] = r

    @pl.when(j == pl.num_programs(1) - 1)
    def _():
        xn = _rms(hrow[...], g_ref[...])
        xn_ref[...] = xn
        lg_ref[...] = _dot(xn.astype(BF16), wr_ref[...].astype(BF16)) + br_ref[...]


def outproj_norm_router(a, w_out, layer, x_res, g_ffn, w_r, b_r, tm, tn):
    t, k = a.shape
    d = w_out.shape[-1]
    return pl.pallas_call(
        functools.partial(_outproj_body, tn=tn),
        grid=(t // tm, d // tn),
        in_specs=[pl.BlockSpec((tm, k), lambda i, j: (i, 0)),
                  pl.BlockSpec((1, k, tn), lambda i, j: (layer, 0, j)),
                  pl.BlockSpec((tm, tn), lambda i, j: (i, j)),
                  pl.BlockSpec((1, d), lambda i, j: (0, 0)),
                  pl.BlockSpec((d, LANES), lambda i, j: (0, 0)),
                  pl.BlockSpec((1, LANES), lambda i, j: (0, 0))],
        out_specs=[pl.BlockSpec((tm, tn), lambda i, j: (i, j)),
                   pl.BlockSpec((tm, d), lambda i, j: (i, 0)),
                   pl.BlockSpec((tm, LANES), lambda i, j: (i, 0))],
        out_shape=[jax.ShapeDtypeStruct((t, d), F32), jax.ShapeDtypeStruct((t, d), F32),
                   jax.ShapeDtypeStruct((t, LANES), F32)],
        scratch_shapes=[pltpu.VMEM((tm, d), F32)],
        compiler_params=_cparams(("parallel", "arbitrary"), VMEM_LIMIT),
        name="outproj_norm_router",
    )(a, w_out, x_res, g_ffn.reshape(1, d), w_r, b_r)


def _routing_body(lg_ref, out_ref, cnt_ref, carry, *, tm):
    i = pl.program_id(0)

    @pl.when(i == 0)
    def _():
        carry[...] = jnp.zeros(carry.shape, F32)

    x = lg_ref[...]
    lane = lax.broadcasted_iota(I32, (tm, LANES), 1)
    neg_inf = -jnp.inf

    def rmax(v):
        return jnp.max(v, axis=1, keepdims=True)

    def rmin(v):
        return jnp.min(v, axis=1, keepdims=True)

    def rsum(v):
        return jnp.sum(v, axis=1, keepdims=True)

    gm = lane < N_GROUPS
    gmax = rmax(jnp.where(gm, x, neg_inf))
    gsel = rmin(jnp.where(gm, jnp.where(x == gmax, lane, LANES), LANES))
    gsum = rsum(jnp.where(gm, jnp.exp(x - gmax), 0.0))
    g_w = 1.0 / gsum
    lo = N_GROUPS + gsel * EXPERTS_PER_GROUP
    em = jnp.logical_and(lane >= lo, lane < lo + EXPERTS_PER_GROUP)
    emax = rmax(jnp.where(em, x, neg_inf))
    ee = jnp.where(em, jnp.exp(x - emax), 0.0)
    p = ee / rsum(ee)
    p1 = rmax(jnp.where(em, p, -1.0))
    i1 = rmin(jnp.where(em, jnp.where(p == p1, lane, LANES), LANES))
    em2 = jnp.logical_and(em, lane != i1)
    p2 = rmax(jnp.where(em2, p, -1.0))
    i2 = rmin(jnp.where(em2, jnp.where(p == p2, lane, LANES), LANES))
    den = p1 + p2
    g1 = g_w * (p1 / den)
    g2 = g_w * (p2 / den)

    oh1 = jnp.where(lane == i1, 1.0, 0.0)
    oh2 = jnp.where(lane == i2, 1.0, 0.0)
    oh = oh1 + oh2
    r = lax.broadcasted_iota(I32, (tm, tm), 0)
    c = lax.broadcasted_iota(I32, (tm, tm), 1)
    lower = jnp.where(c < r, 1.0, 0.0).astype(BF16)
    before = _dot(lower, oh.astype(BF16)) + carry[...]
    rank1 = rsum(oh1 * before)
    rank2 = rsum(oh2 * before)
    carry[...] = carry[...] + jnp.sum(oh, axis=0, keepdims=True)

    e1 = (i1 - N_GROUPS).astype(F32)
    e2 = (i2 - N_GROUPS).astype(F32)
    vals = [e1, e2, rank1, rank2, g1, g2]
    out = jnp.zeros((tm, LANES), F32)
    for k, v in enumerate(vals):
        out = jnp.where(lane == k, v, out)
    out_ref[...] = out
    cnt_ref[...] = carry[...]


def moe_routing(logits, tm):
    t = logits.shape[0]
    return pl.pallas_call(
        functools.partial(_routing_body, tm=tm),
        grid=(t // tm,),
        in_specs=[pl.BlockSpec((tm, LANES), lambda i: (i, 0))],
        out_specs=[pl.BlockSpec((tm, LANES), lambda i: (i, 0)), pl.BlockSpec((1, LANES), lambda i: (0, 0))],
        out_shape=[jax.ShapeDtypeStruct((t, LANES), F32), jax.ShapeDtypeStruct((1, LANES), F32)],
        scratch_shapes=[pltpu.VMEM((1, LANES), F32)],
        compiler_params=_cparams(("arbitrary",)),
        name="moe_routing",
    )(logits)


def _experts_body(be_ref, nv_ref, tok_ref, x_hbm, wg_ref, wu_ref, wd_ref, ys_ref,
                  buf, sem, wgb, wub, wdb, *, bm):
    b = pl.program_id(0)
    nb = pl.num_programs(0)

    def row_copy(tok, r, s):
        return pltpu.make_async_copy(x_hbm.at[pl.ds(tok, 1), :], buf.at[s, pl.ds(r, 1), :], sem.at[s])

    def issue(blk, s):
        @pl.when(nv_ref[blk] > 0)
        def _():
            def one(r, carry):
                row_copy(tok_ref[blk * bm + r], r, s).start()
                return carry

            lax.fori_loop(0, bm, one, 0, unroll=8)

    @pl.when(b == 0)
    def _():
        issue(0, 0)

    @pl.when(b + 1 < nb)
    def _():
        issue(b + 1, (b + 1) % 2)

    slot = b % 2
    n = nv_ref[b]

    @pl.when(n > 0)
    def _():
        pltpu.make_async_copy(x_hbm.at[pl.ds(0, bm), :], buf.at[slot], sem.at[slot]).wait()
        prev = be_ref[jnp.maximum(b - 1, 0)]

        @pl.when(jnp.logical_or(b == 0, be_ref[b] != prev))
        def _():
            wgb[...] = wg_ref[0, 0].astype(BF16)
            wub[...] = wu_ref[0, 0].astype(BF16)
            wdb[...] = wd_ref[0, 0].astype(BF16)

        x = buf[slot].astype(BF16)
        g = _dot(x, wgb[...])
        u = _dot(x, wub[...])
        a = (g * (1.0 / (1.0 + jnp.exp(-g)))) * u
        ys_ref[...] = _dot(a.astype(BF16), wdb[...])

    @pl.when(n == 0)
    def _():
        ys_ref[...] = jnp.zeros(ys_ref.shape, F32)


def moe_experts(xn, block_expert, n_valid, slot_tok, w_gate, w_up, w_down, layer, bm):
    t, d = xn.shape
    nb = block_expert.shape[0]
    de = w_gate.shape[-1]
    grid_spec = pltpu.PrefetchScalarGridSpec(
        num_scalar_prefetch=3,
        grid=(nb,),
        in_specs=[pl.BlockSpec(memory_space=pl.ANY),
                  pl.BlockSpec((1, 1, d, de), lambda b, be, nv, tk: (layer, be[b], 0, 0)),
                  pl.BlockSpec((1, 1, d, de), lambda b, be, nv, tk: (layer, be[b], 0, 0)),
                  pl.BlockSpec((1, 1, de, d), lambda b, be, nv, tk: (layer, be[b], 0, 0))],
        out_specs=pl.BlockSpec((bm, d), lambda b, be, nv, tk: (b, 0)),
        scratch_shapes=[pltpu.VMEM((2, bm, d), F32), pltpu.SemaphoreType.DMA((2,)),
                        pltpu.VMEM((d, de), BF16), pltpu.VMEM((d, de), BF16), pltpu.VMEM((de, d), BF16)],
    )
    return pl.pallas_call(
        functools.partial(_experts_body, bm=bm),
        grid_spec=grid_spec,
        out_shape=jax.ShapeDtypeStruct((nb * bm, d), F32),
        compiler_params=_cparams(("arbitrary",), VMEM_LIMIT),
        name="moe_experts",
    )(block_expert, n_valid, slot_tok, xn, w_gate, w_up, w_down)


def _combine_body(slot_ref, ys_hbm, h_ref, rt_ref, g_ref, *rest, tm, tile0, want_h):
    outs, (buf, sem) = rest[:-2], rest[-2:]
    i = pl.program_id(0)
    ni = pl.num_programs(0)

    def row_copy(src_row, k, r, s):
        return pltpu.make_async_copy(ys_hbm.at[pl.ds(src_row, 1), :], buf.at[s, k, pl.ds(r, 1), :], sem.at[s])

    def issue(tile, s):
        def one(r, carry):
            base = ((tile0 + tile) * tm + r) * 2
            row_copy(slot_ref[base], 0, r, s).start()
            row_copy(slot_ref[base + 1], 1, r, s).start()
            return carry

        lax.fori_loop(0, tm, one, 0, unroll=8)

    @pl.when(i == 0)
    def _():
        issue(0, 0)

    @pl.when(i + 1 < ni)
    def _():
        issue(i + 1, (i + 1) % 2)

    slot = i % 2
    for k in range(2):
        pltpu.make_async_copy(ys_hbm.at[pl.ds(0, tm), :], buf.at[slot, k], sem.at[slot]).wait()

    rt = rt_ref[...]
    g1 = rt[:, 4:5]
    g2 = rt[:, 5:6]
    h2 = h_ref[...] + (buf[slot, 0] * g1 + buf[slot, 1] * g2)
    if want_h:
        outs[0][...] = h2
    outs[-1][...] = _rms(h2, g_ref[...]).astype(outs[-1].dtype)


def moe_combine(ys, slots, h1, route, g_next, tm, u_dtype, row0=0, nrows=None, want_h=True):
    d = h1.shape[1]
    t = h1.shape[0] if nrows is None else nrows
    assert row0 % tm == 0 and t % tm == 0
    tile0 = row0 // tm
    row_spec = pl.BlockSpec((tm, d), lambda i, sl: (i, 0))
    grid_spec = pltpu.PrefetchScalarGridSpec(
        num_scalar_prefetch=1,
        grid=(t // tm,),
        in_specs=[pl.BlockSpec(memory_space=pl.ANY),
                  pl.BlockSpec((tm, d), lambda i, sl: (i + tile0, 0)),
                  pl.BlockSpec((tm, LANES), lambda i, sl: (i + tile0, 0)),
                  pl.BlockSpec((1, d), lambda i, sl: (0, 0))],
        out_specs=[row_spec, row_spec] if want_h else [row_spec],
        scratch_shapes=[pltpu.VMEM((2, 2, tm, d), F32), pltpu.SemaphoreType.DMA((2,))],
    )
    shapes = [jax.ShapeDtypeStruct((t, d), F32)] if want_h else []
    return pl.pallas_call(
        functools.partial(_combine_body, tm=tm, tile0=tile0, want_h=want_h),
        grid_spec=grid_spec,
        out_shape=shapes + [jax.ShapeDtypeStruct((t, d), u_dtype)],
        compiler_params=_cparams(("arbitrary",), VMEM_LIMIT),
        name="moe_combine",
    )(slots, ys, h1, route, g_next.reshape(1, d))


def hier_moe_layer(h1, xn, logits, w_gate, w_up, w_down, layer, g_next, u_dtype, split=None):
    t = h1.shape[0]
    bm = MOE_BM
    route, counts = moe_routing(logits, _tile(t, 256))
    e = route[:, 0:2].astype(I32)
    rank = route[:, 2:4].astype(I32)
    counts = counts[0, N_GROUPS:N_GROUPS + N_EXPERTS].astype(I32)
    padded = (counts + bm - 1) // bm * bm
    pad_end = jnp.cumsum(padded)
    pad_start = pad_end - padded
    slots = (pad_start[e] + rank).reshape(-1)
    nb = (2 * t) // bm + N_EXPERTS
    slot_tok = jnp.zeros((nb * bm,), I32).at[slots].set(
        jnp.repeat(jnp.arange(t, dtype=I32), 2), unique_indices=True, mode="promise_in_bounds")
    blk0 = jnp.arange(nb, dtype=I32) * bm
    block_expert = jnp.minimum(jnp.sum((pad_end[None, :] <= blk0[:, None]).astype(I32), axis=1), N_EXPERTS - 1)
    n_valid = jnp.clip(counts[block_expert] - (blk0 - pad_start[block_expert]), 0, bm).astype(I32)
    ys = moe_experts(xn, block_expert, n_valid, slot_tok, w_gate, w_up, w_down, layer, bm)
    tmc = _tile(t, 256)
    if split is None:
        return moe_combine(ys, slots, h1, route, g_next, tmc, u_dtype)
    (u_a,) = moe_combine(ys, slots, h1, route, g_next, tmc, u_dtype, 0, split, want_h=False)
    (u_b,) = moe_combine(ys, slots, h1, route, g_next, tmc, u_dtype, split, t - split, want_h=False)
    return u_a, u_b


def _swap_halves(x):
    lane = lax.broadcasted_iota(I32, x.shape, 1)
    return jnp.where(lane < QK_ROPE // 2, pltpu.roll(x, LANES - QK_ROPE // 2, 1), pltpu.roll(x, QK_ROPE // 2, 1))


def _mla_prep_body(p_ref, gq_ref, gkv_ref, cos_ref, sin_ref, cq_ref, ckv_ref, ckvb_ref, kr_ref):
    p = p_ref[...]
    cq_ref[...] = _rms(p[:, :Q_LORA], gq_ref[...]).astype(cq_ref.dtype)
    ckv = _rms(p[:, Q_LORA:Q_LORA + KV_LORA], gkv_ref[...])
    ckv_ref[...] = ckv
    ckvb_ref[...] = ckv.astype(ckvb_ref.dtype)
    kr = p[:, Q_LORA + KV_LORA:]
    kr_ref[...] = kr * cos_ref[...] + _swap_halves(kr) * sin_ref[...]


def mla_prep(proj, g_q, g_kv, cos, sin, tm):
    t = proj.shape[0]
    return pl.pallas_call(
        _mla_prep_body,
        grid=(t // tm,),
        in_specs=[pl.BlockSpec((tm, proj.shape[1]), lambda i: (i, 0)),
                  pl.BlockSpec((1, Q_LORA), lambda i: (0, 0)),
                  pl.BlockSpec((1, KV_LORA), lambda i: (0, 0)),
                  pl.BlockSpec((tm, LANES), lambda i: (i, 0)),
                  pl.BlockSpec((tm, LANES), lambda i: (i, 0))],
        out_specs=[pl.BlockSpec((tm, Q_LORA), lambda i: (i, 0)),
                   pl.BlockSpec((tm, KV_LORA), lambda i: (i, 0)),
                   pl.BlockSpec((tm, KV_LORA), lambda i: (i, 0)),
                   pl.BlockSpec((tm, LANES), lambda i: (i, 0))],
        out_shape=[jax.ShapeDtypeStruct((t, Q_LORA), BF16), jax.ShapeDtypeStruct((t, KV_LORA), F32),
                   jax.ShapeDtypeStruct((t, KV_LORA), BF16), jax.ShapeDtypeStruct((t, LANES), F32)],
        compiler_params=_cparams(("parallel",)),
        name="mla_prep",
    )(proj, g_q.reshape(1, -1), g_kv.reshape(1, -1), cos, sin)


def _mla_q_body(cq_ref, w_ref, cos_ref, sin_ref, q_ref, wb):
    @pl.when(pl.program_id(0) == 0)
    def _():
        wb[...] = w_ref[...].astype(BF16)

    cq = cq_ref[...]
    cos = cos_ref[...]
    sin = sin_ref[...]
    for h in range(B_HEADS):
        y = _dot(cq, wb[:, h * MLA_QK_PAD:(h + 1) * MLA_QK_PAD])
        xr = y[:, LANES:]
        q_ref[:, h * MLA_QK_PAD:h * MLA_QK_PAD + LANES] = y[:, :LANES].astype(q_ref.dtype)
        q_ref[:, h * MLA_QK_PAD + LANES:(h + 1) * MLA_QK_PAD] = \
            (xr * cos + _swap_halves(xr) * sin).astype(q_ref.dtype)


def mla_q_up(cq, w_uq_pad, cos, sin, tm):
    t = cq.shape[0]
    n = B_HEADS * MLA_QK_PAD
    return pl.pallas_call(
        _mla_q_body,
        grid=(t // tm,),
        in_specs=[pl.BlockSpec((tm, Q_LORA), lambda i: (i, 0)),
                  pl.BlockSpec((Q_LORA, n), lambda i: (0, 0)),
                  pl.BlockSpec((tm, LANES), lambda i: (i, 0)),
                  pl.BlockSpec((tm, LANES), lambda i: (i, 0))],
        out_specs=pl.BlockSpec((tm, n), lambda i: (i, 0)),
        out_shape=jax.ShapeDtypeStruct((t, n), BF16),
        scratch_shapes=[pltpu.VMEM((Q_LORA, n), BF16)],
        compiler_params=_cparams(("arbitrary",), VMEM_LIMIT),
        name="mla_q_up",
    )(cq, w_uq_pad, cos, sin)


def _mla_kv_body(c_ref, kr_ref, wk_ref, wv_ref, k_ref, v_ref, wkb, wvb):
    @pl.when(pl.program_id(0) == 0)
    def _():
        wkb[...] = wk_ref[0].astype(BF16)
        wvb[...] = wv_ref[0].astype(BF16)

    c = c_ref[...].astype(BF16)
    kr = kr_ref[...].astype(k_ref.dtype)
    kn = _dot(c, wkb[...])
    for h in range(B_HEADS):
        k_ref[:, h * MLA_QK_PAD:h * MLA_QK_PAD + LANES] = kn[:, h * QK_NOPE:(h + 1) * QK_NOPE].astype(k_ref.dtype)
        k_ref[:, h * MLA_QK_PAD + LANES:(h + 1) * MLA_QK_PAD] = kr
    v_ref[...] = _dot(c, wvb[...]).astype(v_ref.dtype)


def mla_kv_up(ckv, kr, w_uk, w_uv, layer, tm):
    t = ckv.shape[0]
    nk = B_HEADS * MLA_QK_PAD
    nv = B_HEADS * V_DIM
    return pl.pallas_call(
        _mla_kv_body,
        grid=(t // tm,),
        in_specs=[pl.BlockSpec((tm, KV_LORA), lambda i: (i, 0)),
                  pl.BlockSpec((tm, LANES), lambda i: (i, 0)),
                  pl.BlockSpec((1, KV_LORA, B_HEADS * QK_NOPE), lambda i: (layer, 0, 0)),
                  pl.BlockSpec((1, KV_LORA, nv), lambda i: (layer, 0, 0))],
        out_specs=[pl.BlockSpec((tm, nk), lambda i: (i, 0)), pl.BlockSpec((tm, nv), lambda i: (i, 0))],
        out_shape=[jax.ShapeDtypeStruct((t, nk), BF16), jax.ShapeDtypeStruct((t, nv), BF16)],
        scratch_shapes=[pltpu.VMEM((KV_LORA, B_HEADS * QK_NOPE), BF16), pltpu.VMEM((KV_LORA, nv), BF16)],
        compiler_params=_cparams(("arbitrary",), VMEM_LIMIT),
        name="mla_kv_up",
    )(ckv, kr, w_uk, w_uv)


def kernel(x_prompt, x_sample, cache_a_k, cache_a_v, cache_a_kidx, cache_b_ckv, cache_b_krope, norm_mix, norm_ffn, norm_final, rel_bias, a_w_in, a_w_out, b_w_in, b_norm_q, b_norm_kv, b_w_uq, b_w_uk, b_w_uv, b_w_out, moe_w_grp, moe_b_grp, moe_w_rtr, moe_b_rtr, moe_w_gate, moe_w_up, moe_w_down):
    n_p, s_p, d = x_prompt.shape
    n_s, s_s, _ = x_sample.shape
    past = cache_a_k.shape[2]
    tp = n_p * s_p
    ts = n_s * s_s
    t = tp + ts
    a_qd = A_HEADS * A_HEAD_DIM
    tm = _tile(t, 1024)
    tm2 = _tile(t, 512)

    x = jnp.concatenate([x_prompt.reshape(tp, d), x_sample.reshape(ts, d)], axis=0)

    def router_params(i):
        w_r = jnp.concatenate([moe_w_grp[i], moe_w_rtr[i],
                               jnp.zeros((d, LANES - N_GROUPS - N_EXPERTS), F32)], axis=1)
        b_r = jnp.concatenate([moe_b_grp[i], moe_b_rtr[i],
                               jnp.zeros((LANES - N_GROUPS - N_EXPERTS,), F32)]).reshape(1, LANES)
        return w_r, b_r

    u0 = rmsnorm(x, norm_mix[0], tm)
    w_a = a_w_in[0]
    (q_b,) = matmul(u0, w_a, 0, a_qd, [BF16], tm, 512, "a_proj_q")
    tmp, tms = _tile(tp, 1024), _tile(ts, 1024)
    kp_f, kp_b = matmul(u0, w_a, a_qd, a_qd, [F32, BF16], tmp, 512, "a_proj_k", 0, tp)
    vp_f, vp_b = matmul(u0, w_a, 2 * a_qd, a_qd, [F32, BF16], tmp, 512, "a_proj_v", 0, tp)
    ks_f, ks_b = matmul(u0, w_a, a_qd, a_qd, [F32, BF16], tms, 512, "a_proj_k", tp, ts)
    vs_f, vs_b = matmul(u0, w_a, 2 * a_qd, a_qd, [F32, BF16], tms, 512, "a_proj_v", tp, ts)
    (qi_b,) = matmul(u0, w_a, 3 * a_qd, IDX_HEADS * IDX_DIM, [BF16], tm, 512, "a_proj_qi")
    c_ki = 3 * a_qd + IDX_HEADS * IDX_DIM
    w_ki = w_a[:, c_ki:c_ki + IDX_DIM]
    w_wi = w_a[:, c_ki + IDX_DIM:c_ki + IDX_DIM + IDX_HEADS]
    zk = jnp.zeros((d, IDX_DIM), F32)
    w_tail = jnp.concatenate([w_ki, zk, zk, w_ki, w_wi, jnp.zeros((d, LANES - IDX_HEADS), F32)], axis=1)
    (tail,) = matmul(u0, w_tail, 0, 3 * LANES, [F32], tm, 3 * LANES, "a_proj_tail")
    kidx = tail[:, :IDX_DIM]

    bias_tab = rel_bias_tables(rel_bias)
    a_scale = A_HEAD_DIM ** -0.5
    hg = 8
    mask_p = dsa_select(qi_b, tail, 2, tail, n_seq=n_p, sq=s_p, bq=128, sk=s_p, sk_real=s_p, tk=512,
                        pos0=0, q_row0=0)
    att_p = attention(q_b, [(kp_b, vp_b, s_p, 0, 0)], bias_tab, mask_p, n_seq=n_p, sq=s_p, bq=128, q_row0=0,
                      q_col0=0, n_heads=A_HEADS, hg=8, dq=A_HEAD_DIM, dv=A_HEAD_DIM, scale=a_scale, pos0=0,
                      dyn_sk=s_p, name="dsa_attention_prompt")
    sk_s = past + s_s
    sk_pad = (sk_s + ATT_TILE - 1) // ATT_TILE * ATT_TILE
    ki_past = cache_a_kidx[0]
    zp = jnp.zeros_like(ki_past)
    kc_past = jnp.concatenate([ki_past, zp, zp, ki_past], axis=-1)
    kc_new = tail[tp:, :2 * LANES].reshape(n_s, s_s, 2 * LANES)
    kc_s = jnp.concatenate([kc_past, kc_new, jnp.zeros((n_s, sk_pad - sk_s, 2 * LANES), F32)], axis=1)
    mask_s = dsa_select(qi_b, tail, 2, kc_s.reshape(n_s * sk_pad, 2 * LANES), n_seq=n_s, sq=s_s, bq=s_s,
                        sk=sk_pad, sk_real=sk_s, tk=ATT_TILE, pos0=past, q_row0=tp)
    tiles_s = [(0, r, r, ATT_TILE) for r in range(0, past, ATT_TILE)] + [(1, 0, past, s_s)]
    att_s = attention(q_b, [(cache_a_k[0].reshape(n_s * past, a_qd), cache_a_v[0].reshape(n_s * past, a_qd),
                             past, 0, 0), (ks_b, vs_b, s_s, 0, 0)],
                      bias_tab, mask_s, n_seq=n_s, sq=s_s, bq=s_s, q_row0=tp, q_col0=0, n_heads=A_HEADS, hg=hg,
                      dq=A_HEAD_DIM, dv=A_HEAD_DIM, scale=a_scale, pos0=past, seg_tiles=tiles_s,
                      name="dsa_attention_sample")
    att0 = jnp.concatenate([att_p, att_s], axis=0)

    w_r0, b_r0 = router_params(0)
    h1, xn0, lg0 = outproj_norm_router(att0, a_w_out, 0, x, norm_ffn[0], w_r0, b_r0, tm2, 512)
    h2, u1 = hier_moe_layer(h1, xn0, lg0, moe_w_gate, moe_w_up, moe_w_down, 0, norm_mix[1], BF16)

    n_in = Q_LORA + KV_LORA + QK_ROPE
    w_b = jnp.concatenate([b_w_in[0], jnp.zeros((d, LANES - QK_ROPE), F32)], axis=1)
    (proj,) = matmul(u1, w_b, 0, n_in + LANES - QK_ROPE, [F32], tm, 384, "b_proj")
    half = QK_ROPE // 2
    inv = ROPE_THETA ** (-jnp.arange(half, dtype=F32) / half)
    pos_all = jnp.concatenate([jnp.tile(jnp.arange(s_p, dtype=I32), n_p),
                               jnp.tile(past + jnp.arange(s_s, dtype=I32), n_s)])
    ang = pos_all.astype(F32)[:, None] * inv[None, :]
    zl = jnp.zeros((t, LANES - QK_ROPE), F32)
    cos_t = jnp.concatenate([jnp.cos(ang), jnp.cos(ang), zl], axis=1)
    sin_t = jnp.concatenate([-jnp.sin(ang), jnp.sin(ang), zl], axis=1)
    cq_b, ckv_f, ckv_b, kr_f = mla_prep(proj, b_norm_q[0], b_norm_kv[0], cos_t, sin_t, tm)
    w_uq_pad = jnp.pad(b_w_uq[0].reshape(Q_LORA, B_HEADS, QK_NOPE + QK_ROPE),
                       ((0, 0), (0, 0), (0, MLA_QK_PAD - QK_NOPE - QK_ROPE))).reshape(Q_LORA, B_HEADS * MLA_QK_PAD)
    qm = mla_q_up(cq_b, w_uq_pad, cos_t, sin_t, tm2)
    km_p, vm_p = mla_kv_up(ckv_b[:tp], kr_f[:tp], b_w_uk, b_w_uv, 0, _tile(tp, 512))
    ctab = causal_tables()
    hgb = 8
    matt_p = attention(qm, [(km_p, vm_p, s_p, 0, 0)], ctab, None, n_seq=n_p, sq=s_p, bq=128, q_row0=0, q_col0=0,
                       n_heads=B_HEADS, hg=8, dq=MLA_QK_PAD, dv=V_DIM, scale=B_SCALE, pos0=0, dyn_sk=s_p,
                       name="mla_attention_prompt")
    ckv_all = jnp.concatenate([cache_b_ckv[0], ckv_f[tp:].reshape(n_s, s_s, KV_LORA)], axis=1)
    kr_past = jnp.concatenate([cache_b_krope[0], jnp.zeros((n_s, past, LANES - QK_ROPE), F32)], axis=-1)
    kr_all = jnp.concatenate([kr_past, kr_f[tp:].reshape(n_s, s_s, LANES)], axis=1)
    km_s, vm_s = mla_kv_up(ckv_all.reshape(n_s * sk_s, KV_LORA), kr_all.reshape(n_s * sk_s, LANES),
                           b_w_uk, b_w_uv, 0, sk_s)
    tiles_m = [(0, r, r, ATT_TILE) for r in range(0, past, ATT_TILE)] + [(0, past, past, s_s)]
    matt_s = attention(qm, [(km_s, vm_s, sk_s, 0, 0)], ctab, None, n_seq=n_s, sq=s_s, bq=s_s, q_row0=tp, q_col0=0,
                       n_heads=B_HEADS, hg=hgb, dq=MLA_QK_PAD, dv=V_DIM, scale=B_SCALE, pos0=past,
                       seg_tiles=tiles_m, name="mla_attention_sample")
    att1 = jnp.concatenate([matt_p, matt_s], axis=0)

    w_r1, b_r1 = router_params(1)
    h3, xn1, lg1 = outproj_norm_router(att1, b_w_out, 0, h2, norm_ffn[1], w_r1, b_r1, tm2, 512)
    y_p, y_s = hier_moe_layer(h3, xn1, lg1, moe_w_gate, moe_w_up, moe_w_down, 1, norm_final, F32, split=tp)
    y_prompt = y_p.reshape(n_p, s_p, d)
    y_sample = y_s.reshape(n_s, s_s, d)

    def heads(a, n, s):
        return a.reshape(1, n, s, A_HEADS, A_HEAD_DIM)

    kr_out = kr_f[:, :QK_ROPE]
    return (y_prompt, y_sample,
            heads(kp_f, n_p, s_p), heads(vp_f, n_p, s_p), kidx[:tp].reshape(1, n_p, s_p, IDX_DIM),
            ckv_f[:tp].reshape(1, n_p, s_p, KV_LORA), kr_out[:tp].reshape(1, n_p, s_p, QK_ROPE),
            heads(ks_f, n_s, s_s), heads(vs_f, n_s, s_s), kidx[tp:].reshape(1, n_s, s_s, IDX_DIM),
            ckv_f[tp:].reshape(1, n_s, s_s, KV_LORA), kr_out[tp:].reshape(1, n_s, s_s, QK_ROPE))
```

```python
import functools
import math

import jax
import jax.numpy as jnp
from jax import lax
from jax.experimental import pallas as pl
from jax.experimental.pallas import tpu as pltpu

F32 = jnp.float32
BF16 = jnp.bfloat16
I32 = jnp.int32

RMS_EPS = 1e-6
CHUNK = 64
NEG = -1e30
INT_MIN = -2 ** 31
INT_MAX = 2 ** 31 - 1

LANES = 128
KEY_BLOCK = 128
ATT_TILE = 256
VMEM_LIMIT = 56 * 1024 * 1024

A_HEADS = 16
A_HEAD_DIM = 128
IDX_HEADS = 16
IDX_DIM = 64
TOPK_MAX = 256
IDX_W_SCALE = float((IDX_HEADS * IDX_DIM) ** -0.5)
NUM_BUCKETS = 32
MAX_DISTANCE = 128
B_HEADS = 16
Q_LORA = 512
KV_LORA = 512
QK_NOPE = 128
QK_ROPE = 64
V_DIM = 128
ROPE_THETA = 10000.0
B_SCALE = float((QK_NOPE + QK_ROPE) ** -0.5)
MLA_QK_PAD = 256
N_GROUPS = 8
EXPERTS_PER_GROUP = 8
N_EXPERTS = 64
D_EXPERT = 512
MOE_BM = 128


def _tile(n, pref):
    for c in range(pref, 0, -LANES):
        if n % c == 0:
            return c
    raise ValueError(f"no 128-multiple tile divides {n}")


def _cparams(sem, vmem=None):
    return pltpu.CompilerParams(dimension_semantics=sem, vmem_limit_bytes=vmem)


def _dot(a, b):
    return jnp.dot(a, b, preferred_element_type=F32)


def _dot_nt(a, b):
    return lax.dot_general(a, b, (((1,), (1,)), ((), ())), preferred_element_type=F32)


def _rms(x, g):
    ms = jnp.mean(x * x, axis=-1, keepdims=True)
    return x * lax.rsqrt(ms + RMS_EPS) * g


def _pick_rows(i, n_first, refs):
    if len(refs) == 1:
        return refs[0][...]
    return jnp.where(i < n_first, refs[0][...], refs[1][...])


def _row_split_specs(parts, tm, width, col_of):
    n_first = parts[0].shape[0] // tm
    assert all(p.shape[0] % tm == 0 for p in parts)
    specs = [pl.BlockSpec((tm, width), lambda i, *r: (jnp.minimum(i, n_first - 1), col_of(i, *r)))]
    if len(parts) == 2:
        specs.append(pl.BlockSpec((tm, width), lambda i, *r: (jnp.maximum(i - n_first, 0), col_of(i, *r))))
    return specs, n_first


def _rmsnorm_body(*refs, n_first):
    x = _pick_rows(pl.program_id(0), n_first, refs[:-2])
    g_ref, o_ref = refs[-2:]
    o_ref[...] = _rms(x, g_ref[...]).astype(o_ref.dtype)


def rmsnorm(xs, g, tm, out_dtype=BF16):
    d = xs[0].shape[1]
    t = sum(x.shape[0] for x in xs)
    specs, n_first = _row_split_specs(xs, tm, d, lambda i: 0)
    return pl.pallas_call(
        functools.partial(_rmsnorm_body, n_first=n_first),
        grid=(t // tm,),
        in_specs=specs + [pl.BlockSpec((1, d), lambda i: (0, 0))],
        out_specs=pl.BlockSpec((tm, d), lambda i: (i, 0)),
        out_shape=jax.ShapeDtypeStruct((t, d), out_dtype),
        compiler_params=_cparams(("parallel",)),
        name="rmsnorm",
    )(*xs, g.reshape(1, d))


def _matmul_body(a_ref, w_ref, *rest):
    outs, wb = rest[:-1], rest[-1]

    @pl.when(pl.program_id(1) == 0)
    def _():
        wb[...] = w_ref[...].astype(BF16)

    r = _dot(a_ref[...], wb[...])
    for o in outs:
        o[...] = r.astype(o.dtype)


def matmul(a, w, col0, ncols, out_dtypes, tm, tn, name, row0=0, nrows=None):
    k = a.shape[1]
    t = a.shape[0] if nrows is None else nrows
    assert ncols % tn == 0 and col0 % tn == 0 and t % tm == 0 and row0 % tm == 0
    cb = col0 // tn
    rb = row0 // tm
    return pl.pallas_call(
        _matmul_body,
        grid=(ncols // tn, t // tm),
        in_specs=[pl.BlockSpec((tm, k), lambda j, i: (i + rb, 0)),
                  pl.BlockSpec((k, tn), lambda j, i: (0, j + cb))],
        out_specs=[pl.BlockSpec((tm, tn), lambda j, i: (i, j)) for _ in out_dtypes],
        out_shape=[jax.ShapeDtypeStruct((t, ncols), dt) for dt in out_dtypes],
        scratch_shapes=[pltpu.VMEM((k, tn), BF16)],
        compiler_params=_cparams(("arbitrary", "arbitrary"), VMEM_LIMIT),
        name=name,
    )(a, w)


def _float_sort_key(x):
    bits = lax.bitcast_convert_type(x, I32)
    return bits ^ ((bits >> 31) & INT_MAX)


def _select_body(qi_ref, wi_ref, kc_ref, mask_ref, key_scr, jm_scr, *, bq, sk, sk_real, tk, pos0, topk):
    b = pl.program_id(1)
    t0 = pos0 + b * bq
    kmax = jnp.minimum(sk_real, ((t0 + bq - 1) // CHUNK + 1) * CHUNK)
    nkt = (kmax + tk - 1) // tk
    nch = tk // LANES

    w = wi_ref[...] * IDX_W_SCALE
    wb = [jnp.broadcast_to(w[:, h:h + 1], (bq, LANES)) for h in range(IDX_HEADS)]
    row_pos = t0 + lax.broadcasted_iota(I32, (bq, LANES), 0)
    lim = jnp.minimum((row_pos // CHUNK + 1) * CHUNK, sk_real)
    lane = lax.broadcasted_iota(I32, (bq, LANES), 1)

    def score_tile(j, carry):
        off = pl.multiple_of(j * tk, tk)
        kc = kc_ref[pl.ds(off, tk), :].astype(BF16)
        ka, kb = kc[:, :LANES], kc[:, LANES:]
        accs = [jnp.zeros((bq, LANES), F32) for _ in range(nch)]
        for g in range(IDX_HEADS // 2):
            qg = qi_ref[:, g * LANES:(g + 1) * LANES]
            sa = _dot_nt(qg, ka)
            sb = _dot_nt(qg, kb)
            for c in range(nch):
                sl = slice(c * LANES, (c + 1) * LANES)
                accs[c] = accs[c] + wb[2 * g] * jnp.maximum(sa[:, sl], 0.0) \
                    + wb[2 * g + 1] * jnp.maximum(sb[:, sl], 0.0)
        for c in range(nch):
            kpos = off + c * LANES + lane
            sc = jnp.where(kpos < lim, accs[c], -jnp.inf)
            key_scr[:, pl.ds(off + c * LANES, LANES)] = _float_sort_key(sc)
        return carry

    lax.fori_loop(0, nkt, score_tile, 0)

    def count(indicator):
        def tile(j, cnt):
            off = pl.multiple_of(j * tk, tk)
            for c in range(nch):
                kt = key_scr[:, pl.ds(off + c * LANES, LANES)]
                cnt = cnt + indicator(kt, off + c * LANES + lane)
            return cnt
        cnt = lax.fori_loop(0, nkt, tile, jnp.zeros((bq, LANES), F32))
        return jnp.sum(cnt, axis=1, keepdims=True)

    def bit_step(i, pfx_u):
        bit = lax.shift_left(jnp.int32(1), 31 - i)
        cand_u = pfx_u | bit
        cand_s = cand_u ^ INT_MIN
        total = count(lambda kt, kp: jnp.where(kt >= cand_s, 1.0, 0.0))
        return jnp.where(total >= topk, cand_u, pfx_u)

    pfx = lax.fori_loop(0, 32, bit_step, jnp.zeros((bq, LANES), I32))
    thr = pfx ^ INT_MIN

    n_gt = count(lambda kt, kp: jnp.where(kt > thr, 1.0, 0.0))
    n_ge = count(lambda kt, kp: jnp.where(kt >= thr, 1.0, 0.0))
    quota = topk - n_gt
    jm_scr[...] = jnp.full((bq, LANES), INT_MAX, I32)
    any_excess = jnp.max(jnp.where(n_ge > topk, 1.0, 0.0)) > 0.0

    @pl.when(any_excess)
    def _():
        nbits = max(1, int(sk - 1).bit_length())

        def idx_step(i, ans):
            cand = ans | lax.shift_left(jnp.int32(1), nbits - 1 - i)
            below = count(lambda kt, kp: jnp.where(kt == thr, jnp.where(kp < cand, 1.0, 0.0), 0.0))
            return jnp.where(below < quota, cand, ans)

        jm_scr[...] = lax.fori_loop(0, nbits, idx_step, jnp.zeros((bq, LANES), I32))

    jm = jm_scr[...]

    def write_tile(j, carry):
        off = pl.multiple_of(j * tk, tk)
        for c in range(nch):
            kt = key_scr[:, pl.ds(off + c * LANES, LANES)]
            kpos = off + c * LANES + lane
            v = jnp.where(kt > thr, 0.0, jnp.where(kt == thr, jnp.where(kpos <= jm, 0.0, NEG), NEG))
            v = jnp.where(kpos < lim, v, NEG)
            mask_ref[:, pl.ds(off + c * LANES, LANES)] = v.astype(mask_ref.dtype)
        return carry

    lax.fori_loop(0, nkt, write_tile, 0)

    def fill_tile(j, carry):
        off = pl.multiple_of(j * tk, tk)
        mask_ref[:, pl.ds(off, tk)] = jnp.full((bq, tk), NEG, mask_ref.dtype)
        return carry

    lax.fori_loop(nkt, sk // tk, fill_tile, 0)


def dsa_select(qi, wi_arr, wi_blk, kc, *, n_seq, sq, bq, sk, sk_real, tk, pos0, q_row0):
    nqb = sq // bq
    qb0 = q_row0 // bq
    assert q_row0 % bq == 0 and sk % tk == 0 and tk >= TOPK_MAX
    topk = min(TOPK_MAX, sk_real // 4)
    body = functools.partial(_select_body, bq=bq, sk=sk, sk_real=sk_real, tk=tk, pos0=pos0, topk=topk)
    return pl.pallas_call(
        body,
        grid=(n_seq, nqb),
        in_specs=[pl.BlockSpec((bq, IDX_HEADS * IDX_DIM), lambda s, b: (qb0 + s * nqb + b, 0)),
                  pl.BlockSpec((bq, LANES), lambda s, b: (qb0 + s * nqb + b, wi_blk)),
                  pl.BlockSpec((sk, 2 * LANES), lambda s, b: (s, 0))],
        out_specs=pl.BlockSpec((bq, sk), lambda s, b: (s * nqb + b, 0)),
        out_shape=jax.ShapeDtypeStruct((n_seq * sq, sk), BF16),
        scratch_shapes=[pltpu.VMEM((bq, sk), I32), pltpu.VMEM((bq, LANES), I32)],
        compiler_params=_cparams(("parallel", "arbitrary"), VMEM_LIMIT),
        name="dsa_select",
    )(qi, wi_arr, kc)


def _rel_tables_body(bias_ref, tab_ref):
    i = lax.broadcasted_iota(I32, (KEY_BLOCK, KEY_BLOCK), 0)
    j = lax.broadcasted_iota(I32, (KEY_BLOCK, KEY_BLOCK), 1)
    nb = NUM_BUCKETS // 2
    max_exact = nb // 2
    edges = [12, 16, 23, 32, 46, 64, 91]
    for d in range(4):
        rel = j - i + (d - 2) * KEY_BLOCK
        n = jnp.abs(rel)
        large = jnp.full_like(n, max_exact)
        for e in edges:
            large = large + jnp.where(n >= e, 1, 0)
        bucket = jnp.where(rel > 0, nb, 0) + jnp.where(n < max_exact, n, large)
        for h in range(A_HEADS):
            acc = jnp.zeros((KEY_BLOCK, KEY_BLOCK), F32)
            for k in range(NUM_BUCKETS):
                acc = jnp.where(bucket == k, bias_ref[k, h], acc)
            tab_ref[h, d] = acc


def rel_bias_tables(rel_bias):
    return pl.pallas_call(
        _rel_tables_body,
        in_specs=[pl.BlockSpec(memory_space=pltpu.SMEM)],
        out_shape=jax.ShapeDtypeStruct((A_HEADS, 4, KEY_BLOCK, KEY_BLOCK), F32),
        name="rel_bias_tables",
    )(rel_bias)


def _causal_tables_body(tab_ref):
    i = lax.broadcasted_iota(I32, (KEY_BLOCK, KEY_BLOCK), 0)
    j = lax.broadcasted_iota(I32, (KEY_BLOCK, KEY_BLOCK), 1)
    zero = jnp.zeros((KEY_BLOCK, KEY_BLOCK), F32)
    tab_ref[0, 0] = zero
    tab_ref[0, 1] = zero
    tab_ref[0, 2] = jnp.where(j // CHUNK <= i // CHUNK, 0.0, NEG)
    tab_ref[0, 3] = jnp.full((KEY_BLOCK, KEY_BLOCK), NEG, F32)


def causal_tables():
    return pl.pallas_call(
        _causal_tables_body,
        out_shape=jax.ShapeDtypeStruct((1, 4, KEY_BLOCK, KEY_BLOCK), F32),
        name="causal_tables",
    )()


def _attn_pipe_body(*refs, has_mask, per_head_tab, bq, hg, dq, dv, scale, pos0, sk):
    q_ref, k_ref, v_ref, tab_ref = refs[:4]
    mask_ref = refs[4] if has_mask else None
    out_ref = refs[5] if has_mask else refs[4]
    s_scr, p_scr, m_scr, l_scr, acc_scr = refs[-5:]

    b = pl.program_id(2)
    t0 = pos0 + b * bq
    qblk = t0 // KEY_BLOCK
    kmax = jnp.minimum(sk, ((t0 + bq - 1) // CHUNK + 1) * CHUNK)
    nkt = (kmax + ATT_TILE - 1) // ATT_TILE
    nch = ATT_TILE // LANES

    m_scr[...] = jnp.full(m_scr.shape, NEG, F32)
    l_scr[...] = jnp.zeros(l_scr.shape, F32)
    acc_scr[...] = jnp.zeros(acc_scr.shape, F32)

    def logits_tile(j, carry):
        off = pl.multiple_of(j * ATT_TILE, ATT_TILE)
        kb0 = off // KEY_BLOCK
        ds = [jnp.clip(kb0 + c - qblk + 2, 0, 3) for c in range(nch)]
        if has_mask:
            mk = mask_ref[:, pl.ds(off, ATT_TILE)].astype(F32)
        for h in range(hg):
            s = _dot_nt(q_ref[:, h * dq:(h + 1) * dq], k_ref[pl.ds(off, ATT_TILE), h * dq:(h + 1) * dq]) * scale
            hh = h if per_head_tab else 0
            s = s + jnp.concatenate([tab_ref[hh, ds[c]] for c in range(nch)], axis=1)
            if has_mask:
                s = s + mk
            s_scr[h, :, pl.ds(off, ATT_TILE)] = s
            mvec = m_scr[h]
            for c in range(nch):
                mvec = jnp.maximum(mvec, s[:, c * LANES:(c + 1) * LANES])
            m_scr[h] = mvec
        return carry

    lax.fori_loop(0, nkt, logits_tile, 0)
    for h in range(hg):
        m_scr[h] = jnp.broadcast_to(jnp.max(m_scr[h], axis=1, keepdims=True), (bq, LANES))

    def exp_tile(j, carry):
        off = pl.multiple_of(j * ATT_TILE, ATT_TILE)
        for h in range(hg):
            s = s_scr[h, :, pl.ds(off, ATT_TILE)]
            m = m_scr[h]
            p = [jnp.exp(s[:, c * LANES:(c + 1) * LANES] - m) for c in range(nch)]
            lvec = l_scr[h]
            for c in range(nch):
                lvec = lvec + p[c]
            l_scr[h] = lvec
            p_scr[h, :, pl.ds(off, ATT_TILE)] = jnp.concatenate(p, axis=1).astype(BF16)
        return carry

    lax.fori_loop(0, nkt, exp_tile, 0)

    def pv_tile(j, carry):
        off = pl.multiple_of(j * ATT_TILE, ATT_TILE)
        for h in range(hg):
            acc_scr[h] = acc_scr[h] + _dot(p_scr[h, :, pl.ds(off, ATT_TILE)],
                                           v_ref[pl.ds(off, ATT_TILE), h * dv:(h + 1) * dv])
        return carry

    lax.fori_loop(0, nkt, pv_tile, 0)
    for h in range(hg):
        l_row = jnp.sum(l_scr[h], axis=1, keepdims=True)
        out_ref[:, h * dv:(h + 1) * dv] = (acc_scr[h] / l_row).astype(out_ref.dtype)


def _attn_body(*refs, n_seg, has_mask, per_head_tab, bq, hg, dq, dv, scale, pos0, seg_tiles):
    q_ref = refs[0]
    kv = refs[1:1 + 2 * n_seg]
    pos = 1 + 2 * n_seg
    tab_ref = refs[pos]
    pos += 1
    mask_ref = refs[pos] if has_mask else None
    pos += 1 if has_mask else 0
    out_ref = refs[pos]
    s_scr, p_scr = refs[pos + 1:pos + 3]

    b = pl.program_id(2)
    t0 = pos0 + b * bq
    qblk = t0 // KEY_BLOCK

    def lane_pad(x, fill):
        w = x.shape[1]
        return x if w == LANES else jnp.concatenate([x, jnp.full((bq, LANES - w), fill, F32)], axis=1)

    mvec = [jnp.full((bq, LANES), NEG, F32) for _ in range(hg)]
    for (si, row0, key0, width) in seg_tiles:
        nch = max(1, width // KEY_BLOCK)
        cw = min(width, KEY_BLOCK)
        ds = [jnp.clip(key0 // KEY_BLOCK + c - qblk + 2, 0, 3) for c in range(nch)]
        k_t = kv[2 * si][row0:row0 + width, :].astype(BF16)
        if has_mask:
            mk = mask_ref[:, key0:key0 + width].astype(F32)
        for h in range(hg):
            hh = h if per_head_tab else 0
            s = _dot_nt(q_ref[:, h * dq:(h + 1) * dq], k_t[:, h * dq:(h + 1) * dq]) * scale
            bias = [tab_ref[hh, ds[c], :bq, :cw] for c in range(nch)]
            s = s + (jnp.concatenate(bias, axis=1) if nch > 1 else bias[0])
            if has_mask:
                s = s + mk
            s_scr[h, :, key0:key0 + width] = s
            for c in range(nch):
                mvec[h] = jnp.maximum(mvec[h], lane_pad(s[:, c * cw:(c + 1) * cw], NEG))

    m = [jnp.broadcast_to(jnp.max(mvec[h], axis=1, keepdims=True), (bq, LANES)) for h in range(hg)]
    lvec = [jnp.zeros((bq, LANES), F32) for _ in range(hg)]
    for (si, row0, key0, width) in seg_tiles:
        nch = max(1, width // KEY_BLOCK)
        cw = min(width, KEY_BLOCK)
        for h in range(hg):
            s = s_scr[h, :, key0:key0 + width]
            p = [jnp.exp(s[:, c * cw:(c + 1) * cw] - m[h][:, :cw]) for c in range(nch)]
            for c in range(nch):
                lvec[h] = lvec[h] + lane_pad(p[c], 0.0)
            p_scr[h, :, key0:key0 + width] = (jnp.concatenate(p, axis=1) if nch > 1 else p[0]).astype(BF16)

    acc = [jnp.zeros((bq, dv), F32) for _ in range(hg)]
    for (si, row0, key0, width) in seg_tiles:
        v_t = kv[2 * si + 1][row0:row0 + width, :].astype(BF16)
        for h in range(hg):
            acc[h] = acc[h] + _dot(p_scr[h, :, key0:key0 + width], v_t[:, h * dv:(h + 1) * dv])

    for h in range(hg):
        l_row = jnp.sum(lvec[h], axis=1, keepdims=True)
        out_ref[:, h * dv:(h + 1) * dv] = (acc[h] / l_row).astype(out_ref.dtype)


def attention(q, segs, tab, mask, *, n_seq, sq, bq, q_row0, q_col0, n_heads, hg, dq, dv, scale, pos0,
              seg_tiles=None, dyn_sk=None, name="attention"):
    nqb = sq // bq
    ng = n_heads // hg
    qb0 = q_row0 // bq
    assert q_row0 % bq == 0 and q_col0 % (hg * dq) == 0 and pos0 % KEY_BLOCK == 0
    assert bq == KEY_BLOCK or nqb == 1
    qc0 = q_col0 // (hg * dq)
    in_specs = [pl.BlockSpec((bq, hg * dq), lambda s, g, b: (qb0 + s * nqb + b, qc0 + g))]
    args = [q]
    for (k, v, rows, kc0, vc0) in segs:
        assert kc0 % (hg * dq) == 0 and vc0 % (hg * dv) == 0
        mode = dict(pipeline_mode=pl.Buffered(1)) if dyn_sk is not None else {}
        in_specs.append(pl.BlockSpec((rows, hg * dq), lambda s, g, b, c=kc0 // (hg * dq): (s, c + g), **mode))
        in_specs.append(pl.BlockSpec((rows, hg * dv), lambda s, g, b, c=vc0 // (hg * dv): (s, c + g), **mode))
        args += [k, v]
    per_head_tab = tab.shape[0] > 1
    if per_head_tab:
        in_specs.append(pl.BlockSpec((hg, 4, KEY_BLOCK, KEY_BLOCK), lambda s, g, b: (g, 0, 0, 0)))
    else:
        in_specs.append(pl.BlockSpec((1, 4, KEY_BLOCK, KEY_BLOCK), lambda s, g, b: (0, 0, 0, 0)))
    args.append(tab)
    if mask is not None:
        in_specs.append(pl.BlockSpec((bq, mask.shape[1]), lambda s, g, b: (s * nqb + b, 0)))
        args.append(mask)
    if dyn_sk is not None:
        assert len(segs) == 1 and dyn_sk % ATT_TILE == 0 and bq == KEY_BLOCK
        body = functools.partial(_attn_pipe_body, has_mask=mask is not None, per_head_tab=per_head_tab,
                                 bq=bq, hg=hg, dq=dq, dv=dv, scale=scale, pos0=pos0, sk=dyn_sk)
        scratch = [pltpu.VMEM((hg, bq, dyn_sk), F32), pltpu.VMEM((hg, bq, dyn_sk), BF16),
                   pltpu.VMEM((hg, bq, LANES), F32), pltpu.VMEM((hg, bq, LANES), F32),
                   pltpu.VMEM((hg, bq, dv), F32)]
    else:
        body = functools.partial(_attn_body, n_seg=len(segs), has_mask=mask is not None,
                                 per_head_tab=per_head_tab, bq=bq, hg=hg, dq=dq, dv=dv, scale=scale, pos0=pos0,
                                 seg_tiles=seg_tiles)
        sk_tot = max(key0 + width for (_, _, key0, width) in seg_tiles)
        sk_tot = (sk_tot + LANES - 1) // LANES * LANES
        scratch = [pltpu.VMEM((hg, bq, sk_tot), F32), pltpu.VMEM((hg, bq, sk_tot), BF16)]
    return pl.pallas_call(
        body,
        grid=(n_seq, ng, nqb),
        in_specs=in_specs,
        out_specs=pl.BlockSpec((bq, hg * dv), lambda s, g, b: (s * nqb + b, g)),
        out_shape=jax.ShapeDtypeStruct((n_seq * sq, n_heads * dv), BF16),
        scratch_shapes=scratch,
        compiler_params=_cparams(("parallel", "parallel", "arbitrary"), VMEM_LIMIT),
        name=name,
    )(*args)


def _outproj_body(*refs, tn, na, nx, a_first, x_first):
    i = pl.program_id(0)
    j = pl.program_id(1)
    a = _pick_rows(i, a_first, refs[:na])
    w_ref = refs[na]
    x = _pick_rows(i, x_first, refs[na + 1:na + 1 + nx])
    g_ref, wr_ref, br_ref, h_ref, xn_ref, lg_ref, hrow = refs[na + 1 + nx:]
    r = x + _dot(a, w_ref[0].astype(BF16))
    h_ref[...] = r
    hrow[:, pl.ds(pl.multiple_of(j * tn, tn), tn)] = r

    @pl.when(j == pl.num_programs(1) - 1)
    def _():
        xn = _rms(hrow[...], g_ref[...])
        xn_ref[...] = xn
        lg_ref[...] = _dot(xn.astype(BF16), wr_ref[...].astype(BF16)) + br_ref[...]


def outproj_norm_router(a_parts, w_out, layer, x_parts, g_ffn, w_r, b_r, tm, tn):
    k = a_parts[0].shape[1]
    t = sum(p.shape[0] for p in a_parts)
    d = w_out.shape[-1]
    a_specs, a_first = _row_split_specs(a_parts, tm, k, lambda i, j: 0)
    x_specs, x_first = _row_split_specs(x_parts, tm, tn, lambda i, j: j)
    return pl.pallas_call(
        functools.partial(_outproj_body, tn=tn, na=len(a_parts), nx=len(x_parts), a_first=a_first,
                          x_first=x_first),
        grid=(t // tm, d // tn),
        in_specs=a_specs + [pl.BlockSpec((1, k, tn), lambda i, j: (layer, 0, j))] + x_specs + [
            pl.BlockSpec((1, d), lambda i, j: (0, 0)),
            pl.BlockSpec((d, LANES), lambda i, j: (0, 0)),
            pl.BlockSpec((1, LANES), lambda i, j: (0, 0))],
        out_specs=[pl.BlockSpec((tm, tn), lambda i, j: (i, j)),
                   pl.BlockSpec((tm, d), lambda i, j: (i, 0)),
                   pl.BlockSpec((tm, LANES), lambda i, j: (i, 0))],
        out_shape=[jax.ShapeDtypeStruct((t, d), F32), jax.ShapeDtypeStruct((t, d), F32),
                   jax.ShapeDtypeStruct((t, LANES), F32)],
        scratch_shapes=[pltpu.VMEM((tm, d), F32)],
        compiler_params=_cparams(("parallel", "arbitrary"), VMEM_LIMIT),
        name="outproj_norm_router",
    )(*a_parts, w_out, *x_parts, g_ffn.reshape(1, d), w_r, b_r)


def _routing_body(lg_ref, out_ref, cnt_ref, carry, *, tm):
    i = pl.program_id(0)

    @pl.when(i == 0)
    def _():
        carry[...] = jnp.zeros(carry.shape, F32)

    x = lg_ref[...]
    lane = lax.broadcasted_iota(I32, (tm, LANES), 1)
    neg_inf = -jnp.inf

    def rmax(v):
        return jnp.max(v, axis=1, keepdims=True)

    def rmin(v):
        return jnp.min(v, axis=1, keepdims=True)

    def rsum(v):
        return jnp.sum(v, axis=1, keepdims=True)

    gm = lane < N_GROUPS
    gmax = rmax(jnp.where(gm, x, neg_inf))
    gsel = rmin(jnp.where(gm, jnp.where(x == gmax, lane, LANES), LANES))
    gsum = rsum(jnp.where(gm, jnp.exp(x - gmax), 0.0))
    g_w = 1.0 / gsum
    lo = N_GROUPS + gsel * EXPERTS_PER_GROUP
    em = jnp.logical_and(lane >= lo, lane < lo + EXPERTS_PER_GROUP)
    emax = rmax(jnp.where(em, x, neg_inf))
    ee = jnp.where(em, jnp.exp(x - emax), 0.0)
    p = ee / rsum(ee)
    p1 = rmax(jnp.where(em, p, -1.0))
    i1 = rmin(jnp.where(em, jnp.where(p == p1, lane, LANES), LANES))
    em2 = jnp.logical_and(em, lane != i1)
    p2 = rmax(jnp.where(em2, p, -1.0))
    i2 = rmin(jnp.where(em2, jnp.where(p == p2, lane, LANES), LANES))
    den = p1 + p2
    g1 = g_w * (p1 / den)
    g2 = g_w * (p2 / den)

    oh1 = jnp.where(lane == i1, 1.0, 0.0)
    oh2 = jnp.where(lane == i2, 1.0, 0.0)
    oh = oh1 + oh2
    r = lax.broadcasted_iota(I32, (tm, tm), 0)
    c = lax.broadcasted_iota(I32, (tm, tm), 1)
    lower = jnp.where(c < r, 1.0, 0.0).astype(BF16)
    before = _dot(lower, oh.astype(BF16)) + carry[...]
    rank1 = rsum(oh1 * before)
    rank2 = rsum(oh2 * before)
    carry[...] = carry[...] + jnp.sum(oh, axis=0, keepdims=True)

    e1 = (i1 - N_GROUPS).astype(F32)
    e2 = (i2 - N_GROUPS).astype(F32)
    vals = [e1, e2, rank1, rank2, g1, g2]
    out = jnp.zeros((tm, LANES), F32)
    for k, v in enumerate(vals):
        out = jnp.where(lane == k, v, out)
    out_ref[...] = out
    cnt_ref[...] = carry[...]


def moe_routing(logits, tm):
    t = logits.shape[0]
    return pl.pallas_call(
        functools.partial(_routing_body, tm=tm),
        grid=(t // tm,),
        in_specs=[pl.BlockSpec((tm, LANES), lambda i: (i, 0))],
        out_specs=[pl.BlockSpec((tm, LANES), lambda i: (i, 0)), pl.BlockSpec((1, LANES), lambda i: (0, 0))],
        out_shape=[jax.ShapeDtypeStruct((t, LANES), F32), jax.ShapeDtypeStruct((1, LANES), F32)],
        scratch_shapes=[pltpu.VMEM((1, LANES), F32)],
        compiler_params=_cparams(("arbitrary",)),
        name="moe_routing",
    )(logits)


def _slot_tokens_body(slots_ref, tok_ref):
    def clear(i, c):
        tok_ref[i] = 0
        return c

    lax.fori_loop(0, tok_ref.shape[0], clear, 0, unroll=8)

    def put(a, c):
        tok_ref[slots_ref[a]] = lax.shift_right_logical(a, 1)
        return c

    lax.fori_loop(0, slots_ref.shape[0], put, 0, unroll=8)


def moe_slot_tokens(slots, n_slots):
    return pl.pallas_call(
        _slot_tokens_body,
        in_specs=[pl.BlockSpec(memory_space=pltpu.SMEM)],
        out_specs=pl.BlockSpec(memory_space=pltpu.SMEM),
        out_shape=jax.ShapeDtypeStruct((n_slots,), I32),
        name="moe_slot_tokens",
    )(slots)


def _dispatch_body(nslot_ref, tok_ref, x_hbm, xs_ref, buf, sem, *, rows):
    i = pl.program_id(0)
    ni = pl.num_programs(0)
    n_used = nslot_ref[0]

    def issue(blk, s):
        @pl.when(blk * rows < n_used)
        def _():
            def one(r, carry):
                pltpu.make_async_copy(x_hbm.at[pl.ds(tok_ref[blk * rows + r], 1), :],
                                      buf.at[s, pl.ds(r, 1), :], sem.at[s]).start()
                return carry

            lax.fori_loop(0, rows, one, 0, unroll=8)

    @pl.when(i == 0)
    def _():
        issue(0, 0)

    @pl.when(i + 1 < ni)
    def _():
        issue(i + 1, (i + 1) % 2)

    slot = i % 2

    @pl.when(i * rows < n_used)
    def _():
        pltpu.make_async_copy(x_hbm.at[pl.ds(0, rows), :], buf.at[slot], sem.at[slot]).wait()
        xs_ref[...] = buf[slot].astype(xs_ref.dtype)

    @pl.when(i * rows >= n_used)
    def _():
        xs_ref[...] = jnp.zeros(xs_ref.shape, xs_ref.dtype)


def moe_dispatch(xn, slot_tok, n_used, rows):
    d = xn.shape[1]
    n_slots = slot_tok.shape[0]
    assert n_slots % rows == 0
    grid_spec = pltpu.PrefetchScalarGridSpec(
        num_scalar_prefetch=2,
        grid=(n_slots // rows,),
        in_specs=[pl.BlockSpec(memory_space=pl.ANY)],
        out_specs=pl.BlockSpec((rows, d), lambda i, ns, tk: (i, 0)),
        scratch_shapes=[pltpu.VMEM((2, rows, d), F32), pltpu.SemaphoreType.DMA((2,))],
    )
    return pl.pallas_call(
        functools.partial(_dispatch_body, rows=rows),
        grid_spec=grid_spec,
        out_shape=jax.ShapeDtypeStruct((n_slots, d), BF16),
        compiler_params=_cparams(("arbitrary",), VMEM_LIMIT),
        name="moe_dispatch",
    )(n_used, slot_tok, xn)


def _experts_body(blk0_ref, nblk_ref, ntot_ref, xs_hbm, wg_ref, wu_ref, wd_ref, ys_hbm,
                  xbuf, ybuf, zbuf, xsem, ysem, zsem, wgb, wub, wdb, *, bm):
    e = pl.program_id(0)
    n_total = ntot_ref[0]

    def x_copy(g, s):
        return pltpu.make_async_copy(xs_hbm.at[pl.ds(g * bm, bm), :], xbuf.at[s], xsem.at[s])

    def y_copy(g, s):
        return pltpu.make_async_copy(ybuf.at[s], ys_hbm.at[pl.ds(g * bm, bm), :], ysem.at[s])

    @pl.when(jnp.logical_and(e == 0, n_total > 0))
    def _():
        x_copy(0, 0).start()

    @pl.when(nblk_ref[e] > 0)
    def _():
        wgb[...] = wg_ref[0, 0].astype(BF16)
        wub[...] = wu_ref[0, 0].astype(BF16)
        wdb[...] = wd_ref[0, 0].astype(BF16)

    def block(i, carry):
        g = blk0_ref[e] + i
        s = g % 2

        @pl.when(g + 1 < n_total)
        def _():
            x_copy(g + 1, 1 - s).start()

        x_copy(g, s).wait()
        x = xbuf[s]
        gt = _dot(x, wgb[...])
        up = _dot(x, wub[...])
        a = (gt * (1.0 / (1.0 + jnp.exp(-gt)))) * up
        y = _dot(a.astype(BF16), wdb[...])

        @pl.when(g >= 2)
        def _():
            y_copy(g - 2, s).wait()

        ybuf[s] = y
        y_copy(g, s).start()
        return carry

    lax.fori_loop(0, nblk_ref[e], block, 0)

    @pl.when(e == pl.num_programs(0) - 1)
    def _():
        for back in (2, 1):
            @pl.when(n_total >= back)
            def _():
                y_copy(n_total - back, (n_total - back) % 2).wait()

        def z_copy(g):
            return pltpu.make_async_copy(zbuf, ys_hbm.at[pl.ds(g * bm, bm), :], zsem)

        zbuf[...] = jnp.zeros(zbuf.shape, F32)
        n_blocks = ys_hbm.shape[0] // bm
        lax.fori_loop(n_total, n_blocks, lambda g, c: (z_copy(g).start(), c)[1], 0)
        lax.fori_loop(n_total, n_blocks, lambda g, c: (z_copy(g).wait(), c)[1], 0)


def moe_experts(xs, blk0, nblk, n_total, w_gate, w_up, w_down, layer, bm):
    n_slots, d = xs.shape
    n_exp = blk0.shape[0]
    de = w_gate.shape[-1]
    grid_spec = pltpu.PrefetchScalarGridSpec(
        num_scalar_prefetch=3,
        grid=(n_exp,),
        in_specs=[pl.BlockSpec(memory_space=pl.ANY),
                  pl.BlockSpec((1, 1, d, de), lambda e, b0, nb, nt: (layer, e, 0, 0)),
                  pl.BlockSpec((1, 1, d, de), lambda e, b0, nb, nt: (layer, e, 0, 0)),
                  pl.BlockSpec((1, 1, de, d), lambda e, b0, nb, nt: (layer, e, 0, 0))],
        out_specs=pl.BlockSpec(memory_space=pl.ANY),
        scratch_shapes=[pltpu.VMEM((2, bm, d), BF16), pltpu.VMEM((2, bm, d), F32), pltpu.VMEM((bm, d), F32),
                        pltpu.SemaphoreType.DMA((2,)), pltpu.SemaphoreType.DMA((2,)), pltpu.SemaphoreType.DMA,
                        pltpu.VMEM((d, de), BF16), pltpu.VMEM((d, de), BF16), pltpu.VMEM((de, d), BF16)],
    )
    return pl.pallas_call(
        functools.partial(_experts_body, bm=bm),
        grid_spec=grid_spec,
        out_shape=jax.ShapeDtypeStruct((n_slots, d), F32),
        compiler_params=_cparams(("arbitrary",), VMEM_LIMIT),
        name="moe_experts",
    )(blk0, nblk, n_total, xs, w_gate, w_up, w_down)


def _combine_body(slot_ref, ys_hbm, h_ref, rt_ref, g_ref, *rest, tm, tile0, want_h):
    outs, (buf, sem) = rest[:-2], rest[-2:]
    i = pl.program_id(0)
    ni = pl.num_programs(0)

    def row_copy(src_row, k, r, s):
        return pltpu.make_async_copy(ys_hbm.at[pl.ds(src_row, 1), :], buf.at[s, k, pl.ds(r, 1), :], sem.at[s])

    def issue(tile, s):
        def one(r, carry):
            base = ((tile0 + tile) * tm + r) * 2
            row_copy(slot_ref[base], 0, r, s).start()
            row_copy(slot_ref[base + 1], 1, r, s).start()
            return carry

        lax.fori_loop(0, tm, one, 0, unroll=8)

    @pl.when(i == 0)
    def _():
        issue(0, 0)

    @pl.when(i + 1 < ni)
    def _():
        issue(i + 1, (i + 1) % 2)

    slot = i % 2
    for k in range(2):
        pltpu.make_async_copy(ys_hbm.at[pl.ds(0, tm), :], buf.at[slot, k], sem.at[slot]).wait()

    rt = rt_ref[...]
    g1 = rt[:, 4:5]
    g2 = rt[:, 5:6]
    h2 = h_ref[...] + (buf[slot, 0] * g1 + buf[slot, 1] * g2)
    if want_h:
        outs[0][...] = h2
    outs[-1][...] = _rms(h2, g_ref[...]).astype(outs[-1].dtype)


def moe_combine(ys, slots, h1, route, g_next, tm, u_dtype, row0=0, nrows=None, want_h=True):
    d = h1.shape[1]
    t = h1.shape[0] if nrows is None else nrows
    assert row0 % tm == 0 and t % tm == 0
    tile0 = row0 // tm
    row_spec = pl.BlockSpec((tm, d), lambda i, sl: (i, 0))
    grid_spec = pltpu.PrefetchScalarGridSpec(
        num_scalar_prefetch=1,
        grid=(t // tm,),
        in_specs=[pl.BlockSpec(memory_space=pl.ANY),
                  pl.BlockSpec((tm, d), lambda i, sl: (i + tile0, 0)),
                  pl.BlockSpec((tm, LANES), lambda i, sl: (i + tile0, 0)),
                  pl.BlockSpec((1, d), lambda i, sl: (0, 0))],
        out_specs=[row_spec, row_spec] if want_h else [row_spec],
        scratch_shapes=[pltpu.VMEM((2, 2, tm, d), F32), pltpu.SemaphoreType.DMA((2,))],
    )
    shapes = [jax.ShapeDtypeStruct((t, d), F32)] if want_h else []
    return pl.pallas_call(
        functools.partial(_combine_body, tm=tm, tile0=tile0, want_h=want_h),
        grid_spec=grid_spec,
        out_shape=shapes + [jax.ShapeDtypeStruct((t, d), u_dtype)],
        compiler_params=_cparams(("arbitrary",), VMEM_LIMIT),
        name="moe_combine",
    )(slots, ys, h1, route, g_next.reshape(1, d))


def hier_moe_layer(h1, xn, logits, w_gate, w_up, w_down, layer, g_next, u_dtype, split=None):
    t = h1.shape[0]
    bm = MOE_BM
    route, counts = moe_routing(logits, _tile(t, 256))
    e = route[:, 0:2].astype(I32)
    rank = route[:, 2:4].astype(I32)
    counts = counts[0, N_GROUPS:N_GROUPS + N_EXPERTS].astype(I32)
    padded = (counts + bm - 1) // bm * bm
    pad_end = jnp.cumsum(padded)
    pad_start = pad_end - padded
    slots = (pad_start[e] + rank).reshape(-1)
    nb = (2 * t) // bm + N_EXPERTS
    slot_tok = moe_slot_tokens(slots, nb * bm)
    n_used = pad_end[-1:]
    xs = moe_dispatch(xn, slot_tok, n_used, bm * max(f for f in (4, 2, 1) if nb % f == 0))
    ys = moe_experts(xs, pad_start // bm, padded // bm, n_used // bm, w_gate, w_up, w_down, layer, bm)
    tmc = _tile(t, 256)
    if split is None:
        return moe_combine(ys, slots, h1, route, g_next, tmc, u_dtype)
    (u_a,) = moe_combine(ys, slots, h1, route, g_next, tmc, u_dtype, 0, split, want_h=False)
    (u_b,) = moe_combine(ys, slots, h1, route, g_next, tmc, u_dtype, split, t - split, want_h=False)
    return u_a, u_b


def _swap_halves(x):
    lane = lax.broadcasted_iota(I32, x.shape, 1)
    return jnp.where(lane < QK_ROPE // 2, pltpu.roll(x, LANES - QK_ROPE // 2, 1), pltpu.roll(x, QK_ROPE // 2, 1))


def _mla_prep_body(p_ref, gq_ref, gkv_ref, cos_ref, sin_ref, cq_ref, ckv_ref, ckvb_ref, kr_ref):
    p = p_ref[...]
    cq_ref[...] = _rms(p[:, :Q_LORA], gq_ref[...]).astype(cq_ref.dtype)
    ckv = _rms(p[:, Q_LORA:Q_LORA + KV_LORA], gkv_ref[...])
    ckv_ref[...] = ckv
    ckvb_ref[...] = ckv.astype(ckvb_ref.dtype)
    kr = p[:, Q_LORA + KV_LORA:]
    kr_ref[...] = kr * cos_ref[...] + _swap_halves(kr) * sin_ref[...]


def mla_prep(proj, g_q, g_kv, cos, sin, tm):
    t = proj.shape[0]
    return pl.pallas_call(
        _mla_prep_body,
        grid=(t // tm,),
        in_specs=[pl.BlockSpec((tm, proj.shape[1]), lambda i: (i, 0)),
                  pl.BlockSpec((1, Q_LORA), lambda i: (0, 0)),
                  pl.BlockSpec((1, KV_LORA), lambda i: (0, 0)),
                  pl.BlockSpec((tm, LANES), lambda i: (i, 0)),
                  pl.BlockSpec((tm, LANES), lambda i: (i, 0))],
        out_specs=[pl.BlockSpec((tm, Q_LORA), lambda i: (i, 0)),
                   pl.BlockSpec((tm, KV_LORA), lambda i: (i, 0)),
                   pl.BlockSpec((tm, KV_LORA), lambda i: (i, 0)),
                   pl.BlockSpec((tm, LANES), lambda i: (i, 0))],
        out_shape=[jax.ShapeDtypeStruct((t, Q_LORA), BF16), jax.ShapeDtypeStruct((t, KV_LORA), F32),
                   jax.ShapeDtypeStruct((t, KV_LORA), BF16), jax.ShapeDtypeStruct((t, LANES), F32)],
        compiler_params=_cparams(("parallel",)),
        name="mla_prep",
    )(proj, g_q.reshape(1, -1), g_kv.reshape(1, -1), cos, sin)


def _mla_q_body(cq_ref, w_ref, cos_ref, sin_ref, q_ref, wb):
    @pl.when(pl.program_id(0) == 0)
    def _():
        wb[...] = w_ref[...].astype(BF16)

    cq = cq_ref[...]
    cos = cos_ref[...]
    sin = sin_ref[...]
    for h in range(B_HEADS):
        y = _dot(cq, wb[:, h * MLA_QK_PAD:(h + 1) * MLA_QK_PAD])
        xr = y[:, LANES:]
        q_ref[:, h * MLA_QK_PAD:h * MLA_QK_PAD + LANES] = y[:, :LANES].astype(q_ref.dtype)
        q_ref[:, h * MLA_QK_PAD + LANES:(h + 1) * MLA_QK_PAD] = \
            (xr * cos + _swap_halves(xr) * sin).astype(q_ref.dtype)


def mla_q_up(cq, w_uq_pad, cos, sin, tm):
    t = cq.shape[0]
    n = B_HEADS * MLA_QK_PAD
    return pl.pallas_call(
        _mla_q_body,
        grid=(t // tm,),
        in_specs=[pl.BlockSpec((tm, Q_LORA), lambda i: (i, 0)),
                  pl.BlockSpec((Q_LORA, n), lambda i: (0, 0)),
                  pl.BlockSpec((tm, LANES), lambda i: (i, 0)),
                  pl.BlockSpec((tm, LANES), lambda i: (i, 0))],
        out_specs=pl.BlockSpec((tm, n), lambda i: (i, 0)),
        out_shape=jax.ShapeDtypeStruct((t, n), BF16),
        scratch_shapes=[pltpu.VMEM((Q_LORA, n), BF16)],
        compiler_params=_cparams(("arbitrary",), VMEM_LIMIT),
        name="mla_q_up",
    )(cq, w_uq_pad, cos, sin)


def _mla_kv_body(c_ref, kr_ref, wk_ref, wv_ref, k_ref, v_ref, wkb, wvb):
    @pl.when(pl.program_id(0) == 0)
    def _():
        wkb[...] = wk_ref[0].astype(BF16)
        wvb[...] = wv_ref[0].astype(BF16)

    c = c_ref[...].astype(BF16)
    kr = kr_ref[...].astype(k_ref.dtype)
    kn = _dot(c, wkb[...])
    for h in range(B_HEADS):
        k_ref[:, h * MLA_QK_PAD:h * MLA_QK_PAD + LANES] = kn[:, h * QK_NOPE:(h + 1) * QK_NOPE].astype(k_ref.dtype)
        k_ref[:, h * MLA_QK_PAD + LANES:(h + 1) * MLA_QK_PAD] = kr
    v_ref[...] = _dot(c, wvb[...]).astype(v_ref.dtype)


def mla_kv_up(ckv, kr, w_uk, w_uv, layer, tm):
    t = ckv.shape[0]
    nk = B_HEADS * MLA_QK_PAD
    nv = B_HEADS * V_DIM
    return pl.pallas_call(
        _mla_kv_body,
        grid=(t // tm,),
        in_specs=[pl.BlockSpec((tm, KV_LORA), lambda i: (i, 0)),
                  pl.BlockSpec((tm, LANES), lambda i: (i, 0)),
                  pl.BlockSpec((1, KV_LORA, B_HEADS * QK_NOPE), lambda i: (layer, 0, 0)),
                  pl.BlockSpec((1, KV_LORA, nv), lambda i: (layer, 0, 0))],
        out_specs=[pl.BlockSpec((tm, nk), lambda i: (i, 0)), pl.BlockSpec((tm, nv), lambda i: (i, 0))],
        out_shape=[jax.ShapeDtypeStruct((t, nk), BF16), jax.ShapeDtypeStruct((t, nv), BF16)],
        scratch_shapes=[pltpu.VMEM((KV_LORA, B_HEADS * QK_NOPE), BF16), pltpu.VMEM((KV_LORA, nv), BF16)],
        compiler_params=_cparams(("arbitrary",), VMEM_LIMIT),
        name="mla_kv_up",
    )(ckv, kr, w_uk, w_uv)


def kernel(x_prompt, x_sample, cache_a_k, cache_a_v, cache_a_kidx, cache_b_ckv, cache_b_krope, norm_mix, norm_ffn, norm_final, rel_bias, a_w_in, a_w_out, b_w_in, b_norm_q, b_norm_kv, b_w_uq, b_w_uk, b_w_uv, b_w_out, moe_w_grp, moe_b_grp, moe_w_rtr, moe_b_rtr, moe_w_gate, moe_w_up, moe_w_down):
    n_p, s_p, d = x_prompt.shape
    n_s, s_s, _ = x_sample.shape
    past = cache_a_k.shape[2]
    tp = n_p * s_p
    ts = n_s * s_s
    t = tp + ts
    a_qd = A_HEADS * A_HEAD_DIM
    tm = _tile(math.gcd(tp, ts), 1024)
    tm2 = _tile(math.gcd(tp, ts), 512)

    x_parts = [x_prompt.reshape(tp, d), x_sample.reshape(ts, d)]

    def router_params(i):
        w_r = jnp.concatenate([moe_w_grp[i], moe_w_rtr[i],
                               jnp.zeros((d, LANES - N_GROUPS - N_EXPERTS), F32)], axis=1)
        b_r = jnp.concatenate([moe_b_grp[i], moe_b_rtr[i],
                               jnp.zeros((LANES - N_GROUPS - N_EXPERTS,), F32)]).reshape(1, LANES)
        return w_r, b_r

    u0 = rmsnorm(x_parts, norm_mix[0], tm)
    w_a = a_w_in[0]
    (q_b,) = matmul(u0, w_a, 0, a_qd, [BF16], tm, 512, "a_proj_q")
    tmp, tms = _tile(tp, 1024), _tile(ts, 1024)
    kp_f, kp_b = matmul(u0, w_a, a_qd, a_qd, [F32, BF16], tmp, 512, "a_proj_k", 0, tp)
    vp_f, vp_b = matmul(u0, w_a, 2 * a_qd, a_qd, [F32, BF16], tmp, 512, "a_proj_v", 0, tp)
    ks_f, ks_b = matmul(u0, w_a, a_qd, a_qd, [F32, BF16], tms, 512, "a_proj_k", tp, ts)
    vs_f, vs_b = matmul(u0, w_a, 2 * a_qd, a_qd, [F32, BF16], tms, 512, "a_proj_v", tp, ts)
    (qi_b,) = matmul(u0, w_a, 3 * a_qd, IDX_HEADS * IDX_DIM, [BF16], tm, 512, "a_proj_qi")
    c_ki = 3 * a_qd + IDX_HEADS * IDX_DIM
    w_ki = w_a[:, c_ki:c_ki + IDX_DIM]
    w_wi = w_a[:, c_ki + IDX_DIM:c_ki + IDX_DIM + IDX_HEADS]
    zk = jnp.zeros((d, IDX_DIM), F32)
    w_tail = jnp.concatenate([w_ki, zk, zk, w_ki, w_wi, jnp.zeros((d, LANES - IDX_HEADS), F32)], axis=1)
    (tail,) = matmul(u0, w_tail, 0, 3 * LANES, [F32], tm, 3 * LANES, "a_proj_tail")
    kidx = tail[:, :IDX_DIM]

    bias_tab = rel_bias_tables(rel_bias)
    a_scale = A_HEAD_DIM ** -0.5
    hg = 8
    mask_p = dsa_select(qi_b, tail, 2, tail, n_seq=n_p, sq=s_p, bq=128, sk=s_p, sk_real=s_p, tk=512,
                        pos0=0, q_row0=0)
    att_p = attention(q_b, [(kp_b, vp_b, s_p, 0, 0)], bias_tab, mask_p, n_seq=n_p, sq=s_p, bq=128, q_row0=0,
                      q_col0=0, n_heads=A_HEADS, hg=8, dq=A_HEAD_DIM, dv=A_HEAD_DIM, scale=a_scale, pos0=0,
                      dyn_sk=s_p, name="dsa_attention_prompt")
    sk_s = past + s_s
    sk_pad = (sk_s + ATT_TILE - 1) // ATT_TILE * ATT_TILE
    ki_past = cache_a_kidx[0]
    zp = jnp.zeros_like(ki_past)
    kc_past = jnp.concatenate([ki_past, zp, zp, ki_past], axis=-1)
    kc_new = tail[tp:, :2 * LANES].reshape(n_s, s_s, 2 * LANES)
    kc_s = jnp.concatenate([kc_past, kc_new, jnp.zeros((n_s, sk_pad - sk_s, 2 * LANES), F32)], axis=1)
    mask_s = dsa_select(qi_b, tail, 2, kc_s.reshape(n_s * sk_pad, 2 * LANES), n_seq=n_s, sq=s_s, bq=s_s,
                        sk=sk_pad, sk_real=sk_s, tk=ATT_TILE, pos0=past, q_row0=tp)
    tiles_s = [(0, r, r, ATT_TILE) for r in range(0, past, ATT_TILE)] + [(1, 0, past, s_s)]
    att_s = attention(q_b, [(cache_a_k[0].reshape(n_s * past, a_qd), cache_a_v[0].reshape(n_s * past, a_qd),
                             past, 0, 0), (ks_b, vs_b, s_s, 0, 0)],
                      bias_tab, mask_s, n_seq=n_s, sq=s_s, bq=s_s, q_row0=tp, q_col0=0, n_heads=A_HEADS, hg=hg,
                      dq=A_HEAD_DIM, dv=A_HEAD_DIM, scale=a_scale, pos0=past, seg_tiles=tiles_s,
                      name="dsa_attention_sample")

    w_r0, b_r0 = router_params(0)
    h1, xn0, lg0 = outproj_norm_router([att_p, att_s], a_w_out, 0, x_parts, norm_ffn[0], w_r0, b_r0, tm2, 512)
    h2, u1 = hier_moe_layer(h1, xn0, lg0, moe_w_gate, moe_w_up, moe_w_down, 0, norm_mix[1], BF16)

    n_in = Q_LORA + KV_LORA + QK_ROPE
    w_b = jnp.concatenate([b_w_in[0], jnp.zeros((d, LANES - QK_ROPE), F32)], axis=1)
    (proj,) = matmul(u1, w_b, 0, n_in + LANES - QK_ROPE, [F32], tm, 384, "b_proj")
    half = QK_ROPE // 2
    inv = ROPE_THETA ** (-jnp.arange(half, dtype=F32) / half)
    pos_all = jnp.concatenate([jnp.tile(jnp.arange(s_p, dtype=I32), n_p),
                               jnp.tile(past + jnp.arange(s_s, dtype=I32), n_s)])
    ang = pos_all.astype(F32)[:, None] * inv[None, :]
    zl = jnp.zeros((t, LANES - QK_ROPE), F32)
    cos_t = jnp.concatenate([jnp.cos(ang), jnp.cos(ang), zl], axis=1)
    sin_t = jnp.concatenate([-jnp.sin(ang), jnp.sin(ang), zl], axis=1)
    cq_b, ckv_f, ckv_b, kr_f = mla_prep(proj, b_norm_q[0], b_norm_kv[0], cos_t, sin_t, tm)
    w_uq_pad = jnp.pad(b_w_uq[0].reshape(Q_LORA, B_HEADS, QK_NOPE + QK_ROPE),
                       ((0, 0), (0, 0), (0, MLA_QK_PAD - QK_NOPE - QK_ROPE))).reshape(Q_LORA, B_HEADS * MLA_QK_PAD)
    qm = mla_q_up(cq_b, w_uq_pad, cos_t, sin_t, tm2)
    km_p, vm_p = mla_kv_up(ckv_b[:tp], kr_f[:tp], b_w_uk, b_w_uv, 0, _tile(tp, 512))
    ctab = causal_tables()
    hgb = 8
    matt_p = attention(qm, [(km_p, vm_p, s_p, 0, 0)], ctab, None, n_seq=n_p, sq=s_p, bq=128, q_row0=0, q_col0=0,
                       n_heads=B_HEADS, hg=8, dq=MLA_QK_PAD, dv=V_DIM, scale=B_SCALE, pos0=0, dyn_sk=s_p,
                       name="mla_attention_prompt")
    ckv_all = jnp.concatenate([cache_b_ckv[0], ckv_f[tp:].reshape(n_s, s_s, KV_LORA)], axis=1)
    kr_past = jnp.concatenate([cache_b_krope[0], jnp.zeros((n_s, past, LANES - QK_ROPE), F32)], axis=-1)
    kr_all = jnp.concatenate([kr_past, kr_f[tp:].reshape(n_s, s_s, LANES)], axis=1)
    km_s, vm_s = mla_kv_up(ckv_all.reshape(n_s * sk_s, KV_LORA), kr_all.reshape(n_s * sk_s, LANES),
                           b_w_uk, b_w_uv, 0, sk_s)
    tiles_m = [(0, r, r, ATT_TILE) for r in range(0, past, ATT_TILE)] + [(0, past, past, s_s)]
    matt_s = attention(qm, [(km_s, vm_s, sk_s, 0, 0)], ctab, None, n_seq=n_s, sq=s_s, bq=s_s, q_row0=tp, q_col0=0,
                       n_heads=B_HEADS, hg=hgb, dq=MLA_QK_PAD, dv=V_DIM, scale=B_SCALE, pos0=past,
                       seg_tiles=tiles_m, name="mla_attention_sample")

    w_r1, b_r1 = router_params(1)
    h3, xn1, lg1 = outproj_norm_router([matt_p, matt_s], b_w_out, 0, [h2], norm_ffn[1], w_r1, b_r1, tm2, 512)
    y_p, y_s = hier_moe_layer(h3, xn1, lg1, moe_w_gate, moe_w_up, moe_w_down, 1, norm_final, F32, split=tp)
    y_prompt = y_p.reshape(n_p, s_p, d)
    y_sample = y_s.reshape(n_s, s_s, d)

    def heads(a, n, s):
        return a.reshape(1, n, s, A_HEADS, A_HEAD_DIM)

    kr_out = kr_f[:, :QK_ROPE]
    return (y_prompt, y_sample,
            heads(kp_f, n_p, s_p), heads(vp_f, n_p, s_p), kidx[:tp].reshape(1, n_p, s_p, IDX_DIM),
            ckv_f[:tp].reshape(1, n_p, s_p, KV_LORA), kr_out[:tp].reshape(1, n_p, s_p, QK_ROPE),
            heads(ks_f, n_s, s_s), heads(vs_f, n_s, s_s), kidx[tp:].reshape(1, n_s, s_s, IDX_DIM),
            ckv_f[tp:].reshape(1, n_s, s_s, KV_LORA), kr_out[tp:].reshape(1, n_s, s_s, QK_ROPE))
```

```python
import functools
import math

import jax
import jax.numpy as jnp
from jax import lax
from jax.experimental import pallas as pl
from jax.experimental.pallas import tpu as pltpu

F32 = jnp.float32
BF16 = jnp.bfloat16
I32 = jnp.int32

RMS_EPS = 1e-6
CHUNK = 64
NEG = -1e30
INT_MIN = -2 ** 31
INT_MAX = 2 ** 31 - 1

LANES = 128
KEY_BLOCK = 128
ATT_TILE = 256
VMEM_LIMIT = 56 * 1024 * 1024

A_HEADS = 16
A_HEAD_DIM = 128
IDX_HEADS = 16
IDX_DIM = 64
TOPK_MAX = 256
IDX_W_SCALE = float((IDX_HEADS * IDX_DIM) ** -0.5)
NUM_BUCKETS = 32
MAX_DISTANCE = 128
B_HEADS = 16
Q_LORA = 512
KV_LORA = 512
QK_NOPE = 128
QK_ROPE = 64
V_DIM = 128
ROPE_THETA = 10000.0
B_SCALE = float((QK_NOPE + QK_ROPE) ** -0.5)
MLA_QK_PAD = 256
N_GROUPS = 8
EXPERTS_PER_GROUP = 8
N_EXPERTS = 64
D_EXPERT = 512
MOE_BM = 128


def _tile(n, pref):
    for c in range(pref, 0, -LANES):
        if n % c == 0:
            return c
    raise ValueError(f"no 128-multiple tile divides {n}")


def _cparams(sem, vmem=None):
    return pltpu.CompilerParams(dimension_semantics=sem, vmem_limit_bytes=vmem)


def _dot(a, b):
    return jnp.dot(a, b, preferred_element_type=F32)


def _dot_nt(a, b):
    return lax.dot_general(a, b, (((1,), (1,)), ((), ())), preferred_element_type=F32)


def _rms(x, g):
    ms = jnp.mean(x * x, axis=-1, keepdims=True)
    return x * lax.rsqrt(ms + RMS_EPS) * g


def _pick_rows(i, n_first, refs):
    if len(refs) == 1:
        return refs[0][...]
    return jnp.where(i < n_first, refs[0][...], refs[1][...])


def _row_split_specs(parts, tm, width, col_of):
    n_first = parts[0].shape[0] // tm
    assert all(p.shape[0] % tm == 0 for p in parts)
    specs = [pl.BlockSpec((tm, width), lambda i, *r: (jnp.minimum(i, n_first - 1), col_of(i, *r)))]
    if len(parts) == 2:
        specs.append(pl.BlockSpec((tm, width), lambda i, *r: (jnp.maximum(i - n_first, 0), col_of(i, *r))))
    return specs, n_first


def _rmsnorm_body(*refs, n_first):
    x = _pick_rows(pl.program_id(0), n_first, refs[:-2])
    g_ref, o_ref = refs[-2:]
    o_ref[...] = _rms(x, g_ref[...]).astype(o_ref.dtype)


def rmsnorm(xs, g, tm, out_dtype=BF16):
    d = xs[0].shape[1]
    t = sum(x.shape[0] for x in xs)
    specs, n_first = _row_split_specs(xs, tm, d, lambda i: 0)
    return pl.pallas_call(
        functools.partial(_rmsnorm_body, n_first=n_first),
        grid=(t // tm,),
        in_specs=specs + [pl.BlockSpec((1, d), lambda i: (0, 0))],
        out_specs=pl.BlockSpec((tm, d), lambda i: (i, 0)),
        out_shape=jax.ShapeDtypeStruct((t, d), out_dtype),
        compiler_params=_cparams(("parallel",)),
        name="rmsnorm",
    )(*xs, g.reshape(1, d))


def _matmul_body(a_ref, w_ref, *rest):
    outs, wb = rest[:-1], rest[-1]

    @pl.when(pl.program_id(1) == 0)
    def _():
        wb[...] = w_ref[...].astype(BF16)

    r = _dot(a_ref[...], wb[...])
    for o in outs:
        o[...] = r.astype(o.dtype)


def matmul(a, w, col0, ncols, out_dtypes, tm, tn, name, row0=0, nrows=None):
    k = a.shape[1]
    t = a.shape[0] if nrows is None else nrows
    assert ncols % tn == 0 and col0 % tn == 0 and t % tm == 0 and row0 % tm == 0
    cb = col0 // tn
    rb = row0 // tm
    return pl.pallas_call(
        _matmul_body,
        grid=(ncols // tn, t // tm),
        in_specs=[pl.BlockSpec((tm, k), lambda j, i: (i + rb, 0)),
                  pl.BlockSpec((k, tn), lambda j, i: (0, j + cb))],
        out_specs=[pl.BlockSpec((tm, tn), lambda j, i: (i, j)) for _ in out_dtypes],
        out_shape=[jax.ShapeDtypeStruct((t, ncols), dt) for dt in out_dtypes],
        scratch_shapes=[pltpu.VMEM((k, tn), BF16)],
        compiler_params=_cparams(("arbitrary", "arbitrary"), VMEM_LIMIT),
        name=name,
    )(a, w)


def _float_sort_key(x):
    bits = lax.bitcast_convert_type(x, I32)
    return bits ^ ((bits >> 31) & INT_MAX)


def _select_body(qi_ref, wi_ref, kc_ref, mask_ref, key_scr, jm_scr, *, bq, sk, sk_real, tk, pos0, topk):
    b = pl.program_id(1)
    t0 = pos0 + b * bq
    kmax = jnp.minimum(sk_real, ((t0 + bq - 1) // CHUNK + 1) * CHUNK)
    nkt = (kmax + tk - 1) // tk
    nch = tk // LANES

    w = wi_ref[...] * IDX_W_SCALE
    wb = [jnp.broadcast_to(w[:, h:h + 1], (bq, LANES)) for h in range(IDX_HEADS)]
    qpos = t0 + lax.broadcasted_iota(I32, (LANES, LANES), 1)
    lim = jnp.minimum((qpos // CHUNK + 1) * CHUNK, sk_real)
    sub = lax.broadcasted_iota(I32, (LANES, LANES), 0)

    def to_lanes(x):
        if bq < LANES:
            x = jnp.concatenate([x, jnp.zeros((LANES - bq, LANES), x.dtype)], axis=0)
        return x.T

    def score_tile(j, carry):
        off = pl.multiple_of(j * tk, tk)
        kc = kc_ref[pl.ds(off, tk), :].astype(BF16)
        ka, kb = kc[:, :LANES], kc[:, LANES:]
        accs = [jnp.zeros((bq, LANES), F32) for _ in range(nch)]
        for g in range(IDX_HEADS // 2):
            qg = qi_ref[:, g * LANES:(g + 1) * LANES]
            sa = _dot_nt(qg, ka)
            sb = _dot_nt(qg, kb)
            for c in range(nch):
                sl = slice(c * LANES, (c + 1) * LANES)
                accs[c] = accs[c] + wb[2 * g] * jnp.maximum(sa[:, sl], 0.0) \
                    + wb[2 * g + 1] * jnp.maximum(sb[:, sl], 0.0)
        for c in range(nch):
            kpos = off + c * LANES + sub
            sc = jnp.where(kpos < lim, to_lanes(accs[c]), -jnp.inf)
            key_scr[pl.ds(off + c * LANES, LANES), :] = _float_sort_key(sc)
        return carry

    lax.fori_loop(0, nkt, score_tile, 0)

    def count(indicator):
        def tile(j, cnt):
            off = pl.multiple_of(j * tk, tk)
            for c in range(nch):
                kt = key_scr[pl.ds(off + c * LANES, LANES), :]
                cnt = cnt + indicator(kt, off + c * LANES + sub)
            return cnt
        cnt = lax.fori_loop(0, nkt, tile, jnp.zeros((LANES, LANES), F32))
        return jnp.sum(cnt, axis=0, keepdims=True)

    def bit_step(i, pfx_u):
        bit = lax.shift_left(jnp.int32(1), 31 - i)
        cand_u = pfx_u | bit
        cand_s = cand_u ^ INT_MIN
        total = count(lambda kt, kp: jnp.where(kt >= cand_s, 1.0, 0.0))
        return jnp.where(total >= topk, cand_u, pfx_u)

    pfx = lax.fori_loop(0, 32, bit_step, jnp.zeros((1, LANES), I32))
    thr = pfx ^ INT_MIN

    n_gt = count(lambda kt, kp: jnp.where(kt > thr, 1.0, 0.0))
    n_ge = count(lambda kt, kp: jnp.where(kt >= thr, 1.0, 0.0))
    quota = topk - n_gt
    jm_scr[...] = jnp.full((1, LANES), INT_MAX, I32)
    any_excess = jnp.max(jnp.where(n_ge > topk, 1.0, 0.0)) > 0.0

    @pl.when(any_excess)
    def _():
        nbits = max(1, int(sk - 1).bit_length())

        def idx_step(i, ans):
            cand = ans | lax.shift_left(jnp.int32(1), nbits - 1 - i)
            below = count(lambda kt, kp: jnp.where(kt == thr, jnp.where(kp < cand, 1.0, 0.0), 0.0))
            return jnp.where(below < quota, cand, ans)

        jm_scr[...] = lax.fori_loop(0, nbits, idx_step, jnp.zeros((1, LANES), I32))

    jm = jm_scr[...]

    def write_tile(j, carry):
        off = pl.multiple_of(j * tk, tk)
        for c in range(nch):
            kt = key_scr[pl.ds(off + c * LANES, LANES), :]
            kpos = off + c * LANES + sub
            v = jnp.where(kt > thr, 0.0, jnp.where(kt == thr, jnp.where(kpos <= jm, 0.0, NEG), NEG))
            v = jnp.where(kpos < lim, v, NEG)
            mask_ref[:, pl.ds(off + c * LANES, LANES)] = v.T[:bq].astype(mask_ref.dtype)
        return carry

    lax.fori_loop(0, nkt, write_tile, 0)

    def fill_tile(j, carry):
        off = pl.multiple_of(j * tk, tk)
        mask_ref[:, pl.ds(off, tk)] = jnp.full((bq, tk), NEG, mask_ref.dtype)
        return carry

    lax.fori_loop(nkt, sk // tk, fill_tile, 0)


def dsa_select(qi, wi_arr, wi_blk, kc, *, n_seq, sq, bq, sk, sk_real, tk, pos0, q_row0):
    nqb = sq // bq
    qb0 = q_row0 // bq
    assert q_row0 % bq == 0 and sk % tk == 0 and tk >= TOPK_MAX
    topk = min(TOPK_MAX, sk_real // 4)
    body = functools.partial(_select_body, bq=bq, sk=sk, sk_real=sk_real, tk=tk, pos0=pos0, topk=topk)
    return pl.pallas_call(
        body,
        grid=(n_seq, nqb),
        in_specs=[pl.BlockSpec((bq, IDX_HEADS * IDX_DIM), lambda s, b: (qb0 + s * nqb + b, 0)),
                  pl.BlockSpec((bq, LANES), lambda s, b: (qb0 + s * nqb + b, wi_blk)),
                  pl.BlockSpec((sk, 2 * LANES), lambda s, b: (s, 0))],
        out_specs=pl.BlockSpec((bq, sk), lambda s, b: (s * nqb + b, 0)),
        out_shape=jax.ShapeDtypeStruct((n_seq * sq, sk), BF16),
        scratch_shapes=[pltpu.VMEM((sk, LANES), I32), pltpu.VMEM((1, LANES), I32)],
        compiler_params=_cparams(("parallel", "arbitrary"), VMEM_LIMIT),
        name="dsa_select",
    )(qi, wi_arr, kc)


def _rel_tables_body(bias_ref, tab_ref):
    i = lax.broadcasted_iota(I32, (KEY_BLOCK, KEY_BLOCK), 0)
    j = lax.broadcasted_iota(I32, (KEY_BLOCK, KEY_BLOCK), 1)
    nb = NUM_BUCKETS // 2
    max_exact = nb // 2
    edges = [12, 16, 23, 32, 46, 64, 91]
    for d in range(4):
        rel = j - i + (d - 2) * KEY_BLOCK
        n = jnp.abs(rel)
        large = jnp.full_like(n, max_exact)
        for e in edges:
            large = large + jnp.where(n >= e, 1, 0)
        bucket = jnp.where(rel > 0, nb, 0) + jnp.where(n < max_exact, n, large)
        for h in range(A_HEADS):
            acc = jnp.zeros((KEY_BLOCK, KEY_BLOCK), F32)
            for k in range(NUM_BUCKETS):
                acc = jnp.where(bucket == k, bias_ref[k, h], acc)
            tab_ref[h, d] = acc


def rel_bias_tables(rel_bias):
    return pl.pallas_call(
        _rel_tables_body,
        in_specs=[pl.BlockSpec(memory_space=pltpu.SMEM)],
        out_shape=jax.ShapeDtypeStruct((A_HEADS, 4, KEY_BLOCK, KEY_BLOCK), F32),
        name="rel_bias_tables",
    )(rel_bias)


def _causal_tables_body(tab_ref):
    i = lax.broadcasted_iota(I32, (KEY_BLOCK, KEY_BLOCK), 0)
    j = lax.broadcasted_iota(I32, (KEY_BLOCK, KEY_BLOCK), 1)
    zero = jnp.zeros((KEY_BLOCK, KEY_BLOCK), F32)
    tab_ref[0, 0] = zero
    tab_ref[0, 1] = zero
    tab_ref[0, 2] = jnp.where(j // CHUNK <= i // CHUNK, 0.0, NEG)
    tab_ref[0, 3] = jnp.full((KEY_BLOCK, KEY_BLOCK), NEG, F32)


def causal_tables():
    return pl.pallas_call(
        _causal_tables_body,
        out_shape=jax.ShapeDtypeStruct((1, 4, KEY_BLOCK, KEY_BLOCK), F32),
        name="causal_tables",
    )()


def _attn_pipe_body(*refs, has_mask, per_head_tab, bq, hg, dq, dv, scale, pos0, sk):
    q_ref, k_ref, v_ref, tab_ref = refs[:4]
    mask_ref = refs[4] if has_mask else None
    out_ref = refs[5] if has_mask else refs[4]
    s_scr, p_scr, m_scr, l_scr, acc_scr = refs[-5:]

    b = pl.program_id(2)
    t0 = pos0 + b * bq
    qblk = t0 // KEY_BLOCK
    kmax = jnp.minimum(sk, ((t0 + bq - 1) // CHUNK + 1) * CHUNK)
    nkt = (kmax + ATT_TILE - 1) // ATT_TILE
    nch = ATT_TILE // LANES

    m_scr[...] = jnp.full(m_scr.shape, NEG, F32)
    l_scr[...] = jnp.zeros(l_scr.shape, F32)
    acc_scr[...] = jnp.zeros(acc_scr.shape, F32)

    def logits_tile(j, carry):
        off = pl.multiple_of(j * ATT_TILE, ATT_TILE)
        kb0 = off // KEY_BLOCK
        ds = [jnp.clip(kb0 + c - qblk + 2, 0, 3) for c in range(nch)]
        if has_mask:
            mk = mask_ref[:, pl.ds(off, ATT_TILE)].astype(F32)
        for h in range(hg):
            s = _dot(q_ref[:, h * dq:(h + 1) * dq], k_ref[h * dq:(h + 1) * dq, pl.ds(off, ATT_TILE)]) * scale
            hh = h if per_head_tab else 0
            s = s + jnp.concatenate([tab_ref[hh, ds[c]] for c in range(nch)], axis=1)
            if has_mask:
                s = s + mk
            s_scr[h, :, pl.ds(off, ATT_TILE)] = s
            mvec = m_scr[h]
            for c in range(nch):
                mvec = jnp.maximum(mvec, s[:, c * LANES:(c + 1) * LANES])
            m_scr[h] = mvec
        return carry

    lax.fori_loop(0, nkt, logits_tile, 0)
    for h in range(hg):
        m_scr[h] = jnp.broadcast_to(jnp.max(m_scr[h], axis=1, keepdims=True), (bq, LANES))

    def exp_tile(j, carry):
        off = pl.multiple_of(j * ATT_TILE, ATT_TILE)
        for h in range(hg):
            s = s_scr[h, :, pl.ds(off, ATT_TILE)]
            m = m_scr[h]
            p = [jnp.exp(s[:, c * LANES:(c + 1) * LANES] - m) for c in range(nch)]
            lvec = l_scr[h]
            for c in range(nch):
                lvec = lvec + p[c]
            l_scr[h] = lvec
            p_scr[h, :, pl.ds(off, ATT_TILE)] = jnp.concatenate(p, axis=1).astype(BF16)
        return carry

    lax.fori_loop(0, nkt, exp_tile, 0)

    def pv_tile(j, carry):
        off = pl.multiple_of(j * ATT_TILE, ATT_TILE)
        for h in range(hg):
            acc_scr[h] = acc_scr[h] + _dot(p_scr[h, :, pl.ds(off, ATT_TILE)],
                                           v_ref[pl.ds(off, ATT_TILE), h * dv:(h + 1) * dv])
        return carry

    lax.fori_loop(0, nkt, pv_tile, 0)
    for h in range(hg):
        l_row = jnp.sum(l_scr[h], axis=1, keepdims=True)
        out_ref[:, h * dv:(h + 1) * dv] = (acc_scr[h] / l_row).astype(out_ref.dtype)


def _attn_body(*refs, n_seg, has_mask, per_head_tab, bq, hg, dq, dv, scale, pos0, seg_tiles):
    q_ref = refs[0]
    kv = refs[1:1 + 2 * n_seg]
    pos = 1 + 2 * n_seg
    tab_ref = refs[pos]
    pos += 1
    mask_ref = refs[pos] if has_mask else None
    pos += 1 if has_mask else 0
    out_ref = refs[pos]
    s_scr, p_scr = refs[pos + 1:pos + 3]

    b = pl.program_id(2)
    t0 = pos0 + b * bq
    qblk = t0 // KEY_BLOCK

    def lane_pad(x, fill):
        w = x.shape[1]
        return x if w == LANES else jnp.concatenate([x, jnp.full((bq, LANES - w), fill, F32)], axis=1)

    mvec = [jnp.full((bq, LANES), NEG, F32) for _ in range(hg)]
    for (si, row0, key0, width) in seg_tiles:
        nch = max(1, width // KEY_BLOCK)
        cw = min(width, KEY_BLOCK)
        ds = [jnp.clip(key0 // KEY_BLOCK + c - qblk + 2, 0, 3) for c in range(nch)]
        k_t = kv[2 * si][row0:row0 + width, :].astype(BF16)
        if has_mask:
            mk = mask_ref[:, key0:key0 + width].astype(F32)
        for h in range(hg):
            hh = h if per_head_tab else 0
            s = _dot_nt(q_ref[:, h * dq:(h + 1) * dq], k_t[:, h * dq:(h + 1) * dq]) * scale
            bias = [tab_ref[hh, ds[c], :bq, :cw] for c in range(nch)]
            s = s + (jnp.concatenate(bias, axis=1) if nch > 1 else bias[0])
            if has_mask:
                s = s + mk
            s_scr[h, :, key0:key0 + width] = s
            for c in range(nch):
                mvec[h] = jnp.maximum(mvec[h], lane_pad(s[:, c * cw:(c + 1) * cw], NEG))

    m = [jnp.broadcast_to(jnp.max(mvec[h], axis=1, keepdims=True), (bq, LANES)) for h in range(hg)]
    lvec = [jnp.zeros((bq, LANES), F32) for _ in range(hg)]
    for (si, row0, key0, width) in seg_tiles:
        nch = max(1, width // KEY_BLOCK)
        cw = min(width, KEY_BLOCK)
        for h in range(hg):
            s = s_scr[h, :, key0:key0 + width]
            p = [jnp.exp(s[:, c * cw:(c + 1) * cw] - m[h][:, :cw]) for c in range(nch)]
            for c in range(nch):
                lvec[h] = lvec[h] + lane_pad(p[c], 0.0)
            p_scr[h, :, key0:key0 + width] = (jnp.concatenate(p, axis=1) if nch > 1 else p[0]).astype(BF16)

    acc = [jnp.zeros((bq, dv), F32) for _ in range(hg)]
    for (si, row0, key0, width) in seg_tiles:
        v_t = kv[2 * si + 1][row0:row0 + width, :].astype(BF16)
        for h in range(hg):
            acc[h] = acc[h] + _dot(p_scr[h, :, key0:key0 + width], v_t[:, h * dv:(h + 1) * dv])

    for h in range(hg):
        l_row = jnp.sum(lvec[h], axis=1, keepdims=True)
        out_ref[:, h * dv:(h + 1) * dv] = (acc[h] / l_row).astype(out_ref.dtype)


def attention(q, segs, tab, mask, *, n_seq, sq, bq, q_row0, q_col0, n_heads, hg, dq, dv, scale, pos0,
              seg_tiles=None, dyn_sk=None, name="attention"):
    nqb = sq // bq
    ng = n_heads // hg
    qb0 = q_row0 // bq
    assert q_row0 % bq == 0 and q_col0 % (hg * dq) == 0 and pos0 % KEY_BLOCK == 0
    assert bq == KEY_BLOCK or nqb == 1
    qc0 = q_col0 // (hg * dq)
    in_specs = [pl.BlockSpec((bq, hg * dq), lambda s, g, b: (qb0 + s * nqb + b, qc0 + g))]
    args = [q]
    for (k, v, rows, kc0, vc0) in segs:
        assert kc0 % (hg * dq) == 0 and vc0 % (hg * dv) == 0
        mode = dict(pipeline_mode=pl.Buffered(1)) if dyn_sk is not None else {}
        if dyn_sk is not None:
            assert kc0 == 0 and k.shape == (n_seq * n_heads * dq, rows)
            in_specs.append(pl.BlockSpec((hg * dq, rows), lambda s, g, b: (s * ng + g, 0), **mode))
        else:
            in_specs.append(pl.BlockSpec((rows, hg * dq), lambda s, g, b, c=kc0 // (hg * dq): (s, c + g)))
        in_specs.append(pl.BlockSpec((rows, hg * dv), lambda s, g, b, c=vc0 // (hg * dv): (s, c + g), **mode))
        args += [k, v]
    per_head_tab = tab.shape[0] > 1
    if per_head_tab:
        in_specs.append(pl.BlockSpec((hg, 4, KEY_BLOCK, KEY_BLOCK), lambda s, g, b: (g, 0, 0, 0)))
    else:
        in_specs.append(pl.BlockSpec((1, 4, KEY_BLOCK, KEY_BLOCK), lambda s, g, b: (0, 0, 0, 0)))
    args.append(tab)
    if mask is not None:
        in_specs.append(pl.BlockSpec((bq, mask.shape[1]), lambda s, g, b: (s * nqb + b, 0)))
        args.append(mask)
    if dyn_sk is not None:
        assert len(segs) == 1 and dyn_sk % ATT_TILE == 0 and bq == KEY_BLOCK
        body = functools.partial(_attn_pipe_body, has_mask=mask is not None, per_head_tab=per_head_tab,
                                 bq=bq, hg=hg, dq=dq, dv=dv, scale=scale, pos0=pos0, sk=dyn_sk)
        scratch = [pltpu.VMEM((hg, bq, dyn_sk), F32), pltpu.VMEM((hg, bq, dyn_sk), BF16),
                   pltpu.VMEM((hg, bq, LANES), F32), pltpu.VMEM((hg, bq, LANES), F32),
                   pltpu.VMEM((hg, bq, dv), F32)]
    else:
        body = functools.partial(_attn_body, n_seg=len(segs), has_mask=mask is not None,
                                 per_head_tab=per_head_tab, bq=bq, hg=hg, dq=dq, dv=dv, scale=scale, pos0=pos0,
                                 seg_tiles=seg_tiles)
        sk_tot = max(key0 + width for (_, _, key0, width) in seg_tiles)
        sk_tot = (sk_tot + LANES - 1) // LANES * LANES
        scratch = [pltpu.VMEM((hg, bq, sk_tot), F32), pltpu.VMEM((hg, bq, sk_tot), BF16)]
    return pl.pallas_call(
        body,
        grid=(n_seq, ng, nqb),
        in_specs=in_specs,
        out_specs=pl.BlockSpec((bq, hg * dv), lambda s, g, b: (s * nqb + b, g)),
        out_shape=jax.ShapeDtypeStruct((n_seq * sq, n_heads * dv), BF16),
        scratch_shapes=scratch,
        compiler_params=_cparams(("parallel", "parallel", "arbitrary"), VMEM_LIMIT),
        name=name,
    )(*args)


def _outproj_body(*refs, tn, na, nx, a_first, x_first):
    i = pl.program_id(0)
    j = pl.program_id(1)
    a = _pick_rows(i, a_first, refs[:na])
    w_ref = refs[na]
    x = _pick_rows(i, x_first, refs[na + 1:na + 1 + nx])
    g_ref, wr_ref, br_ref, h_ref, xn_ref, lg_ref, hrow = refs[na + 1 + nx:]
    r = x + _dot(a, w_ref[0].astype(BF16))
    h_ref[...] = r
    hrow[:, pl.ds(pl.multiple_of(j * tn, tn), tn)] = r

    @pl.when(j == pl.num_programs(1) - 1)
    def _():
        xn = _rms(hrow[...], g_ref[...])
        xn_ref[...] = xn
        lg_ref[...] = _dot(xn.astype(BF16), wr_ref[...].astype(BF16)) + br_ref[...]


def outproj_norm_router(a_parts, w_out, layer, x_parts, g_ffn, w_r, b_r, tm, tn):
    k = a_parts[0].shape[1]
    t = sum(p.shape[0] for p in a_parts)
    d = w_out.shape[-1]
    a_specs, a_first = _row_split_specs(a_parts, tm, k, lambda i, j: 0)
    x_specs, x_first = _row_split_specs(x_parts, tm, tn, lambda i, j: j)
    return pl.pallas_call(
        functools.partial(_outproj_body, tn=tn, na=len(a_parts), nx=len(x_parts), a_first=a_first,
                          x_first=x_first),
        grid=(t // tm, d // tn),
        in_specs=a_specs + [pl.BlockSpec((1, k, tn), lambda i, j: (layer, 0, j))] + x_specs + [
            pl.BlockSpec((1, d), lambda i, j: (0, 0)),
            pl.BlockSpec((d, LANES), lambda i, j: (0, 0)),
            pl.BlockSpec((1, LANES), lambda i, j: (0, 0))],
        out_specs=[pl.BlockSpec((tm, tn), lambda i, j: (i, j)),
                   pl.BlockSpec((tm, d), lambda i, j: (i, 0)),
                   pl.BlockSpec((tm, LANES), lambda i, j: (i, 0))],
        out_shape=[jax.ShapeDtypeStruct((t, d), F32), jax.ShapeDtypeStruct((t, d), F32),
                   jax.ShapeDtypeStruct((t, LANES), F32)],
        scratch_shapes=[pltpu.VMEM((tm, d), F32)],
        compiler_params=_cparams(("parallel", "arbitrary"), VMEM_LIMIT),
        name="outproj_norm_router",
    )(*a_parts, w_out, *x_parts, g_ffn.reshape(1, d), w_r, b_r)


def _routing_body(lg_ref, out_ref, cnt_ref, carry, *, tm):
    i = pl.program_id(0)

    @pl.when(i == 0)
    def _():
        carry[...] = jnp.zeros(carry.shape, F32)

    x = lg_ref[...]
    lane = lax.broadcasted_iota(I32, (tm, LANES), 1)
    neg_inf = -jnp.inf

    def rmax(v):
        return jnp.max(v, axis=1, keepdims=True)

    def rmin(v):
        return jnp.min(v, axis=1, keepdims=True)

    def rsum(v):
        return jnp.sum(v, axis=1, keepdims=True)

    gm = lane < N_GROUPS
    gmax = rmax(jnp.where(gm, x, neg_inf))
    gsel = rmin(jnp.where(gm, jnp.where(x == gmax, lane, LANES), LANES))
    gsum = rsum(jnp.where(gm, jnp.exp(x - gmax), 0.0))
    g_w = 1.0 / gsum
    lo = N_GROUPS + gsel * EXPERTS_PER_GROUP
    em = jnp.logical_and(lane >= lo, lane < lo + EXPERTS_PER_GROUP)
    emax = rmax(jnp.where(em, x, neg_inf))
    ee = jnp.where(em, jnp.exp(x - emax), 0.0)
    p = ee / rsum(ee)
    p1 = rmax(jnp.where(em, p, -1.0))
    i1 = rmin(jnp.where(em, jnp.where(p == p1, lane, LANES), LANES))
    em2 = jnp.logical_and(em, lane != i1)
    p2 = rmax(jnp.where(em2, p, -1.0))
    i2 = rmin(jnp.where(em2, jnp.where(p == p2, lane, LANES), LANES))
    den = p1 + p2
    g1 = g_w * (p1 / den)
    g2 = g_w * (p2 / den)

    oh1 = jnp.where(lane == i1, 1.0, 0.0)
    oh2 = jnp.where(lane == i2, 1.0, 0.0)
    oh = oh1 + oh2
    r = lax.broadcasted_iota(I32, (tm, tm), 0)
    c = lax.broadcasted_iota(I32, (tm, tm), 1)
    lower = jnp.where(c < r, 1.0, 0.0).astype(BF16)
    before = _dot(lower, oh.astype(BF16)) + carry[...]
    rank1 = rsum(oh1 * before)
    rank2 = rsum(oh2 * before)
    carry[...] = carry[...] + jnp.sum(oh, axis=0, keepdims=True)

    e1 = (i1 - N_GROUPS).astype(F32)
    e2 = (i2 - N_GROUPS).astype(F32)
    vals = [e1, e2, rank1, rank2, g1, g2]
    out = jnp.zeros((tm, LANES), F32)
    for k, v in enumerate(vals):
        out = jnp.where(lane == k, v, out)
    out_ref[...] = out
    cnt_ref[...] = carry[...]


def moe_routing(logits, tm):
    t = logits.shape[0]
    return pl.pallas_call(
        functools.partial(_routing_body, tm=tm),
        grid=(t // tm,),
        in_specs=[pl.BlockSpec((tm, LANES), lambda i: (i, 0))],
        out_specs=[pl.BlockSpec((tm, LANES), lambda i: (i, 0)), pl.BlockSpec((1, LANES), lambda i: (0, 0))],
        out_shape=[jax.ShapeDtypeStruct((t, LANES), F32), jax.ShapeDtypeStruct((1, LANES), F32)],
        scratch_shapes=[pltpu.VMEM((1, LANES), F32)],
        compiler_params=_cparams(("arbitrary",)),
        name="moe_routing",
    )(logits)


def _slot_tokens_body(slots_ref, tok_ref):
    def clear(i, c):
        tok_ref[i] = 0
        return c

    lax.fori_loop(0, tok_ref.shape[0], clear, 0, unroll=8)

    def put(a, c):
        tok_ref[slots_ref[a]] = lax.shift_right_logical(a, 1)
        return c

    lax.fori_loop(0, slots_ref.shape[0], put, 0, unroll=8)


def moe_slot_tokens(slots, n_slots):
    return pl.pallas_call(
        _slot_tokens_body,
        in_specs=[pl.BlockSpec(memory_space=pltpu.SMEM)],
        out_specs=pl.BlockSpec(memory_space=pltpu.SMEM),
        out_shape=jax.ShapeDtypeStruct((n_slots,), I32),
        name="moe_slot_tokens",
    )(slots)


def _dispatch_body(nslot_ref, tok_ref, x_hbm, xs_ref, buf, sem, *, rows):
    i = pl.program_id(0)
    ni = pl.num_programs(0)
    n_used = nslot_ref[0]

    def issue(blk, s):
        @pl.when(blk * rows < n_used)
        def _():
            def one(r, carry):
                pltpu.make_async_copy(x_hbm.at[pl.ds(tok_ref[blk * rows + r], 1), :],
                                      buf.at[s, pl.ds(r, 1), :], sem.at[s]).start()
                return carry

            lax.fori_loop(0, rows, one, 0, unroll=8)

    @pl.when(i == 0)
    def _():
        issue(0, 0)

    @pl.when(i + 1 < ni)
    def _():
        issue(i + 1, (i + 1) % 2)

    slot = i % 2

    @pl.when(i * rows < n_used)
    def _():
        pltpu.make_async_copy(x_hbm.at[pl.ds(0, rows), :], buf.at[slot], sem.at[slot]).wait()
        xs_ref[...] = buf[slot].astype(xs_ref.dtype)

    @pl.when(i * rows >= n_used)
    def _():
        xs_ref[...] = jnp.zeros(xs_ref.shape, xs_ref.dtype)


def moe_dispatch(xn, slot_tok, n_used, rows):
    d = xn.shape[1]
    n_slots = slot_tok.shape[0]
    assert n_slots % rows == 0
    grid_spec = pltpu.PrefetchScalarGridSpec(
        num_scalar_prefetch=2,
        grid=(n_slots // rows,),
        in_specs=[pl.BlockSpec(memory_space=pl.ANY)],
        out_specs=pl.BlockSpec((rows, d), lambda i, ns, tk: (i, 0)),
        scratch_shapes=[pltpu.VMEM((2, rows, d), F32), pltpu.SemaphoreType.DMA((2,))],
    )
    return pl.pallas_call(
        functools.partial(_dispatch_body, rows=rows),
        grid_spec=grid_spec,
        out_shape=jax.ShapeDtypeStruct((n_slots, d), BF16),
        compiler_params=_cparams(("arbitrary",), VMEM_LIMIT),
        name="moe_dispatch",
    )(n_used, slot_tok, xn)


def _experts_body(blk0_ref, nblk_ref, ntot_ref, xs_hbm, wg_ref, wu_ref, wd_ref, ys_hbm,
                  xbuf, ybuf, zbuf, xsem, ysem, zsem, wgb, wub, wdb, *, bm):
    e = pl.program_id(0)
    n_total = ntot_ref[0]

    def x_copy(g, s):
        return pltpu.make_async_copy(xs_hbm.at[pl.ds(g * bm, bm), :], xbuf.at[s], xsem.at[s])

    def y_copy(g, s):
        return pltpu.make_async_copy(ybuf.at[s], ys_hbm.at[pl.ds(g * bm, bm), :], ysem.at[s])

    @pl.when(jnp.logical_and(e == 0, n_total > 0))
    def _():
        x_copy(0, 0).start()

    @pl.when(nblk_ref[e] > 0)
    def _():
        wgb[...] = wg_ref[0, 0].astype(BF16)
        wub[...] = wu_ref[0, 0].astype(BF16)
        wdb[...] = wd_ref[0, 0].astype(BF16)

    def block(i, carry):
        g = blk0_ref[e] + i
        s = g % 2

        @pl.when(g + 1 < n_total)
        def _():
            x_copy(g + 1, 1 - s).start()

        x_copy(g, s).wait()
        x = xbuf[s]
        gt = _dot(x, wgb[...])
        up = _dot(x, wub[...])
        a = (gt * (1.0 / (1.0 + jnp.exp(-gt)))) * up
        y = _dot(a.astype(BF16), wdb[...])

        @pl.when(g >= 2)
        def _():
            y_copy(g - 2, s).wait()

        ybuf[s] = y
        y_copy(g, s).start()
        return carry

    lax.fori_loop(0, nblk_ref[e], block, 0)

    @pl.when(e == pl.num_programs(0) - 1)
    def _():
        for back in (2, 1):
            @pl.when(n_total >= back)
            def _():
                y_copy(n_total - back, (n_total - back) % 2).wait()

        def z_copy(g):
            return pltpu.make_async_copy(zbuf, ys_hbm.at[pl.ds(g * bm, bm), :], zsem)

        zbuf[...] = jnp.zeros(zbuf.shape, F32)
        n_blocks = ys_hbm.shape[0] // bm
        lax.fori_loop(n_total, n_blocks, lambda g, c: (z_copy(g).start(), c)[1], 0)
        lax.fori_loop(n_total, n_blocks, lambda g, c: (z_copy(g).wait(), c)[1], 0)


def moe_experts(xs, blk0, nblk, n_total, w_gate, w_up, w_down, layer, bm):
    n_slots, d = xs.shape
    n_exp = blk0.shape[0]
    de = w_gate.shape[-1]
    grid_spec = pltpu.PrefetchScalarGridSpec(
        num_scalar_prefetch=3,
        grid=(n_exp,),
        in_specs=[pl.BlockSpec(memory_space=pl.ANY),
                  pl.BlockSpec((1, 1, d, de), lambda e, b0, nb, nt: (layer, e, 0, 0)),
                  pl.BlockSpec((1, 1, d, de), lambda e, b0, nb, nt: (layer, e, 0, 0)),
                  pl.BlockSpec((1, 1, de, d), lambda e, b0, nb, nt: (layer, e, 0, 0))],
        out_specs=pl.BlockSpec(memory_space=pl.ANY),
        scratch_shapes=[pltpu.VMEM((2, bm, d), BF16), pltpu.VMEM((2, bm, d), F32), pltpu.VMEM((bm, d), F32),
                        pltpu.SemaphoreType.DMA((2,)), pltpu.SemaphoreType.DMA((2,)), pltpu.SemaphoreType.DMA,
                        pltpu.VMEM((d, de), BF16), pltpu.VMEM((d, de), BF16), pltpu.VMEM((de, d), BF16)],
    )
    return pl.pallas_call(
        functools.partial(_experts_body, bm=bm),
        grid_spec=grid_spec,
        out_shape=jax.ShapeDtypeStruct((n_slots, d), F32),
        compiler_params=_cparams(("arbitrary",), VMEM_LIMIT),
        name="moe_experts",
    )(blk0, nblk, n_total, xs, w_gate, w_up, w_down)


def _combine_body(slot_ref, ys_hbm, h_ref, rt_ref, g_ref, *rest, tm, tile0, want_h):
    outs, (buf, sem) = rest[:-2], rest[-2:]
    i = pl.program_id(0)
    ni = pl.num_programs(0)

    def row_copy(src_row, k, r, s):
        return pltpu.make_async_copy(ys_hbm.at[pl.ds(src_row, 1), :], buf.at[s, k, pl.ds(r, 1), :], sem.at[s])

    def issue(tile, s):
        def one(r, carry):
            base = ((tile0 + tile) * tm + r) * 2
            row_copy(slot_ref[base], 0, r, s).start()
            row_copy(slot_ref[base + 1], 1, r, s).start()
            return carry

        lax.fori_loop(0, tm, one, 0, unroll=8)

    @pl.when(i == 0)
    def _():
        issue(0, 0)

    @pl.when(i + 1 < ni)
    def _():
        issue(i + 1, (i + 1) % 2)

    slot = i % 2
    for k in range(2):
        pltpu.make_async_copy(ys_hbm.at[pl.ds(0, tm), :], buf.at[slot, k], sem.at[slot]).wait()

    rt = rt_ref[...]
    g1 = rt[:, 4:5]
    g2 = rt[:, 5:6]
    h2 = h_ref[...] + (buf[slot, 0] * g1 + buf[slot, 1] * g2)
    if want_h:
        outs[0][...] = h2
    outs[-1][...] = _rms(h2, g_ref[...]).astype(outs[-1].dtype)


def moe_combine(ys, slots, h1, route, g_next, tm, u_dtype, row0=0, nrows=None, want_h=True):
    d = h1.shape[1]
    t = h1.shape[0] if nrows is None else nrows
    assert row0 % tm == 0 and t % tm == 0
    tile0 = row0 // tm
    row_spec = pl.BlockSpec((tm, d), lambda i, sl: (i, 0))
    grid_spec = pltpu.PrefetchScalarGridSpec(
        num_scalar_prefetch=1,
        grid=(t // tm,),
        in_specs=[pl.BlockSpec(memory_space=pl.ANY),
                  pl.BlockSpec((tm, d), lambda i, sl: (i + tile0, 0)),
                  pl.BlockSpec((tm, LANES), lambda i, sl: (i + tile0, 0)),
                  pl.BlockSpec((1, d), lambda i, sl: (0, 0))],
        out_specs=[row_spec, row_spec] if want_h else [row_spec],
        scratch_shapes=[pltpu.VMEM((2, 2, tm, d), F32), pltpu.SemaphoreType.DMA((2,))],
    )
    shapes = [jax.ShapeDtypeStruct((t, d), F32)] if want_h else []
    return pl.pallas_call(
        functools.partial(_combine_body, tm=tm, tile0=tile0, want_h=want_h),
        grid_spec=grid_spec,
        out_shape=shapes + [jax.ShapeDtypeStruct((t, d), u_dtype)],
        compiler_params=_cparams(("arbitrary",), VMEM_LIMIT),
        name="moe_combine",
    )(slots, ys, h1, route, g_next.reshape(1, d))


def hier_moe_layer(h1, xn, logits, w_gate, w_up, w_down, layer, g_next, u_dtype, split=None):
    t = h1.shape[0]
    bm = MOE_BM
    route, counts = moe_routing(logits, _tile(t, 256))
    e = route[:, 0:2].astype(I32)
    rank = route[:, 2:4].astype(I32)
    counts = counts[0, N_GROUPS:N_GROUPS + N_EXPERTS].astype(I32)
    padded = (counts + bm - 1) // bm * bm
    pad_end = jnp.cumsum(padded)
    pad_start = pad_end - padded
    start_of = jnp.sum(jnp.where(e[..., None] == jnp.arange(N_EXPERTS, dtype=I32), pad_start, 0), axis=-1)
    slots = (start_of + rank).reshape(-1)
    nb = (2 * t) // bm + N_EXPERTS
    slot_tok = moe_slot_tokens(slots, nb * bm)
    n_used = pad_end[-1:]
    xs = moe_dispatch(xn, slot_tok, n_used, bm * max(f for f in (4, 2, 1) if nb % f == 0))
    ys = moe_experts(xs, pad_start // bm, padded // bm, n_used // bm, w_gate, w_up, w_down, layer, bm)
    tmc = _tile(t, 256)
    if split is None:
        return moe_combine(ys, slots, h1, route, g_next, tmc, u_dtype)
    (u_a,) = moe_combine(ys, slots, h1, route, g_next, tmc, u_dtype, 0, split, want_h=False)
    (u_b,) = moe_combine(ys, slots, h1, route, g_next, tmc, u_dtype, split, t - split, want_h=False)
    return u_a, u_b


def _swap_halves(x):
    lane = lax.broadcasted_iota(I32, x.shape, 1)
    return jnp.where(lane < QK_ROPE // 2, pltpu.roll(x, LANES - QK_ROPE // 2, 1), pltpu.roll(x, QK_ROPE // 2, 1))


def _mla_prep_body(p_ref, gq_ref, gkv_ref, cos_ref, sin_ref, cq_ref, ckv_ref, ckvb_ref, kr_ref):
    p = p_ref[...]
    cq_ref[...] = _rms(p[:, :Q_LORA], gq_ref[...]).astype(cq_ref.dtype)
    ckv = _rms(p[:, Q_LORA:Q_LORA + KV_LORA], gkv_ref[...])
    ckv_ref[...] = ckv
    ckvb_ref[...] = ckv.astype(ckvb_ref.dtype)
    kr = p[:, Q_LORA + KV_LORA:]
    kr_ref[...] = kr * cos_ref[...] + _swap_halves(kr) * sin_ref[...]


def mla_prep(proj, g_q, g_kv, cos, sin, tm):
    t = proj.shape[0]
    return pl.pallas_call(
        _mla_prep_body,
        grid=(t // tm,),
        in_specs=[pl.BlockSpec((tm, proj.shape[1]), lambda i: (i, 0)),
                  pl.BlockSpec((1, Q_LORA), lambda i: (0, 0)),
                  pl.BlockSpec((1, KV_LORA), lambda i: (0, 0)),
                  pl.BlockSpec((tm, LANES), lambda i: (i, 0)),
                  pl.BlockSpec((tm, LANES), lambda i: (i, 0))],
        out_specs=[pl.BlockSpec((tm, Q_LORA), lambda i: (i, 0)),
                   pl.BlockSpec((tm, KV_LORA), lambda i: (i, 0)),
                   pl.BlockSpec((tm, KV_LORA), lambda i: (i, 0)),
                   pl.BlockSpec((tm, LANES), lambda i: (i, 0))],
        out_shape=[jax.ShapeDtypeStruct((t, Q_LORA), BF16), jax.ShapeDtypeStruct((t, KV_LORA), F32),
                   jax.ShapeDtypeStruct((t, KV_LORA), BF16), jax.ShapeDtypeStruct((t, LANES), F32)],
        compiler_params=_cparams(("parallel",)),
        name="mla_prep",
    )(proj, g_q.reshape(1, -1), g_kv.reshape(1, -1), cos, sin)


def _mla_q_body(cq_ref, w_ref, cos_ref, sin_ref, q_ref, wb):
    @pl.when(pl.program_id(0) == 0)
    def _():
        wb[...] = w_ref[...].astype(BF16)

    cq = cq_ref[...]
    cos = cos_ref[...]
    sin = sin_ref[...]
    for h in range(B_HEADS):
        y = _dot(cq, wb[:, h * MLA_QK_PAD:(h + 1) * MLA_QK_PAD])
        xr = y[:, LANES:]
        q_ref[:, h * MLA_QK_PAD:h * MLA_QK_PAD + LANES] = y[:, :LANES].astype(q_ref.dtype)
        q_ref[:, h * MLA_QK_PAD + LANES:(h + 1) * MLA_QK_PAD] = \
            (xr * cos + _swap_halves(xr) * sin).astype(q_ref.dtype)


def mla_q_up(cq, w_uq_pad, cos, sin, tm):
    t = cq.shape[0]
    n = B_HEADS * MLA_QK_PAD
    return pl.pallas_call(
        _mla_q_body,
        grid=(t // tm,),
        in_specs=[pl.BlockSpec((tm, Q_LORA), lambda i: (i, 0)),
                  pl.BlockSpec((Q_LORA, n), lambda i: (0, 0)),
                  pl.BlockSpec((tm, LANES), lambda i: (i, 0)),
                  pl.BlockSpec((tm, LANES), lambda i: (i, 0))],
        out_specs=pl.BlockSpec((tm, n), lambda i: (i, 0)),
        out_shape=jax.ShapeDtypeStruct((t, n), BF16),
        scratch_shapes=[pltpu.VMEM((Q_LORA, n), BF16)],
        compiler_params=_cparams(("arbitrary",), VMEM_LIMIT),
        name="mla_q_up",
    )(cq, w_uq_pad, cos, sin)


def _mla_kv_body(c_ref, kr_ref, wk_ref, wv_ref, k_ref, v_ref, wkb, wvb):
    @pl.when(pl.program_id(0) == 0)
    def _():
        wkb[...] = wk_ref[0].astype(BF16)
        wvb[...] = wv_ref[0].astype(BF16)

    c = c_ref[...].astype(BF16)
    kr = kr_ref[...].astype(k_ref.dtype)
    kn = _dot(c, wkb[...])
    for h in range(B_HEADS):
        k_ref[:, h * MLA_QK_PAD:h * MLA_QK_PAD + LANES] = kn[:, h * QK_NOPE:(h + 1) * QK_NOPE].astype(k_ref.dtype)
        k_ref[:, h * MLA_QK_PAD + LANES:(h + 1) * MLA_QK_PAD] = kr
    v_ref[...] = _dot(c, wvb[...]).astype(v_ref.dtype)


def mla_kv_up(ckv, kr, w_uk, w_uv, layer, tm):
    t = ckv.shape[0]
    nk = B_HEADS * MLA_QK_PAD
    nv = B_HEADS * V_DIM
    return pl.pallas_call(
        _mla_kv_body,
        grid=(t // tm,),
        in_specs=[pl.BlockSpec((tm, KV_LORA), lambda i: (i, 0)),
                  pl.BlockSpec((tm, LANES), lambda i: (i, 0)),
                  pl.BlockSpec((1, KV_LORA, B_HEADS * QK_NOPE), lambda i: (layer, 0, 0)),
                  pl.BlockSpec((1, KV_LORA, nv), lambda i: (layer, 0, 0))],
        out_specs=[pl.BlockSpec((tm, nk), lambda i: (i, 0)), pl.BlockSpec((tm, nv), lambda i: (i, 0))],
        out_shape=[jax.ShapeDtypeStruct((t, nk), BF16), jax.ShapeDtypeStruct((t, nv), BF16)],
        scratch_shapes=[pltpu.VMEM((KV_LORA, B_HEADS * QK_NOPE), BF16), pltpu.VMEM((KV_LORA, nv), BF16)],
        compiler_params=_cparams(("arbitrary",), VMEM_LIMIT),
        name="mla_kv_up",
    )(ckv, kr, w_uk, w_uv)


def kernel(x_prompt, x_sample, cache_a_k, cache_a_v, cache_a_kidx, cache_b_ckv, cache_b_krope, norm_mix, norm_ffn, norm_final, rel_bias, a_w_in, a_w_out, b_w_in, b_norm_q, b_norm_kv, b_w_uq, b_w_uk, b_w_uv, b_w_out, moe_w_grp, moe_b_grp, moe_w_rtr, moe_b_rtr, moe_w_gate, moe_w_up, moe_w_down):
    n_p, s_p, d = x_prompt.shape
    n_s, s_s, _ = x_sample.shape
    past = cache_a_k.shape[2]
    tp = n_p * s_p
    ts = n_s * s_s
    t = tp + ts
    a_qd = A_HEADS * A_HEAD_DIM
    tm = _tile(math.gcd(tp, ts), 1024)
    tm2 = _tile(math.gcd(tp, ts), 512)

    x_parts = [x_prompt.reshape(tp, d), x_sample.reshape(ts, d)]

    def router_params(i):
        w_r = jnp.concatenate([moe_w_grp[i], moe_w_rtr[i],
                               jnp.zeros((d, LANES - N_GROUPS - N_EXPERTS), F32)], axis=1)
        b_r = jnp.concatenate([moe_b_grp[i], moe_b_rtr[i],
                               jnp.zeros((LANES - N_GROUPS - N_EXPERTS,), F32)]).reshape(1, LANES)
        return w_r, b_r

    u0 = rmsnorm(x_parts, norm_mix[0], tm)
    w_a = a_w_in[0]
    (q_b,) = matmul(u0, w_a, 0, a_qd, [BF16], tm, 512, "a_proj_q")
    tmp, tms = _tile(tp, 1024), _tile(ts, 1024)
    kp_f, kp_b = matmul(u0, w_a, a_qd, a_qd, [F32, BF16], tmp, 512, "a_proj_k", 0, tp)
    vp_f, vp_b = matmul(u0, w_a, 2 * a_qd, a_qd, [F32, BF16], tmp, 512, "a_proj_v", 0, tp)
    ks_f, ks_b = matmul(u0, w_a, a_qd, a_qd, [F32, BF16], tms, 512, "a_proj_k", tp, ts)
    vs_f, vs_b = matmul(u0, w_a, 2 * a_qd, a_qd, [F32, BF16], tms, 512, "a_proj_v", tp, ts)
    (qi_b,) = matmul(u0, w_a, 3 * a_qd, IDX_HEADS * IDX_DIM, [BF16], tm, 512, "a_proj_qi")
    c_ki = 3 * a_qd + IDX_HEADS * IDX_DIM
    w_ki = w_a[:, c_ki:c_ki + IDX_DIM]
    w_wi = w_a[:, c_ki + IDX_DIM:c_ki + IDX_DIM + IDX_HEADS]
    zk = jnp.zeros((d, IDX_DIM), F32)
    w_tail = jnp.concatenate([w_ki, zk, zk, w_ki, w_wi, jnp.zeros((d, LANES - IDX_HEADS), F32)], axis=1)
    (tail,) = matmul(u0, w_tail, 0, 3 * LANES, [F32], tm, 3 * LANES, "a_proj_tail")
    kidx = tail[:, :IDX_DIM]

    bias_tab = rel_bias_tables(rel_bias)
    a_scale = A_HEAD_DIM ** -0.5
    hg = 8
    mask_p = dsa_select(qi_b, tail, 2, tail, n_seq=n_p, sq=s_p, bq=128, sk=s_p, sk_real=s_p, tk=512,
                        pos0=0, q_row0=0)
    kp_t = kp_b.reshape(n_p, s_p, a_qd).transpose(0, 2, 1).reshape(n_p * a_qd, s_p)
    att_p = attention(q_b, [(kp_t, vp_b, s_p, 0, 0)], bias_tab, mask_p, n_seq=n_p, sq=s_p, bq=128, q_row0=0,
                      q_col0=0, n_heads=A_HEADS, hg=8, dq=A_HEAD_DIM, dv=A_HEAD_DIM, scale=a_scale, pos0=0,
                      dyn_sk=s_p, name="dsa_attention_prompt")
    sk_s = past + s_s
    sk_pad = (sk_s + ATT_TILE - 1) // ATT_TILE * ATT_TILE
    ki_past = cache_a_kidx[0]
    zp = jnp.zeros_like(ki_past)
    kc_past = jnp.concatenate([ki_past, zp, zp, ki_past], axis=-1)
    kc_new = tail[tp:, :2 * LANES].reshape(n_s, s_s, 2 * LANES)
    kc_s = jnp.concatenate([kc_past, kc_new, jnp.zeros((n_s, sk_pad - sk_s, 2 * LANES), F32)], axis=1)
    mask_s = dsa_select(qi_b, tail, 2, kc_s.reshape(n_s * sk_pad, 2 * LANES), n_seq=n_s, sq=s_s, bq=s_s,
                        sk=sk_pad, sk_real=sk_s, tk=ATT_TILE, pos0=past, q_row0=tp)
    tiles_s = [(0, r, r, ATT_TILE) for r in range(0, past, ATT_TILE)] + [(1, 0, past, s_s)]
    att_s = attention(q_b, [(cache_a_k[0].reshape(n_s * past, a_qd), cache_a_v[0].reshape(n_s * past, a_qd),
                             past, 0, 0), (ks_b, vs_b, s_s, 0, 0)],
                      bias_tab, mask_s, n_seq=n_s, sq=s_s, bq=s_s, q_row0=tp, q_col0=0, n_heads=A_HEADS, hg=hg,
                      dq=A_HEAD_DIM, dv=A_HEAD_DIM, scale=a_scale, pos0=past, seg_tiles=tiles_s,
                      name="dsa_attention_sample")

    w_r0, b_r0 = router_params(0)
    h1, xn0, lg0 = outproj_norm_router([att_p, att_s], a_w_out, 0, x_parts, norm_ffn[0], w_r0, b_r0, tm2, 512)
    h2, u1 = hier_moe_layer(h1, xn0, lg0, moe_w_gate, moe_w_up, moe_w_down, 0, norm_mix[1], BF16)

    n_in = Q_LORA + KV_LORA + QK_ROPE
    w_b = jnp.concatenate([b_w_in[0], jnp.zeros((d, LANES - QK_ROPE), F32)], axis=1)
    (proj,) = matmul(u1, w_b, 0, n_in + LANES - QK_ROPE, [F32], tm, 384, "b_proj")
    half = QK_ROPE // 2
    inv = ROPE_THETA ** (-jnp.arange(half, dtype=F32) / half)
    pos_all = jnp.concatenate([jnp.tile(jnp.arange(s_p, dtype=I32), n_p),
                               jnp.tile(past + jnp.arange(s_s, dtype=I32), n_s)])
    ang = pos_all.astype(F32)[:, None] * inv[None, :]
    zl = jnp.zeros((t, LANES - QK_ROPE), F32)
    cos_t = jnp.concatenate([jnp.cos(ang), jnp.cos(ang), zl], axis=1)
    sin_t = jnp.concatenate([-jnp.sin(ang), jnp.sin(ang), zl], axis=1)
    cq_b, ckv_f, ckv_b, kr_f = mla_prep(proj, b_norm_q[0], b_norm_kv[0], cos_t, sin_t, tm)
    w_uq_pad = jnp.pad(b_w_uq[0].reshape(Q_LORA, B_HEADS, QK_NOPE + QK_ROPE),
                       ((0, 0), (0, 0), (0, MLA_QK_PAD - QK_NOPE - QK_ROPE))).reshape(Q_LORA, B_HEADS * MLA_QK_PAD)
    qm = mla_q_up(cq_b, w_uq_pad, cos_t, sin_t, tm2)
    km_p, vm_p = mla_kv_up(ckv_b[:tp], kr_f[:tp], b_w_uk, b_w_uv, 0, _tile(tp, 512))
    ctab = causal_tables()
    hgb = 8
    km_t = km_p.reshape(n_p, s_p, B_HEADS * MLA_QK_PAD).transpose(0, 2, 1).reshape(n_p * B_HEADS * MLA_QK_PAD, s_p)
    matt_p = attention(qm, [(km_t, vm_p, s_p, 0, 0)], ctab, None, n_seq=n_p, sq=s_p, bq=128, q_row0=0, q_col0=0,
                       n_heads=B_HEADS, hg=8, dq=MLA_QK_PAD, dv=V_DIM, scale=B_SCALE, pos0=0, dyn_sk=s_p,
                       name="mla_attention_prompt")
    ckv_all = jnp.concatenate([cache_b_ckv[0], ckv_f[tp:].reshape(n_s, s_s, KV_LORA)], axis=1)
    kr_past = jnp.concatenate([cache_b_krope[0], jnp.zeros((n_s, past, LANES - QK_ROPE), F32)], axis=-1)
    kr_all = jnp.concatenate([kr_past, kr_f[tp:].reshape(n_s, s_s, LANES)], axis=1)
    km_s, vm_s = mla_kv_up(ckv_all.reshape(n_s * sk_s, KV_LORA), kr_all.reshape(n_s * sk_s, LANES),
                           b_w_uk, b_w_uv, 0, sk_s)
    tiles_m = [(0, r, r, ATT_TILE) for r in range(0, past, ATT_TILE)] + [(0, past, past, s_s)]
    matt_s = attention(qm, [(km_s, vm_s, sk_s, 0, 0)], ctab, None, n_seq=n_s, sq=s_s, bq=s_s, q_row0=tp, q_col0=0,
                       n_heads=B_HEADS, hg=hgb, dq=MLA_QK_PAD, dv=V_DIM, scale=B_SCALE, pos0=past,
                       seg_tiles=tiles_m, name="mla_attention_sample")

    w_r1, b_r1 = router_params(1)
    h3, xn1, lg1 = outproj_norm_router([matt_p, matt_s], b_w_out, 0, [h2], norm_ffn[1], w_r1, b_r1, tm2, 512)
    y_p, y_s = hier_moe_layer(h3, xn1, lg1, moe_w_gate, moe_w_up, moe_w_down, 1, norm_final, F32, split=tp)
    y_prompt = y_p.reshape(n_p, s_p, d)
    y_sample = y_s.reshape(n_s, s_s, d)

    def heads(a, n, s):
        return a.reshape(1, n, s, A_HEADS, A_HEAD_DIM)

    kr_out = kr_f[:, :QK_ROPE]
    return (y_prompt, y_sample,
            heads(kp_f, n_p, s_p), heads(vp_f, n_p, s_p), kidx[:tp].reshape(1, n_p, s_p, IDX_DIM),
            ckv_f[:tp].reshape(1, n_p, s_p, KV_LORA), kr_out[:tp].reshape(1, n_p, s_p, QK_ROPE),
            heads(ks_f, n_s, s_s), heads(vs_f, n_s, s_s), kidx[tp:].reshape(1, n_s, s_s, IDX_DIM),
            ckv_f[tp:].reshape(1, n_s, s_s, KV_LORA), kr_out[tp:].reshape(1, n_s, s_s, QK_ROPE))
```

```python
import functools
import math

import jax
import jax.numpy as jnp
from jax import lax
from jax.experimental import pallas as pl
from jax.experimental.pallas import tpu as pltpu

F32 = jnp.float32
BF16 = jnp.bfloat16
I32 = jnp.int32

RMS_EPS = 1e-6
CHUNK = 64
NEG = -1e30
INT_MIN = -2 ** 31
INT_MAX = 2 ** 31 - 1
LOG2E = math.log2(math.e)

LANES = 128
KEY_BLOCK = 128
ATT_TILE = 256
VMEM_LIMIT = 56 * 1024 * 1024

A_HEADS = 16
A_HEAD_DIM = 128
IDX_HEADS = 16
IDX_DIM = 64
TOPK_MAX = 256
IDX_W_SCALE = float((IDX_HEADS * IDX_DIM) ** -0.5)
NUM_BUCKETS = 32
MAX_DISTANCE = 128
B_HEADS = 16
Q_LORA = 512
KV_LORA = 512
QK_NOPE = 128
QK_ROPE = 64
V_DIM = 128
ROPE_THETA = 10000.0
B_SCALE = float((QK_NOPE + QK_ROPE) ** -0.5)
MLA_QK_PAD = 256
N_GROUPS = 8
EXPERTS_PER_GROUP = 8
N_EXPERTS = 64
D_EXPERT = 512
MOE_BM = 128
ROW_CHUNKS = 16


def _tile(n, pref):
    for c in range(pref, 0, -LANES):
        if n % c == 0:
            return c
    raise ValueError(f"no 128-multiple tile divides {n}")


def _cparams(sem, vmem=None):
    return pltpu.CompilerParams(dimension_semantics=sem, vmem_limit_bytes=vmem)


def _dot(a, b):
    return jnp.dot(a, b, preferred_element_type=F32)


def _dot_nt(a, b):
    return lax.dot_general(a, b, (((1,), (1,)), ((), ())), preferred_element_type=F32)


def _rms(x, g):
    ms = jnp.mean(x * x, axis=-1, keepdims=True)
    return x * lax.rsqrt(ms + RMS_EPS) * g


def _pick_rows(i, n_first, refs):
    if len(refs) == 1:
        return refs[0][...]
    return jnp.where(i < n_first, refs[0][...], refs[1][...])


def _row_split_specs(parts, tm, width, col_of):
    n_first = parts[0].shape[0] // tm
    assert all(p.shape[0] % tm == 0 for p in parts)
    specs = [pl.BlockSpec((tm, width), lambda i, *r: (jnp.minimum(i, n_first - 1), col_of(i, *r)))]
    if len(parts) == 2:
        specs.append(pl.BlockSpec((tm, width), lambda i, *r: (jnp.maximum(i - n_first, 0), col_of(i, *r))))
    return specs, n_first


def _rmsnorm_body(*refs, n_first):
    x = _pick_rows(pl.program_id(0), n_first, refs[:-2])
    g_ref, o_ref = refs[-2:]
    o_ref[...] = _rms(x, g_ref[...]).astype(o_ref.dtype)


def rmsnorm(xs, g, tm, out_dtype=BF16):
    d = xs[0].shape[1]
    t = sum(x.shape[0] for x in xs)
    specs, n_first = _row_split_specs(xs, tm, d, lambda i: 0)
    return pl.pallas_call(
        functools.partial(_rmsnorm_body, n_first=n_first),
        grid=(t // tm,),
        in_specs=specs + [pl.BlockSpec((1, d), lambda i: (0, 0))],
        out_specs=pl.BlockSpec((tm, d), lambda i: (i, 0)),
        out_shape=jax.ShapeDtypeStruct((t, d), out_dtype),
        compiler_params=_cparams(("parallel",)),
        name="rmsnorm",
    )(*xs, g.reshape(1, d))


def _matmul_body(a_ref, w_ref, *rest):
    outs, wb = rest[:-1], rest[-1]

    @pl.when(pl.program_id(1) == 0)
    def _():
        wb[...] = w_ref[...].astype(BF16)

    r = _dot(a_ref[...], wb[...])
    for o in outs:
        o[...] = r.astype(o.dtype)


def matmul(a, w, col0, ncols, out_dtypes, tm, tn, name, row0=0, nrows=None):
    k = a.shape[1]
    t = a.shape[0] if nrows is None else nrows
    assert ncols % tn == 0 and col0 % tn == 0 and t % tm == 0 and row0 % tm == 0
    cb = col0 // tn
    rb = row0 // tm
    return pl.pallas_call(
        _matmul_body,
        grid=(ncols // tn, t // tm),
        in_specs=[pl.BlockSpec((tm, k), lambda j, i: (i + rb, 0)),
                  pl.BlockSpec((k, tn), lambda j, i: (0, j + cb))],
        out_specs=[pl.BlockSpec((tm, tn), lambda j, i: (i, j)) for _ in out_dtypes],
        out_shape=[jax.ShapeDtypeStruct((t, ncols), dt) for dt in out_dtypes],
        scratch_shapes=[pltpu.VMEM((k, tn), BF16)],
        compiler_params=_cparams(("arbitrary", "arbitrary"), VMEM_LIMIT),
        name=name,
    )(a, w)


def _float_sort_key(x):
    bits = lax.bitcast_convert_type(x, I32)
    return bits ^ ((bits >> 31) & INT_MAX)


def _select_body(qi_ref, wi_ref, kc_ref, mask_ref, key_scr, jm_scr, *, bq, sk, sk_real, tk, pos0, topk):
    b = pl.program_id(1)
    t0 = pos0 + b * bq
    kmax = jnp.minimum(sk_real, ((t0 + bq - 1) // CHUNK + 1) * CHUNK)
    nkt = (kmax + tk - 1) // tk
    nch = tk // LANES

    w = wi_ref[...] * IDX_W_SCALE
    wb = [jnp.broadcast_to(w[:, h:h + 1], (bq, LANES)) for h in range(IDX_HEADS)]
    qpos = t0 + lax.broadcasted_iota(I32, (LANES, LANES), 1)
    lim = jnp.minimum((qpos // CHUNK + 1) * CHUNK, sk_real)
    sub = lax.broadcasted_iota(I32, (LANES, LANES), 0)

    def to_lanes(x):
        if bq < LANES:
            x = jnp.concatenate([x, jnp.zeros((LANES - bq, LANES), x.dtype)], axis=0)
        return x.T

    def score_tile(j, carry):
        off = pl.multiple_of(j * tk, tk)
        kc = kc_ref[pl.ds(off, tk), :].astype(BF16)
        ka, kb = kc[:, :LANES], kc[:, LANES:]
        accs = [jnp.zeros((bq, LANES), F32) for _ in range(nch)]
        for g in range(IDX_HEADS // 2):
            qg = qi_ref[:, g * LANES:(g + 1) * LANES]
            sa = _dot_nt(qg, ka)
            sb = _dot_nt(qg, kb)
            for c in range(nch):
                sl = slice(c * LANES, (c + 1) * LANES)
                accs[c] = accs[c] + wb[2 * g] * jnp.maximum(sa[:, sl], 0.0) \
                    + wb[2 * g + 1] * jnp.maximum(sb[:, sl], 0.0)
        for c in range(nch):
            kpos = off + c * LANES + sub
            sc = jnp.where(kpos < lim, to_lanes(accs[c]), -jnp.inf)
            key_scr[pl.ds(off + c * LANES, LANES), :] = _float_sort_key(sc)
        return carry

    lax.fori_loop(0, nkt, score_tile, 0)

    def count(indicator):
        def tile(j, cnt):
            off = pl.multiple_of(j * tk, tk)
            for c in range(nch):
                kt = key_scr[pl.ds(off + c * LANES, LANES), :]
                cnt = cnt + indicator(kt, off + c * LANES + sub)
            return cnt
        cnt = lax.fori_loop(0, nkt, tile, jnp.zeros((LANES, LANES), F32))
        return jnp.sum(cnt, axis=0, keepdims=True)

    def bit_step(i, pfx_u):
        bit = lax.shift_left(jnp.int32(1), 31 - i)
        cand_u = pfx_u | bit
        cand_s = cand_u ^ INT_MIN
        total = count(lambda kt, kp: jnp.where(kt >= cand_s, 1.0, 0.0))
        return jnp.where(total >= topk, cand_u, pfx_u)

    pfx = lax.fori_loop(0, 32, bit_step, jnp.zeros((1, LANES), I32))
    thr = pfx ^ INT_MIN

    n_gt = count(lambda kt, kp: jnp.where(kt > thr, 1.0, 0.0))
    n_ge = count(lambda kt, kp: jnp.where(kt >= thr, 1.0, 0.0))
    quota = topk - n_gt
    jm_scr[...] = jnp.full((1, LANES), INT_MAX, I32)
    any_excess = jnp.max(jnp.where(n_ge > topk, 1.0, 0.0)) > 0.0

    @pl.when(any_excess)
    def _():
        nbits = max(1, int(sk - 1).bit_length())

        def idx_step(i, ans):
            cand = ans | lax.shift_left(jnp.int32(1), nbits - 1 - i)
            below = count(lambda kt, kp: jnp.where(kt == thr, jnp.where(kp < cand, 1.0, 0.0), 0.0))
            return jnp.where(below < quota, cand, ans)

        jm_scr[...] = lax.fori_loop(0, nbits, idx_step, jnp.zeros((1, LANES), I32))

    jm = jm_scr[...]

    def write_tile(j, carry):
        off = pl.multiple_of(j * tk, tk)
        for c in range(nch):
            kt = key_scr[pl.ds(off + c * LANES, LANES), :]
            kpos = off + c * LANES + sub
            v = jnp.where(kt > thr, 0.0, jnp.where(kt == thr, jnp.where(kpos <= jm, 0.0, NEG), NEG))
            v = jnp.where(kpos < lim, v, NEG)
            mask_ref[:, pl.ds(off + c * LANES, LANES)] = v.T[:bq].astype(mask_ref.dtype)
        return carry

    lax.fori_loop(0, nkt, write_tile, 0)

    def fill_tile(j, carry):
        off = pl.multiple_of(j * tk, tk)
        mask_ref[:, pl.ds(off, tk)] = jnp.full((bq, tk), NEG, mask_ref.dtype)
        return carry

    lax.fori_loop(nkt, sk // tk, fill_tile, 0)


def dsa_select(qi, wi_arr, wi_blk, kc, *, n_seq, sq, bq, sk, sk_real, tk, pos0, q_row0):
    nqb = sq // bq
    qb0 = q_row0 // bq
    assert q_row0 % bq == 0 and sk % tk == 0 and tk >= TOPK_MAX
    topk = min(TOPK_MAX, sk_real // 4)
    body = functools.partial(_select_body, bq=bq, sk=sk, sk_real=sk_real, tk=tk, pos0=pos0, topk=topk)
    return pl.pallas_call(
        body,
        grid=(n_seq, nqb),
        in_specs=[pl.BlockSpec((bq, IDX_HEADS * IDX_DIM), lambda s, b: (qb0 + s * nqb + b, 0)),
                  pl.BlockSpec((bq, LANES), lambda s, b: (qb0 + s * nqb + b, wi_blk)),
                  pl.BlockSpec((sk, 2 * LANES), lambda s, b: (s, 0))],
        out_specs=pl.BlockSpec((bq, sk), lambda s, b: (s * nqb + b, 0)),
        out_shape=jax.ShapeDtypeStruct((n_seq * sq, sk), BF16),
        scratch_shapes=[pltpu.VMEM((sk, LANES), I32), pltpu.VMEM((1, LANES), I32)],
        compiler_params=_cparams(("parallel", "arbitrary"), VMEM_LIMIT),
        name="dsa_select",
    )(qi, wi_arr, kc)


def _rel_tables_body(bias_ref, tab_ref):
    i = lax.broadcasted_iota(I32, (KEY_BLOCK, KEY_BLOCK), 0)
    j = lax.broadcasted_iota(I32, (KEY_BLOCK, KEY_BLOCK), 1)
    nb = NUM_BUCKETS // 2
    max_exact = nb // 2
    edges = [12, 16, 23, 32, 46, 64, 91]
    for d in range(4):
        rel = j - i + (d - 2) * KEY_BLOCK
        n = jnp.abs(rel)
        large = jnp.full_like(n, max_exact)
        for e in edges:
            large = large + jnp.where(n >= e, 1, 0)
        bucket = jnp.where(rel > 0, nb, 0) + jnp.where(n < max_exact, n, large)
        for h in range(A_HEADS):
            acc = jnp.zeros((KEY_BLOCK, KEY_BLOCK), F32)
            for k in range(NUM_BUCKETS):
                acc = jnp.where(bucket == k, bias_ref[k, h], acc)
            tab_ref[h, d] = acc * LOG2E


def rel_bias_tables(rel_bias):
    return pl.pallas_call(
        _rel_tables_body,
        in_specs=[pl.BlockSpec(memory_space=pltpu.SMEM)],
        out_shape=jax.ShapeDtypeStruct((A_HEADS, 4, KEY_BLOCK, KEY_BLOCK), F32),
        name="rel_bias_tables",
    )(rel_bias)


def _causal_tables_body(tab_ref):
    i = lax.broadcasted_iota(I32, (KEY_BLOCK, KEY_BLOCK), 0)
    j = lax.broadcasted_iota(I32, (KEY_BLOCK, KEY_BLOCK), 1)
    zero = jnp.zeros((KEY_BLOCK, KEY_BLOCK), F32)
    tab_ref[0, 0] = zero
    tab_ref[0, 1] = zero
    tab_ref[0, 2] = jnp.where(j // CHUNK <= i // CHUNK, 0.0, NEG)
    tab_ref[0, 3] = jnp.full((KEY_BLOCK, KEY_BLOCK), NEG, F32)


def causal_tables():
    return pl.pallas_call(
        _causal_tables_body,
        out_shape=jax.ShapeDtypeStruct((1, 4, KEY_BLOCK, KEY_BLOCK), F32),
        name="causal_tables",
    )()


def _attn_pipe_body(*refs, has_mask, per_head_tab, bq, hg, dq, dv, scale, pos0, sk):
    q_ref, k_ref, v_ref, tab_ref = refs[:4]
    mask_ref = refs[4] if has_mask else None
    out_ref = refs[5] if has_mask else refs[4]
    s_scr, p_scr, m_scr, l_scr, acc_scr = refs[-5:]

    b = pl.program_id(2)
    t0 = pos0 + b * bq
    qblk = t0 // KEY_BLOCK
    kmax = jnp.minimum(sk, ((t0 + bq - 1) // CHUNK + 1) * CHUNK)
    nkt = (kmax + ATT_TILE - 1) // ATT_TILE
    nch = ATT_TILE // LANES

    m_scr[...] = jnp.full(m_scr.shape, NEG, F32)
    l_scr[...] = jnp.zeros(l_scr.shape, F32)
    acc_scr[...] = jnp.zeros(acc_scr.shape, F32)

    def logits_tile(j, carry):
        off = pl.multiple_of(j * ATT_TILE, ATT_TILE)
        kb0 = off // KEY_BLOCK
        ds = [jnp.clip(kb0 + c - qblk + 2, 0, 3) for c in range(nch)]
        if has_mask:
            mk = mask_ref[:, pl.ds(off, ATT_TILE)].astype(F32)
        for h in range(hg):
            s = _dot(q_ref[:, h * dq:(h + 1) * dq], k_ref[h * dq:(h + 1) * dq, pl.ds(off, ATT_TILE)]) * scale
            hh = h if per_head_tab else 0
            s = s + jnp.concatenate([tab_ref[hh, ds[c]] for c in range(nch)], axis=1)
            if has_mask:
                s = s + mk
            s_scr[h, :, pl.ds(off, ATT_TILE)] = s
            mvec = m_scr[h]
            for c in range(nch):
                mvec = jnp.maximum(mvec, s[:, c * LANES:(c + 1) * LANES])
            m_scr[h] = mvec
        return carry

    lax.fori_loop(0, nkt, logits_tile, 0)
    for h in range(hg):
        m_scr[h] = jnp.broadcast_to(jnp.max(m_scr[h], axis=1, keepdims=True), (bq, LANES))

    def exp_tile(j, carry):
        off = pl.multiple_of(j * ATT_TILE, ATT_TILE)
        for h in range(hg):
            s = s_scr[h, :, pl.ds(off, ATT_TILE)]
            m = m_scr[h]
            p = [jnp.exp2(s[:, c * LANES:(c + 1) * LANES] - m) for c in range(nch)]
            lvec = l_scr[h]
            for c in range(nch):
                lvec = lvec + p[c]
            l_scr[h] = lvec
            p_scr[h, :, pl.ds(off, ATT_TILE)] = jnp.concatenate(p, axis=1).astype(BF16)
        return carry

    lax.fori_loop(0, nkt, exp_tile, 0)

    def pv_tile(j, carry):
        off = pl.multiple_of(j * ATT_TILE, ATT_TILE)
        for h in range(hg):
            acc_scr[h] = acc_scr[h] + _dot(p_scr[h, :, pl.ds(off, ATT_TILE)],
                                           v_ref[pl.ds(off, ATT_TILE), h * dv:(h + 1) * dv])
        return carry

    lax.fori_loop(0, nkt, pv_tile, 0)
    for h in range(hg):
        l_row = jnp.sum(l_scr[h], axis=1, keepdims=True)
        out_ref[:, h * dv:(h + 1) * dv] = (acc_scr[h] / l_row).astype(out_ref.dtype)


def _attn_body(*refs, n_seg, has_mask, per_head_tab, bq, hg, dq, dv, scale, pos0, seg_tiles):
    q_ref = refs[0]
    kv = refs[1:1 + 2 * n_seg]
    pos = 1 + 2 * n_seg
    tab_ref = refs[pos]
    pos += 1
    mask_ref = refs[pos] if has_mask else None
    pos += 1 if has_mask else 0
    out_ref = refs[pos]
    s_scr, p_scr = refs[pos + 1:pos + 3]

    b = pl.program_id(2)
    t0 = pos0 + b * bq
    qblk = t0 // KEY_BLOCK

    def lane_pad(x, fill):
        w = x.shape[1]
        return x if w == LANES else jnp.concatenate([x, jnp.full((bq, LANES - w), fill, F32)], axis=1)

    mvec = [jnp.full((bq, LANES), NEG, F32) for _ in range(hg)]
    for (si, row0, key0, width) in seg_tiles:
        nch = max(1, width // KEY_BLOCK)
        cw = min(width, KEY_BLOCK)
        ds = [jnp.clip(key0 // KEY_BLOCK + c - qblk + 2, 0, 3) for c in range(nch)]
        k_t = kv[2 * si][row0:row0 + width, :].astype(BF16)
        if has_mask:
            mk = mask_ref[:, key0:key0 + width].astype(F32)
        for h in range(hg):
            hh = h if per_head_tab else 0
            s = _dot_nt(q_ref[:, h * dq:(h + 1) * dq], k_t[:, h * dq:(h + 1) * dq]) * scale
            bias = [tab_ref[hh, ds[c], :bq, :cw] for c in range(nch)]
            s = s + (jnp.concatenate(bias, axis=1) if nch > 1 else bias[0])
            if has_mask:
                s = s + mk
            s_scr[h, :, key0:key0 + width] = s
            for c in range(nch):
                mvec[h] = jnp.maximum(mvec[h], lane_pad(s[:, c * cw:(c + 1) * cw], NEG))

    m = [jnp.broadcast_to(jnp.max(mvec[h], axis=1, keepdims=True), (bq, LANES)) for h in range(hg)]
    lvec = [jnp.zeros((bq, LANES), F32) for _ in range(hg)]
    for (si, row0, key0, width) in seg_tiles:
        nch = max(1, width // KEY_BLOCK)
        cw = min(width, KEY_BLOCK)
        for h in range(hg):
            s = s_scr[h, :, key0:key0 + width]
            p = [jnp.exp2(s[:, c * cw:(c + 1) * cw] - m[h][:, :cw]) for c in range(nch)]
            for c in range(nch):
                lvec[h] = lvec[h] + lane_pad(p[c], 0.0)
            p_scr[h, :, key0:key0 + width] = (jnp.concatenate(p, axis=1) if nch > 1 else p[0]).astype(BF16)

    acc = [jnp.zeros((bq, dv), F32) for _ in range(hg)]
    for (si, row0, key0, width) in seg_tiles:
        v_t = kv[2 * si + 1][row0:row0 + width, :].astype(BF16)
        for h in range(hg):
            acc[h] = acc[h] + _dot(p_scr[h, :, key0:key0 + width], v_t[:, h * dv:(h + 1) * dv])

    for h in range(hg):
        l_row = jnp.sum(lvec[h], axis=1, keepdims=True)
        out_ref[:, h * dv:(h + 1) * dv] = (acc[h] / l_row).astype(out_ref.dtype)


def attention(q, segs, tab, mask, *, n_seq, sq, bq, q_row0, q_col0, n_heads, hg, dq, dv, scale, pos0,
              seg_tiles=None, dyn_sk=None, name="attention"):
    nqb = sq // bq
    ng = n_heads // hg
    qb0 = q_row0 // bq
    assert q_row0 % bq == 0 and q_col0 % (hg * dq) == 0 and pos0 % KEY_BLOCK == 0
    scale = scale * LOG2E
    assert bq == KEY_BLOCK or nqb == 1
    qc0 = q_col0 // (hg * dq)
    in_specs = [pl.BlockSpec((bq, hg * dq), lambda s, g, b: (qb0 + s * nqb + b, qc0 + g))]
    args = [q]
    for (k, v, rows, kc0, vc0) in segs:
        assert kc0 % (hg * dq) == 0 and vc0 % (hg * dv) == 0
        mode = dict(pipeline_mode=pl.Buffered(1)) if dyn_sk is not None else {}
        if dyn_sk is not None:
            assert kc0 == 0 and k.shape == (n_seq * n_heads * dq, rows)
            in_specs.append(pl.BlockSpec((hg * dq, rows), lambda s, g, b: (s * ng + g, 0), **mode))
        else:
            in_specs.append(pl.BlockSpec((rows, hg * dq), lambda s, g, b, c=kc0 // (hg * dq): (s, c + g)))
        in_specs.append(pl.BlockSpec((rows, hg * dv), lambda s, g, b, c=vc0 // (hg * dv): (s, c + g), **mode))
        args += [k, v]
    per_head_tab = tab.shape[0] > 1
    if per_head_tab:
        in_specs.append(pl.BlockSpec((hg, 4, KEY_BLOCK, KEY_BLOCK), lambda s, g, b: (g, 0, 0, 0)))
    else:
        in_specs.append(pl.BlockSpec((1, 4, KEY_BLOCK, KEY_BLOCK), lambda s, g, b: (0, 0, 0, 0)))
    args.append(tab)
    if mask is not None:
        in_specs.append(pl.BlockSpec((bq, mask.shape[1]), lambda s, g, b: (s * nqb + b, 0)))
        args.append(mask)
    if dyn_sk is not None:
        assert len(segs) == 1 and dyn_sk % ATT_TILE == 0 and bq == KEY_BLOCK
        body = functools.partial(_attn_pipe_body, has_mask=mask is not None, per_head_tab=per_head_tab,
                                 bq=bq, hg=hg, dq=dq, dv=dv, scale=scale, pos0=pos0, sk=dyn_sk)
        scratch = [pltpu.VMEM((hg, bq, dyn_sk), F32), pltpu.VMEM((hg, bq, dyn_sk), BF16),
                   pltpu.VMEM((hg, bq, LANES), F32), pltpu.VMEM((hg, bq, LANES), F32),
                   pltpu.VMEM((hg, bq, dv), F32)]
    else:
        body = functools.partial(_attn_body, n_seg=len(segs), has_mask=mask is not None,
                                 per_head_tab=per_head_tab, bq=bq, hg=hg, dq=dq, dv=dv, scale=scale, pos0=pos0,
                                 seg_tiles=seg_tiles)
        sk_tot = max(key0 + width for (_, _, key0, width) in seg_tiles)
        sk_tot = (sk_tot + LANES - 1) // LANES * LANES
        scratch = [pltpu.VMEM((hg, bq, sk_tot), F32), pltpu.VMEM((hg, bq, sk_tot), BF16)]
    return pl.pallas_call(
        body,
        grid=(n_seq, ng, nqb),
        in_specs=in_specs,
        out_specs=pl.BlockSpec((bq, hg * dv), lambda s, g, b: (s * nqb + b, g)),
        out_shape=jax.ShapeDtypeStruct((n_seq * sq, n_heads * dv), BF16),
        scratch_shapes=scratch,
        compiler_params=_cparams(("parallel", "parallel", "arbitrary"), VMEM_LIMIT),
        name=name,
    )(*args)


def _outproj_body(*refs, tn, na, nx, a_first, x_first):
    i = pl.program_id(0)
    j = pl.program_id(1)
    a = _pick_rows(i, a_first, refs[:na])
    w_ref = refs[na]
    x = _pick_rows(i, x_first, refs[na + 1:na + 1 + nx])
    g_ref, wr_ref, br_ref, h_ref, xn_ref, lg_ref, hrow = refs[na + 1 + nx:]
    r = x + _dot(a, w_ref[0].astype(BF16))
    h_ref[...] = r
    hrow[:, pl.ds(pl.multiple_of(j * tn, tn), tn)] = r

    @pl.when(j == pl.num_programs(1) - 1)
    def _():
        xn = _rms(hrow[...], g_ref[...])
        xn_ref[...] = xn
        lg_ref[...] = _dot(xn.astype(BF16), wr_ref[...].astype(BF16)) + br_ref[...]


def outproj_norm_router(a_parts, w_out, layer, x_parts, g_ffn, w_r, b_r, tm, tn):
    k = a_parts[0].shape[1]
    t = sum(p.shape[0] for p in a_parts)
    d = w_out.shape[-1]
    a_specs, a_first = _row_split_specs(a_parts, tm, k, lambda i, j: 0)
    x_specs, x_first = _row_split_specs(x_parts, tm, tn, lambda i, j: j)
    return pl.pallas_call(
        functools.partial(_outproj_body, tn=tn, na=len(a_parts), nx=len(x_parts), a_first=a_first,
                          x_first=x_first),
        grid=(t // tm, d // tn),
        in_specs=a_specs + [pl.BlockSpec((1, k, tn), lambda i, j: (layer, 0, j))] + x_specs + [
            pl.BlockSpec((1, d), lambda i, j: (0, 0)),
            pl.BlockSpec((d, LANES), lambda i, j: (0, 0)),
            pl.BlockSpec((1, LANES), lambda i, j: (0, 0))],
        out_specs=[pl.BlockSpec((tm, tn), lambda i, j: (i, j)),
                   pl.BlockSpec((tm, d), lambda i, j: (i, 0)),
                   pl.BlockSpec((tm, LANES), lambda i, j: (i, 0))],
        out_shape=[jax.ShapeDtypeStruct((t, d), F32), jax.ShapeDtypeStruct((t, d), F32),
                   jax.ShapeDtypeStruct((t, LANES), F32)],
        scratch_shapes=[pltpu.VMEM((tm, d), F32)],
        compiler_params=_cparams(("parallel", "arbitrary"), VMEM_LIMIT),
        name="outproj_norm_router",
    )(*a_parts, w_out, *x_parts, g_ffn.reshape(1, d), w_r, b_r)


def _routing_body(lg_ref, out_ref, cnt_ref, carry, *, tm):
    i = pl.program_id(0)

    @pl.when(i == 0)
    def _():
        carry[...] = jnp.zeros(carry.shape, F32)

    x = lg_ref[...]
    lane = lax.broadcasted_iota(I32, (tm, LANES), 1)
    neg_inf = -jnp.inf

    def rmax(v):
        return jnp.max(v, axis=1, keepdims=True)

    def rmin(v):
        return jnp.min(v, axis=1, keepdims=True)

    def rsum(v):
        return jnp.sum(v, axis=1, keepdims=True)

    gm = lane < N_GROUPS
    gmax = rmax(jnp.where(gm, x, neg_inf))
    gsel = rmin(jnp.where(gm, jnp.where(x == gmax, lane, LANES), LANES))
    gsum = rsum(jnp.where(gm, jnp.exp(x - gmax), 0.0))
    g_w = 1.0 / gsum
    lo = N_GROUPS + gsel * EXPERTS_PER_GROUP
    em = jnp.logical_and(lane >= lo, lane < lo + EXPERTS_PER_GROUP)
    emax = rmax(jnp.where(em, x, neg_inf))
    ee = jnp.where(em, jnp.exp(x - emax), 0.0)
    p = ee / rsum(ee)
    p1 = rmax(jnp.where(em, p, -1.0))
    i1 = rmin(jnp.where(em, jnp.where(p == p1, lane, LANES), LANES))
    em2 = jnp.logical_and(em, lane != i1)
    p2 = rmax(jnp.where(em2, p, -1.0))
    i2 = rmin(jnp.where(em2, jnp.where(p == p2, lane, LANES), LANES))
    den = p1 + p2
    g1 = g_w * (p1 / den)
    g2 = g_w * (p2 / den)

    oh1 = jnp.where(lane == i1, 1.0, 0.0)
    oh2 = jnp.where(lane == i2, 1.0, 0.0)
    oh = oh1 + oh2
    r = lax.broadcasted_iota(I32, (tm, tm), 0)
    c = lax.broadcasted_iota(I32, (tm, tm), 1)
    lower = jnp.where(c < r, 1.0, 0.0).astype(BF16)
    before = _dot(lower, oh.astype(BF16)) + carry[...]
    rank1 = rsum(oh1 * before)
    rank2 = rsum(oh2 * before)
    carry[...] = carry[...] + jnp.sum(oh, axis=0, keepdims=True)

    e1 = (i1 - N_GROUPS).astype(F32)
    e2 = (i2 - N_GROUPS).astype(F32)
    vals = [e1, e2, rank1, rank2, g1, g2]
    out = jnp.zeros((tm, LANES), F32)
    for k, v in enumerate(vals):
        out = jnp.where(lane == k, v, out)
    out_ref[...] = out
    cnt_ref[...] = carry[...]


def moe_routing(logits, tm):
    t = logits.shape[0]
    return pl.pallas_call(
        functools.partial(_routing_body, tm=tm),
        grid=(t // tm,),
        in_specs=[pl.BlockSpec((tm, LANES), lambda i: (i, 0))],
        out_specs=[pl.BlockSpec((tm, LANES), lambda i: (i, 0)), pl.BlockSpec((1, LANES), lambda i: (0, 0))],
        out_shape=[jax.ShapeDtypeStruct((t, LANES), F32), jax.ShapeDtypeStruct((1, LANES), F32)],
        scratch_shapes=[pltpu.VMEM((1, LANES), F32)],
        compiler_params=_cparams(("arbitrary",)),
        name="moe_routing",
    )(logits)


def _slot_tokens_body(slots_ref, tok_ref):
    def clear(i, c):
        tok_ref[i] = 0
        return c

    lax.fori_loop(0, tok_ref.shape[0], clear, 0, unroll=8)

    def put(a, c):
        tok_ref[slots_ref[a]] = lax.shift_right_logical(a, 1)
        return c

    lax.fori_loop(0, slots_ref.shape[0], put, 0, unroll=8)


def moe_slot_tokens(slots, n_slots):
    return pl.pallas_call(
        _slot_tokens_body,
        in_specs=[pl.BlockSpec(memory_space=pltpu.SMEM)],
        out_specs=pl.BlockSpec(memory_space=pltpu.SMEM),
        out_shape=jax.ShapeDtypeStruct((n_slots,), I32),
        name="moe_slot_tokens",
    )(slots)


def _dispatch_body(nslot_ref, tok_ref, x_hbm, xs_ref, buf, sem, *, rows):
    i = pl.program_id(0)
    ni = pl.num_programs(0)
    n_used = nslot_ref[0]

    def issue(blk, s):
        @pl.when(blk * rows < n_used)
        def _():
            def one(r, carry):
                src = pl.multiple_of(tok_ref[blk * rows + r] * ROW_CHUNKS, ROW_CHUNKS)
                dst = pl.multiple_of(r * ROW_CHUNKS, ROW_CHUNKS)
                pltpu.make_async_copy(x_hbm.at[pl.ds(src, ROW_CHUNKS), :],
                                      buf.at[s, pl.ds(dst, ROW_CHUNKS), :], sem.at[s]).start()
                return carry

            lax.fori_loop(0, rows, one, 0, unroll=8)

    @pl.when(i == 0)
    def _():
        issue(0, 0)

    @pl.when(i + 1 < ni)
    def _():
        issue(i + 1, (i + 1) % 2)

    slot = i % 2

    @pl.when(i * rows < n_used)
    def _():
        pltpu.make_async_copy(x_hbm.at[pl.ds(0, rows * ROW_CHUNKS), :], buf.at[slot], sem.at[slot]).wait()
        for j in range(ROW_CHUNKS):
            xs_ref[:, j * LANES:(j + 1) * LANES] = \
                buf[slot, pl.ds(j, rows, stride=ROW_CHUNKS), :].astype(xs_ref.dtype)

    @pl.when(i * rows >= n_used)
    def _():
        xs_ref[...] = jnp.zeros(xs_ref.shape, xs_ref.dtype)


def moe_dispatch(xn, slot_tok, n_used, rows):
    t, d = xn.shape
    assert d == ROW_CHUNKS * LANES
    xn = xn.reshape(t * ROW_CHUNKS, LANES)
    n_slots = slot_tok.shape[0]
    assert n_slots % rows == 0
    grid_spec = pltpu.PrefetchScalarGridSpec(
        num_scalar_prefetch=2,
        grid=(n_slots // rows,),
        in_specs=[pl.BlockSpec(memory_space=pl.ANY)],
        out_specs=pl.BlockSpec((rows, d), lambda i, ns, tk: (i, 0)),
        scratch_shapes=[pltpu.VMEM((2, rows * ROW_CHUNKS, LANES), F32), pltpu.SemaphoreType.DMA((2,))],
    )
    return pl.pallas_call(
        functools.partial(_dispatch_body, rows=rows),
        grid_spec=grid_spec,
        out_shape=jax.ShapeDtypeStruct((n_slots, d), BF16),
        compiler_params=_cparams(("arbitrary",), VMEM_LIMIT),
        name="moe_dispatch",
    )(n_used, slot_tok, xn)


def _experts_body(blk0_ref, nblk_ref, ntot_ref, xs_hbm, wg_ref, wu_ref, wd_ref, ys_hbm,
                  xbuf, ybuf, zbuf, xsem, ysem, zsem, wgb, wub, wdb, *, bm):
    e = pl.program_id(0)
    n_total = ntot_ref[0]

    def x_copy(g, s):
        return pltpu.make_async_copy(xs_hbm.at[pl.ds(g * bm, bm), :], xbuf.at[s], xsem.at[s])

    def y_copy(g, s):
        return pltpu.make_async_copy(ybuf.at[s], ys_hbm.at[pl.ds(g * bm, bm), :], ysem.at[s])

    @pl.when(jnp.logical_and(e == 0, n_total > 0))
    def _():
        x_copy(0, 0).start()

    @pl.when(nblk_ref[e] > 0)
    def _():
        wgb[...] = wg_ref[0, 0].astype(BF16)
        wub[...] = wu_ref[0, 0].astype(BF16)
        wdb[...] = wd_ref[0, 0].astype(BF16)

    def block(i, carry):
        g = blk0_ref[e] + i
        s = g % 2

        @pl.when(g + 1 < n_total)
        def _():
            x_copy(g + 1, 1 - s).start()

        x_copy(g, s).wait()
        x = xbuf[s]
        gt = _dot(x, wgb[...])
        up = _dot(x, wub[...])
        a = (gt * (1.0 / (1.0 + jnp.exp(-gt)))) * up
        y = _dot(a.astype(BF16), wdb[...])

        @pl.when(g >= 2)
        def _():
            y_copy(g - 2, s).wait()

        ybuf[s] = y
        y_copy(g, s).start()
        return carry

    lax.fori_loop(0, nblk_ref[e], block, 0)

    @pl.when(e == pl.num_programs(0) - 1)
    def _():
        for back in (2, 1):
            @pl.when(n_total >= back)
            def _():
                y_copy(n_total - back, (n_total - back) % 2).wait()

        def z_copy(g):
            return pltpu.make_async_copy(zbuf, ys_hbm.at[pl.ds(g * bm, bm), :], zsem)

        zbuf[...] = jnp.zeros(zbuf.shape, F32)
        n_blocks = ys_hbm.shape[0] // bm
        lax.fori_loop(n_total, n_blocks, lambda g, c: (z_copy(g).start(), c)[1], 0)
        lax.fori_loop(n_total, n_blocks, lambda g, c: (z_copy(g).wait(), c)[1], 0)


def moe_experts(xs, blk0, nblk, n_total, w_gate, w_up, w_down, layer, bm):
    n_slots, d = xs.shape
    n_exp = blk0.shape[0]
    de = w_gate.shape[-1]
    grid_spec = pltpu.PrefetchScalarGridSpec(
        num_scalar_prefetch=3,
        grid=(n_exp,),
        in_specs=[pl.BlockSpec(memory_space=pl.ANY),
                  pl.BlockSpec((1, 1, d, de), lambda e, b0, nb, nt: (layer, e, 0, 0)),
                  pl.BlockSpec((1, 1, d, de), lambda e, b0, nb, nt: (layer, e, 0, 0)),
                  pl.BlockSpec((1, 1, de, d), lambda e, b0, nb, nt: (layer, e, 0, 0))],
        out_specs=pl.BlockSpec(memory_space=pl.ANY),
        scratch_shapes=[pltpu.VMEM((2, bm, d), BF16), pltpu.VMEM((2, bm, d), F32), pltpu.VMEM((bm, d), F32),
                        pltpu.SemaphoreType.DMA((2,)), pltpu.SemaphoreType.DMA((2,)), pltpu.SemaphoreType.DMA,
                        pltpu.VMEM((d, de), BF16), pltpu.VMEM((d, de), BF16), pltpu.VMEM((de, d), BF16)],
    )
    return pl.pallas_call(
        functools.partial(_experts_body, bm=bm),
        grid_spec=grid_spec,
        out_shape=jax.ShapeDtypeStruct((n_slots, d), F32),
        compiler_params=_cparams(("arbitrary",), VMEM_LIMIT),
        name="moe_experts",
    )(blk0, nblk, n_total, xs, w_gate, w_up, w_down)


def _combine_body(slot_ref, ys_hbm, h_ref, rt_ref, g_ref, *rest, tm, tile0, want_h):
    outs, (buf, sem) = rest[:-2], rest[-2:]
    i = pl.program_id(0)
    ni = pl.num_programs(0)

    def row_copy(src_row, k, r, s):
        return pltpu.make_async_copy(ys_hbm.at[pl.ds(src_row, 1), :], buf.at[s, k, pl.ds(r, 1), :], sem.at[s])

    def issue(tile, s):
        def one(r, carry):
            base = ((tile0 + tile) * tm + r) * 2
            row_copy(slot_ref[base], 0, r, s).start()
            row_copy(slot_ref[base + 1], 1, r, s).start()
            return carry

        lax.fori_loop(0, tm, one, 0, unroll=8)

    @pl.when(i == 0)
    def _():
        issue(0, 0)

    @pl.when(i + 1 < ni)
    def _():
        issue(i + 1, (i + 1) % 2)

    slot = i % 2
    for k in range(2):
        pltpu.make_async_copy(ys_hbm.at[pl.ds(0, tm), :], buf.at[slot, k], sem.at[slot]).wait()

    rt = rt_ref[...]
    g1 = rt[:, 4:5]
    g2 = rt[:, 5:6]
    h2 = h_ref[...] + (buf[slot, 0] * g1 + buf[slot, 1] * g2)
    if want_h:
        outs[0][...] = h2
    outs[-1][...] = _rms(h2, g_ref[...]).astype(outs[-1].dtype)


def moe_combine(ys, slots, h1, route, g_next, tm, u_dtype, row0=0, nrows=None, want_h=True):
    d = h1.shape[1]
    t = h1.shape[0] if nrows is None else nrows
    assert row0 % tm == 0 and t % tm == 0
    tile0 = row0 // tm
    row_spec = pl.BlockSpec((tm, d), lambda i, sl: (i, 0))
    grid_spec = pltpu.PrefetchScalarGridSpec(
        num_scalar_prefetch=1,
        grid=(t // tm,),
        in_specs=[pl.BlockSpec(memory_space=pl.ANY),
                  pl.BlockSpec((tm, d), lambda i, sl: (i + tile0, 0)),
                  pl.BlockSpec((tm, LANES), lambda i, sl: (i + tile0, 0)),
                  pl.BlockSpec((1, d), lambda i, sl: (0, 0))],
        out_specs=[row_spec, row_spec] if want_h else [row_spec],
        scratch_shapes=[pltpu.VMEM((2, 2, tm, d), F32), pltpu.SemaphoreType.DMA((2,))],
    )
    shapes = [jax.ShapeDtypeStruct((t, d), F32)] if want_h else []
    return pl.pallas_call(
        functools.partial(_combine_body, tm=tm, tile0=tile0, want_h=want_h),
        grid_spec=grid_spec,
        out_shape=shapes + [jax.ShapeDtypeStruct((t, d), u_dtype)],
        compiler_params=_cparams(("arbitrary",), VMEM_LIMIT),
        name="moe_combine",
    )(slots, ys, h1, route, g_next.reshape(1, d))


def hier_moe_layer(h1, xn, logits, w_gate, w_up, w_down, layer, g_next, u_dtype, split=None):
    t = h1.shape[0]
    bm = MOE_BM
    route, counts = moe_routing(logits, _tile(t, 256))
    e = route[:, 0:2].astype(I32)
    rank = route[:, 2:4].astype(I32)
    counts = counts[0, N_GROUPS:N_GROUPS + N_EXPERTS].astype(I32)
    padded = (counts + bm - 1) // bm * bm
    pad_end = jnp.cumsum(padded)
    pad_start = pad_end - padded
    start_of = jnp.sum(jnp.where(e[..., None] == jnp.arange(N_EXPERTS, dtype=I32), pad_start, 0), axis=-1)
    slots = (start_of + rank).reshape(-1)
    nb = (2 * t) // bm + N_EXPERTS
    slot_tok = moe_slot_tokens(slots, nb * bm)
    n_used = pad_end[-1:]
    xs = moe_dispatch(xn, slot_tok, n_used, bm * max(f for f in (4, 2, 1) if nb % f == 0))
    ys = moe_experts(xs, pad_start // bm, padded // bm, n_used // bm, w_gate, w_up, w_down, layer, bm)
    tmc = _tile(t, 256)
    if split is None:
        return moe_combine(ys, slots, h1, route, g_next, tmc, u_dtype)
    (u_a,) = moe_combine(ys, slots, h1, route, g_next, tmc, u_dtype, 0, split, want_h=False)
    (u_b,) = moe_combine(ys, slots, h1, route, g_next, tmc, u_dtype, split, t - split, want_h=False)
    return u_a, u_b


def _swap_halves(x):
    lane = lax.broadcasted_iota(I32, x.shape, 1)
    return jnp.where(lane < QK_ROPE // 2, pltpu.roll(x, LANES - QK_ROPE // 2, 1), pltpu.roll(x, QK_ROPE // 2, 1))


def _mla_prep_body(p_ref, gq_ref, gkv_ref, cos_ref, sin_ref, cq_ref, ckv_ref, ckvb_ref, kr_ref):
    p = p_ref[...]
    cq_ref[...] = _rms(p[:, :Q_LORA], gq_ref[...]).astype(cq_ref.dtype)
    ckv = _rms(p[:, Q_LORA:Q_LORA + KV_LORA], gkv_ref[...])
    ckv_ref[...] = ckv
    ckvb_ref[...] = ckv.astype(ckvb_ref.dtype)
    kr = p[:, Q_LORA + KV_LORA:]
    kr_ref[...] = kr * cos_ref[...] + _swap_halves(kr) * sin_ref[...]


def mla_prep(proj, g_q, g_kv, cos, sin, tm):
    t = proj.shape[0]
    return pl.pallas_call(
        _mla_prep_body,
        grid=(t // tm,),
        in_specs=[pl.BlockSpec((tm, proj.shape[1]), lambda i: (i, 0)),
                  pl.BlockSpec((1, Q_LORA), lambda i: (0, 0)),
                  pl.BlockSpec((1, KV_LORA), lambda i: (0, 0)),
                  pl.BlockSpec((tm, LANES), lambda i: (i, 0)),
                  pl.BlockSpec((tm, LANES), lambda i: (i, 0))],
        out_specs=[pl.BlockSpec((tm, Q_LORA), lambda i: (i, 0)),
                   pl.BlockSpec((tm, KV_LORA), lambda i: (i, 0)),
                   pl.BlockSpec((tm, KV_LORA), lambda i: (i, 0)),
                   pl.BlockSpec((tm, LANES), lambda i: (i, 0))],
        out_shape=[jax.ShapeDtypeStruct((t, Q_LORA), BF16), jax.ShapeDtypeStruct((t, KV_LORA), F32),
                   jax.ShapeDtypeStruct((t, KV_LORA), BF16), jax.ShapeDtypeStruct((t, LANES), F32)],
        compiler_params=_cparams(("parallel",)),
        name="mla_prep",
    )(proj, g_q.reshape(1, -1), g_kv.reshape(1, -1), cos, sin)


def _mla_q_body(cq_ref, w_ref, cos_ref, sin_ref, q_ref, wb):
    @pl.when(pl.program_id(0) == 0)
    def _():
        wb[...] = w_ref[...].astype(BF16)

    cq = cq_ref[...]
    cos = cos_ref[...]
    sin = sin_ref[...]
    for h in range(B_HEADS):
        y = _dot(cq, wb[:, h * MLA_QK_PAD:(h + 1) * MLA_QK_PAD])
        xr = y[:, LANES:]
        q_ref[:, h * MLA_QK_PAD:h * MLA_QK_PAD + LANES] = y[:, :LANES].astype(q_ref.dtype)
        q_ref[:, h * MLA_QK_PAD + LANES:(h + 1) * MLA_QK_PAD] = \
            (xr * cos + _swap_halves(xr) * sin).astype(q_ref.dtype)


def mla_q_up(cq, w_uq_pad, cos, sin, tm):
    t = cq.shape[0]
    n = B_HEADS * MLA_QK_PAD
    return pl.pallas_call(
        _mla_q_body,
        grid=(t // tm,),
        in_specs=[pl.BlockSpec((tm, Q_LORA), lambda i: (i, 0)),
                  pl.BlockSpec((Q_LORA, n), lambda i: (0, 0)),
                  pl.BlockSpec((tm, LANES), lambda i: (i, 0)),
                  pl.BlockSpec((tm, LANES), lambda i: (i, 0))],
        out_specs=pl.BlockSpec((tm, n), lambda i: (i, 0)),
        out_shape=jax.ShapeDtypeStruct((t, n), BF16),
        scratch_shapes=[pltpu.VMEM((Q_LORA, n), BF16)],
        compiler_params=_cparams(("arbitrary",), VMEM_LIMIT),
        name="mla_q_up",
    )(cq, w_uq_pad, cos, sin)


def _mla_kv_body(c_ref, kr_ref, wk_ref, wv_ref, k_ref, v_ref, wkb, wvb):
    @pl.when(pl.program_id(0) == 0)
    def _():
        wkb[...] = wk_ref[0].astype(BF16)
        wvb[...] = wv_ref[0].astype(BF16)

    c = c_ref[...].astype(BF16)
    kr = kr_ref[...].astype(k_ref.dtype)
    kn = _dot(c, wkb[...])
    for h in range(B_HEADS):
        k_ref[:, h * MLA_QK_PAD:h * MLA_QK_PAD + LANES] = kn[:, h * QK_NOPE:(h + 1) * QK_NOPE].astype(k_ref.dtype)
        k_ref[:, h * MLA_QK_PAD + LANES:(h + 1) * MLA_QK_PAD] = kr
    v_ref[...] = _dot(c, wvb[...]).astype(v_ref.dtype)


def mla_kv_up(ckv, kr, w_uk, w_uv, layer, tm):
    t = ckv.shape[0]
    nk = B_HEADS * MLA_QK_PAD
    nv = B_HEADS * V_DIM
    return pl.pallas_call(
        _mla_kv_body,
        grid=(t // tm,),
        in_specs=[pl.BlockSpec((tm, KV_LORA), lambda i: (i, 0)),
                  pl.BlockSpec((tm, LANES), lambda i: (i, 0)),
                  pl.BlockSpec((1, KV_LORA, B_HEADS * QK_NOPE), lambda i: (layer, 0, 0)),
                  pl.BlockSpec((1, KV_LORA, nv), lambda i: (layer, 0, 0))],
        out_specs=[pl.BlockSpec((tm, nk), lambda i: (i, 0)), pl.BlockSpec((tm, nv), lambda i: (i, 0))],
        out_shape=[jax.ShapeDtypeStruct((t, nk), BF16), jax.ShapeDtypeStruct((t, nv), BF16)],
        scratch_shapes=[pltpu.VMEM((KV_LORA, B_HEADS * QK_NOPE), BF16), pltpu.VMEM((KV_LORA, nv), BF16)],
        compiler_params=_cparams(("arbitrary",), VMEM_LIMIT),
        name="mla_kv_up",
    )(ckv, kr, w_uk, w_uv)


def kernel(x_prompt, x_sample, cache_a_k, cache_a_v, cache_a_kidx, cache_b_ckv, cache_b_krope, norm_mix, norm_ffn, norm_final, rel_bias, a_w_in, a_w_out, b_w_in, b_norm_q, b_norm_kv, b_w_uq, b_w_uk, b_w_uv, b_w_out, moe_w_grp, moe_b_grp, moe_w_rtr, moe_b_rtr, moe_w_gate, moe_w_up, moe_w_down):
    n_p, s_p, d = x_prompt.shape
    n_s, s_s, _ = x_sample.shape
    past = cache_a_k.shape[2]
    tp = n_p * s_p
    ts = n_s * s_s
    t = tp + ts
    a_qd = A_HEADS * A_HEAD_DIM
    tm = _tile(math.gcd(tp, ts), 1024)
    tm2 = _tile(math.gcd(tp, ts), 512)

    x_parts = [x_prompt.reshape(tp, d), x_sample.reshape(ts, d)]

    def router_params(i):
        w_r = jnp.concatenate([moe_w_grp[i], moe_w_rtr[i],
                               jnp.zeros((d, LANES - N_GROUPS - N_EXPERTS), F32)], axis=1)
        b_r = jnp.concatenate([moe_b_grp[i], moe_b_rtr[i],
                               jnp.zeros((LANES - N_GROUPS - N_EXPERTS,), F32)]).reshape(1, LANES)
        return w_r, b_r

    u0 = rmsnorm(x_parts, norm_mix[0], tm)
    w_a = a_w_in[0]
    (q_b,) = matmul(u0, w_a, 0, a_qd, [BF16], tm, 512, "a_proj_q")
    tmp, tms = _tile(tp, 1024), _tile(ts, 1024)
    kp_f, kp_b = matmul(u0, w_a, a_qd, a_qd, [F32, BF16], tmp, 512, "a_proj_k", 0, tp)
    vp_f, vp_b = matmul(u0, w_a, 2 * a_qd, a_qd, [F32, BF16], tmp, 512, "a_proj_v", 0, tp)
    ks_f, ks_b = matmul(u0, w_a, a_qd, a_qd, [F32, BF16], tms, 512, "a_proj_k", tp, ts)
    vs_f, vs_b = matmul(u0, w_a, 2 * a_qd, a_qd, [F32, BF16], tms, 512, "a_proj_v", tp, ts)
    (qi_b,) = matmul(u0, w_a, 3 * a_qd, IDX_HEADS * IDX_DIM, [BF16], tm, 512, "a_proj_qi")
    c_ki = 3 * a_qd + IDX_HEADS * IDX_DIM
    w_ki = w_a[:, c_ki:c_ki + IDX_DIM]
    w_wi = w_a[:, c_ki + IDX_DIM:c_ki + IDX_DIM + IDX_HEADS]
    zk = jnp.zeros((d, IDX_DIM), F32)
    w_tail = jnp.concatenate([w_ki, zk, zk, w_ki, w_wi, jnp.zeros((d, LANES - IDX_HEADS), F32)], axis=1)
    (tail,) = matmul(u0, w_tail, 0, 3 * LANES, [F32], tm, 3 * LANES, "a_proj_tail")
    kidx = tail[:, :IDX_DIM]

    bias_tab = rel_bias_tables(rel_bias)
    a_scale = A_HEAD_DIM ** -0.5
    hg = 8
    mask_p = dsa_select(qi_b, tail, 2, tail, n_seq=n_p, sq=s_p, bq=128, sk=s_p, sk_real=s_p, tk=512,
                        pos0=0, q_row0=0)
    kp_t = kp_b.reshape(n_p, s_p, a_qd).transpose(0, 2, 1).reshape(n_p * a_qd, s_p)
    att_p = attention(q_b, [(kp_t, vp_b, s_p, 0, 0)], bias_tab, mask_p, n_seq=n_p, sq=s_p, bq=128, q_row0=0,
                      q_col0=0, n_heads=A_HEADS, hg=8, dq=A_HEAD_DIM, dv=A_HEAD_DIM, scale=a_scale, pos0=0,
                      dyn_sk=s_p, name="dsa_attention_prompt")
    sk_s = past + s_s
    sk_pad = (sk_s + ATT_TILE - 1) // ATT_TILE * ATT_TILE
    ki_past = cache_a_kidx[0]
    zp = jnp.zeros_like(ki_past)
    kc_past = jnp.concatenate([ki_past, zp, zp, ki_past], axis=-1)
    kc_new = tail[tp:, :2 * LANES].reshape(n_s, s_s, 2 * LANES)
    kc_s = jnp.concatenate([kc_past, kc_new, jnp.zeros((n_s, sk_pad - sk_s, 2 * LANES), F32)], axis=1)
    mask_s = dsa_select(qi_b, tail, 2, kc_s.reshape(n_s * sk_pad, 2 * LANES), n_seq=n_s, sq=s_s, bq=s_s,
                        sk=sk_pad, sk_real=sk_s, tk=ATT_TILE, pos0=past, q_row0=tp)
    tiles_s = [(0, r, r, ATT_TILE) for r in range(0, past, ATT_TILE)] + [(1, 0, past, s_s)]
    att_s = attention(q_b, [(cache_a_k[0].reshape(n_s * past, a_qd), cache_a_v[0].reshape(n_s * past, a_qd),
                             past, 0, 0), (ks_b, vs_b, s_s, 0, 0)],
                      bias_tab, mask_s, n_seq=n_s, sq=s_s, bq=s_s, q_row0=tp, q_col0=0, n_heads=A_HEADS, hg=hg,
                      dq=A_HEAD_DIM, dv=A_HEAD_DIM, scale=a_scale, pos0=past, seg_tiles=tiles_s,
                      name="dsa_attention_sample")

    w_r0, b_r0 = router_params(0)
    h1, xn0, lg0 = outproj_norm_router([att_p, att_s], a_w_out, 0, x_parts, norm_ffn[0], w_r0, b_r0, tm2, 512)
    h2, u1 = hier_moe_layer(h1, xn0, lg0, moe_w_gate, moe_w_up, moe_w_down, 0, norm_mix[1], BF16)

    n_in = Q_LORA + KV_LORA + QK_ROPE
    w_b = jnp.concatenate([b_w_in[0], jnp.zeros((d, LANES - QK_ROPE), F32)], axis=1)
    (proj,) = matmul(u1, w_b, 0, n_in + LANES - QK_ROPE, [F32], tm, 384, "b_proj")
    half = QK_ROPE // 2
    inv = ROPE_THETA ** (-jnp.arange(half, dtype=F32) / half)
    pos_all = jnp.concatenate([jnp.tile(jnp.arange(s_p, dtype=I32), n_p),
                               jnp.tile(past + jnp.arange(s_s, dtype=I32), n_s)])
    ang = pos_all.astype(F32)[:, None] * inv[None, :]
    zl = jnp.zeros((t, LANES - QK_ROPE), F32)
    cos_t = jnp.concatenate([jnp.cos(ang), jnp.cos(ang), zl], axis=1)
    sin_t = jnp.concatenate([-jnp.sin(ang), jnp.sin(ang), zl], axis=1)
    cq_b, ckv_f, ckv_b, kr_f = mla_prep(proj, b_norm_q[0], b_norm_kv[0], cos_t, sin_t, tm)
    w_uq_pad = jnp.pad(b_w_uq[0].reshape(Q_LORA, B_HEADS, QK_NOPE + QK_ROPE),
                       ((0, 0), (0, 0), (0, MLA_QK_PAD - QK_NOPE - QK_ROPE))).reshape(Q_LORA, B_HEADS * MLA_QK_PAD)
    qm = mla_q_up(cq_b, w_uq_pad, cos_t, sin_t, tm2)
    km_p, vm_p = mla_kv_up(ckv_b[:tp], kr_f[:tp], b_w_uk, b_w_uv, 0, _tile(tp, 512))
    ctab = causal_tables()
    hgb = 8
    km_t = km_p.reshape(n_p, s_p, B_HEADS * MLA_QK_PAD).transpose(0, 2, 1).reshape(n_p * B_HEADS * MLA_QK_PAD, s_p)
    matt_p = attention(qm, [(km_t, vm_p, s_p, 0, 0)], ctab, None, n_seq=n_p, sq=s_p, bq=128, q_row0=0, q_col0=0,
                       n_heads=B_HEADS, hg=8, dq=MLA_QK_PAD, dv=V_DIM, scale=B_SCALE, pos0=0, dyn_sk=s_p,
                       name="mla_attention_prompt")
    ckv_all = jnp.concatenate([cache_b_ckv[0], ckv_f[tp:].reshape(n_s, s_s, KV_LORA)], axis=1)
    kr_past = jnp.concatenate([cache_b_krope[0], jnp.zeros((n_s, past, LANES - QK_ROPE), F32)], axis=-1)
    kr_all = jnp.concatenate([kr_past, kr_f[tp:].reshape(n_s, s_s, LANES)], axis=1)
    km_s, vm_s = mla_kv_up(ckv_all.reshape(n_s * sk_s, KV_LORA), kr_all.reshape(n_s * sk_s, LANES),
                           b_w_uk, b_w_uv, 0, sk_s)
    tiles_m = [(0, r, r, ATT_TILE) for r in range(0, past, ATT_TILE)] + [(0, past, past, s_s)]
    matt_s = attention(qm, [(km_s, vm_s, sk_s, 0, 0)], ctab, None, n_seq=n_s, sq=s_s, bq=s_s, q_row0=tp, q_col0=0,
                       n_heads=B_HEADS, hg=hgb, dq=MLA_QK_PAD, dv=V_DIM, scale=B_SCALE, pos0=past,
                       seg_tiles=tiles_m, name="mla_attention_sample")

    w_r1, b_r1 = router_params(1)
    h3, xn1, lg1 = outproj_norm_router([matt_p, matt_s], b_w_out, 0, [h2], norm_ffn[1], w_r1, b_r1, tm2, 512)
    y_p, y_s = hier_moe_layer(h3, xn1, lg1, moe_w_gate, moe_w_up, moe_w_down, 1, norm_final, F32, split=tp)
    y_prompt = y_p.reshape(n_p, s_p, d)
    y_sample = y_s.reshape(n_s, s_s, d)

    def heads(a, n, s):
        return a.reshape(1, n, s, A_HEADS, A_HEAD_DIM)

    kr_out = kr_f[:, :QK_ROPE]
    return (y_prompt, y_sample,
            heads(kp_f, n_p, s_p), heads(vp_f, n_p, s_p), kidx[:tp].reshape(1, n_p, s_p, IDX_DIM),
            ckv_f[:tp].reshape(1, n_p, s_p, KV_LORA), kr_out[:tp].reshape(1, n_p, s_p, QK_ROPE),
            heads(ks_f, n_s, s_s), heads(vs_f, n_s, s_s), kidx[tp:].reshape(1, n_s, s_s, IDX_DIM),
            ckv_f[tp:].reshape(1, n_s, s_s, KV_LORA), kr_out[tp:].reshape(1, n_s, s_s, QK_ROPE))
```

```python
import functools
import math

import jax
import jax.numpy as jnp
from jax import lax
from jax.experimental import pallas as pl
from jax.experimental.pallas import tpu as pltpu

F32 = jnp.float32
BF16 = jnp.bfloat16
I32 = jnp.int32

RMS_EPS = 1e-6
CHUNK = 64
NEG = -1e30
INT_MIN = -2 ** 31
INT_MAX = 2 ** 31 - 1
LOG2E = math.log2(math.e)

LANES = 128
KEY_BLOCK = 128
ATT_TILE = 256
VMEM_LIMIT = 56 * 1024 * 1024

A_HEADS = 16
A_HEAD_DIM = 128
IDX_HEADS = 16
IDX_DIM = 64
TOPK_MAX = 256
IDX_W_SCALE = float((IDX_HEADS * IDX_DIM) ** -0.5)
NUM_BUCKETS = 32
MAX_DISTANCE = 128
B_HEADS = 16
Q_LORA = 512
KV_LORA = 512
QK_NOPE = 128
QK_ROPE = 64
V_DIM = 128
ROPE_THETA = 10000.0
B_SCALE = float((QK_NOPE + QK_ROPE) ** -0.5)
MLA_QK_PAD = 256
N_GROUPS = 8
EXPERTS_PER_GROUP = 8
N_EXPERTS = 64
D_EXPERT = 512
MOE_BM = 128
ROW_CHUNKS = 16


def _tile(n, pref):
    for c in range(pref, 0, -LANES):
        if n % c == 0:
            return c
    raise ValueError(f"no 128-multiple tile divides {n}")


def _cparams(sem, vmem=None):
    return pltpu.CompilerParams(dimension_semantics=sem, vmem_limit_bytes=vmem)


def _dot(a, b):
    return jnp.dot(a, b, preferred_element_type=F32)


def _dot_nt(a, b):
    return lax.dot_general(a, b, (((1,), (1,)), ((), ())), preferred_element_type=F32)


def _rms(x, g):
    ms = jnp.mean(x * x, axis=-1, keepdims=True)
    return x * lax.rsqrt(ms + RMS_EPS) * g


def _pick_rows(i, n_first, refs):
    if len(refs) == 1:
        return refs[0][...]
    return jnp.where(i < n_first, refs[0][...], refs[1][...])


def _row_split_specs(parts, tm, width, col_of):
    n_first = parts[0].shape[0] // tm
    assert all(p.shape[0] % tm == 0 for p in parts)
    specs = [pl.BlockSpec((tm, width), lambda i, *r: (jnp.minimum(i, n_first - 1), col_of(i, *r)))]
    if len(parts) == 2:
        specs.append(pl.BlockSpec((tm, width), lambda i, *r: (jnp.maximum(i - n_first, 0), col_of(i, *r))))
    return specs, n_first


def _rmsnorm_body(*refs, n_first):
    x = _pick_rows(pl.program_id(0), n_first, refs[:-2])
    g_ref, o_ref = refs[-2:]
    o_ref[...] = _rms(x, g_ref[...]).astype(o_ref.dtype)


def rmsnorm(xs, g, tm, out_dtype=BF16):
    d = xs[0].shape[1]
    t = sum(x.shape[0] for x in xs)
    specs, n_first = _row_split_specs(xs, tm, d, lambda i: 0)
    return pl.pallas_call(
        functools.partial(_rmsnorm_body, n_first=n_first),
        grid=(t // tm,),
        in_specs=specs + [pl.BlockSpec((1, d), lambda i: (0, 0))],
        out_specs=pl.BlockSpec((tm, d), lambda i: (i, 0)),
        out_shape=jax.ShapeDtypeStruct((t, d), out_dtype),
        compiler_params=_cparams(("parallel",)),
        name="rmsnorm",
    )(*xs, g.reshape(1, d))


def _matmul_body(a_ref, w_ref, *rest):
    outs, wb = rest[:-1], rest[-1]

    @pl.when(pl.program_id(1) == 0)
    def _():
        wb[...] = w_ref[...].astype(BF16)

    r = _dot(a_ref[...], wb[...])
    for o in outs:
        o[...] = r.astype(o.dtype)


def matmul(a, w, col0, ncols, out_dtypes, tm, tn, name, row0=0, nrows=None):
    k = a.shape[1]
    t = a.shape[0] if nrows is None else nrows
    assert ncols % tn == 0 and col0 % tn == 0 and t % tm == 0 and row0 % tm == 0
    cb = col0 // tn
    rb = row0 // tm
    return pl.pallas_call(
        _matmul_body,
        grid=(ncols // tn, t // tm),
        in_specs=[pl.BlockSpec((tm, k), lambda j, i: (i + rb, 0)),
                  pl.BlockSpec((k, tn), lambda j, i: (0, j + cb))],
        out_specs=[pl.BlockSpec((tm, tn), lambda j, i: (i, j)) for _ in out_dtypes],
        out_shape=[jax.ShapeDtypeStruct((t, ncols), dt) for dt in out_dtypes],
        scratch_shapes=[pltpu.VMEM((k, tn), BF16)],
        compiler_params=_cparams(("arbitrary", "arbitrary"), VMEM_LIMIT),
        name=name,
    )(a, w)


def _float_sort_key(x):
    bits = lax.bitcast_convert_type(x, I32)
    return bits ^ ((bits >> 31) & INT_MAX)


def _select_body(qi_ref, wi_ref, kc_ref, mask_ref, key_scr, jm_scr, *, bq, sk, sk_real, tk, pos0, topk):
    b = pl.program_id(1)
    t0 = pos0 + b * bq
    kmax = jnp.minimum(sk_real, ((t0 + bq - 1) // CHUNK + 1) * CHUNK)
    nkt = (kmax + tk - 1) // tk
    nch = tk // LANES

    w = wi_ref[...] * IDX_W_SCALE
    wb = [jnp.broadcast_to(w[:, h:h + 1], (bq, LANES)) for h in range(IDX_HEADS)]
    qpos = t0 + lax.broadcasted_iota(I32, (LANES, LANES), 1)
    lim = jnp.minimum((qpos // CHUNK + 1) * CHUNK, sk_real)
    sub = lax.broadcasted_iota(I32, (LANES, LANES), 0)

    def to_lanes(x):
        if bq < LANES:
            x = jnp.concatenate([x, jnp.zeros((LANES - bq, LANES), x.dtype)], axis=0)
        return x.T

    def score_tile(j, carry):
        off = pl.multiple_of(j * tk, tk)
        kc = kc_ref[pl.ds(off, tk), :].astype(BF16)
        ka, kb = kc[:, :LANES], kc[:, LANES:]
        accs = [jnp.zeros((bq, LANES), F32) for _ in range(nch)]
        for g in range(IDX_HEADS // 2):
            qg = qi_ref[:, g * LANES:(g + 1) * LANES]
            sa = _dot_nt(qg, ka)
            sb = _dot_nt(qg, kb)
            for c in range(nch):
                sl = slice(c * LANES, (c + 1) * LANES)
                accs[c] = accs[c] + wb[2 * g] * jnp.maximum(sa[:, sl], 0.0) \
                    + wb[2 * g + 1] * jnp.maximum(sb[:, sl], 0.0)
        for c in range(nch):
            kpos = off + c * LANES + sub
            sc = jnp.where(kpos < lim, to_lanes(accs[c]), -jnp.inf)
            key_scr[pl.ds(off + c * LANES, LANES), :] = _float_sort_key(sc)
        return carry

    lax.fori_loop(0, nkt, score_tile, 0)

    def count(indicator):
        def tile(j, cnt):
            off = pl.multiple_of(j * tk, tk)
            for c in range(nch):
                kt = key_scr[pl.ds(off + c * LANES, LANES), :]
                cnt = cnt + indicator(kt, off + c * LANES + sub)
            return cnt
        cnt = lax.fori_loop(0, nkt, tile, jnp.zeros((LANES, LANES), F32))
        return jnp.sum(cnt, axis=0, keepdims=True)

    def bit_step(i, pfx_u):
        bit = lax.shift_left(jnp.int32(1), 31 - i)
        cand_u = pfx_u | bit
        cand_s = cand_u ^ INT_MIN
        total = count(lambda kt, kp: jnp.where(kt >= cand_s, 1.0, 0.0))
        return jnp.where(total >= topk, cand_u, pfx_u)

    pfx = lax.fori_loop(0, 32, bit_step, jnp.zeros((1, LANES), I32))
    thr = pfx ^ INT_MIN

    n_gt = count(lambda kt, kp: jnp.where(kt > thr, 1.0, 0.0))
    n_ge = count(lambda kt, kp: jnp.where(kt >= thr, 1.0, 0.0))
    quota = topk - n_gt
    jm_scr[...] = jnp.full((1, LANES), INT_MAX, I32)
    any_excess = jnp.max(jnp.where(n_ge > topk, 1.0, 0.0)) > 0.0

    @pl.when(any_excess)
    def _():
        nbits = max(1, int(sk - 1).bit_length())

        def idx_step(i, ans):
            cand = ans | lax.shift_left(jnp.int32(1), nbits - 1 - i)
            below = count(lambda kt, kp: jnp.where(kt == thr, jnp.where(kp < cand, 1.0, 0.0), 0.0))
            return jnp.where(below < quota, cand, ans)

        jm_scr[...] = lax.fori_loop(0, nbits, idx_step, jnp.zeros((1, LANES), I32))

    jm = jm_scr[...]

    def write_tile(j, carry):
        off = pl.multiple_of(j * tk, tk)
        for c in range(nch):
            kt = key_scr[pl.ds(off + c * LANES, LANES), :]
            kpos = off + c * LANES + sub
            v = jnp.where(kt > thr, 0.0, jnp.where(kt == thr, jnp.where(kpos <= jm, 0.0, NEG), NEG))
            v = jnp.where(kpos < lim, v, NEG)
            mask_ref[:, pl.ds(off + c * LANES, LANES)] = v.T[:bq].astype(mask_ref.dtype)
        return carry

    lax.fori_loop(0, nkt, write_tile, 0)

    def fill_tile(j, carry):
        off = pl.multiple_of(j * tk, tk)
        mask_ref[:, pl.ds(off, tk)] = jnp.full((bq, tk), NEG, mask_ref.dtype)
        return carry

    lax.fori_loop(nkt, sk // tk, fill_tile, 0)


def dsa_select(qi, wi_arr, wi_blk, kc, *, n_seq, sq, bq, sk, sk_real, tk, pos0, q_row0):
    nqb = sq // bq
    qb0 = q_row0 // bq
    assert q_row0 % bq == 0 and sk % tk == 0 and tk >= TOPK_MAX
    topk = min(TOPK_MAX, sk_real // 4)
    body = functools.partial(_select_body, bq=bq, sk=sk, sk_real=sk_real, tk=tk, pos0=pos0, topk=topk)
    return pl.pallas_call(
        body,
        grid=(n_seq, nqb),
        in_specs=[pl.BlockSpec((bq, IDX_HEADS * IDX_DIM), lambda s, b: (qb0 + s * nqb + b, 0)),
                  pl.BlockSpec((bq, LANES), lambda s, b: (qb0 + s * nqb + b, wi_blk)),
                  pl.BlockSpec((sk, 2 * LANES), lambda s, b: (s, 0))],
        out_specs=pl.BlockSpec((bq, sk), lambda s, b: (s * nqb + b, 0)),
        out_shape=jax.ShapeDtypeStruct((n_seq * sq, sk), BF16),
        scratch_shapes=[pltpu.VMEM((sk, LANES), I32), pltpu.VMEM((1, LANES), I32)],
        compiler_params=_cparams(("parallel", "arbitrary"), VMEM_LIMIT),
        name="dsa_select",
    )(qi, wi_arr, kc)


def _rel_tables_body(bias_ref, tab_ref):
    i = lax.broadcasted_iota(I32, (KEY_BLOCK, KEY_BLOCK), 0)
    j = lax.broadcasted_iota(I32, (KEY_BLOCK, KEY_BLOCK), 1)
    nb = NUM_BUCKETS // 2
    max_exact = nb // 2
    edges = [12, 16, 23, 32, 46, 64, 91]
    for d in range(4):
        rel = j - i + (d - 2) * KEY_BLOCK
        n = jnp.abs(rel)
        large = jnp.full_like(n, max_exact)
        for e in edges:
            large = large + jnp.where(n >= e, 1, 0)
        bucket = jnp.where(rel > 0, nb, 0) + jnp.where(n < max_exact, n, large)
        for h in range(A_HEADS):
            acc = jnp.zeros((KEY_BLOCK, KEY_BLOCK), F32)
            for k in range(NUM_BUCKETS):
                acc = jnp.where(bucket == k, bias_ref[k, h], acc)
            tab_ref[h, d] = acc * LOG2E


def rel_bias_tables(rel_bias):
    return pl.pallas_call(
        _rel_tables_body,
        in_specs=[pl.BlockSpec(memory_space=pltpu.SMEM)],
        out_shape=jax.ShapeDtypeStruct((A_HEADS, 4, KEY_BLOCK, KEY_BLOCK), F32),
        name="rel_bias_tables",
    )(rel_bias)


def _causal_tables_body(tab_ref):
    i = lax.broadcasted_iota(I32, (KEY_BLOCK, KEY_BLOCK), 0)
    j = lax.broadcasted_iota(I32, (KEY_BLOCK, KEY_BLOCK), 1)
    zero = jnp.zeros((KEY_BLOCK, KEY_BLOCK), F32)
    tab_ref[0, 0] = zero
    tab_ref[0, 1] = zero
    tab_ref[0, 2] = jnp.where(j // CHUNK <= i // CHUNK, 0.0, NEG)
    tab_ref[0, 3] = jnp.full((KEY_BLOCK, KEY_BLOCK), NEG, F32)


def causal_tables():
    return pl.pallas_call(
        _causal_tables_body,
        out_shape=jax.ShapeDtypeStruct((1, 4, KEY_BLOCK, KEY_BLOCK), F32),
        name="causal_tables",
    )()


def _attn_pipe_body(*refs, has_mask, per_head_tab, bq, hg, dq, dv, scale, pos0, sk):
    q_ref, k_ref, v_ref, tab_ref = refs[:4]
    mask_ref = refs[4] if has_mask else None
    out_ref = refs[5] if has_mask else refs[4]
    s_scr, p_scr, m_scr, l_scr, acc_scr = refs[-5:]

    b = pl.program_id(2)
    t0 = pos0 + b * bq
    qblk = t0 // KEY_BLOCK
    kmax = jnp.minimum(sk, ((t0 + bq - 1) // CHUNK + 1) * CHUNK)
    nkt = (kmax + ATT_TILE - 1) // ATT_TILE
    nch = ATT_TILE // LANES

    m_scr[...] = jnp.full(m_scr.shape, NEG, F32)
    l_scr[...] = jnp.zeros(l_scr.shape, F32)
    acc_scr[...] = jnp.zeros(acc_scr.shape, F32)

    def logits_tile(j, carry):
        off = pl.multiple_of(j * ATT_TILE, ATT_TILE)
        kb0 = off // KEY_BLOCK
        ds = [jnp.clip(kb0 + c - qblk + 2, 0, 3) for c in range(nch)]
        if has_mask:
            mk = mask_ref[:, pl.ds(off, ATT_TILE)].astype(F32)
        for h in range(hg):
            s = _dot(q_ref[:, h * dq:(h + 1) * dq], k_ref[h * dq:(h + 1) * dq, pl.ds(off, ATT_TILE)]) * scale
            hh = h if per_head_tab else 0
            s = s + jnp.concatenate([tab_ref[hh, ds[c]] for c in range(nch)], axis=1)
            if has_mask:
                s = s + mk
            s_scr[h, :, pl.ds(off, ATT_TILE)] = s
            mvec = m_scr[h]
            for c in range(nch):
                mvec = jnp.maximum(mvec, s[:, c * LANES:(c + 1) * LANES])
            m_scr[h] = mvec
        return carry

    lax.fori_loop(0, nkt, logits_tile, 0)
    for h in range(hg):
        m_scr[h] = jnp.broadcast_to(jnp.max(m_scr[h], axis=1, keepdims=True), (bq, LANES))

    def exp_tile(j, carry):
        off = pl.multiple_of(j * ATT_TILE, ATT_TILE)
        for h in range(hg):
            s = s_scr[h, :, pl.ds(off, ATT_TILE)]
            m = m_scr[h]
            p = [jnp.exp2(s[:, c * LANES:(c + 1) * LANES] - m) for c in range(nch)]
            lvec = l_scr[h]
            for c in range(nch):
                lvec = lvec + p[c]
            l_scr[h] = lvec
            p_scr[h, :, pl.ds(off, ATT_TILE)] = jnp.concatenate(p, axis=1).astype(BF16)
        return carry

    lax.fori_loop(0, nkt, exp_tile, 0)

    def pv_tile(j, carry):
        off = pl.multiple_of(j * ATT_TILE, ATT_TILE)
        for h in range(hg):
            acc_scr[h] = acc_scr[h] + _dot(p_scr[h, :, pl.ds(off, ATT_TILE)],
                                           v_ref[pl.ds(off, ATT_TILE), h * dv:(h + 1) * dv])
        return carry

    lax.fori_loop(0, nkt, pv_tile, 0)
    for h in range(hg):
        l_row = jnp.sum(l_scr[h], axis=1, keepdims=True)
        out_ref[:, h * dv:(h + 1) * dv] = (acc_scr[h] / l_row).astype(out_ref.dtype)


def _attn_body(*refs, n_seg, has_mask, per_head_tab, bq, hg, dq, dv, scale, pos0, seg_tiles, by_head):
    q_ref = refs[0]
    kv = refs[1:1 + 2 * n_seg]
    pos = 1 + 2 * n_seg
    tab_ref = refs[pos]
    pos += 1
    mask_ref = refs[pos] if has_mask else None
    pos += 1 if has_mask else 0
    out_ref = refs[pos]
    s_scr, p_scr = refs[pos + 1:pos + 3]

    b = pl.program_id(2)
    t0 = pos0 + b * bq
    qblk = t0 // KEY_BLOCK

    def lane_pad(x, fill):
        w = x.shape[1]
        return x if w == LANES else jnp.concatenate([x, jnp.full((bq, LANES - w), fill, F32)], axis=1)

    mvec = [jnp.full((bq, LANES), NEG, F32) for _ in range(hg)]
    for (si, row0, key0, width) in seg_tiles:
        nch = max(1, width // KEY_BLOCK)
        cw = min(width, KEY_BLOCK)
        ds = [jnp.clip(key0 // KEY_BLOCK + c - qblk + 2, 0, 3) for c in range(nch)]
        k_t = None if by_head[si] else kv[2 * si][row0:row0 + width, :].astype(BF16)
        if has_mask:
            mk = mask_ref[:, key0:key0 + width].astype(F32)
        for h in range(hg):
            hh = h if per_head_tab else 0
            if by_head[si]:
                k_h = kv[2 * si][pl.ds(row0 * hg + h, width, stride=hg), :].astype(BF16)
            else:
                k_h = k_t[:, h * dq:(h + 1) * dq]
            s = _dot_nt(q_ref[:, h * dq:(h + 1) * dq], k_h) * scale
            bias = [tab_ref[hh, ds[c], :bq, :cw] for c in range(nch)]
            s = s + (jnp.concatenate(bias, axis=1) if nch > 1 else bias[0])
            if has_mask:
                s = s + mk
            s_scr[h, :, key0:key0 + width] = s
            for c in range(nch):
                mvec[h] = jnp.maximum(mvec[h], lane_pad(s[:, c * cw:(c + 1) * cw], NEG))

    m = [jnp.broadcast_to(jnp.max(mvec[h], axis=1, keepdims=True), (bq, LANES)) for h in range(hg)]
    lvec = [jnp.zeros((bq, LANES), F32) for _ in range(hg)]
    for (si, row0, key0, width) in seg_tiles:
        nch = max(1, width // KEY_BLOCK)
        cw = min(width, KEY_BLOCK)
        for h in range(hg):
            s = s_scr[h, :, key0:key0 + width]
            p = [jnp.exp2(s[:, c * cw:(c + 1) * cw] - m[h][:, :cw]) for c in range(nch)]
            for c in range(nch):
                lvec[h] = lvec[h] + lane_pad(p[c], 0.0)
            p_scr[h, :, key0:key0 + width] = (jnp.concatenate(p, axis=1) if nch > 1 else p[0]).astype(BF16)

    acc = [jnp.zeros((bq, dv), F32) for _ in range(hg)]
    for (si, row0, key0, width) in seg_tiles:
        v_t = None if by_head[si] else kv[2 * si + 1][row0:row0 + width, :].astype(BF16)
        for h in range(hg):
            if by_head[si]:
                v_h = kv[2 * si + 1][pl.ds(row0 * hg + h, width, stride=hg), :].astype(BF16)
            else:
                v_h = v_t[:, h * dv:(h + 1) * dv]
            acc[h] = acc[h] + _dot(p_scr[h, :, key0:key0 + width], v_h)

    for h in range(hg):
        l_row = jnp.sum(lvec[h], axis=1, keepdims=True)
        out_ref[:, h * dv:(h + 1) * dv] = (acc[h] / l_row).astype(out_ref.dtype)


def attention(q, segs, tab, mask, *, n_seq, sq, bq, q_row0, q_col0, n_heads, hg, dq, dv, scale, pos0,
              seg_tiles=None, dyn_sk=None, name="attention"):
    nqb = sq // bq
    ng = n_heads // hg
    qb0 = q_row0 // bq
    assert q_row0 % bq == 0 and q_col0 % (hg * dq) == 0 and pos0 % KEY_BLOCK == 0
    scale = scale * LOG2E
    assert bq == KEY_BLOCK or nqb == 1
    qc0 = q_col0 // (hg * dq)
    in_specs = [pl.BlockSpec((bq, hg * dq), lambda s, g, b: (qb0 + s * nqb + b, qc0 + g))]
    args = [q]
    by_head = []
    for (k, v, rows, kc0, vc0) in segs:
        assert kc0 % (hg * dq) == 0 and vc0 % (hg * dv) == 0
        by_head.append(k.shape[1] == dq and n_heads > 1)
        if by_head[-1]:
            assert hg == n_heads and dq == dv and k.shape[0] == n_seq * rows * n_heads and dyn_sk is None
            for _ in range(2):
                in_specs.append(pl.BlockSpec((rows * n_heads, dq), lambda s, g, b: (s, 0)))
            args += [k, v]
            continue
        mode = dict(pipeline_mode=pl.Buffered(1)) if dyn_sk is not None else {}
        if dyn_sk is not None:
            assert kc0 == 0 and k.shape == (n_seq * n_heads * dq, rows)
            in_specs.append(pl.BlockSpec((hg * dq, rows), lambda s, g, b: (s * ng + g, 0), **mode))
        else:
            in_specs.append(pl.BlockSpec((rows, hg * dq), lambda s, g, b, c=kc0 // (hg * dq): (s, c + g)))
        in_specs.append(pl.BlockSpec((rows, hg * dv), lambda s, g, b, c=vc0 // (hg * dv): (s, c + g), **mode))
        args += [k, v]
    per_head_tab = tab.shape[0] > 1
    if per_head_tab:
        in_specs.append(pl.BlockSpec((hg, 4, KEY_BLOCK, KEY_BLOCK), lambda s, g, b: (g, 0, 0, 0)))
    else:
        in_specs.append(pl.BlockSpec((1, 4, KEY_BLOCK, KEY_BLOCK), lambda s, g, b: (0, 0, 0, 0)))
    args.append(tab)
    if mask is not None:
        in_specs.append(pl.BlockSpec((bq, mask.shape[1]), lambda s, g, b: (s * nqb + b, 0)))
        args.append(mask)
    if dyn_sk is not None:
        assert len(segs) == 1 and dyn_sk % ATT_TILE == 0 and bq == KEY_BLOCK
        body = functools.partial(_attn_pipe_body, has_mask=mask is not None, per_head_tab=per_head_tab,
                                 bq=bq, hg=hg, dq=dq, dv=dv, scale=scale, pos0=pos0, sk=dyn_sk)
        scratch = [pltpu.VMEM((hg, bq, dyn_sk), F32), pltpu.VMEM((hg, bq, dyn_sk), BF16),
                   pltpu.VMEM((hg, bq, LANES), F32), pltpu.VMEM((hg, bq, LANES), F32),
                   pltpu.VMEM((hg, bq, dv), F32)]
    else:
        body = functools.partial(_attn_body, n_seg=len(segs), has_mask=mask is not None,
                                 per_head_tab=per_head_tab, bq=bq, hg=hg, dq=dq, dv=dv, scale=scale, pos0=pos0,
                                 seg_tiles=seg_tiles, by_head=tuple(by_head))
        sk_tot = max(key0 + width for (_, _, key0, width) in seg_tiles)
        sk_tot = (sk_tot + LANES - 1) // LANES * LANES
        scratch = [pltpu.VMEM((hg, bq, sk_tot), F32), pltpu.VMEM((hg, bq, sk_tot), BF16)]
    return pl.pallas_call(
        body,
        grid=(n_seq, ng, nqb),
        in_specs=in_specs,
        out_specs=pl.BlockSpec((bq, hg * dv), lambda s, g, b: (s * nqb + b, g)),
        out_shape=jax.ShapeDtypeStruct((n_seq * sq, n_heads * dv), BF16),
        scratch_shapes=scratch,
        compiler_params=_cparams(("parallel", "parallel", "arbitrary"), VMEM_LIMIT),
        name=name,
    )(*args)


def _outproj_body(*refs, tn, na, nx, a_first, x_first):
    i = pl.program_id(0)
    j = pl.program_id(1)
    a = _pick_rows(i, a_first, refs[:na])
    w_ref = refs[na]
    x = _pick_rows(i, x_first, refs[na + 1:na + 1 + nx])
    g_ref, wr_ref, br_ref, h_ref, xn_ref, lg_ref, hrow = refs[na + 1 + nx:]
    r = x + _dot(a, w_ref[0].astype(BF16))
    h_ref[...] = r
    hrow[:, pl.ds(pl.multiple_of(j * tn, tn), tn)] = r

    @pl.when(j == pl.num_programs(1) - 1)
    def _():
        xn = _rms(hrow[...], g_ref[...])
        xn_ref[...] = xn
        lg_ref[...] = _dot(xn.astype(BF16), wr_ref[...].astype(BF16)) + br_ref[...]


def outproj_norm_router(a_parts, w_out, layer, x_parts, g_ffn, w_r, b_r, tm, tn):
    k = a_parts[0].shape[1]
    t = sum(p.shape[0] for p in a_parts)
    d = w_out.shape[-1]
    a_specs, a_first = _row_split_specs(a_parts, tm, k, lambda i, j: 0)
    x_specs, x_first = _row_split_specs(x_parts, tm, tn, lambda i, j: j)
    return pl.pallas_call(
        functools.partial(_outproj_body, tn=tn, na=len(a_parts), nx=len(x_parts), a_first=a_first,
                          x_first=x_first),
        grid=(t // tm, d // tn),
        in_specs=a_specs + [pl.BlockSpec((1, k, tn), lambda i, j: (layer, 0, j))] + x_specs + [
            pl.BlockSpec((1, d), lambda i, j: (0, 0)),
            pl.BlockSpec((d, LANES), lambda i, j: (0, 0)),
            pl.BlockSpec((1, LANES), lambda i, j: (0, 0))],
        out_specs=[pl.BlockSpec((tm, tn), lambda i, j: (i, j)),
                   pl.BlockSpec((tm, d), lambda i, j: (i, 0)),
                   pl.BlockSpec((tm, LANES), lambda i, j: (i, 0))],
        out_shape=[jax.ShapeDtypeStruct((t, d), F32), jax.ShapeDtypeStruct((t, d), F32),
                   jax.ShapeDtypeStruct((t, LANES), F32)],
        scratch_shapes=[pltpu.VMEM((tm, d), F32)],
        compiler_params=_cparams(("parallel", "arbitrary"), VMEM_LIMIT),
        name="outproj_norm_router",
    )(*a_parts, w_out, *x_parts, g_ffn.reshape(1, d), w_r, b_r)


def _routing_body(lg_ref, out_ref, cnt_ref, carry, *, tm):
    i = pl.program_id(0)

    @pl.when(i == 0)
    def _():
        carry[...] = jnp.zeros(carry.shape, F32)

    x = lg_ref[...]
    lane = lax.broadcasted_iota(I32, (tm, LANES), 1)
    neg_inf = -jnp.inf

    def rmax(v):
        return jnp.max(v, axis=1, keepdims=True)

    def rmin(v):
        return jnp.min(v, axis=1, keepdims=True)

    def rsum(v):
        return jnp.sum(v, axis=1, keepdims=True)

    gm = lane < N_GROUPS
    gmax = rmax(jnp.where(gm, x, neg_inf))
    gsel = rmin(jnp.where(gm, jnp.where(x == gmax, lane, LANES), LANES))
    gsum = rsum(jnp.where(gm, jnp.exp(x - gmax), 0.0))
    g_w = 1.0 / gsum
    lo = N_GROUPS + gsel * EXPERTS_PER_GROUP
    em = jnp.logical_and(lane >= lo, lane < lo + EXPERTS_PER_GROUP)
    emax = rmax(jnp.where(em, x, neg_inf))
    ee = jnp.where(em, jnp.exp(x - emax), 0.0)
    p = ee / rsum(ee)
    p1 = rmax(jnp.where(em, p, -1.0))
    i1 = rmin(jnp.where(em, jnp.where(p == p1, lane, LANES), LANES))
    em2 = jnp.logical_and(em, lane != i1)
    p2 = rmax(jnp.where(em2, p, -1.0))
    i2 = rmin(jnp.where(em2, jnp.where(p == p2, lane, LANES), LANES))
    den = p1 + p2
    g1 = g_w * (p1 / den)
    g2 = g_w * (p2 / den)

    oh1 = jnp.where(lane == i1, 1.0, 0.0)
    oh2 = jnp.where(lane == i2, 1.0, 0.0)
    oh = oh1 + oh2
    r = lax.broadcasted_iota(I32, (tm, tm), 0)
    c = lax.broadcasted_iota(I32, (tm, tm), 1)
    lower = jnp.where(c < r, 1.0, 0.0).astype(BF16)
    before = _dot(lower, oh.astype(BF16)) + carry[...]
    rank1 = rsum(oh1 * before)
    rank2 = rsum(oh2 * before)
    carry[...] = carry[...] + jnp.sum(oh, axis=0, keepdims=True)

    e1 = (i1 - N_GROUPS).astype(F32)
    e2 = (i2 - N_GROUPS).astype(F32)
    vals = [e1, e2, rank1, rank2, g1, g2]
    out = jnp.zeros((tm, LANES), F32)
    for k, v in enumerate(vals):
        out = jnp.where(lane == k, v, out)
    out_ref[...] = out
    cnt_ref[...] = carry[...]


def moe_routing(logits, tm):
    t = logits.shape[0]
    return pl.pallas_call(
        functools.partial(_routing_body, tm=tm),
        grid=(t // tm,),
        in_specs=[pl.BlockSpec((tm, LANES), lambda i: (i, 0))],
        out_specs=[pl.BlockSpec((tm, LANES), lambda i: (i, 0)), pl.BlockSpec((1, LANES), lambda i: (0, 0))],
        out_shape=[jax.ShapeDtypeStruct((t, LANES), F32), jax.ShapeDtypeStruct((1, LANES), F32)],
        scratch_shapes=[pltpu.VMEM((1, LANES), F32)],
        compiler_params=_cparams(("arbitrary",)),
        name="moe_routing",
    )(logits)


def _slot_tokens_body(slots_ref, tok_ref):
    n_tok = slots_ref.shape[0] // 2

    def clear(i, c):
        tok_ref[i] = lax.rem(i, n_tok)
        return c

    lax.fori_loop(0, tok_ref.shape[0], clear, 0, unroll=8)

    def put(a, c):
        tok_ref[slots_ref[a]] = lax.shift_right_logical(a, 1)
        return c

    lax.fori_loop(0, slots_ref.shape[0], put, 0, unroll=8)


def moe_slot_tokens(slots, n_slots):
    return pl.pallas_call(
        _slot_tokens_body,
        in_specs=[pl.BlockSpec(memory_space=pltpu.SMEM)],
        out_specs=pl.BlockSpec(memory_space=pltpu.SMEM),
        out_shape=jax.ShapeDtypeStruct((n_slots,), I32),
        name="moe_slot_tokens",
    )(slots)


def _dispatch_body(nslot_ref, tok_ref, x_hbm, xs_ref, buf, sem, *, rows):
    i = pl.program_id(0)
    ni = pl.num_programs(0)
    n_used = nslot_ref[0]

    def issue(blk, s):
        @pl.when(blk * rows < n_used)
        def _():
            def pair(rp, carry):
                for k in range(2):
                    r = 2 * rp + k
                    src = pl.multiple_of(tok_ref[blk * rows + r] * ROW_CHUNKS, ROW_CHUNKS)
                    dst = pl.multiple_of(r * ROW_CHUNKS, ROW_CHUNKS)
                    pltpu.make_async_copy(x_hbm.at[pl.ds(src, ROW_CHUNKS), :],
                                          buf.at[s, pl.ds(dst, ROW_CHUNKS), :], sem.at[s]).start(priority=k)
                return carry

            lax.fori_loop(0, rows // 2, pair, 0, unroll=4)

    @pl.when(i == 0)
    def _():
        issue(0, 0)

    @pl.when(i + 1 < ni)
    def _():
        issue(i + 1, (i + 1) % 2)

    slot = i % 2

    @pl.when(i * rows < n_used)
    def _():
        pltpu.make_async_copy(x_hbm.at[pl.ds(0, rows * ROW_CHUNKS), :], buf.at[slot], sem.at[slot]).wait()
        for j in range(ROW_CHUNKS):
            xs_ref[:, j * LANES:(j + 1) * LANES] = \
                buf[slot, pl.ds(j, rows, stride=ROW_CHUNKS), :].astype(xs_ref.dtype)

    @pl.when(i * rows >= n_used)
    def _():
        xs_ref[...] = jnp.zeros(xs_ref.shape, xs_ref.dtype)


def moe_dispatch(xn, slot_tok, n_used, rows):
    t, d = xn.shape
    assert d == ROW_CHUNKS * LANES
    xn = xn.reshape(t * ROW_CHUNKS, LANES)
    n_slots = slot_tok.shape[0]
    assert n_slots % rows == 0
    grid_spec = pltpu.PrefetchScalarGridSpec(
        num_scalar_prefetch=2,
        grid=(n_slots // rows,),
        in_specs=[pl.BlockSpec(memory_space=pl.ANY)],
        out_specs=pl.BlockSpec((rows, d), lambda i, ns, tk: (i, 0)),
        scratch_shapes=[pltpu.VMEM((2, rows * ROW_CHUNKS, LANES), F32), pltpu.SemaphoreType.DMA((2,))],
    )
    return pl.pallas_call(
        functools.partial(_dispatch_body, rows=rows),
        grid_spec=grid_spec,
        out_shape=jax.ShapeDtypeStruct((n_slots, d), BF16),
        compiler_params=_cparams(("arbitrary",), VMEM_LIMIT),
        name="moe_dispatch",
    )(n_used, slot_tok, xn)


def _experts_body(blk0_ref, nblk_ref, ntot_ref, xs_hbm, wg_ref, wu_ref, wd_ref, ys_hbm,
                  xbuf, ybuf, zbuf, xsem, ysem, zsem, wgb, wub, wdb, *, bm):
    e = pl.program_id(0)
    n_total = ntot_ref[0]

    def x_copy(g, s):
        return pltpu.make_async_copy(xs_hbm.at[pl.ds(g * bm, bm), :], xbuf.at[s], xsem.at[s])

    def y_copy(g, s):
        return pltpu.make_async_copy(ybuf.at[s], ys_hbm.at[pl.ds(g * bm, bm), :], ysem.at[s])

    @pl.when(jnp.logical_and(e == 0, n_total > 0))
    def _():
        x_copy(0, 0).start(priority=1)

    @pl.when(nblk_ref[e] > 0)
    def _():
        wgb[...] = wg_ref[0, 0].astype(BF16)
        wub[...] = wu_ref[0, 0].astype(BF16)
        wdb[...] = wd_ref[0, 0].astype(BF16)

    def block(i, carry):
        g = blk0_ref[e] + i
        s = g % 2

        @pl.when(g + 1 < n_total)
        def _():
            x_copy(g + 1, 1 - s).start(priority=1)

        x_copy(g, s).wait()
        x = xbuf[s]
        gt = _dot(x, wgb[...])
        up = _dot(x, wub[...])
        a = (gt * (1.0 / (1.0 + jnp.exp(-gt)))) * up
        y = _dot(a.astype(BF16), wdb[...])

        @pl.when(g >= 2)
        def _():
            y_copy(g - 2, s).wait()

        ybuf[s] = y
        y_copy(g, s).start()
        return carry

    lax.fori_loop(0, nblk_ref[e], block, 0)

    @pl.when(e == pl.num_programs(0) - 1)
    def _():
        for back in (2, 1):
            @pl.when(n_total >= back)
            def _():
                y_copy(n_total - back, (n_total - back) % 2).wait()

        def z_copy(g):
            return pltpu.make_async_copy(zbuf, ys_hbm.at[pl.ds(g * bm, bm), :], zsem)

        zbuf[...] = jnp.zeros(zbuf.shape, F32)
        n_blocks = ys_hbm.shape[0] // bm
        lax.fori_loop(n_total, n_blocks, lambda g, c: (z_copy(g).start(), c)[1], 0)
        lax.fori_loop(n_total, n_blocks, lambda g, c: (z_copy(g).wait(), c)[1], 0)


def moe_experts(xs, blk0, nblk, n_total, w_gate, w_up, w_down, layer, bm):
    n_slots, d = xs.shape
    n_exp = blk0.shape[0]
    de = w_gate.shape[-1]
    grid_spec = pltpu.PrefetchScalarGridSpec(
        num_scalar_prefetch=3,
        grid=(n_exp,),
        in_specs=[pl.BlockSpec(memory_space=pl.ANY),
                  pl.BlockSpec((1, 1, d, de), lambda e, b0, nb, nt: (layer, e, 0, 0)),
                  pl.BlockSpec((1, 1, d, de), lambda e, b0, nb, nt: (layer, e, 0, 0)),
                  pl.BlockSpec((1, 1, de, d), lambda e, b0, nb, nt: (layer, e, 0, 0))],
        out_specs=pl.BlockSpec(memory_space=pl.ANY),
        scratch_shapes=[pltpu.VMEM((2, bm, d), BF16), pltpu.VMEM((2, bm, d), F32), pltpu.VMEM((bm, d), F32),
                        pltpu.SemaphoreType.DMA((2,)), pltpu.SemaphoreType.DMA((2,)), pltpu.SemaphoreType.DMA,
                        pltpu.VMEM((d, de), BF16), pltpu.VMEM((d, de), BF16), pltpu.VMEM((de, d), BF16)],
    )
    return pl.pallas_call(
        functools.partial(_experts_body, bm=bm),
        grid_spec=grid_spec,
        out_shape=jax.ShapeDtypeStruct((n_slots, d), F32),
        compiler_params=_cparams(("arbitrary",), VMEM_LIMIT),
        name="moe_experts",
    )(blk0, nblk, n_total, xs, w_gate, w_up, w_down)


def _combine_body(slot_ref, ys_hbm, h_ref, rt_ref, g_ref, *rest, tm, tile0, want_h):
    outs, (buf, sem) = rest[:-2], rest[-2:]
    i = pl.program_id(0)
    ni = pl.num_programs(0)

    def row_copy(src_row, k, r, s):
        return pltpu.make_async_copy(ys_hbm.at[pl.ds(src_row, 1), :], buf.at[s, k, pl.ds(r, 1), :], sem.at[s])

    def issue(tile, s):
        def one(r, carry):
            base = ((tile0 + tile) * tm + r) * 2
            row_copy(slot_ref[base], 0, r, s).start(priority=0)
            row_copy(slot_ref[base + 1], 1, r, s).start(priority=1)
            return carry

        lax.fori_loop(0, tm, one, 0, unroll=8)

    @pl.when(i == 0)
    def _():
        issue(0, 0)

    @pl.when(i + 1 < ni)
    def _():
        issue(i + 1, (i + 1) % 2)

    slot = i % 2
    for k in range(2):
        pltpu.make_async_copy(ys_hbm.at[pl.ds(0, tm), :], buf.at[slot, k], sem.at[slot]).wait()

    rt = rt_ref[...]
    g1 = rt[:, 4:5]
    g2 = rt[:, 5:6]
    h2 = h_ref[...] + (buf[slot, 0] * g1 + buf[slot, 1] * g2)
    if want_h:
        outs[0][...] = h2
    outs[-1][...] = _rms(h2, g_ref[...]).astype(outs[-1].dtype)


def moe_combine(ys, slots, h1, route, g_next, tm, u_dtype, row0=0, nrows=None, want_h=True):
    d = h1.shape[1]
    t = h1.shape[0] if nrows is None else nrows
    assert row0 % tm == 0 and t % tm == 0
    tile0 = row0 // tm
    row_spec = pl.BlockSpec((tm, d), lambda i, sl: (i, 0))
    grid_spec = pltpu.PrefetchScalarGridSpec(
        num_scalar_prefetch=1,
        grid=(t // tm,),
        in_specs=[pl.BlockSpec(memory_space=pl.ANY),
                  pl.BlockSpec((tm, d), lambda i, sl: (i + tile0, 0)),
                  pl.BlockSpec((tm, LANES), lambda i, sl: (i + tile0, 0)),
                  pl.BlockSpec((1, d), lambda i, sl: (0, 0))],
        out_specs=[row_spec, row_spec] if want_h else [row_spec],
        scratch_shapes=[pltpu.VMEM((2, 2, tm, d), F32), pltpu.SemaphoreType.DMA((2,))],
    )
    shapes = [jax.ShapeDtypeStruct((t, d), F32)] if want_h else []
    return pl.pallas_call(
        functools.partial(_combine_body, tm=tm, tile0=tile0, want_h=want_h),
        grid_spec=grid_spec,
        out_shape=shapes + [jax.ShapeDtypeStruct((t, d), u_dtype)],
        compiler_params=_cparams(("arbitrary",), VMEM_LIMIT),
        name="moe_combine",
    )(slots, ys, h1, route, g_next.reshape(1, d))


def hier_moe_layer(h1, xn, logits, w_gate, w_up, w_down, layer, g_next, u_dtype, split=None):
    t = h1.shape[0]
    bm = MOE_BM
    route, counts = moe_routing(logits, _tile(t, 256))
    e = route[:, 0:2].astype(I32)
    rank = route[:, 2:4].astype(I32)
    counts = counts[0, N_GROUPS:N_GROUPS + N_EXPERTS].astype(I32)
    padded = (counts + bm - 1) // bm * bm
    pad_end = jnp.cumsum(padded)
    pad_start = pad_end - padded
    start_of = jnp.sum(jnp.where(e[..., None] == jnp.arange(N_EXPERTS, dtype=I32), pad_start, 0), axis=-1)
    slots = (start_of + rank).reshape(-1)
    nb = (2 * t) // bm + N_EXPERTS
    slot_tok = moe_slot_tokens(slots, nb * bm)
    n_used = pad_end[-1:]
    xs = moe_dispatch(xn, slot_tok, n_used, bm * max(f for f in (4, 2, 1) if nb % f == 0))
    ys = moe_experts(xs, pad_start // bm, padded // bm, n_used // bm, w_gate, w_up, w_down, layer, bm)
    tmc = _tile(t, 256)
    if split is None:
        return moe_combine(ys, slots, h1, route, g_next, tmc, u_dtype)
    (u_a,) = moe_combine(ys, slots, h1, route, g_next, tmc, u_dtype, 0, split, want_h=False)
    (u_b,) = moe_combine(ys, slots, h1, route, g_next, tmc, u_dtype, split, t - split, want_h=False)
    return u_a, u_b


def _swap_halves(x):
    lane = lax.broadcasted_iota(I32, x.shape, 1)
    return jnp.where(lane < QK_ROPE // 2, pltpu.roll(x, LANES - QK_ROPE // 2, 1), pltpu.roll(x, QK_ROPE // 2, 1))


def _mla_prep_body(p_ref, gq_ref, gkv_ref, cos_ref, sin_ref, cq_ref, ckv_ref, ckvb_ref, kr_ref):
    p = p_ref[...]
    cq_ref[...] = _rms(p[:, :Q_LORA], gq_ref[...]).astype(cq_ref.dtype)
    ckv = _rms(p[:, Q_LORA:Q_LORA + KV_LORA], gkv_ref[...])
    ckv_ref[...] = ckv
    ckvb_ref[...] = ckv.astype(ckvb_ref.dtype)
    kr = p[:, Q_LORA + KV_LORA:]
    kr_ref[...] = kr * cos_ref[...] + _swap_halves(kr) * sin_ref[...]


def mla_prep(proj, g_q, g_kv, cos, sin, tm):
    t = proj.shape[0]
    return pl.pallas_call(
        _mla_prep_body,
        grid=(t // tm,),
        in_specs=[pl.BlockSpec((tm, proj.shape[1]), lambda i: (i, 0)),
                  pl.BlockSpec((1, Q_LORA), lambda i: (0, 0)),
                  pl.BlockSpec((1, KV_LORA), lambda i: (0, 0)),
                  pl.BlockSpec((tm, LANES), lambda i: (i, 0)),
                  pl.BlockSpec((tm, LANES), lambda i: (i, 0))],
        out_specs=[pl.BlockSpec((tm, Q_LORA), lambda i: (i, 0)),
                   pl.BlockSpec((tm, KV_LORA), lambda i: (i, 0)),
                   pl.BlockSpec((tm, KV_LORA), lambda i: (i, 0)),
                   pl.BlockSpec((tm, LANES), lambda i: (i, 0))],
        out_shape=[jax.ShapeDtypeStruct((t, Q_LORA), BF16), jax.ShapeDtypeStruct((t, KV_LORA), F32),
                   jax.ShapeDtypeStruct((t, KV_LORA), BF16), jax.ShapeDtypeStruct((t, LANES), F32)],
        compiler_params=_cparams(("parallel",)),
        name="mla_prep",
    )(proj, g_q.reshape(1, -1), g_kv.reshape(1, -1), cos, sin)


def _mla_q_body(cq_ref, w_ref, cos_ref, sin_ref, q_ref, wb):
    @pl.when(pl.program_id(0) == 0)
    def _():
        wb[...] = w_ref[...].astype(BF16)

    cq = cq_ref[...]
    cos = cos_ref[...]
    sin = sin_ref[...]
    for h in range(B_HEADS):
        y = _dot(cq, wb[:, h * MLA_QK_PAD:(h + 1) * MLA_QK_PAD])
        xr = y[:, LANES:]
        q_ref[:, h * MLA_QK_PAD:h * MLA_QK_PAD + LANES] = y[:, :LANES].astype(q_ref.dtype)
        q_ref[:, h * MLA_QK_PAD + LANES:(h + 1) * MLA_QK_PAD] = \
            (xr * cos + _swap_halves(xr) * sin).astype(q_ref.dtype)


def mla_q_up(cq, w_uq_pad, cos, sin, tm):
    t = cq.shape[0]
    n = B_HEADS * MLA_QK_PAD
    return pl.pallas_call(
        _mla_q_body,
        grid=(t // tm,),
        in_specs=[pl.BlockSpec((tm, Q_LORA), lambda i: (i, 0)),
                  pl.BlockSpec((Q_LORA, n), lambda i: (0, 0)),
                  pl.BlockSpec((tm, LANES), lambda i: (i, 0)),
                  pl.BlockSpec((tm, LANES), lambda i: (i, 0))],
        out_specs=pl.BlockSpec((tm, n), lambda i: (i, 0)),
        out_shape=jax.ShapeDtypeStruct((t, n), BF16),
        scratch_shapes=[pltpu.VMEM((Q_LORA, n), BF16)],
        compiler_params=_cparams(("arbitrary",), VMEM_LIMIT),
        name="mla_q_up",
    )(cq, w_uq_pad, cos, sin)


def _mla_kv_body(c_ref, kr_ref, wk_ref, wv_ref, k_ref, v_ref, wkb, wvb):
    @pl.when(pl.program_id(0) == 0)
    def _():
        wkb[...] = wk_ref[0].astype(BF16)
        wvb[...] = wv_ref[0].astype(BF16)

    c = c_ref[...].astype(BF16)
    kr = kr_ref[...].astype(k_ref.dtype)
    kn = _dot(c, wkb[...])
    for h in range(B_HEADS):
        k_ref[:, h * MLA_QK_PAD:h * MLA_QK_PAD + LANES] = kn[:, h * QK_NOPE:(h + 1) * QK_NOPE].astype(k_ref.dtype)
        k_ref[:, h * MLA_QK_PAD + LANES:(h + 1) * MLA_QK_PAD] = kr
    v_ref[...] = _dot(c, wvb[...]).astype(v_ref.dtype)


def mla_kv_up(ckv, kr, w_uk, w_uv, layer, tm):
    t = ckv.shape[0]
    nk = B_HEADS * MLA_QK_PAD
    nv = B_HEADS * V_DIM
    return pl.pallas_call(
        _mla_kv_body,
        grid=(t // tm,),
        in_specs=[pl.BlockSpec((tm, KV_LORA), lambda i: (i, 0)),
                  pl.BlockSpec((tm, LANES), lambda i: (i, 0)),
                  pl.BlockSpec((1, KV_LORA, B_HEADS * QK_NOPE), lambda i: (layer, 0, 0)),
                  pl.BlockSpec((1, KV_LORA, nv), lambda i: (layer, 0, 0))],
        out_specs=[pl.BlockSpec((tm, nk), lambda i: (i, 0)), pl.BlockSpec((tm, nv), lambda i: (i, 0))],
        out_shape=[jax.ShapeDtypeStruct((t, nk), BF16), jax.ShapeDtypeStruct((t, nv), BF16)],
        scratch_shapes=[pltpu.VMEM((KV_LORA, B_HEADS * QK_NOPE), BF16), pltpu.VMEM((KV_LORA, nv), BF16)],
        compiler_params=_cparams(("arbitrary",), VMEM_LIMIT),
        name="mla_kv_up",
    )(ckv, kr, w_uk, w_uv)


def kernel(x_prompt, x_sample, cache_a_k, cache_a_v, cache_a_kidx, cache_b_ckv, cache_b_krope, norm_mix, norm_ffn, norm_final, rel_bias, a_w_in, a_w_out, b_w_in, b_norm_q, b_norm_kv, b_w_uq, b_w_uk, b_w_uv, b_w_out, moe_w_grp, moe_b_grp, moe_w_rtr, moe_b_rtr, moe_w_gate, moe_w_up, moe_w_down):
    n_p, s_p, d = x_prompt.shape
    n_s, s_s, _ = x_sample.shape
    past = cache_a_k.shape[2]
    tp = n_p * s_p
    ts = n_s * s_s
    t = tp + ts
    a_qd = A_HEADS * A_HEAD_DIM
    tm = _tile(math.gcd(tp, ts), 1024)
    tm2 = _tile(math.gcd(tp, ts), 512)

    x_parts = [x_prompt.reshape(tp, d), x_sample.reshape(ts, d)]

    def router_params(i):
        w_r = jnp.concatenate([moe_w_grp[i], moe_w_rtr[i],
                               jnp.zeros((d, LANES - N_GROUPS - N_EXPERTS), F32)], axis=1)
        b_r = jnp.concatenate([moe_b_grp[i], moe_b_rtr[i],
                               jnp.zeros((LANES - N_GROUPS - N_EXPERTS,), F32)]).reshape(1, LANES)
        return w_r, b_r

    u0 = rmsnorm(x_parts, norm_mix[0], tm)
    w_a = a_w_in[0]
    (q_b,) = matmul(u0, w_a, 0, a_qd, [BF16], tm, 512, "a_proj_q")
    tmp, tms = _tile(tp, 1024), _tile(ts, 1024)
    kp_f, kp_b = matmul(u0, w_a, a_qd, a_qd, [F32, BF16], tmp, 512, "a_proj_k", 0, tp)
    vp_f, vp_b = matmul(u0, w_a, 2 * a_qd, a_qd, [F32, BF16], tmp, 512, "a_proj_v", 0, tp)
    ks_f, ks_b = matmul(u0, w_a, a_qd, a_qd, [F32, BF16], tms, 512, "a_proj_k", tp, ts)
    vs_f, vs_b = matmul(u0, w_a, 2 * a_qd, a_qd, [F32, BF16], tms, 512, "a_proj_v", tp, ts)
    (qi_b,) = matmul(u0, w_a, 3 * a_qd, IDX_HEADS * IDX_DIM, [BF16], tm, 512, "a_proj_qi")
    c_ki = 3 * a_qd + IDX_HEADS * IDX_DIM
    w_ki = w_a[:, c_ki:c_ki + IDX_DIM]
    w_wi = w_a[:, c_ki + IDX_DIM:c_ki + IDX_DIM + IDX_HEADS]
    zk = jnp.zeros((d, IDX_DIM), F32)
    w_tail = jnp.concatenate([w_ki, zk, zk, w_ki, w_wi, jnp.zeros((d, LANES - IDX_HEADS), F32)], axis=1)
    (tail,) = matmul(u0, w_tail, 0, 3 * LANES, [F32], tm, 3 * LANES, "a_proj_tail")
    kidx = tail[:, :IDX_DIM]

    bias_tab = rel_bias_tables(rel_bias)
    a_scale = A_HEAD_DIM ** -0.5
    mask_p = dsa_select(qi_b, tail, 2, tail, n_seq=n_p, sq=s_p, bq=128, sk=s_p, sk_real=s_p, tk=512,
                        pos0=0, q_row0=0)
    kp_t = kp_b.reshape(n_p, s_p, a_qd).transpose(0, 2, 1).reshape(n_p * a_qd, s_p)
    att_p = attention(q_b, [(kp_t, vp_b, s_p, 0, 0)], bias_tab, mask_p, n_seq=n_p, sq=s_p, bq=128, q_row0=0,
                      q_col0=0, n_heads=A_HEADS, hg=8, dq=A_HEAD_DIM, dv=A_HEAD_DIM, scale=a_scale, pos0=0,
                      dyn_sk=s_p, name="dsa_attention_prompt")
    sk_s = past + s_s
    sk_pad = (sk_s + ATT_TILE - 1) // ATT_TILE * ATT_TILE
    ki_past = cache_a_kidx[0]
    zp = jnp.zeros_like(ki_past)
    kc_past = jnp.concatenate([ki_past, zp, zp, ki_past], axis=-1)
    kc_new = tail[tp:, :2 * LANES].reshape(n_s, s_s, 2 * LANES)
    kc_s = jnp.concatenate([kc_past, kc_new, jnp.zeros((n_s, sk_pad - sk_s, 2 * LANES), F32)], axis=1)
    mask_s = dsa_select(qi_b, tail, 2, kc_s.reshape(n_s * sk_pad, 2 * LANES), n_seq=n_s, sq=s_s, bq=s_s,
                        sk=sk_pad, sk_real=sk_s, tk=ATT_TILE, pos0=past, q_row0=tp)
    tiles_s = [(0, r, r, ATT_TILE) for r in range(0, past, ATT_TILE)] + [(1, 0, past, s_s)]
    att_s = attention(q_b, [(cache_a_k[0].reshape(n_s * past * A_HEADS, A_HEAD_DIM),
                             cache_a_v[0].reshape(n_s * past * A_HEADS, A_HEAD_DIM), past, 0, 0),
                            (ks_b, vs_b, s_s, 0, 0)],
                      bias_tab, mask_s, n_seq=n_s, sq=s_s, bq=s_s, q_row0=tp, q_col0=0, n_heads=A_HEADS, hg=A_HEADS,
                      dq=A_HEAD_DIM, dv=A_HEAD_DIM, scale=a_scale, pos0=past, seg_tiles=tiles_s,
                      name="dsa_attention_sample")

    w_r0, b_r0 = router_params(0)
    h1, xn0, lg0 = outproj_norm_router([att_p, att_s], a_w_out, 0, x_parts, norm_ffn[0], w_r0, b_r0, tm2, 512)
    h2, u1 = hier_moe_layer(h1, xn0, lg0, moe_w_gate, moe_w_up, moe_w_down, 0, norm_mix[1], BF16)

    n_in = Q_LORA + KV_LORA + QK_ROPE
    w_b = jnp.concatenate([b_w_in[0], jnp.zeros((d, LANES - QK_ROPE), F32)], axis=1)
    (proj,) = matmul(u1, w_b, 0, n_in + LANES - QK_ROPE, [F32], tm, 384, "b_proj")
    half = QK_ROPE // 2
    inv = ROPE_THETA ** (-jnp.arange(half, dtype=F32) / half)
    pos_all = jnp.concatenate([jnp.tile(jnp.arange(s_p, dtype=I32), n_p),
                               jnp.tile(past + jnp.arange(s_s, dtype=I32), n_s)])
    ang = pos_all.astype(F32)[:, None] * inv[None, :]
    zl = jnp.zeros((t, LANES - QK_ROPE), F32)
    cos_t = jnp.concatenate([jnp.cos(ang), jnp.cos(ang), zl], axis=1)
    sin_t = jnp.concatenate([-jnp.sin(ang), jnp.sin(ang), zl], axis=1)
    cq_b, ckv_f, ckv_b, kr_f = mla_prep(proj, b_norm_q[0], b_norm_kv[0], cos_t, sin_t, tm)
    w_uq_pad = jnp.pad(b_w_uq[0].reshape(Q_LORA, B_HEADS, QK_NOPE + QK_ROPE),
                       ((0, 0), (0, 0), (0, MLA_QK_PAD - QK_NOPE - QK_ROPE))).reshape(Q_LORA, B_HEADS * MLA_QK_PAD)
    qm = mla_q_up(cq_b, w_uq_pad, cos_t, sin_t, tm2)
    km_p, vm_p = mla_kv_up(ckv_b[:tp], kr_f[:tp], b_w_uk, b_w_uv, 0, _tile(tp, 512))
    ctab = causal_tables()
    hgb = 8
    km_t = km_p.reshape(n_p, s_p, B_HEADS * MLA_QK_PAD).transpose(0, 2, 1).reshape(n_p * B_HEADS * MLA_QK_PAD, s_p)
    matt_p = attention(qm, [(km_t, vm_p, s_p, 0, 0)], ctab, None, n_seq=n_p, sq=s_p, bq=128, q_row0=0, q_col0=0,
                       n_heads=B_HEADS, hg=8, dq=MLA_QK_PAD, dv=V_DIM, scale=B_SCALE, pos0=0, dyn_sk=s_p,
                       name="mla_attention_prompt")
    ckv_all = jnp.concatenate([cache_b_ckv[0], ckv_f[tp:].reshape(n_s, s_s, KV_LORA)], axis=1)
    kr_past = jnp.concatenate([cache_b_krope[0], jnp.zeros((n_s, past, LANES - QK_ROPE), F32)], axis=-1)
    kr_all = jnp.concatenate([kr_past, kr_f[tp:].reshape(n_s, s_s, LANES)], axis=1)
    km_s, vm_s = mla_kv_up(ckv_all.reshape(n_s * sk_s, KV_LORA), kr_all.reshape(n_s * sk_s, LANES),
                           b_w_uk, b_w_uv, 0, sk_s)
    tiles_m = [(0, r, r, ATT_TILE) for r in range(0, past, ATT_TILE)] + [(0, past, past, s_s)]
    matt_s = attention(qm, [(km_s, vm_s, sk_s, 0, 0)], ctab, None, n_seq=n_s, sq=s_s, bq=s_s, q_row0=tp, q_col0=0,
                       n_heads=B_HEADS, hg=hgb, dq=MLA_QK_PAD, dv=V_DIM, scale=B_SCALE, pos0=past,
                       seg_tiles=tiles_m, name="mla_attention_sample")

    w_r1, b_r1 = router_params(1)
    h3, xn1, lg1 = outproj_norm_router([matt_p, matt_s], b_w_out, 0, [h2], norm_ffn[1], w_r1, b_r1, tm2, 512)
    y_p, y_s = hier_moe_layer(h3, xn1, lg1, moe_w_gate, moe_w_up, moe_w_down, 1, norm_final, F32, split=tp)
    y_prompt = y_p.reshape(n_p, s_p, d)
    y_sample = y_s.reshape(n_s, s_s, d)

    def heads(a, n, s):
        return a.reshape(1, n, s, A_HEADS, A_HEAD_DIM)

    kr_out = kr_f[:, :QK_ROPE]
    return (y_prompt, y_sample,
            heads(kp_f, n_p, s_p), heads(vp_f, n_p, s_p), kidx[:tp].reshape(1, n_p, s_p, IDX_DIM),
            ckv_f[:tp].reshape(1, n_p, s_p, KV_LORA), kr_out[:tp].reshape(1, n_p, s_p, QK_ROPE),
            heads(ks_f, n_s, s_s), heads(vs_f, n_s, s_s), kidx[tp:].reshape(1, n_s, s_s, IDX_DIM),
            ckv_f[tp:].reshape(1, n_s, s_s, KV_LORA), kr_out[tp:].reshape(1, n_s, s_s, QK_ROPE))
```

```python
import functools
import math

import jax
import jax.numpy as jnp
from jax import lax
from jax.experimental import pallas as pl
from jax.experimental.pallas import tpu as pltpu

F32 = jnp.float32
BF16 = jnp.bfloat16
I32 = jnp.int32

RMS_EPS = 1e-6
CHUNK = 64
NEG = -1e30
INT_MIN = -2 ** 31
INT_MAX = 2 ** 31 - 1
LOG2E = math.log2(math.e)

LANES = 128
KEY_BLOCK = 128
ATT_TILE = 256
ATT_BQ = 128
ATT_HG = 8
VMEM_LIMIT = 56 * 1024 * 1024

A_HEADS = 16
A_HEAD_DIM = 128
IDX_HEADS = 16
IDX_DIM = 64
TOPK_MAX = 256
IDX_W_SCALE = float((IDX_HEADS * IDX_DIM) ** -0.5)
NUM_BUCKETS = 32
MAX_DISTANCE = 128
B_HEADS = 16
Q_LORA = 512
KV_LORA = 512
QK_NOPE = 128
QK_ROPE = 64
V_DIM = 128
ROPE_THETA = 10000.0
B_SCALE = float((QK_NOPE + QK_ROPE) ** -0.5)
MLA_QK_PAD = 256
N_GROUPS = 8
EXPERTS_PER_GROUP = 8
N_EXPERTS = 64
D_EXPERT = 512
MOE_BM = 128
ROW_CHUNKS = 16
X_AHEAD = 2


def _tile(n, pref):
    for c in range(pref, 0, -LANES):
        if n % c == 0:
            return c
    raise ValueError(f"no 128-multiple tile divides {n}")


def _cparams(sem, vmem=None):
    return pltpu.CompilerParams(dimension_semantics=sem, vmem_limit_bytes=vmem)


def _dot(a, b):
    return jnp.dot(a, b, preferred_element_type=F32)


def _dot_nt(a, b):
    return lax.dot_general(a, b, (((1,), (1,)), ((), ())), preferred_element_type=F32)


def _rms(x, g):
    ms = jnp.mean(x * x, axis=-1, keepdims=True)
    return x * lax.rsqrt(ms + RMS_EPS) * g


def _pick_rows(i, n_first, refs):
    if len(refs) == 1:
        return refs[0][...]
    return jnp.where(i < n_first, refs[0][...], refs[1][...])


def _row_split_specs(parts, tm, width, col_of):
    n_first = parts[0].shape[0] // tm
    assert all(p.shape[0] % tm == 0 for p in parts)
    specs = [pl.BlockSpec((tm, width), lambda i, *r: (jnp.minimum(i, n_first - 1), col_of(i, *r)))]
    if len(parts) == 2:
        specs.append(pl.BlockSpec((tm, width), lambda i, *r: (jnp.maximum(i - n_first, 0), col_of(i, *r))))
    return specs, n_first


def _rmsnorm_body(*refs, n_first):
    x = _pick_rows(pl.program_id(0), n_first, refs[:-2])
    g_ref, o_ref = refs[-2:]
    o_ref[...] = _rms(x, g_ref[...]).astype(o_ref.dtype)


def rmsnorm(xs, g, tm, out_dtype=BF16):
    d = xs[0].shape[1]
    t = sum(x.shape[0] for x in xs)
    specs, n_first = _row_split_specs(xs, tm, d, lambda i: 0)
    return pl.pallas_call(
        functools.partial(_rmsnorm_body, n_first=n_first),
        grid=(t // tm,),
        in_specs=specs + [pl.BlockSpec((1, d), lambda i: (0, 0))],
        out_specs=pl.BlockSpec((tm, d), lambda i: (i, 0)),
        out_shape=jax.ShapeDtypeStruct((t, d), out_dtype),
        compiler_params=_cparams(("parallel",)),
        name="rmsnorm",
    )(*xs, g.reshape(1, d))


def _matmul_body(a_ref, w_ref, *rest):
    outs, wb = rest[:-1], rest[-1]

    @pl.when(pl.program_id(1) == 0)
    def _():
        wb[...] = w_ref[...].astype(BF16)

    r = _dot(a_ref[...], wb[...])
    for o in outs:
        o[...] = r.astype(o.dtype)


def matmul(a, w, col0, ncols, out_dtypes, tm, tn, name, row0=0, nrows=None):
    k = a.shape[1]
    t = a.shape[0] if nrows is None else nrows
    assert ncols % tn == 0 and col0 % tn == 0 and t % tm == 0 and row0 % tm == 0
    cb = col0 // tn
    rb = row0 // tm
    return pl.pallas_call(
        _matmul_body,
        grid=(ncols // tn, t // tm),
        in_specs=[pl.BlockSpec((tm, k), lambda j, i: (i + rb, 0)),
                  pl.BlockSpec((k, tn), lambda j, i: (0, j + cb))],
        out_specs=[pl.BlockSpec((tm, tn), lambda j, i: (i, j)) for _ in out_dtypes],
        out_shape=[jax.ShapeDtypeStruct((t, ncols), dt) for dt in out_dtypes],
        scratch_shapes=[pltpu.VMEM((k, tn), BF16)],
        compiler_params=_cparams(("arbitrary", "arbitrary"), VMEM_LIMIT),
        name=name,
    )(a, w)


def _float_sort_key(x):
    bits = lax.bitcast_convert_type(x, I32)
    return bits ^ ((bits >> 31) & INT_MAX)


def _select_body(qi_ref, wi_ref, kc_ref, mask_ref, key_scr, jm_scr, *, bq, sk, sk_real, tk, pos0, topk):
    b = pl.program_id(1)
    t0 = pos0 + b * bq
    kmax = jnp.minimum(sk_real, ((t0 + bq - 1) // CHUNK + 1) * CHUNK)
    nkt = (kmax + tk - 1) // tk
    nch = tk // LANES

    w = wi_ref[...] * IDX_W_SCALE
    wb = [jnp.broadcast_to(w[:, h:h + 1], (bq, LANES)) for h in range(IDX_HEADS)]
    qpos = t0 + lax.broadcasted_iota(I32, (LANES, LANES), 1)
    lim = jnp.minimum((qpos // CHUNK + 1) * CHUNK, sk_real)
    sub = lax.broadcasted_iota(I32, (LANES, LANES), 0)

    def to_lanes(x):
        if bq < LANES:
            x = jnp.concatenate([x, jnp.zeros((LANES - bq, LANES), x.dtype)], axis=0)
        return x.T

    def score_tile(j, carry):
        off = pl.multiple_of(j * tk, tk)
        kc = kc_ref[pl.ds(off, tk), :].astype(BF16)
        ka, kb = kc[:, :LANES], kc[:, LANES:]
        accs = [jnp.zeros((bq, LANES), F32) for _ in range(nch)]
        for g in range(IDX_HEADS // 2):
            qg = qi_ref[:, g * LANES:(g + 1) * LANES]
            sa = _dot_nt(qg, ka)
            sb = _dot_nt(qg, kb)
            for c in range(nch):
                sl = slice(c * LANES, (c + 1) * LANES)
                accs[c] = accs[c] + wb[2 * g] * jnp.maximum(sa[:, sl], 0.0) \
                    + wb[2 * g + 1] * jnp.maximum(sb[:, sl], 0.0)
        for c in range(nch):
            kpos = off + c * LANES + sub
            sc = jnp.where(kpos < lim, to_lanes(accs[c]), -jnp.inf)
            key_scr[pl.ds(off + c * LANES, LANES), :] = _float_sort_key(sc)
        return carry

    lax.fori_loop(0, nkt, score_tile, 0)

    def count(indicator):
        def tile(j, cnt):
            off = pl.multiple_of(j * tk, tk)
            for c in range(nch):
                kt = key_scr[pl.ds(off + c * LANES, LANES), :]
                cnt = cnt + indicator(kt, off + c * LANES + sub)
            return cnt
        cnt = lax.fori_loop(0, nkt, tile, jnp.zeros((LANES, LANES), F32))
        return jnp.sum(cnt, axis=0, keepdims=True)

    def bit_step(i, pfx_u):
        bit = lax.shift_left(jnp.int32(1), 31 - i)
        cand_u = pfx_u | bit
        cand_s = cand_u ^ INT_MIN
        total = count(lambda kt, kp: jnp.where(kt >= cand_s, 1.0, 0.0))
        return jnp.where(total >= topk, cand_u, pfx_u)

    pfx = lax.fori_loop(0, 32, bit_step, jnp.zeros((1, LANES), I32))
    thr = pfx ^ INT_MIN

    n_gt = count(lambda kt, kp: jnp.where(kt > thr, 1.0, 0.0))
    n_ge = count(lambda kt, kp: jnp.where(kt >= thr, 1.0, 0.0))
    quota = topk - n_gt
    jm_scr[...] = jnp.full((1, LANES), INT_MAX, I32)
    any_excess = jnp.max(jnp.where(n_ge > topk, 1.0, 0.0)) > 0.0

    @pl.when(any_excess)
    def _():
        nbits = max(1, int(sk - 1).bit_length())

        def idx_step(i, ans):
            cand = ans | lax.shift_left(jnp.int32(1), nbits - 1 - i)
            below = count(lambda kt, kp: jnp.where(kt == thr, jnp.where(kp < cand, 1.0, 0.0), 0.0))
            return jnp.where(below < quota, cand, ans)

        jm_scr[...] = lax.fori_loop(0, nbits, idx_step, jnp.zeros((1, LANES), I32))

    jm = jm_scr[...]

    def write_tile(j, carry):
        off = pl.multiple_of(j * tk, tk)
        for c in range(nch):
            kt = key_scr[pl.ds(off + c * LANES, LANES), :]
            kpos = off + c * LANES + sub
            v = jnp.where(kt > thr, 0.0, jnp.where(kt == thr, jnp.where(kpos <= jm, 0.0, NEG), NEG))
            v = jnp.where(kpos < lim, v, NEG)
            mask_ref[:, pl.ds(off + c * LANES, LANES)] = v.T[:bq].astype(mask_ref.dtype)
        return carry

    lax.fori_loop(0, nkt, write_tile, 0)

    def fill_tile(j, carry):
        off = pl.multiple_of(j * tk, tk)
        mask_ref[:, pl.ds(off, tk)] = jnp.full((bq, tk), NEG, mask_ref.dtype)
        return carry

    lax.fori_loop(nkt, sk // tk, fill_tile, 0)


def dsa_select(qi, wi_arr, wi_blk, kc, *, n_seq, sq, bq, sk, sk_real, tk, pos0, q_row0):
    nqb = sq // bq
    qb0 = q_row0 // bq
    assert q_row0 % bq == 0 and sk % tk == 0 and tk >= TOPK_MAX
    topk = min(TOPK_MAX, sk_real // 4)
    body = functools.partial(_select_body, bq=bq, sk=sk, sk_real=sk_real, tk=tk, pos0=pos0, topk=topk)
    return pl.pallas_call(
        body,
        grid=(n_seq, nqb),
        in_specs=[pl.BlockSpec((bq, IDX_HEADS * IDX_DIM), lambda s, b: (qb0 + s * nqb + b, 0)),
                  pl.BlockSpec((bq, LANES), lambda s, b: (qb0 + s * nqb + b, wi_blk)),
                  pl.BlockSpec((sk, 2 * LANES), lambda s, b: (s, 0))],
        out_specs=pl.BlockSpec((bq, sk), lambda s, b: (s * nqb + b, 0)),
        out_shape=jax.ShapeDtypeStruct((n_seq * sq, sk), BF16),
        scratch_shapes=[pltpu.VMEM((sk, LANES), I32), pltpu.VMEM((1, LANES), I32)],
        compiler_params=_cparams(("parallel", "arbitrary"), VMEM_LIMIT),
        name="dsa_select",
    )(qi, wi_arr, kc)


def _rel_tables_body(bias_ref, tab_ref):
    i = lax.broadcasted_iota(I32, (KEY_BLOCK, KEY_BLOCK), 0)
    j = lax.broadcasted_iota(I32, (KEY_BLOCK, KEY_BLOCK), 1)
    nb = NUM_BUCKETS // 2
    max_exact = nb // 2
    edges = [12, 16, 23, 32, 46, 64, 91]
    for d in range(4):
        rel = j - i + (d - 2) * KEY_BLOCK
        n = jnp.abs(rel)
        large = jnp.full_like(n, max_exact)
        for e in edges:
            large = large + jnp.where(n >= e, 1, 0)
        bucket = jnp.where(rel > 0, nb, 0) + jnp.where(n < max_exact, n, large)
        for h in range(A_HEADS):
            acc = jnp.zeros((KEY_BLOCK, KEY_BLOCK), F32)
            for k in range(NUM_BUCKETS):
                acc = jnp.where(bucket == k, bias_ref[k, h], acc)
            tab_ref[h, d] = acc * LOG2E


def rel_bias_tables(rel_bias):
    return pl.pallas_call(
        _rel_tables_body,
        in_specs=[pl.BlockSpec(memory_space=pltpu.SMEM)],
        out_shape=jax.ShapeDtypeStruct((A_HEADS, 4, KEY_BLOCK, KEY_BLOCK), F32),
        name="rel_bias_tables",
    )(rel_bias)


def _causal_tables_body(tab_ref):
    i = lax.broadcasted_iota(I32, (KEY_BLOCK, KEY_BLOCK), 0)
    j = lax.broadcasted_iota(I32, (KEY_BLOCK, KEY_BLOCK), 1)
    zero = jnp.zeros((KEY_BLOCK, KEY_BLOCK), F32)
    tab_ref[0, 0] = zero
    tab_ref[0, 1] = zero
    tab_ref[0, 2] = jnp.where(j // CHUNK <= i // CHUNK, 0.0, NEG)
    tab_ref[0, 3] = jnp.full((KEY_BLOCK, KEY_BLOCK), NEG, F32)


def causal_tables():
    return pl.pallas_call(
        _causal_tables_body,
        out_shape=jax.ShapeDtypeStruct((1, 4, KEY_BLOCK, KEY_BLOCK), F32),
        name="causal_tables",
    )()


def _attn_pipe_body(*refs, has_mask, per_head_tab, bq, hg, dq, dv, scale, pos0, sk):
    q_ref, k_ref, v_ref, tab_ref = refs[:4]
    mask_ref = refs[4] if has_mask else None
    out_ref = refs[5] if has_mask else refs[4]
    s_scr, p_scr, m_scr, l_scr, acc_scr = refs[-5:]

    b = pl.program_id(2)
    t0 = pos0 + b * bq
    qblk = t0 // KEY_BLOCK
    kmax = jnp.minimum(sk, ((t0 + bq - 1) // CHUNK + 1) * CHUNK)
    nkt = (kmax + ATT_TILE - 1) // ATT_TILE
    nch = ATT_TILE // LANES

    m_scr[...] = jnp.full(m_scr.shape, NEG, F32)
    l_scr[...] = jnp.zeros(l_scr.shape, F32)
    acc_scr[...] = jnp.zeros(acc_scr.shape, F32)

    def paired_loop(tile):
        def two(jj, carry):
            tile(2 * jj, carry)
            return tile(2 * jj + 1, carry)

        lax.fori_loop(0, nkt // 2, two, 0)

        @pl.when(nkt % 2 == 1)
        def _():
            tile(nkt - 1, 0)

    def logits_tile(j, carry):
        off = pl.multiple_of(j * ATT_TILE, ATT_TILE)
        kb0 = off // KEY_BLOCK
        ds = [[jnp.clip(kb0 + c - (qblk + r) + 2, 0, 3) for c in range(nch)] for r in range(bq // KEY_BLOCK)]
        if has_mask:
            mk = mask_ref[:, pl.ds(off, ATT_TILE)].astype(F32)
        for h in range(hg):
            s = _dot(q_ref[:, h * dq:(h + 1) * dq], k_ref[h * dq:(h + 1) * dq, pl.ds(off, ATT_TILE)]) * scale
            hh = h if per_head_tab else 0
            s = s + jnp.concatenate(
                [jnp.concatenate([tab_ref[hh, d] for d in row], axis=1) for row in ds], axis=0)
            if has_mask:
                s = s + mk
            s_scr[h, :, pl.ds(off, ATT_TILE)] = s
            mvec = m_scr[h]
            for c in range(nch):
                mvec = jnp.maximum(mvec, s[:, c * LANES:(c + 1) * LANES])
            m_scr[h] = mvec
        return carry

    paired_loop(logits_tile)
    for h in range(hg):
        m_scr[h] = jnp.broadcast_to(jnp.max(m_scr[h], axis=1, keepdims=True), (bq, LANES))

    def exp_tile(j, carry):
        off = pl.multiple_of(j * ATT_TILE, ATT_TILE)
        for h in range(hg):
            s = s_scr[h, :, pl.ds(off, ATT_TILE)]
            m = m_scr[h]
            p = [jnp.exp2(s[:, c * LANES:(c + 1) * LANES] - m) for c in range(nch)]
            lvec = l_scr[h]
            for c in range(nch):
                lvec = lvec + p[c]
            l_scr[h] = lvec
            p_scr[h, :, pl.ds(off, ATT_TILE)] = jnp.concatenate(p, axis=1).astype(BF16)
        return carry

    lax.fori_loop(0, nkt, exp_tile, 0)

    def pv_tile(j, carry):
        off = pl.multiple_of(j * ATT_TILE, ATT_TILE)
        for h in range(hg):
            acc_scr[h] = acc_scr[h] + _dot(p_scr[h, :, pl.ds(off, ATT_TILE)],
                                           v_ref[pl.ds(off, ATT_TILE), h * dv:(h + 1) * dv])
        return carry

    paired_loop(pv_tile)
    for h in range(hg):
        l_row = jnp.sum(l_scr[h], axis=1, keepdims=True)
        out_ref[:, h * dv:(h + 1) * dv] = (acc_scr[h] / l_row).astype(out_ref.dtype)


def _attn_body(*refs, n_seg, has_mask, per_head_tab, bq, hg, dq, dv, scale, pos0, seg_tiles, by_head):
    q_ref = refs[0]
    kv = refs[1:1 + 2 * n_seg]
    pos = 1 + 2 * n_seg
    tab_ref = refs[pos]
    pos += 1
    mask_ref = refs[pos] if has_mask else None
    pos += 1 if has_mask else 0
    out_ref = refs[pos]
    s_scr, p_scr = refs[pos + 1:pos + 3]

    b = pl.program_id(2)
    t0 = pos0 + b * bq
    qblk = t0 // KEY_BLOCK

    def lane_pad(x, fill):
        w = x.shape[1]
        return x if w == LANES else jnp.concatenate([x, jnp.full((bq, LANES - w), fill, F32)], axis=1)

    mvec = [jnp.full((bq, LANES), NEG, F32) for _ in range(hg)]
    for (si, row0, key0, width) in seg_tiles:
        nch = max(1, width // KEY_BLOCK)
        cw = min(width, KEY_BLOCK)
        ds = [jnp.clip(key0 // KEY_BLOCK + c - qblk + 2, 0, 3) for c in range(nch)]
        k_t = None if by_head[si] else kv[2 * si][row0:row0 + width, :].astype(BF16)
        if has_mask:
            mk = mask_ref[:, key0:key0 + width].astype(F32)
        for h in range(hg):
            hh = h if per_head_tab else 0
            if by_head[si]:
                k_h = kv[2 * si][pl.ds(row0 * hg + h, width, stride=hg), :].astype(BF16)
            else:
                k_h = k_t[:, h * dq:(h + 1) * dq]
            s = _dot_nt(q_ref[:, h * dq:(h + 1) * dq], k_h) * scale
            bias = [tab_ref[hh, ds[c], :bq, :cw] for c in range(nch)]
            s = s + (jnp.concatenate(bias, axis=1) if nch > 1 else bias[0])
            if has_mask:
                s = s + mk
            s_scr[h, :, key0:key0 + width] = s
            for c in range(nch):
                mvec[h] = jnp.maximum(mvec[h], lane_pad(s[:, c * cw:(c + 1) * cw], NEG))

    m = [jnp.broadcast_to(jnp.max(mvec[h], axis=1, keepdims=True), (bq, LANES)) for h in range(hg)]
    lvec = [jnp.zeros((bq, LANES), F32) for _ in range(hg)]
    for (si, row0, key0, width) in seg_tiles:
        nch = max(1, width // KEY_BLOCK)
        cw = min(width, KEY_BLOCK)
        for h in range(hg):
            s = s_scr[h, :, key0:key0 + width]
            p = [jnp.exp2(s[:, c * cw:(c + 1) * cw] - m[h][:, :cw]) for c in range(nch)]
            for c in range(nch):
                lvec[h] = lvec[h] + lane_pad(p[c], 0.0)
            p_scr[h, :, key0:key0 + width] = (jnp.concatenate(p, axis=1) if nch > 1 else p[0]).astype(BF16)

    acc = [jnp.zeros((bq, dv), F32) for _ in range(hg)]
    for (si, row0, key0, width) in seg_tiles:
        v_t = None if by_head[si] else kv[2 * si + 1][row0:row0 + width, :].astype(BF16)
        for h in range(hg):
            if by_head[si]:
                v_h = kv[2 * si + 1][pl.ds(row0 * hg + h, width, stride=hg), :].astype(BF16)
            else:
                v_h = v_t[:, h * dv:(h + 1) * dv]
            acc[h] = acc[h] + _dot(p_scr[h, :, key0:key0 + width], v_h)

    for h in range(hg):
        l_row = jnp.sum(lvec[h], axis=1, keepdims=True)
        out_ref[:, h * dv:(h + 1) * dv] = (acc[h] / l_row).astype(out_ref.dtype)


def attention(q, segs, tab, mask, *, n_seq, sq, bq, q_row0, q_col0, n_heads, hg, dq, dv, scale, pos0,
              seg_tiles=None, dyn_sk=None, name="attention"):
    nqb = sq // bq
    ng = n_heads // hg
    qb0 = q_row0 // bq
    assert q_row0 % bq == 0 and q_col0 % (hg * dq) == 0 and pos0 % KEY_BLOCK == 0
    scale = scale * LOG2E
    assert bq % KEY_BLOCK == 0 or nqb == 1
    qc0 = q_col0 // (hg * dq)
    in_specs = [pl.BlockSpec((bq, hg * dq), lambda s, g, b: (qb0 + s * nqb + b, qc0 + g))]
    args = [q]
    by_head = []
    for (k, v, rows, kc0, vc0) in segs:
        assert kc0 % (hg * dq) == 0 and vc0 % (hg * dv) == 0
        by_head.append(k.shape[1] == dq and n_heads > 1)
        if by_head[-1]:
            assert hg == n_heads and dq == dv and k.shape[0] == n_seq * rows * n_heads and dyn_sk is None
            for _ in range(2):
                in_specs.append(pl.BlockSpec((rows * n_heads, dq), lambda s, g, b: (s, 0)))
            args += [k, v]
            continue
        mode = dict(pipeline_mode=pl.Buffered(1)) if dyn_sk is not None else {}
        if dyn_sk is not None:
            assert kc0 == 0 and k.shape == (n_seq * n_heads * dq, rows)
            in_specs.append(pl.BlockSpec((hg * dq, rows), lambda s, g, b: (s * ng + g, 0), **mode))
        else:
            in_specs.append(pl.BlockSpec((rows, hg * dq), lambda s, g, b, c=kc0 // (hg * dq): (s, c + g)))
        in_specs.append(pl.BlockSpec((rows, hg * dv), lambda s, g, b, c=vc0 // (hg * dv): (s, c + g), **mode))
        args += [k, v]
    per_head_tab = tab.shape[0] > 1
    if per_head_tab:
        in_specs.append(pl.BlockSpec((hg, 4, KEY_BLOCK, KEY_BLOCK), lambda s, g, b: (g, 0, 0, 0)))
    else:
        in_specs.append(pl.BlockSpec((1, 4, KEY_BLOCK, KEY_BLOCK), lambda s, g, b: (0, 0, 0, 0)))
    args.append(tab)
    if mask is not None:
        in_specs.append(pl.BlockSpec((bq, mask.shape[1]), lambda s, g, b: (s * nqb + b, 0)))
        args.append(mask)
    if dyn_sk is not None:
        assert len(segs) == 1 and dyn_sk % ATT_TILE == 0 and bq % KEY_BLOCK == 0
        body = functools.partial(_attn_pipe_body, has_mask=mask is not None, per_head_tab=per_head_tab,
                                 bq=bq, hg=hg, dq=dq, dv=dv, scale=scale, pos0=pos0, sk=dyn_sk)
        scratch = [pltpu.VMEM((hg, bq, dyn_sk), F32), pltpu.VMEM((hg, bq, dyn_sk), BF16),
                   pltpu.VMEM((hg, bq, LANES), F32), pltpu.VMEM((hg, bq, LANES), F32),
                   pltpu.VMEM((hg, bq, dv), F32)]
    else:
        body = functools.partial(_attn_body, n_seg=len(segs), has_mask=mask is not None,
                                 per_head_tab=per_head_tab, bq=bq, hg=hg, dq=dq, dv=dv, scale=scale, pos0=pos0,
                                 seg_tiles=seg_tiles, by_head=tuple(by_head))
        sk_tot = max(key0 + width for (_, _, key0, width) in seg_tiles)
        sk_tot = (sk_tot + LANES - 1) // LANES * LANES
        scratch = [pltpu.VMEM((hg, bq, sk_tot), F32), pltpu.VMEM((hg, bq, sk_tot), BF16)]
    return pl.pallas_call(
        body,
        grid=(n_seq, ng, nqb),
        in_specs=in_specs,
        out_specs=pl.BlockSpec((bq, hg * dv), lambda s, g, b: (s * nqb + b, g)),
        out_shape=jax.ShapeDtypeStruct((n_seq * sq, n_heads * dv), BF16),
        scratch_shapes=scratch,
        compiler_params=_cparams(("parallel", "parallel", "arbitrary"), VMEM_LIMIT),
        name=name,
    )(*args)


def _outproj_body(*refs, tn, na, nx, a_first, x_first):
    i = pl.program_id(0)
    j = pl.program_id(1)
    a = _pick_rows(i, a_first, refs[:na])
    w_ref = refs[na]
    x = _pick_rows(i, x_first, refs[na + 1:na + 1 + nx])
    g_ref, wr_ref, br_ref, h_ref, xn_ref, lg_ref, hrow = refs[na + 1 + nx:]
    r = x + _dot(a, w_ref[0].astype(BF16))
    h_ref[...] = r
    hrow[:, pl.ds(pl.multiple_of(j * tn, tn), tn)] = r

    @pl.when(j == pl.num_programs(1) - 1)
    def _():
        xn = _rms(hrow[...], g_ref[...])
        tm = hrow.shape[0]
        for c in range(ROW_CHUNKS):
            xn_ref[pl.ds(c, tm, stride=ROW_CHUNKS), :] = xn[:, c * LANES:(c + 1) * LANES]
        lg_ref[...] = _dot(xn.astype(BF16), wr_ref[...].astype(BF16)) + br_ref[...]


def outproj_norm_router(a_parts, w_out, layer, x_parts, g_ffn, w_r, b_r, tm, tn):
    k = a_parts[0].shape[1]
    t = sum(p.shape[0] for p in a_parts)
    d = w_out.shape[-1]
    a_specs, a_first = _row_split_specs(a_parts, tm, k, lambda i, j: 0)
    x_specs, x_first = _row_split_specs(x_parts, tm, tn, lambda i, j: j)
    return pl.pallas_call(
        functools.partial(_outproj_body, tn=tn, na=len(a_parts), nx=len(x_parts), a_first=a_first,
                          x_first=x_first),
        grid=(t // tm, d // tn),
        in_specs=a_specs + [pl.BlockSpec((1, k, tn), lambda i, j: (layer, 0, j))] + x_specs + [
            pl.BlockSpec((1, d), lambda i, j: (0, 0)),
            pl.BlockSpec((d, LANES), lambda i, j: (0, 0)),
            pl.BlockSpec((1, LANES), lambda i, j: (0, 0))],
        out_specs=[pl.BlockSpec((tm, tn), lambda i, j: (i, j)),
                   pl.BlockSpec((tm * ROW_CHUNKS, LANES), lambda i, j: (i, 0)),
                   pl.BlockSpec((tm, LANES), lambda i, j: (i, 0))],
        out_shape=[jax.ShapeDtypeStruct((t, d), F32), jax.ShapeDtypeStruct((t * ROW_CHUNKS, LANES), F32),
                   jax.ShapeDtypeStruct((t, LANES), F32)],
        scratch_shapes=[pltpu.VMEM((tm, d), F32)],
        compiler_params=_cparams(("parallel", "arbitrary"), VMEM_LIMIT),
        name="outproj_norm_router",
    )(*a_parts, w_out, *x_parts, g_ffn.reshape(1, d), w_r, b_r)


def _routing_body(lg_ref, out_ref, cnt_ref, carry, *, tm):
    i = pl.program_id(0)

    @pl.when(i == 0)
    def _():
        carry[...] = jnp.zeros(carry.shape, F32)

    x = lg_ref[...]
    lane = lax.broadcasted_iota(I32, (tm, LANES), 1)
    neg_inf = -jnp.inf

    def rmax(v):
        return jnp.max(v, axis=1, keepdims=True)

    def rmin(v):
        return jnp.min(v, axis=1, keepdims=True)

    def rsum(v):
        return jnp.sum(v, axis=1, keepdims=True)

    gm = lane < N_GROUPS
    gmax = rmax(jnp.where(gm, x, neg_inf))
    gsel = rmin(jnp.where(gm, jnp.where(x == gmax, lane, LANES), LANES))
    gsum = rsum(jnp.where(gm, jnp.exp(x - gmax), 0.0))
    g_w = 1.0 / gsum
    lo = N_GROUPS + gsel * EXPERTS_PER_GROUP
    em = jnp.logical_and(lane >= lo, lane < lo + EXPERTS_PER_GROUP)
    emax = rmax(jnp.where(em, x, neg_inf))
    ee = jnp.where(em, jnp.exp(x - emax), 0.0)
    p = ee / rsum(ee)
    p1 = rmax(jnp.where(em, p, -1.0))
    i1 = rmin(jnp.where(em, jnp.where(p == p1, lane, LANES), LANES))
    em2 = jnp.logical_and(em, lane != i1)
    p2 = rmax(jnp.where(em2, p, -1.0))
    i2 = rmin(jnp.where(em2, jnp.where(p == p2, lane, LANES), LANES))
    den = p1 + p2
    g1 = g_w * (p1 / den)
    g2 = g_w * (p2 / den)

    oh1 = jnp.where(lane == i1, 1.0, 0.0)
    oh2 = jnp.where(lane == i2, 1.0, 0.0)
    oh = oh1 + oh2
    r = lax.broadcasted_iota(I32, (tm, tm), 0)
    c = lax.broadcasted_iota(I32, (tm, tm), 1)
    lower = jnp.where(c < r, 1.0, 0.0).astype(BF16)
    before = _dot(lower, oh.astype(BF16)) + carry[...]
    rank1 = rsum(oh1 * before)
    rank2 = rsum(oh2 * before)
    carry[...] = carry[...] + jnp.sum(oh, axis=0, keepdims=True)

    e1 = (i1 - N_GROUPS).astype(F32)
    e2 = (i2 - N_GROUPS).astype(F32)
    vals = [e1, e2, rank1, rank2, g1, g2]
    out = jnp.zeros((tm, LANES), F32)
    for k, v in enumerate(vals):
        out = jnp.where(lane == k, v, out)
    out_ref[...] = out
    cnt_ref[...] = carry[...]


def moe_routing(logits, tm):
    t = logits.shape[0]
    return pl.pallas_call(
        functools.partial(_routing_body, tm=tm),
        grid=(t // tm,),
        in_specs=[pl.BlockSpec((tm, LANES), lambda i: (i, 0))],
        out_specs=[pl.BlockSpec((tm, LANES), lambda i: (i, 0)), pl.BlockSpec((1, LANES), lambda i: (0, 0))],
        out_shape=[jax.ShapeDtypeStruct((t, LANES), F32), jax.ShapeDtypeStruct((1, LANES), F32)],
        scratch_shapes=[pltpu.VMEM((1, LANES), F32)],
        compiler_params=_cparams(("arbitrary",)),
        name="moe_routing",
    )(logits)


def _slot_tokens_body(slots_ref, init_hbm, tok_ref, sem):
    init = pltpu.make_async_copy(init_hbm, tok_ref, sem)
    init.start()
    init.wait()

    def put(a, c):
        tok_ref[slots_ref[a]] = lax.shift_right_logical(a, 1)
        return c

    lax.fori_loop(0, slots_ref.shape[0], put, 0, unroll=8)


def moe_slot_tokens(slots, n_slots):
    init = jnp.arange(n_slots, dtype=I32) % (slots.shape[0] // 2)
    return pl.pallas_call(
        _slot_tokens_body,
        in_specs=[pl.BlockSpec(memory_space=pltpu.SMEM), pl.BlockSpec(memory_space=pl.ANY)],
        out_specs=pl.BlockSpec(memory_space=pltpu.SMEM),
        out_shape=jax.ShapeDtypeStruct((n_slots,), I32),
        scratch_shapes=[pltpu.SemaphoreType.DMA],
        name="moe_slot_tokens",
    )(slots, init)


def _dispatch_body(nslot_ref, tok_ref, x_hbm, xs_ref, buf, sem, *, rows):
    i = pl.program_id(0)
    ni = pl.num_programs(0)
    n_used = nslot_ref[0]

    def issue(blk, s):
        @pl.when(blk * rows < n_used)
        def _():
            def pair(rp, carry):
                for k in range(2):
                    r = 2 * rp + k
                    src = pl.multiple_of(tok_ref[blk * rows + r] * ROW_CHUNKS, ROW_CHUNKS)
                    dst = pl.multiple_of(r * ROW_CHUNKS, ROW_CHUNKS)
                    pltpu.make_async_copy(x_hbm.at[pl.ds(src, ROW_CHUNKS), :],
                                          buf.at[s, pl.ds(dst, ROW_CHUNKS), :], sem.at[s]).start(priority=k)
                return carry

            lax.fori_loop(0, rows // 2, pair, 0, unroll=4)

    @pl.when(i == 0)
    def _():
        issue(0, 0)

    @pl.when(i + 1 < ni)
    def _():
        issue(i + 1, (i + 1) % 2)

    slot = i % 2

    @pl.when(i * rows < n_used)
    def _():
        pltpu.make_async_copy(x_hbm.at[pl.ds(0, rows * ROW_CHUNKS), :], buf.at[slot], sem.at[slot]).wait()
        for j in range(ROW_CHUNKS):
            xs_ref[:, j * LANES:(j + 1) * LANES] = \
                buf[slot, pl.ds(j, rows, stride=ROW_CHUNKS), :].astype(xs_ref.dtype)

    @pl.when(i * rows >= n_used)
    def _():
        xs_ref[...] = jnp.zeros(xs_ref.shape, xs_ref.dtype)


def moe_dispatch(xn, slot_tok, n_used, rows):
    assert xn.shape[1] == LANES
    d = ROW_CHUNKS * LANES
    n_slots = slot_tok.shape[0]
    assert n_slots % rows == 0
    grid_spec = pltpu.PrefetchScalarGridSpec(
        num_scalar_prefetch=2,
        grid=(n_slots // rows,),
        in_specs=[pl.BlockSpec(memory_space=pl.ANY)],
        out_specs=pl.BlockSpec((rows, d), lambda i, ns, tk: (i, 0)),
        scratch_shapes=[pltpu.VMEM((2, rows * ROW_CHUNKS, LANES), F32), pltpu.SemaphoreType.DMA((2,))],
    )
    return pl.pallas_call(
        functools.partial(_dispatch_body, rows=rows),
        grid_spec=grid_spec,
        out_shape=jax.ShapeDtypeStruct((n_slots, d), BF16),
        compiler_params=_cparams(("arbitrary",), VMEM_LIMIT),
        name="moe_dispatch",
    )(n_used, slot_tok, xn)


def _experts_body(blk0_ref, nblk_ref, ntot_ref, xs_hbm, wg_ref, wu_ref, wd_ref, ys_hbm,
                  xbuf, ybuf, zbuf, xsem, ysem, zsem, wgb, wub, wdb, *, bm):
    e = pl.program_id(0)
    n_total = ntot_ref[0]

    def x_copy(g, s):
        return pltpu.make_async_copy(xs_hbm.at[pl.ds(g * bm, bm), :], xbuf.at[s], xsem.at[s])

    def y_copy(g, s):
        return pltpu.make_async_copy(ybuf.at[s], ys_hbm.at[pl.ds(g * bm, bm), :], ysem.at[s])

    for g0 in range(X_AHEAD):
        @pl.when(jnp.logical_and(e == 0, n_total > g0))
        def _():
            x_copy(g0, g0).start(priority=1)

    @pl.when(nblk_ref[e] > 0)
    def _():
        wgb[...] = wg_ref[0, 0].astype(BF16)
        wub[...] = wu_ref[0, 0].astype(BF16)
        wdb[...] = wd_ref[0, 0].astype(BF16)

    def block(i, carry):
        g = blk0_ref[e] + i
        s = g % 2
        sx = g % (X_AHEAD + 1)

        @pl.when(g + X_AHEAD < n_total)
        def _():
            x_copy(g + X_AHEAD, (g + X_AHEAD) % (X_AHEAD + 1)).start(priority=1)

        x_copy(g, sx).wait()
        x = xbuf[sx]
        gt = _dot(x, wgb[...])
        up = _dot(x, wub[...])
        a = (gt * (1.0 / (1.0 + jnp.exp(-gt)))) * up
        y = _dot(a.astype(BF16), wdb[...])

        @pl.when(g >= 2)
        def _():
            y_copy(g - 2, s).wait()

        ybuf[s] = y
        y_copy(g, s).start()
        return carry

    lax.fori_loop(0, nblk_ref[e], block, 0)

    @pl.when(e == pl.num_programs(0) - 1)
    def _():
        for back in (2, 1):
            @pl.when(n_total >= back)
            def _():
                y_copy(n_total - back, (n_total - back) % 2).wait()

        def z_copy(g):
            return pltpu.make_async_copy(zbuf, ys_hbm.at[pl.ds(g * bm, bm), :], zsem)

        zbuf[...] = jnp.zeros(zbuf.shape, F32)
        n_blocks = ys_hbm.shape[0] // bm
        lax.fori_loop(n_total, n_blocks, lambda g, c: (z_copy(g).start(), c)[1], 0)
        lax.fori_loop(n_total, n_blocks, lambda g, c: (z_copy(g).wait(), c)[1], 0)


def moe_experts(xs, blk0, nblk, n_total, w_gate, w_up, w_down, layer, bm):
    n_slots, d = xs.shape
    n_exp = blk0.shape[0]
    de = w_gate.shape[-1]
    grid_spec = pltpu.PrefetchScalarGridSpec(
        num_scalar_prefetch=3,
        grid=(n_exp,),
        in_specs=[pl.BlockSpec(memory_space=pl.ANY),
                  pl.BlockSpec((1, 1, d, de), lambda e, b0, nb, nt: (layer, e, 0, 0)),
                  pl.BlockSpec((1, 1, d, de), lambda e, b0, nb, nt: (layer, e, 0, 0)),
                  pl.BlockSpec((1, 1, de, d), lambda e, b0, nb, nt: (layer, e, 0, 0))],
        out_specs=pl.BlockSpec(memory_space=pl.ANY),
        scratch_shapes=[pltpu.VMEM((X_AHEAD + 1, bm, d), BF16), pltpu.VMEM((2, bm, d), F32),
                        pltpu.VMEM((bm, d), F32),
                        pltpu.SemaphoreType.DMA((X_AHEAD + 1,)), pltpu.SemaphoreType.DMA((2,)),
                        pltpu.SemaphoreType.DMA,
                        pltpu.VMEM((d, de), BF16), pltpu.VMEM((d, de), BF16), pltpu.VMEM((de, d), BF16)],
    )
    return pl.pallas_call(
        functools.partial(_experts_body, bm=bm),
        grid_spec=grid_spec,
        out_shape=jax.ShapeDtypeStruct((n_slots, d), F32),
        compiler_params=_cparams(("arbitrary",), VMEM_LIMIT),
        name="moe_experts",
    )(blk0, nblk, n_total, xs, w_gate, w_up, w_down)


def _combine_body(slot_ref, ys_hbm, h_ref, rt_ref, g_ref, *rest, tm, tile0, want_h):
    outs, (buf, sem) = rest[:-2], rest[-2:]
    i = pl.program_id(0)
    ni = pl.num_programs(0)

    def row_copy(src_row, k, r, s):
        return pltpu.make_async_copy(ys_hbm.at[pl.ds(src_row, 1), :], buf.at[s, k, pl.ds(r, 1), :], sem.at[s])

    def issue(tile, s):
        def one(r, carry):
            base = ((tile0 + tile) * tm + r) * 2
            row_copy(slot_ref[base], 0, r, s).start(priority=0)
            row_copy(slot_ref[base + 1], 1, r, s).start(priority=1)
            return carry

        lax.fori_loop(0, tm, one, 0, unroll=8)

    @pl.when(i == 0)
    def _():
        issue(0, 0)

    @pl.when(i + 1 < ni)
    def _():
        issue(i + 1, (i + 1) % 2)

    slot = i % 2
    for k in range(2):
        pltpu.make_async_copy(ys_hbm.at[pl.ds(0, tm), :], buf.at[slot, k], sem.at[slot]).wait()

    rt = rt_ref[...]
    g1 = rt[:, 4:5]
    g2 = rt[:, 5:6]
    h2 = h_ref[...] + (buf[slot, 0] * g1 + buf[slot, 1] * g2)
    if want_h:
        outs[0][...] = h2
    outs[-1][...] = _rms(h2, g_ref[...]).astype(outs[-1].dtype)


def moe_combine(ys, slots, h1, route, g_next, tm, u_dtype, row0=0, nrows=None, want_h=True):
    d = h1.shape[1]
    t = h1.shape[0] if nrows is None else nrows
    assert row0 % tm == 0 and t % tm == 0
    tile0 = row0 // tm
    row_spec = pl.BlockSpec((tm, d), lambda i, sl: (i, 0))
    grid_spec = pltpu.PrefetchScalarGridSpec(
        num_scalar_prefetch=1,
        grid=(t // tm,),
        in_specs=[pl.BlockSpec(memory_space=pl.ANY),
                  pl.BlockSpec((tm, d), lambda i, sl: (i + tile0, 0)),
                  pl.BlockSpec((tm, LANES), lambda i, sl: (i + tile0, 0)),
                  pl.BlockSpec((1, d), lambda i, sl: (0, 0))],
        out_specs=[row_spec, row_spec] if want_h else [row_spec],
        scratch_shapes=[pltpu.VMEM((2, 2, tm, d), F32), pltpu.SemaphoreType.DMA((2,))],
    )
    shapes = [jax.ShapeDtypeStruct((t, d), F32)] if want_h else []
    return pl.pallas_call(
        functools.partial(_combine_body, tm=tm, tile0=tile0, want_h=want_h),
        grid_spec=grid_spec,
        out_shape=shapes + [jax.ShapeDtypeStruct((t, d), u_dtype)],
        compiler_params=_cparams(("arbitrary",), VMEM_LIMIT),
        name="moe_combine",
    )(slots, ys, h1, route, g_next.reshape(1, d))


def hier_moe_layer(h1, xn, logits, w_gate, w_up, w_down, layer, g_next, u_dtype, split=None):
    t = h1.shape[0]
    bm = MOE_BM
    route, counts = moe_routing(logits, _tile(t, 256))
    e = route[:, 0:2].astype(I32)
    rank = route[:, 2:4].astype(I32)
    counts = counts[0, N_GROUPS:N_GROUPS + N_EXPERTS].astype(I32)
    padded = (counts + bm - 1) // bm * bm
    pad_end = jnp.cumsum(padded)
    pad_start = pad_end - padded
    start_of = jnp.sum(jnp.where(e[..., None] == jnp.arange(N_EXPERTS, dtype=I32), pad_start, 0), axis=-1)
    slots = (start_of + rank).reshape(-1)
    nb = (2 * t) // bm + N_EXPERTS
    slot_tok = moe_slot_tokens(slots, nb * bm)
    n_used = pad_end[-1:]
    xs = moe_dispatch(xn, slot_tok, n_used, bm * max(f for f in (4, 2, 1) if nb % f == 0))
    ys = moe_experts(xs, pad_start // bm, padded // bm, n_used // bm, w_gate, w_up, w_down, layer, bm)
    tmc = _tile(t, 256)
    if split is None:
        return moe_combine(ys, slots, h1, route, g_next, tmc, u_dtype)
    (u_a,) = moe_combine(ys, slots, h1, route, g_next, tmc, u_dtype, 0, split, want_h=False)
    (u_b,) = moe_combine(ys, slots, h1, route, g_next, tmc, u_dtype, split, t - split, want_h=False)
    return u_a, u_b


def _swap_halves(x):
    lane = lax.broadcasted_iota(I32, x.shape, 1)
    return jnp.where(lane < QK_ROPE // 2, pltpu.roll(x, LANES - QK_ROPE // 2, 1), pltpu.roll(x, QK_ROPE // 2, 1))


def _mla_prep_body(p_ref, gq_ref, gkv_ref, cos_ref, sin_ref, cq_ref, ckv_ref, ckvb_ref, kr_ref):
    p = p_ref[...]
    cq_ref[...] = _rms(p[:, :Q_LORA], gq_ref[...]).astype(cq_ref.dtype)
    ckv = _rms(p[:, Q_LORA:Q_LORA + KV_LORA], gkv_ref[...])
    ckv_ref[...] = ckv
    ckvb_ref[...] = ckv.astype(ckvb_ref.dtype)
    kr = p[:, Q_LORA + KV_LORA:]
    kr_ref[...] = kr * cos_ref[...] + _swap_halves(kr) * sin_ref[...]


def mla_prep(proj, g_q, g_kv, cos, sin, tm):
    t = proj.shape[0]
    return pl.pallas_call(
        _mla_prep_body,
        grid=(t // tm,),
        in_specs=[pl.BlockSpec((tm, proj.shape[1]), lambda i: (i, 0)),
                  pl.BlockSpec((1, Q_LORA), lambda i: (0, 0)),
                  pl.BlockSpec((1, KV_LORA), lambda i: (0, 0)),
                  pl.BlockSpec((tm, LANES), lambda i: (i, 0)),
                  pl.BlockSpec((tm, LANES), lambda i: (i, 0))],
        out_specs=[pl.BlockSpec((tm, Q_LORA), lambda i: (i, 0)),
                   pl.BlockSpec((tm, KV_LORA), lambda i: (i, 0)),
                   pl.BlockSpec((tm, KV_LORA), lambda i: (i, 0)),
                   pl.BlockSpec((tm, LANES), lambda i: (i, 0))],
        out_shape=[jax.ShapeDtypeStruct((t, Q_LORA), BF16), jax.ShapeDtypeStruct((t, KV_LORA), F32),
                   jax.ShapeDtypeStruct((t, KV_LORA), BF16), jax.ShapeDtypeStruct((t, LANES), F32)],
        compiler_params=_cparams(("parallel",)),
        name="mla_prep",
    )(proj, g_q.reshape(1, -1), g_kv.reshape(1, -1), cos, sin)


def _mla_q_body(cq_ref, w_ref, cos_ref, sin_ref, q_ref, wb):
    @pl.when(pl.program_id(0) == 0)
    def _():
        wb[...] = w_ref[...].astype(BF16)

    cq = cq_ref[...]
    cos = cos_ref[...]
    sin = sin_ref[...]
    for h in range(B_HEADS):
        y = _dot(cq, wb[:, h * MLA_QK_PAD:(h + 1) * MLA_QK_PAD])
        xr = y[:, LANES:]
        q_ref[:, h * MLA_QK_PAD:h * MLA_QK_PAD + LANES] = y[:, :LANES].astype(q_ref.dtype)
        q_ref[:, h * MLA_QK_PAD + LANES:(h + 1) * MLA_QK_PAD] = \
            (xr * cos + _swap_halves(xr) * sin).astype(q_ref.dtype)


def mla_q_up(cq, w_uq_pad, cos, sin, tm):
    t = cq.shape[0]
    n = B_HEADS * MLA_QK_PAD
    return pl.pallas_call(
        _mla_q_body,
        grid=(t // tm,),
        in_specs=[pl.BlockSpec((tm, Q_LORA), lambda i: (i, 0)),
                  pl.BlockSpec((Q_LORA, n), lambda i: (0, 0)),
                  pl.BlockSpec((tm, LANES), lambda i: (i, 0)),
                  pl.BlockSpec((tm, LANES), lambda i: (i, 0))],
        out_specs=pl.BlockSpec((tm, n), lambda i: (i, 0)),
        out_shape=jax.ShapeDtypeStruct((t, n), BF16),
        scratch_shapes=[pltpu.VMEM((Q_LORA, n), BF16)],
        compiler_params=_cparams(("arbitrary",), VMEM_LIMIT),
        name="mla_q_up",
    )(cq, w_uq_pad, cos, sin)


def _mla_kv_body(c_ref, kr_ref, wk_ref, wv_ref, k_ref, v_ref, wkb, wvb):
    @pl.when(pl.program_id(0) == 0)
    def _():
        wkb[...] = wk_ref[0].astype(BF16)
        wvb[...] = wv_ref[0].astype(BF16)

    c = c_ref[...].astype(BF16)
    kr = kr_ref[...].astype(k_ref.dtype)
    kn = _dot(c, wkb[...])
    for h in range(B_HEADS):
        k_ref[:, h * MLA_QK_PAD:h * MLA_QK_PAD + LANES] = kn[:, h * QK_NOPE:(h + 1) * QK_NOPE].astype(k_ref.dtype)
        k_ref[:, h * MLA_QK_PAD + LANES:(h + 1) * MLA_QK_PAD] = kr
    v_ref[...] = _dot(c, wvb[...]).astype(v_ref.dtype)


def mla_kv_up(ckv, kr, w_uk, w_uv, layer, tm):
    t = ckv.shape[0]
    nk = B_HEADS * MLA_QK_PAD
    nv = B_HEADS * V_DIM
    return pl.pallas_call(
        _mla_kv_body,
        grid=(t // tm,),
        in_specs=[pl.BlockSpec((tm, KV_LORA), lambda i: (i, 0)),
                  pl.BlockSpec((tm, LANES), lambda i: (i, 0)),
                  pl.BlockSpec((1, KV_LORA, B_HEADS * QK_NOPE), lambda i: (layer, 0, 0)),
                  pl.BlockSpec((1, KV_LORA, nv), lambda i: (layer, 0, 0))],
        out_specs=[pl.BlockSpec((tm, nk), lambda i: (i, 0)), pl.BlockSpec((tm, nv), lambda i: (i, 0))],
        out_shape=[jax.ShapeDtypeStruct((t, nk), BF16), jax.ShapeDtypeStruct((t, nv), BF16)],
        scratch_shapes=[pltpu.VMEM((KV_LORA, B_HEADS * QK_NOPE), BF16), pltpu.VMEM((KV_LORA, nv), BF16)],
        compiler_params=_cparams(("arbitrary",), VMEM_LIMIT),
        name="mla_kv_up",
    )(ckv, kr, w_uk, w_uv)


def kernel(x_prompt, x_sample, cache_a_k, cache_a_v, cache_a_kidx, cache_b_ckv, cache_b_krope, norm_mix, norm_ffn, norm_final, rel_bias, a_w_in, a_w_out, b_w_in, b_norm_q, b_norm_kv, b_w_uq, b_w_uk, b_w_uv, b_w_out, moe_w_grp, moe_b_grp, moe_w_rtr, moe_b_rtr, moe_w_gate, moe_w_up, moe_w_down):
    n_p, s_p, d = x_prompt.shape
    n_s, s_s, _ = x_sample.shape
    past = cache_a_k.shape[2]
    tp = n_p * s_p
    ts = n_s * s_s
    t = tp + ts
    a_qd = A_HEADS * A_HEAD_DIM
    tm = _tile(math.gcd(tp, ts), 1024)
    tm2 = _tile(math.gcd(tp, ts), 512)

    x_parts = [x_prompt.reshape(tp, d), x_sample.reshape(ts, d)]

    def router_params(i):
        w_r = jnp.concatenate([moe_w_grp[i], moe_w_rtr[i],
                               jnp.zeros((d, LANES - N_GROUPS - N_EXPERTS), F32)], axis=1)
        b_r = jnp.concatenate([moe_b_grp[i], moe_b_rtr[i],
                               jnp.zeros((LANES - N_GROUPS - N_EXPERTS,), F32)]).reshape(1, LANES)
        return w_r, b_r

    u0 = rmsnorm(x_parts, norm_mix[0], tm)
    w_a = a_w_in[0]
    (q_b,) = matmul(u0, w_a, 0, a_qd, [BF16], tm, 512, "a_proj_q")
    tmp, tms = _tile(tp, 1024), _tile(ts, 1024)
    kp_f, kp_b = matmul(u0, w_a, a_qd, a_qd, [F32, BF16], tmp, 512, "a_proj_k", 0, tp)
    vp_f, vp_b = matmul(u0, w_a, 2 * a_qd, a_qd, [F32, BF16], tmp, 512, "a_proj_v", 0, tp)
    ks_f, ks_b = matmul(u0, w_a, a_qd, a_qd, [F32, BF16], tms, 512, "a_proj_k", tp, ts)
    vs_f, vs_b = matmul(u0, w_a, 2 * a_qd, a_qd, [F32, BF16], tms, 512, "a_proj_v", tp, ts)
    (qi_b,) = matmul(u0, w_a, 3 * a_qd, IDX_HEADS * IDX_DIM, [BF16], tm, 512, "a_proj_qi")
    c_ki = 3 * a_qd + IDX_HEADS * IDX_DIM
    w_ki = w_a[:, c_ki:c_ki + IDX_DIM]
    w_wi = w_a[:, c_ki + IDX_DIM:c_ki + IDX_DIM + IDX_HEADS]
    zk = jnp.zeros((d, IDX_DIM), F32)
    w_tail = jnp.concatenate([w_ki, zk, zk, w_ki, w_wi, jnp.zeros((d, LANES - IDX_HEADS), F32)], axis=1)
    (tail,) = matmul(u0, w_tail, 0, 3 * LANES, [F32], tm, 3 * LANES, "a_proj_tail")
    kidx = tail[:, :IDX_DIM]

    bias_tab = rel_bias_tables(rel_bias)
    a_scale = A_HEAD_DIM ** -0.5
    mask_p = dsa_select(qi_b, tail, 2, tail, n_seq=n_p, sq=s_p, bq=128, sk=s_p, sk_real=s_p, tk=512,
                        pos0=0, q_row0=0)
    kp_t = kp_b.reshape(n_p, s_p, a_qd).transpose(0, 2, 1).reshape(n_p * a_qd, s_p)
    att_p = attention(q_b, [(kp_t, vp_b, s_p, 0, 0)], bias_tab, mask_p, n_seq=n_p, sq=s_p, bq=ATT_BQ, q_row0=0,
                      q_col0=0, n_heads=A_HEADS, hg=ATT_HG, dq=A_HEAD_DIM, dv=A_HEAD_DIM, scale=a_scale, pos0=0,
                      dyn_sk=s_p, name="dsa_attention_prompt")
    sk_s = past + s_s
    sk_pad = (sk_s + ATT_TILE - 1) // ATT_TILE * ATT_TILE
    ki_past = cache_a_kidx[0]
    zp = jnp.zeros_like(ki_past)
    kc_past = jnp.concatenate([ki_past, zp, zp, ki_past], axis=-1)
    kc_new = tail[tp:, :2 * LANES].reshape(n_s, s_s, 2 * LANES)
    kc_s = jnp.concatenate([kc_past, kc_new, jnp.zeros((n_s, sk_pad - sk_s, 2 * LANES), F32)], axis=1)
    mask_s = dsa_select(qi_b, tail, 2, kc_s.reshape(n_s * sk_pad, 2 * LANES), n_seq=n_s, sq=s_s, bq=s_s,
                        sk=sk_pad, sk_real=sk_s, tk=ATT_TILE, pos0=past, q_row0=tp)
    tiles_s = [(0, r, r, ATT_TILE) for r in range(0, past, ATT_TILE)] + [(1, 0, past, s_s)]
    att_s = attention(q_b, [(cache_a_k[0].reshape(n_s * past * A_HEADS, A_HEAD_DIM),
                             cache_a_v[0].reshape(n_s * past * A_HEADS, A_HEAD_DIM), past, 0, 0),
                            (ks_b, vs_b, s_s, 0, 0)],
                      bias_tab, mask_s, n_seq=n_s, sq=s_s, bq=s_s, q_row0=tp, q_col0=0, n_heads=A_HEADS, hg=A_HEADS,
                      dq=A_HEAD_DIM, dv=A_HEAD_DIM, scale=a_scale, pos0=past, seg_tiles=tiles_s,
                      name="dsa_attention_sample")

    w_r0, b_r0 = router_params(0)
    h1, xn0, lg0 = outproj_norm_router([att_p, att_s], a_w_out, 0, x_parts, norm_ffn[0], w_r0, b_r0, tm2, 512)
    h2, u1 = hier_moe_layer(h1, xn0, lg0, moe_w_gate, moe_w_up, moe_w_down, 0, norm_mix[1], BF16)

    n_in = Q_LORA + KV_LORA + QK_ROPE
    w_b = jnp.concatenate([b_w_in[0], jnp.zeros((d, LANES - QK_ROPE), F32)], axis=1)
    (proj,) = matmul(u1, w_b, 0, n_in + LANES - QK_ROPE, [F32], tm, 384, "b_proj")
    half = QK_ROPE // 2
    inv = ROPE_THETA ** (-jnp.arange(half, dtype=F32) / half)
    pos_all = jnp.concatenate([jnp.tile(jnp.arange(s_p, dtype=I32), n_p),
                               jnp.tile(past + jnp.arange(s_s, dtype=I32), n_s)])
    ang = pos_all.astype(F32)[:, None] * inv[None, :]
    zl = jnp.zeros((t, LANES - QK_ROPE), F32)
    cos_t = jnp.concatenate([jnp.cos(ang), jnp.cos(ang), zl], axis=1)
    sin_t = jnp.concatenate([-jnp.sin(ang), jnp.sin(ang), zl], axis=1)
    cq_b, ckv_f, ckv_b, kr_f = mla_prep(proj, b_norm_q[0], b_norm_kv[0], cos_t, sin_t, tm)
    w_uq_pad = jnp.pad(b_w_uq[0].reshape(Q_LORA, B_HEADS, QK_NOPE + QK_ROPE),
                       ((0, 0), (0, 0), (0, MLA_QK_PAD - QK_NOPE - QK_ROPE))).reshape(Q_LORA, B_HEADS * MLA_QK_PAD)
    qm = mla_q_up(cq_b, w_uq_pad, cos_t, sin_t, tm2)
    km_p, vm_p = mla_kv_up(ckv_b[:tp], kr_f[:tp], b_w_uk, b_w_uv, 0, _tile(tp, 512))
    ctab = causal_tables()
    hgb = 8
    km_t = km_p.reshape(n_p, s_p, B_HEADS * MLA_QK_PAD).transpose(0, 2, 1).reshape(n_p * B_HEADS * MLA_QK_PAD, s_p)
    matt_p = attention(qm, [(km_t, vm_p, s_p, 0, 0)], ctab, None, n_seq=n_p, sq=s_p, bq=ATT_BQ, q_row0=0, q_col0=0,
                       n_heads=B_HEADS, hg=ATT_HG, dq=MLA_QK_PAD, dv=V_DIM, scale=B_SCALE, pos0=0, dyn_sk=s_p,
                       name="mla_attention_prompt")
    ckv_all = jnp.concatenate([cache_b_ckv[0], ckv_f[tp:].reshape(n_s, s_s, KV_LORA)], axis=1)
    kr_past = jnp.concatenate([cache_b_krope[0], jnp.zeros((n_s, past, LANES - QK_ROPE), F32)], axis=-1)
    kr_all = jnp.concatenate([kr_past, kr_f[tp:].reshape(n_s, s_s, LANES)], axis=1)
    km_s, vm_s = mla_kv_up(ckv_all.reshape(n_s * sk_s, KV_LORA), kr_all.reshape(n_s * sk_s, LANES),
                           b_w_uk, b_w_uv, 0, sk_s)
    tiles_m = [(0, r, r, ATT_TILE) for r in range(0, past, ATT_TILE)] + [(0, past, past, s_s)]
    matt_s = attention(qm, [(km_s, vm_s, sk_s, 0, 0)], ctab, None, n_seq=n_s, sq=s_s, bq=s_s, q_row0=tp, q_col0=0,
                       n_heads=B_HEADS, hg=hgb, dq=MLA_QK_PAD, dv=V_DIM, scale=B_SCALE, pos0=past,
                       seg_tiles=tiles_m, name="mla_attention_sample")

    w_r1, b_r1 = router_params(1)
    h3, xn1, lg1 = outproj_norm_router([matt_p, matt_s], b_w_out, 0, [h2], norm_ffn[1], w_r1, b_r1, tm2, 512)
    y_p, y_s = hier_moe_layer(h3, xn1, lg1, moe_w_gate, moe_w_up, moe_w_down, 1, norm_final, F32, split=tp)
    y_prompt = y_p.reshape(n_p, s_p, d)
    y_sample = y_s.reshape(n_s, s_s, d)

    def heads(a, n, s):
        return a.reshape(1, n, s, A_HEADS, A_HEAD_DIM)

    kr_out = kr_f[:, :QK_ROPE]
    return (y_prompt, y_sample,
            heads(kp_f, n_p, s_p), heads(vp_f, n_p, s_p), kidx[:tp].reshape(1, n_p, s_p, IDX_DIM),
            ckv_f[:tp].reshape(1, n_p, s_p, KV_LORA), kr_out[:tp].reshape(1, n_p, s_p, QK_ROPE),
            heads(ks_f, n_s, s_s), heads(vs_f, n_s, s_s), kidx[tp:].reshape(1, n_s, s_s, IDX_DIM),
            ckv_f[tp:].reshape(1, n_s, s_s, KV_LORA), kr_out[tp:].reshape(1, n_s, s_s, QK_ROPE))
```

```python
import functools
import math

import jax
import jax.numpy as jnp
from jax import lax
from jax.experimental import pallas as pl
from jax.experimental.pallas import tpu as pltpu

F32 = jnp.float32
BF16 = jnp.bfloat16
I32 = jnp.int32

RMS_EPS = 1e-6
CHUNK = 64
NEG = -1e30
INT_MIN = -2 ** 31
INT_MAX = 2 ** 31 - 1
LOG2E = math.log2(math.e)

LANES = 128
KEY_BLOCK = 128
ATT_TILE = 256
ATT_BQ = 128
ATT_HG = 8
VMEM_LIMIT = 56 * 1024 * 1024

A_HEADS = 16
A_HEAD_DIM = 128
IDX_HEADS = 16
IDX_DIM = 64
TOPK_MAX = 256
IDX_W_SCALE = float((IDX_HEADS * IDX_DIM) ** -0.5)
NUM_BUCKETS = 32
MAX_DISTANCE = 128
B_HEADS = 16
Q_LORA = 512
KV_LORA = 512
QK_NOPE = 128
QK_ROPE = 64
V_DIM = 128
ROPE_THETA = 10000.0
B_SCALE = float((QK_NOPE + QK_ROPE) ** -0.5)
MLA_QK_PAD = 256
N_GROUPS = 8
EXPERTS_PER_GROUP = 8
N_EXPERTS = 64
D_EXPERT = 512
MOE_BM = 128
ROW_CHUNKS = 16
X_AHEAD = 2


def _tile(n, pref):
    for c in range(pref, 0, -LANES):
        if n % c == 0:
            return c
    raise ValueError(f"no 128-multiple tile divides {n}")


def _cparams(sem, vmem=None):
    return pltpu.CompilerParams(dimension_semantics=sem, vmem_limit_bytes=vmem)


def _dot(a, b):
    return jnp.dot(a, b, preferred_element_type=F32)


def _dot_nt(a, b):
    return lax.dot_general(a, b, (((1,), (1,)), ((), ())), preferred_element_type=F32)


def _rms(x, g):
    ms = jnp.mean(x * x, axis=-1, keepdims=True)
    return x * lax.rsqrt(ms + RMS_EPS) * g


def _pick_rows(i, n_first, refs):
    if len(refs) == 1:
        return refs[0][...]
    return jnp.where(i < n_first, refs[0][...], refs[1][...])


def _row_split_specs(parts, tm, width, col_of):
    n_first = parts[0].shape[0] // tm
    assert all(p.shape[0] % tm == 0 for p in parts)
    specs = [pl.BlockSpec((tm, width), lambda i, *r: (jnp.minimum(i, n_first - 1), col_of(i, *r)))]
    if len(parts) == 2:
        specs.append(pl.BlockSpec((tm, width), lambda i, *r: (jnp.maximum(i - n_first, 0), col_of(i, *r))))
    return specs, n_first


def _rmsnorm_body(*refs, n_first):
    x = _pick_rows(pl.program_id(0), n_first, refs[:-2])
    g_ref, o_ref = refs[-2:]
    o_ref[...] = _rms(x, g_ref[...]).astype(o_ref.dtype)


def rmsnorm(xs, g, tm, out_dtype=BF16):
    d = xs[0].shape[1]
    t = sum(x.shape[0] for x in xs)
    specs, n_first = _row_split_specs(xs, tm, d, lambda i: 0)
    return pl.pallas_call(
        functools.partial(_rmsnorm_body, n_first=n_first),
        grid=(t // tm,),
        in_specs=specs + [pl.BlockSpec((1, d), lambda i: (0, 0))],
        out_specs=pl.BlockSpec((tm, d), lambda i: (i, 0)),
        out_shape=jax.ShapeDtypeStruct((t, d), out_dtype),
        compiler_params=_cparams(("parallel",)),
        name="rmsnorm",
    )(*xs, g.reshape(1, d))


def _matmul_body(a_ref, w_ref, *rest):
    outs, wb = rest[:-1], rest[-1]

    @pl.when(pl.program_id(1) == 0)
    def _():
        wb[...] = w_ref[...].astype(BF16)

    r = _dot(a_ref[...], wb[...])
    for o in outs:
        o[...] = r.astype(o.dtype)


def matmul(a, w, col0, ncols, out_dtypes, tm, tn, name, row0=0, nrows=None):
    k = a.shape[1]
    t = a.shape[0] if nrows is None else nrows
    assert ncols % tn == 0 and col0 % tn == 0 and t % tm == 0 and row0 % tm == 0
    cb = col0 // tn
    rb = row0 // tm
    return pl.pallas_call(
        _matmul_body,
        grid=(ncols // tn, t // tm),
        in_specs=[pl.BlockSpec((tm, k), lambda j, i: (i + rb, 0)),
                  pl.BlockSpec((k, tn), lambda j, i: (0, j + cb))],
        out_specs=[pl.BlockSpec((tm, tn), lambda j, i: (i, j)) for _ in out_dtypes],
        out_shape=[jax.ShapeDtypeStruct((t, ncols), dt) for dt in out_dtypes],
        scratch_shapes=[pltpu.VMEM((k, tn), BF16)],
        compiler_params=_cparams(("arbitrary", "arbitrary"), VMEM_LIMIT),
        name=name,
    )(a, w)


def _float_sort_key(x):
    bits = lax.bitcast_convert_type(x, I32)
    return bits ^ ((bits >> 31) & INT_MAX)


def _select_body(qi_ref, wi_ref, kc_ref, mask_ref, key_scr, jm_scr, *, bq, sk, sk_real, tk, pos0, topk):
    b = pl.program_id(1)
    t0 = pos0 + b * bq
    kmax = jnp.minimum(sk_real, ((t0 + bq - 1) // CHUNK + 1) * CHUNK)
    nkt = (kmax + tk - 1) // tk
    nch = tk // LANES

    w = wi_ref[...] * IDX_W_SCALE
    wb = [jnp.broadcast_to(w[:, h:h + 1], (bq, LANES)) for h in range(IDX_HEADS)]
    qpos = t0 + lax.broadcasted_iota(I32, (LANES, LANES), 1)
    lim = jnp.minimum((qpos // CHUNK + 1) * CHUNK, sk_real)
    sub = lax.broadcasted_iota(I32, (LANES, LANES), 0)

    def to_lanes(x):
        if bq < LANES:
            x = jnp.concatenate([x, jnp.zeros((LANES - bq, LANES), x.dtype)], axis=0)
        return x.T

    def score_tile(j, carry):
        off = pl.multiple_of(j * tk, tk)
        kc = kc_ref[pl.ds(off, tk), :].astype(BF16)
        ka, kb = kc[:, :LANES], kc[:, LANES:]
        accs = [jnp.zeros((bq, LANES), F32) for _ in range(nch)]
        for g in range(IDX_HEADS // 2):
            qg = qi_ref[:, g * LANES:(g + 1) * LANES]
            sa = _dot_nt(qg, ka)
            sb = _dot_nt(qg, kb)
            for c in range(nch):
                sl = slice(c * LANES, (c + 1) * LANES)
                accs[c] = accs[c] + wb[2 * g] * jnp.maximum(sa[:, sl], 0.0) \
                    + wb[2 * g + 1] * jnp.maximum(sb[:, sl], 0.0)
        for c in range(nch):
            kpos = off + c * LANES + sub
            sc = jnp.where(kpos < lim, to_lanes(accs[c]), -jnp.inf)
            key_scr[pl.ds(off + c * LANES, LANES), :] = _float_sort_key(sc)
        return carry

    lax.fori_loop(0, nkt, score_tile, 0)

    def count(indicator):
        def tile(j, cnt):
            off = pl.multiple_of(j * tk, tk)
            for c in range(nch):
                kt = key_scr[pl.ds(off + c * LANES, LANES), :]
                cnt = cnt + indicator(kt, off + c * LANES + sub)
            return cnt
        cnt = lax.fori_loop(0, nkt, tile, jnp.zeros((LANES, LANES), F32))
        return jnp.sum(cnt, axis=0, keepdims=True)

    def bit_step(i, pfx_u):
        bit = lax.shift_left(jnp.int32(1), 31 - i)
        cand_u = pfx_u | bit
        cand_s = cand_u ^ INT_MIN
        total = count(lambda kt, kp: jnp.where(kt >= cand_s, 1.0, 0.0))
        return jnp.where(total >= topk, cand_u, pfx_u)

    pfx = lax.fori_loop(0, 32, bit_step, jnp.zeros((1, LANES), I32))
    thr = pfx ^ INT_MIN

    n_gt = count(lambda kt, kp: jnp.where(kt > thr, 1.0, 0.0))
    n_ge = count(lambda kt, kp: jnp.where(kt >= thr, 1.0, 0.0))
    quota = topk - n_gt
    jm_scr[...] = jnp.full((1, LANES), INT_MAX, I32)
    any_excess = jnp.max(jnp.where(n_ge > topk, 1.0, 0.0)) > 0.0

    @pl.when(any_excess)
    def _():
        nbits = max(1, int(sk - 1).bit_length())

        def idx_step(i, ans):
            cand = ans | lax.shift_left(jnp.int32(1), nbits - 1 - i)
            below = count(lambda kt, kp: jnp.where(kt == thr, jnp.where(kp < cand, 1.0, 0.0), 0.0))
            return jnp.where(below < quota, cand, ans)

        jm_scr[...] = lax.fori_loop(0, nbits, idx_step, jnp.zeros((1, LANES), I32))

    jm = jm_scr[...]

    def write_tile(j, carry):
        off = pl.multiple_of(j * tk, tk)
        for c in range(nch):
            kt = key_scr[pl.ds(off + c * LANES, LANES), :]
            kpos = off + c * LANES + sub
            v = jnp.where(kt > thr, 0.0, jnp.where(kt == thr, jnp.where(kpos <= jm, 0.0, NEG), NEG))
            v = jnp.where(kpos < lim, v, NEG)
            mask_ref[:, pl.ds(off + c * LANES, LANES)] = v.T[:bq].astype(mask_ref.dtype)
        return carry

    lax.fori_loop(0, nkt, write_tile, 0)

    def fill_tile(j, carry):
        off = pl.multiple_of(j * tk, tk)
        mask_ref[:, pl.ds(off, tk)] = jnp.full((bq, tk), NEG, mask_ref.dtype)
        return carry

    lax.fori_loop(nkt, sk // tk, fill_tile, 0)


def dsa_select(qi, wi_arr, wi_blk, kc, *, n_seq, sq, bq, sk, sk_real, tk, pos0, q_row0):
    nqb = sq // bq
    qb0 = q_row0 // bq
    assert q_row0 % bq == 0 and sk % tk == 0 and tk >= TOPK_MAX
    topk = min(TOPK_MAX, sk_real // 4)
    body = functools.partial(_select_body, bq=bq, sk=sk, sk_real=sk_real, tk=tk, pos0=pos0, topk=topk)
    return pl.pallas_call(
        body,
        grid=(n_seq, nqb),
        in_specs=[pl.BlockSpec((bq, IDX_HEADS * IDX_DIM), lambda s, b: (qb0 + s * nqb + b, 0)),
                  pl.BlockSpec((bq, LANES), lambda s, b: (qb0 + s * nqb + b, wi_blk)),
                  pl.BlockSpec((sk, 2 * LANES), lambda s, b: (s, 0))],
        out_specs=pl.BlockSpec((bq, sk), lambda s, b: (s * nqb + b, 0)),
        out_shape=jax.ShapeDtypeStruct((n_seq * sq, sk), BF16),
        scratch_shapes=[pltpu.VMEM((sk, LANES), I32), pltpu.VMEM((1, LANES), I32)],
        compiler_params=_cparams(("parallel", "arbitrary"), VMEM_LIMIT),
        name="dsa_select",
    )(qi, wi_arr, kc)


def _rel_tables_body(bias_ref, tab_ref):
    i = lax.broadcasted_iota(I32, (KEY_BLOCK, KEY_BLOCK), 0)
    j = lax.broadcasted_iota(I32, (KEY_BLOCK, KEY_BLOCK), 1)
    nb = NUM_BUCKETS // 2
    max_exact = nb // 2
    edges = [12, 16, 23, 32, 46, 64, 91]
    for d in range(4):
        rel = j - i + (d - 2) * KEY_BLOCK
        n = jnp.abs(rel)
        large = jnp.full_like(n, max_exact)
        for e in edges:
            large = large + jnp.where(n >= e, 1, 0)
        bucket = jnp.where(rel > 0, nb, 0) + jnp.where(n < max_exact, n, large)
        for h in range(A_HEADS):
            acc = jnp.zeros((KEY_BLOCK, KEY_BLOCK), F32)
            for k in range(NUM_BUCKETS):
                acc = jnp.where(bucket == k, bias_ref[k, h], acc)
            tab_ref[h, d] = acc * LOG2E


def rel_bias_tables(rel_bias):
    return pl.pallas_call(
        _rel_tables_body,
        in_specs=[pl.BlockSpec(memory_space=pltpu.SMEM)],
        out_shape=jax.ShapeDtypeStruct((A_HEADS, 4, KEY_BLOCK, KEY_BLOCK), F32),
        name="rel_bias_tables",
    )(rel_bias)


def _causal_tables_body(tab_ref):
    i = lax.broadcasted_iota(I32, (KEY_BLOCK, KEY_BLOCK), 0)
    j = lax.broadcasted_iota(I32, (KEY_BLOCK, KEY_BLOCK), 1)
    zero = jnp.zeros((KEY_BLOCK, KEY_BLOCK), F32)
    tab_ref[0, 0] = zero
    tab_ref[0, 1] = zero
    tab_ref[0, 2] = jnp.where(j // CHUNK <= i // CHUNK, 0.0, NEG)
    tab_ref[0, 3] = jnp.full((KEY_BLOCK, KEY_BLOCK), NEG, F32)


def causal_tables():
    return pl.pallas_call(
        _causal_tables_body,
        out_shape=jax.ShapeDtypeStruct((1, 4, KEY_BLOCK, KEY_BLOCK), F32),
        name="causal_tables",
    )()


def _attn_pipe_body(*refs, has_mask, per_head_tab, bq, hg, dq, dv, scale, pos0, sk):
    q_ref, k_ref, v_ref, tab_ref = refs[:4]
    mask_ref = refs[4] if has_mask else None
    out_ref = refs[5] if has_mask else refs[4]
    s_scr, p_scr, m_scr, l_scr, acc_scr = refs[-5:]

    b = pl.program_id(2)
    t0 = pos0 + b * bq
    qblk = t0 // KEY_BLOCK
    kmax = jnp.minimum(sk, ((t0 + bq - 1) // CHUNK + 1) * CHUNK)
    nkt = (kmax + ATT_TILE - 1) // ATT_TILE
    nch = ATT_TILE // LANES

    m_scr[...] = jnp.full(m_scr.shape, NEG, F32)
    l_scr[...] = jnp.zeros(l_scr.shape, F32)
    acc_scr[...] = jnp.zeros(acc_scr.shape, F32)

    def paired_loop(tile):
        def four(jj, carry):
            for u in range(4):
                tile(4 * jj + u, carry)
            return carry

        n4 = nkt // 4
        lax.fori_loop(0, n4, four, 0)

        @pl.when(nkt % 4 >= 2)
        def _():
            tile(4 * n4, 0)
            tile(4 * n4 + 1, 0)

        @pl.when(nkt % 2 == 1)
        def _():
            tile(nkt - 1, 0)

    def logits_tile(j, carry):
        off = pl.multiple_of(j * ATT_TILE, ATT_TILE)
        kb0 = off // KEY_BLOCK
        ds = [[jnp.clip(kb0 + c - (qblk + r) + 2, 0, 3) for c in range(nch)] for r in range(bq // KEY_BLOCK)]
        if has_mask:
            mk = mask_ref[:, pl.ds(off, ATT_TILE)].astype(F32)
        for h in range(hg):
            s = _dot(q_ref[:, h * dq:(h + 1) * dq], k_ref[h * dq:(h + 1) * dq, pl.ds(off, ATT_TILE)]) * scale
            hh = h if per_head_tab else 0
            s = s + jnp.concatenate(
                [jnp.concatenate([tab_ref[hh, d] for d in row], axis=1) for row in ds], axis=0)
            if has_mask:
                s = s + mk
            s_scr[h, :, pl.ds(off, ATT_TILE)] = s
            mvec = m_scr[h]
            for c in range(nch):
                mvec = jnp.maximum(mvec, s[:, c * LANES:(c + 1) * LANES])
            m_scr[h] = mvec
        return carry

    paired_loop(logits_tile)
    for h in range(hg):
        m_scr[h] = jnp.broadcast_to(jnp.max(m_scr[h], axis=1, keepdims=True), (bq, LANES))

    def exp_tile(j, carry):
        off = pl.multiple_of(j * ATT_TILE, ATT_TILE)
        for h in range(hg):
            s = s_scr[h, :, pl.ds(off, ATT_TILE)]
            m = m_scr[h]
            p = [jnp.exp2(s[:, c * LANES:(c + 1) * LANES] - m) for c in range(nch)]
            lvec = l_scr[h]
            for c in range(nch):
                lvec = lvec + p[c]
            l_scr[h] = lvec
            p_scr[h, :, pl.ds(off, ATT_TILE)] = jnp.concatenate(p, axis=1).astype(BF16)
        return carry

    lax.fori_loop(0, nkt, exp_tile, 0)

    def pv_tile(j, carry):
        off = pl.multiple_of(j * ATT_TILE, ATT_TILE)
        for h in range(hg):
            acc_scr[h] = acc_scr[h] + _dot(p_scr[h, :, pl.ds(off, ATT_TILE)],
                                           v_ref[pl.ds(off, ATT_TILE), h * dv:(h + 1) * dv])
        return carry

    paired_loop(pv_tile)
    for h in range(hg):
        l_row = jnp.sum(l_scr[h], axis=1, keepdims=True)
        out_ref[:, h * dv:(h + 1) * dv] = (acc_scr[h] / l_row).astype(out_ref.dtype)


def _attn_body(*refs, n_seg, has_mask, per_head_tab, bq, hg, dq, dv, scale, pos0, seg_tiles, by_head):
    q_ref = refs[0]
    kv = refs[1:1 + 2 * n_seg]
    pos = 1 + 2 * n_seg
    tab_ref = refs[pos]
    pos += 1
    mask_ref = refs[pos] if has_mask else None
    pos += 1 if has_mask else 0
    out_ref = refs[pos]
    s_scr, p_scr = refs[pos + 1:pos + 3]

    b = pl.program_id(2)
    t0 = pos0 + b * bq
    qblk = t0 // KEY_BLOCK

    def lane_pad(x, fill):
        w = x.shape[1]
        return x if w == LANES else jnp.concatenate([x, jnp.full((bq, LANES - w), fill, F32)], axis=1)

    mvec = [jnp.full((bq, LANES), NEG, F32) for _ in range(hg)]
    for (si, row0, key0, width) in seg_tiles:
        nch = max(1, width // KEY_BLOCK)
        cw = min(width, KEY_BLOCK)
        ds = [jnp.clip(key0 // KEY_BLOCK + c - qblk + 2, 0, 3) for c in range(nch)]
        k_t = None if by_head[si] else kv[2 * si][row0:row0 + width, :].astype(BF16)
        if has_mask:
            mk = mask_ref[:, key0:key0 + width].astype(F32)
        for h in range(hg):
            hh = h if per_head_tab else 0
            if by_head[si]:
                k_h = kv[2 * si][pl.ds(row0 * hg + h, width, stride=hg), :].astype(BF16)
            else:
                k_h = k_t[:, h * dq:(h + 1) * dq]
            s = _dot_nt(q_ref[:, h * dq:(h + 1) * dq], k_h) * scale
            bias = [tab_ref[hh, ds[c], :bq, :cw] for c in range(nch)]
            s = s + (jnp.concatenate(bias, axis=1) if nch > 1 else bias[0])
            if has_mask:
                s = s + mk
            s_scr[h, :, key0:key0 + width] = s
            for c in range(nch):
                mvec[h] = jnp.maximum(mvec[h], lane_pad(s[:, c * cw:(c + 1) * cw], NEG))

    m = [jnp.broadcast_to(jnp.max(mvec[h], axis=1, keepdims=True), (bq, LANES)) for h in range(hg)]
    lvec = [jnp.zeros((bq, LANES), F32) for _ in range(hg)]
    for (si, row0, key0, width) in seg_tiles:
        nch = max(1, width // KEY_BLOCK)
        cw = min(width, KEY_BLOCK)
        for h in range(hg):
            s = s_scr[h, :, key0:key0 + width]
            p = [jnp.exp2(s[:, c * cw:(c + 1) * cw] - m[h][:, :cw]) for c in range(nch)]
            for c in range(nch):
                lvec[h] = lvec[h] + lane_pad(p[c], 0.0)
            p_scr[h, :, key0:key0 + width] = (jnp.concatenate(p, axis=1) if nch > 1 else p[0]).astype(BF16)

    acc = [jnp.zeros((bq, dv), F32) for _ in range(hg)]
    for (si, row0, key0, width) in seg_tiles:
        v_t = None if by_head[si] else kv[2 * si + 1][row0:row0 + width, :].astype(BF16)
        for h in range(hg):
            if by_head[si]:
                v_h = kv[2 * si + 1][pl.ds(row0 * hg + h, width, stride=hg), :].astype(BF16)
            else:
                v_h = v_t[:, h * dv:(h + 1) * dv]
            acc[h] = acc[h] + _dot(p_scr[h, :, key0:key0 + width], v_h)

    for h in range(hg):
        l_row = jnp.sum(lvec[h], axis=1, keepdims=True)
        out_ref[:, h * dv:(h + 1) * dv] = (acc[h] / l_row).astype(out_ref.dtype)


def attention(q, segs, tab, mask, *, n_seq, sq, bq, q_row0, q_col0, n_heads, hg, dq, dv, scale, pos0,
              seg_tiles=None, dyn_sk=None, name="attention"):
    nqb = sq // bq
    ng = n_heads // hg
    qb0 = q_row0 // bq
    assert q_row0 % bq == 0 and q_col0 % (hg * dq) == 0 and pos0 % KEY_BLOCK == 0
    scale = scale * LOG2E
    assert bq % KEY_BLOCK == 0 or nqb == 1
    qc0 = q_col0 // (hg * dq)
    in_specs = [pl.BlockSpec((bq, hg * dq), lambda s, g, b: (qb0 + s * nqb + b, qc0 + g))]
    args = [q]
    by_head = []
    for (k, v, rows, kc0, vc0) in segs:
        assert kc0 % (hg * dq) == 0 and vc0 % (hg * dv) == 0
        by_head.append(k.shape[1] == dq and n_heads > 1)
        if by_head[-1]:
            assert hg == n_heads and dq == dv and k.shape[0] == n_seq * rows * n_heads and dyn_sk is None
            for _ in range(2):
                in_specs.append(pl.BlockSpec((rows * n_heads, dq), lambda s, g, b: (s, 0)))
            args += [k, v]
            continue
        mode = dict(pipeline_mode=pl.Buffered(1)) if dyn_sk is not None else {}
        if dyn_sk is not None:
            assert kc0 == 0 and k.shape == (n_seq * n_heads * dq, rows)
            in_specs.append(pl.BlockSpec((hg * dq, rows), lambda s, g, b: (s * ng + g, 0), **mode))
        else:
            in_specs.append(pl.BlockSpec((rows, hg * dq), lambda s, g, b, c=kc0 // (hg * dq): (s, c + g)))
        in_specs.append(pl.BlockSpec((rows, hg * dv), lambda s, g, b, c=vc0 // (hg * dv): (s, c + g), **mode))
        args += [k, v]
    per_head_tab = tab.shape[0] > 1
    if per_head_tab:
        in_specs.append(pl.BlockSpec((hg, 4, KEY_BLOCK, KEY_BLOCK), lambda s, g, b: (g, 0, 0, 0)))
    else:
        in_specs.append(pl.BlockSpec((1, 4, KEY_BLOCK, KEY_BLOCK), lambda s, g, b: (0, 0, 0, 0)))
    args.append(tab)
    if mask is not None:
        in_specs.append(pl.BlockSpec((bq, mask.shape[1]), lambda s, g, b: (s * nqb + b, 0)))
        args.append(mask)
    if dyn_sk is not None:
        assert len(segs) == 1 and dyn_sk % ATT_TILE == 0 and bq % KEY_BLOCK == 0
        body = functools.partial(_attn_pipe_body, has_mask=mask is not None, per_head_tab=per_head_tab,
                                 bq=bq, hg=hg, dq=dq, dv=dv, scale=scale, pos0=pos0, sk=dyn_sk)
        scratch = [pltpu.VMEM((hg, bq, dyn_sk), F32), pltpu.VMEM((hg, bq, dyn_sk), BF16),
                   pltpu.VMEM((hg, bq, LANES), F32), pltpu.VMEM((hg, bq, LANES), F32),
                   pltpu.VMEM((hg, bq, dv), F32)]
    else:
        body = functools.partial(_attn_body, n_seg=len(segs), has_mask=mask is not None,
                                 per_head_tab=per_head_tab, bq=bq, hg=hg, dq=dq, dv=dv, scale=scale, pos0=pos0,
                                 seg_tiles=seg_tiles, by_head=tuple(by_head))
        sk_tot = max(key0 + width for (_, _, key0, width) in seg_tiles)
        sk_tot = (sk_tot + LANES - 1) // LANES * LANES
        scratch = [pltpu.VMEM((hg, bq, sk_tot), F32), pltpu.VMEM((hg, bq, sk_tot), BF16)]
    return pl.pallas_call(
        body,
        grid=(n_seq, ng, nqb),
        in_specs=in_specs,
        out_specs=pl.BlockSpec((bq, hg * dv), lambda s, g, b: (s * nqb + b, g)),
        out_shape=jax.ShapeDtypeStruct((n_seq * sq, n_heads * dv), BF16),
        scratch_shapes=scratch,
        compiler_params=_cparams(("parallel", "parallel", "arbitrary"), VMEM_LIMIT),
        name=name,
    )(*args)


def _outproj_body(*refs, tn, na, nx, a_first, x_first):
    i = pl.program_id(0)
    j = pl.program_id(1)
    a = _pick_rows(i, a_first, refs[:na])
    w_ref = refs[na]
    x = _pick_rows(i, x_first, refs[na + 1:na + 1 + nx])
    g_ref, wr_ref, br_ref, h_ref, xn_ref, lg_ref, hrow, wb = refs[na + 1 + nx:]
    col = pl.ds(pl.multiple_of(j * tn, tn), tn)

    @pl.when(i == 0)
    def _():
        wb[:, col] = w_ref[0].astype(BF16)

    r = x + _dot(a, wb[:, col])
    h_ref[...] = r
    hrow[:, col] = r

    @pl.when(j == pl.num_programs(1) - 1)
    def _():
        xn = _rms(hrow[...], g_ref[...])
        tm = hrow.shape[0]
        for c in range(ROW_CHUNKS):
            xn_ref[pl.ds(c, tm, stride=ROW_CHUNKS), :] = xn[:, c * LANES:(c + 1) * LANES]
        lg_ref[...] = _dot(xn.astype(BF16), wr_ref[...].astype(BF16)) + br_ref[...]


def outproj_norm_router(a_parts, w_out, layer, x_parts, g_ffn, w_r, b_r, tm, tn):
    k = a_parts[0].shape[1]
    t = sum(p.shape[0] for p in a_parts)
    d = w_out.shape[-1]
    a_specs, a_first = _row_split_specs(a_parts, tm, k, lambda i, j: 0)
    x_specs, x_first = _row_split_specs(x_parts, tm, tn, lambda i, j: j)
    nj = d // tn
    return pl.pallas_call(
        functools.partial(_outproj_body, tn=tn, na=len(a_parts), nx=len(x_parts), a_first=a_first,
                          x_first=x_first),
        grid=(t // tm, d // tn),
        in_specs=a_specs + [pl.BlockSpec((1, k, tn), lambda i, j: (layer, 0, jnp.where(i == 0, j, nj - 1)))] + x_specs + [
            pl.BlockSpec((1, d), lambda i, j: (0, 0)),
            pl.BlockSpec((d, LANES), lambda i, j: (0, 0)),
            pl.BlockSpec((1, LANES), lambda i, j: (0, 0))],
        out_specs=[pl.BlockSpec((tm, tn), lambda i, j: (i, j)),
                   pl.BlockSpec((tm * ROW_CHUNKS, LANES), lambda i, j: (i, 0)),
                   pl.BlockSpec((tm, LANES), lambda i, j: (i, 0))],
        out_shape=[jax.ShapeDtypeStruct((t, d), F32), jax.ShapeDtypeStruct((t * ROW_CHUNKS, LANES), F32),
                   jax.ShapeDtypeStruct((t, LANES), F32)],
        scratch_shapes=[pltpu.VMEM((tm, d), F32), pltpu.VMEM((k, d), BF16)],
        compiler_params=_cparams(("arbitrary", "arbitrary"), VMEM_LIMIT),
        name="outproj_norm_router",
    )(*a_parts, w_out, *x_parts, g_ffn.reshape(1, d), w_r, b_r)


def _routing_body(lg_ref, out_ref, cnt_ref, carry, *, tm):
    i = pl.program_id(0)

    @pl.when(i == 0)
    def _():
        carry[...] = jnp.zeros(carry.shape, F32)

    x = lg_ref[...]
    lane = lax.broadcasted_iota(I32, (tm, LANES), 1)
    neg_inf = -jnp.inf

    def rmax(v):
        return jnp.max(v, axis=1, keepdims=True)

    def rmin(v):
        return jnp.min(v, axis=1, keepdims=True)

    def rsum(v):
        return jnp.sum(v, axis=1, keepdims=True)

    gm = lane < N_GROUPS
    gmax = rmax(jnp.where(gm, x, neg_inf))
    gsel = rmin(jnp.where(gm, jnp.where(x == gmax, lane, LANES), LANES))
    gsum = rsum(jnp.where(gm, jnp.exp(x - gmax), 0.0))
    g_w = 1.0 / gsum
    lo = N_GROUPS + gsel * EXPERTS_PER_GROUP
    em = jnp.logical_and(lane >= lo, lane < lo + EXPERTS_PER_GROUP)
    emax = rmax(jnp.where(em, x, neg_inf))
    ee = jnp.where(em, jnp.exp(x - emax), 0.0)
    p = ee / rsum(ee)
    p1 = rmax(jnp.where(em, p, -1.0))
    i1 = rmin(jnp.where(em, jnp.where(p == p1, lane, LANES), LANES))
    em2 = jnp.logical_and(em, lane != i1)
    p2 = rmax(jnp.where(em2, p, -1.0))
    i2 = rmin(jnp.where(em2, jnp.where(p == p2, lane, LANES), LANES))
    den = p1 + p2
    g1 = g_w * (p1 / den)
    g2 = g_w * (p2 / den)

    oh1 = jnp.where(lane == i1, 1.0, 0.0)
    oh2 = jnp.where(lane == i2, 1.0, 0.0)
    oh = oh1 + oh2
    r = lax.broadcasted_iota(I32, (tm, tm), 0)
    c = lax.broadcasted_iota(I32, (tm, tm), 1)
    lower = jnp.where(c < r, 1.0, 0.0).astype(BF16)
    before = _dot(lower, oh.astype(BF16)) + carry[...]
    rank1 = rsum(oh1 * before)
    rank2 = rsum(oh2 * before)
    carry[...] = carry[...] + jnp.sum(oh, axis=0, keepdims=True)

    e1 = (i1 - N_GROUPS).astype(F32)
    e2 = (i2 - N_GROUPS).astype(F32)
    vals = [e1, e2, rank1, rank2, g1, g2]
    out = jnp.zeros((tm, LANES), F32)
    for k, v in enumerate(vals):
        out = jnp.where(lane == k, v, out)
    out_ref[...] = out
    cnt_ref[...] = carry[...]


def moe_routing(logits, tm):
    t = logits.shape[0]
    return pl.pallas_call(
        functools.partial(_routing_body, tm=tm),
        grid=(t // tm,),
        in_specs=[pl.BlockSpec((tm, LANES), lambda i: (i, 0))],
        out_specs=[pl.BlockSpec((tm, LANES), lambda i: (i, 0)), pl.BlockSpec((1, LANES), lambda i: (0, 0))],
        out_shape=[jax.ShapeDtypeStruct((t, LANES), F32), jax.ShapeDtypeStruct((1, LANES), F32)],
        scratch_shapes=[pltpu.VMEM((1, LANES), F32)],
        compiler_params=_cparams(("arbitrary",)),
        name="moe_routing",
    )(logits)


def _slot_tokens_body(slots_ref, init_hbm, tok_ref, sem):
    init = pltpu.make_async_copy(init_hbm, tok_ref, sem)
    init.start()
    init.wait()

    def put(a, c):
        tok_ref[slots_ref[a]] = lax.shift_right_logical(a, 1)
        return c

    lax.fori_loop(0, slots_ref.shape[0], put, 0, unroll=8)


def moe_slot_tokens(slots, n_slots):
    init = jnp.arange(n_slots, dtype=I32) % (slots.shape[0] // 2)
    return pl.pallas_call(
        _slot_tokens_body,
        in_specs=[pl.BlockSpec(memory_space=pltpu.SMEM), pl.BlockSpec(memory_space=pl.ANY)],
        out_specs=pl.BlockSpec(memory_space=pltpu.SMEM),
        out_shape=jax.ShapeDtypeStruct((n_slots,), I32),
        scratch_shapes=[pltpu.SemaphoreType.DMA],
        name="moe_slot_tokens",
    )(slots, init)


def _dispatch_body(nslot_ref, tok_ref, x_hbm, xs_ref, buf, sem, *, rows):
    i = pl.program_id(0)
    ni = pl.num_programs(0)
    n_used = nslot_ref[0]

    def issue(blk, s):
        @pl.when(blk * rows < n_used)
        def _():
            def pair(rp, carry):
                for k in range(2):
                    r = 2 * rp + k
                    src = pl.multiple_of(tok_ref[blk * rows + r] * ROW_CHUNKS, ROW_CHUNKS)
                    dst = pl.multiple_of(r * ROW_CHUNKS, ROW_CHUNKS)
                    pltpu.make_async_copy(x_hbm.at[pl.ds(src, ROW_CHUNKS), :],
                                          buf.at[s, pl.ds(dst, ROW_CHUNKS), :], sem.at[s]).start(priority=k)
                return carry

            lax.fori_loop(0, rows // 2, pair, 0, unroll=4)

    @pl.when(i == 0)
    def _():
        issue(0, 0)

    @pl.when(i + 1 < ni)
    def _():
        issue(i + 1, (i + 1) % 2)

    slot = i % 2

    @pl.when(i * rows < n_used)
    def _():
        pltpu.make_async_copy(x_hbm.at[pl.ds(0, rows * ROW_CHUNKS), :], buf.at[slot], sem.at[slot]).wait()
        for j in range(ROW_CHUNKS):
            xs_ref[:, j * LANES:(j + 1) * LANES] = \
                buf[slot, pl.ds(j, rows, stride=ROW_CHUNKS), :].astype(xs_ref.dtype)

    @pl.when(i * rows >= n_used)
    def _():
        xs_ref[...] = jnp.zeros(xs_ref.shape, xs_ref.dtype)


def moe_dispatch(xn, slot_tok, n_used, rows):
    assert xn.shape[1] == LANES
    d = ROW_CHUNKS * LANES
    n_slots = slot_tok.shape[0]
    assert n_slots % rows == 0
    grid_spec = pltpu.PrefetchScalarGridSpec(
        num_scalar_prefetch=2,
        grid=(n_slots // rows,),
        in_specs=[pl.BlockSpec(memory_space=pl.ANY)],
        out_specs=pl.BlockSpec((rows, d), lambda i, ns, tk: (i, 0)),
        scratch_shapes=[pltpu.VMEM((2, rows * ROW_CHUNKS, LANES), F32), pltpu.SemaphoreType.DMA((2,))],
    )
    return pl.pallas_call(
        functools.partial(_dispatch_body, rows=rows),
        grid_spec=grid_spec,
        out_shape=jax.ShapeDtypeStruct((n_slots, d), BF16),
        compiler_params=_cparams(("arbitrary",), VMEM_LIMIT),
        name="moe_dispatch",
    )(n_used, slot_tok, xn)


def _experts_body(blk0_ref, nblk_ref, ntot_ref, xs_hbm, wg_ref, wu_ref, wd_ref, ys_hbm,
                  xbuf, ybuf, zbuf, xsem, ysem, zsem, wgb, wub, wdb, *, bm):
    e = pl.program_id(0)
    n_total = ntot_ref[0]

    def x_copy(g, s):
        return pltpu.make_async_copy(xs_hbm.at[pl.ds(g * bm, bm), :], xbuf.at[s], xsem.at[s])

    def y_copy(g, s):
        return pltpu.make_async_copy(ybuf.at[s], ys_hbm.at[pl.ds(g * bm, bm), :], ysem.at[s])

    for g0 in range(X_AHEAD):
        @pl.when(jnp.logical_and(e == 0, n_total > g0))
        def _():
            x_copy(g0, g0).start(priority=1)

    @pl.when(nblk_ref[e] > 0)
    def _():
        wgb[...] = wg_ref[0, 0].astype(BF16)
        wub[...] = wu_ref[0, 0].astype(BF16)
        wdb[...] = wd_ref[0, 0].astype(BF16)

    def block(i, carry):
        g = blk0_ref[e] + i
        s = g % 2
        sx = g % (X_AHEAD + 1)

        @pl.when(g + X_AHEAD < n_total)
        def _():
            x_copy(g + X_AHEAD, (g + X_AHEAD) % (X_AHEAD + 1)).start(priority=1)

        x_copy(g, sx).wait()
        x = xbuf[sx]
        gt = _dot(x, wgb[...])
        up = _dot(x, wub[...])
        a = (gt * (1.0 / (1.0 + jnp.exp(-gt)))) * up
        y = _dot(a.astype(BF16), wdb[...])

        @pl.when(g >= 2)
        def _():
            y_copy(g - 2, s).wait()

        ybuf[s] = y
        y_copy(g, s).start()
        return carry

    lax.fori_loop(0, nblk_ref[e], block, 0)

    @pl.when(e == pl.num_programs(0) - 1)
    def _():
        for back in (2, 1):
            @pl.when(n_total >= back)
            def _():
                y_copy(n_total - back, (n_total - back) % 2).wait()

        def z_copy(g):
            return pltpu.make_async_copy(zbuf, ys_hbm.at[pl.ds(g * bm, bm), :], zsem)

        zbuf[...] = jnp.zeros(zbuf.shape, F32)
        n_blocks = ys_hbm.shape[0] // bm
        lax.fori_loop(n_total, n_blocks, lambda g, c: (z_copy(g).start(), c)[1], 0)
        lax.fori_loop(n_total, n_blocks, lambda g, c: (z_copy(g).wait(), c)[1], 0)


def moe_experts(xs, blk0, nblk, n_total, w_gate, w_up, w_down, layer, bm):
    n_slots, d = xs.shape
    n_exp = blk0.shape[0]
    de = w_gate.shape[-1]
    grid_spec = pltpu.PrefetchScalarGridSpec(
        num_scalar_prefetch=3,
        grid=(n_exp,),
        in_specs=[pl.BlockSpec(memory_space=pl.ANY),
                  pl.BlockSpec((1, 1, d, de), lambda e, b0, nb, nt: (layer, e, 0, 0)),
                  pl.BlockSpec((1, 1, d, de), lambda e, b0, nb, nt: (layer, e, 0, 0)),
                  pl.BlockSpec((1, 1, de, d), lambda e, b0, nb, nt: (layer, e, 0, 0))],
        out_specs=pl.BlockSpec(memory_space=pl.ANY),
        scratch_shapes=[pltpu.VMEM((X_AHEAD + 1, bm, d), BF16), pltpu.VMEM((2, bm, d), F32),
                        pltpu.VMEM((bm, d), F32),
                        pltpu.SemaphoreType.DMA((X_AHEAD + 1,)), pltpu.SemaphoreType.DMA((2,)),
                        pltpu.SemaphoreType.DMA,
                        pltpu.VMEM((d, de), BF16), pltpu.VMEM((d, de), BF16), pltpu.VMEM((de, d), BF16)],
    )
    return pl.pallas_call(
        functools.partial(_experts_body, bm=bm),
        grid_spec=grid_spec,
        out_shape=jax.ShapeDtypeStruct((n_slots, d), F32),
        compiler_params=_cparams(("arbitrary",), VMEM_LIMIT),
        name="moe_experts",
    )(blk0, nblk, n_total, xs, w_gate, w_up, w_down)


def _combine_body(slot_ref, ys_hbm, h_ref, rt_ref, g_ref, *rest, tm, tile0, want_h):
    outs, (buf, sem) = rest[:-2], rest[-2:]
    i = pl.program_id(0)
    ni = pl.num_programs(0)

    def row_copy(src_row, k, r, s):
        return pltpu.make_async_copy(ys_hbm.at[pl.ds(src_row, 1), :], buf.at[s, k, pl.ds(r, 1), :], sem.at[s])

    def issue(tile, s):
        def one(r, carry):
            base = ((tile0 + tile) * tm + r) * 2
            row_copy(slot_ref[base], 0, r, s).start(priority=0)
            row_copy(slot_ref[base + 1], 1, r, s).start(priority=1)
            return carry

        lax.fori_loop(0, tm, one, 0, unroll=8)

    @pl.when(i == 0)
    def _():
        issue(0, 0)

    @pl.when(i + 1 < ni)
    def _():
        issue(i + 1, (i + 1) % 2)

    slot = i % 2
    for k in range(2):
        pltpu.make_async_copy(ys_hbm.at[pl.ds(0, tm), :], buf.at[slot, k], sem.at[slot]).wait()

    rt = rt_ref[...]
    g1 = rt[:, 4:5]
    g2 = rt[:, 5:6]
    h2 = h_ref[...] + (buf[slot, 0] * g1 + buf[slot, 1] * g2)
    if want_h:
        outs[0][...] = h2
    outs[-1][...] = _rms(h2, g_ref[...]).astype(outs[-1].dtype)


def moe_combine(ys, slots, h1, route, g_next, tm, u_dtype, row0=0, nrows=None, want_h=True):
    d = h1.shape[1]
    t = h1.shape[0] if nrows is None else nrows
    assert row0 % tm == 0 and t % tm == 0
    tile0 = row0 // tm
    row_spec = pl.BlockSpec((tm, d), lambda i, sl: (i, 0))
    grid_spec = pltpu.PrefetchScalarGridSpec(
        num_scalar_prefetch=1,
        grid=(t // tm,),
        in_specs=[pl.BlockSpec(memory_space=pl.ANY),
                  pl.BlockSpec((tm, d), lambda i, sl: (i + tile0, 0)),
                  pl.BlockSpec((tm, LANES), lambda i, sl: (i + tile0, 0)),
                  pl.BlockSpec((1, d), lambda i, sl: (0, 0))],
        out_specs=[row_spec, row_spec] if want_h else [row_spec],
        scratch_shapes=[pltpu.VMEM((2, 2, tm, d), F32), pltpu.SemaphoreType.DMA((2,))],
    )
    shapes = [jax.ShapeDtypeStruct((t, d), F32)] if want_h else []
    return pl.pallas_call(
        functools.partial(_combine_body, tm=tm, tile0=tile0, want_h=want_h),
        grid_spec=grid_spec,
        out_shape=shapes + [jax.ShapeDtypeStruct((t, d), u_dtype)],
        compiler_params=_cparams(("arbitrary",), VMEM_LIMIT),
        name="moe_combine",
    )(slots, ys, h1, route, g_next.reshape(1, d))


def hier_moe_layer(h1, xn, logits, w_gate, w_up, w_down, layer, g_next, u_dtype, split=None):
    t = h1.shape[0]
    bm = MOE_BM
    route, counts = moe_routing(logits, _tile(t, 256))
    e = route[:, 0:2].astype(I32)
    rank = route[:, 2:4].astype(I32)
    counts = counts[0, N_GROUPS:N_GROUPS + N_EXPERTS].astype(I32)
    padded = (counts + bm - 1) // bm * bm
    pad_end = jnp.cumsum(padded)
    pad_start = pad_end - padded
    start_of = jnp.sum(jnp.where(e[..., None] == jnp.arange(N_EXPERTS, dtype=I32), pad_start, 0), axis=-1)
    slots = (start_of + rank).reshape(-1)
    nb = (2 * t) // bm + N_EXPERTS
    slot_tok = moe_slot_tokens(slots, nb * bm)
    n_used = pad_end[-1:]
    xs = moe_dispatch(xn, slot_tok, n_used, bm * max(f for f in (4, 2, 1) if nb % f == 0))
    ys = moe_experts(xs, pad_start // bm, padded // bm, n_used // bm, w_gate, w_up, w_down, layer, bm)
    tmc = _tile(t, 256)
    if split is None:
        return moe_combine(ys, slots, h1, route, g_next, tmc, u_dtype)
    (u_a,) = moe_combine(ys, slots, h1, route, g_next, tmc, u_dtype, 0, split, want_h=False)
    (u_b,) = moe_combine(ys, slots, h1, route, g_next, tmc, u_dtype, split, t - split, want_h=False)
    return u_a, u_b


def _swap_halves(x):
    lane = lax.broadcasted_iota(I32, x.shape, 1)
    return jnp.where(lane < QK_ROPE // 2, pltpu.roll(x, LANES - QK_ROPE // 2, 1), pltpu.roll(x, QK_ROPE // 2, 1))


def _mla_prep_body(p_ref, gq_ref, gkv_ref, cos_ref, sin_ref, cq_ref, ckv_ref, ckvb_ref, kr_ref):
    p = p_ref[...]
    cq_ref[...] = _rms(p[:, :Q_LORA], gq_ref[...]).astype(cq_ref.dtype)
    ckv = _rms(p[:, Q_LORA:Q_LORA + KV_LORA], gkv_ref[...])
    ckv_ref[...] = ckv
    ckvb_ref[...] = ckv.astype(ckvb_ref.dtype)
    kr = p[:, Q_LORA + KV_LORA:]
    kr_ref[...] = kr * cos_ref[...] + _swap_halves(kr) * sin_ref[...]


def mla_prep(proj, g_q, g_kv, cos, sin, tm):
    t = proj.shape[0]
    return pl.pallas_call(
        _mla_prep_body,
        grid=(t // tm,),
        in_specs=[pl.BlockSpec((tm, proj.shape[1]), lambda i: (i, 0)),
                  pl.BlockSpec((1, Q_LORA), lambda i: (0, 0)),
                  pl.BlockSpec((1, KV_LORA), lambda i: (0, 0)),
                  pl.BlockSpec((tm, LANES), lambda i: (i, 0)),
                  pl.BlockSpec((tm, LANES), lambda i: (i, 0))],
        out_specs=[pl.BlockSpec((tm, Q_LORA), lambda i: (i, 0)),
                   pl.BlockSpec((tm, KV_LORA), lambda i: (i, 0)),
                   pl.BlockSpec((tm, KV_LORA), lambda i: (i, 0)),
                   pl.BlockSpec((tm, LANES), lambda i: (i, 0))],
        out_shape=[jax.ShapeDtypeStruct((t, Q_LORA), BF16), jax.ShapeDtypeStruct((t, KV_LORA), F32),
                   jax.ShapeDtypeStruct((t, KV_LORA), BF16), jax.ShapeDtypeStruct((t, LANES), F32)],
        compiler_params=_cparams(("parallel",)),
        name="mla_prep",
    )(proj, g_q.reshape(1, -1), g_kv.reshape(1, -1), cos, sin)


def _mla_q_body(cq_ref, w_ref, cos_ref, sin_ref, q_ref, wb):
    @pl.when(pl.program_id(0) == 0)
    def _():
        wb[...] = w_ref[...].astype(BF16)

    cq = cq_ref[...]
    cos = cos_ref[...]
    sin = sin_ref[...]
    for h in range(B_HEADS):
        y = _dot(cq, wb[:, h * MLA_QK_PAD:(h + 1) * MLA_QK_PAD])
        xr = y[:, LANES:]
        q_ref[:, h * MLA_QK_PAD:h * MLA_QK_PAD + LANES] = y[:, :LANES].astype(q_ref.dtype)
        q_ref[:, h * MLA_QK_PAD + LANES:(h + 1) * MLA_QK_PAD] = \
            (xr * cos + _swap_halves(xr) * sin).astype(q_ref.dtype)


def mla_q_up(cq, w_uq_pad, cos, sin, tm):
    t = cq.shape[0]
    n = B_HEADS * MLA_QK_PAD
    return pl.pallas_call(
        _mla_q_body,
        grid=(t // tm,),
        in_specs=[pl.BlockSpec((tm, Q_LORA), lambda i: (i, 0)),
                  pl.BlockSpec((Q_LORA, n), lambda i: (0, 0)),
                  pl.BlockSpec((tm, LANES), lambda i: (i, 0)),
                  pl.BlockSpec((tm, LANES), lambda i: (i, 0))],
        out_specs=pl.BlockSpec((tm, n), lambda i: (i, 0)),
        out_shape=jax.ShapeDtypeStruct((t, n), BF16),
        scratch_shapes=[pltpu.VMEM((Q_LORA, n), BF16)],
        compiler_params=_cparams(("arbitrary",), VMEM_LIMIT),
        name="mla_q_up",
    )(cq, w_uq_pad, cos, sin)


def _mla_kv_body(c_ref, kr_ref, wk_ref, wv_ref, k_ref, v_ref, wkb, wvb):
    @pl.when(pl.program_id(0) == 0)
    def _():
        wkb[...] = wk_ref[0].astype(BF16)
        wvb[...] = wv_ref[0].astype(BF16)

    c = c_ref[...].astype(BF16)
    kr = kr_ref[...].astype(k_ref.dtype)
    kn = _dot(c, wkb[...])
    for h in range(B_HEADS):
        k_ref[:, h * MLA_QK_PAD:h * MLA_QK_PAD + LANES] = kn[:, h * QK_NOPE:(h + 1) * QK_NOPE].astype(k_ref.dtype)
        k_ref[:, h * MLA_QK_PAD + LANES:(h + 1) * MLA_QK_PAD] = kr
    v_ref[...] = _dot(c, wvb[...]).astype(v_ref.dtype)


def mla_kv_up(ckv, kr, w_uk, w_uv, layer, tm):
    t = ckv.shape[0]
    nk = B_HEADS * MLA_QK_PAD
    nv = B_HEADS * V_DIM
    return pl.pallas_call(
        _mla_kv_body,
        grid=(t // tm,),
        in_specs=[pl.BlockSpec((tm, KV_LORA), lambda i: (i, 0)),
                  pl.BlockSpec((tm, LANES), lambda i: (i, 0)),
                  pl.BlockSpec((1, KV_LORA, B_HEADS * QK_NOPE), lambda i: (layer, 0, 0)),
                  pl.BlockSpec((1, KV_LORA, nv), lambda i: (layer, 0, 0))],
        out_specs=[pl.BlockSpec((tm, nk), lambda i: (i, 0)), pl.BlockSpec((tm, nv), lambda i: (i, 0))],
        out_shape=[jax.ShapeDtypeStruct((t, nk), BF16), jax.ShapeDtypeStruct((t, nv), BF16)],
        scratch_shapes=[pltpu.VMEM((KV_LORA, B_HEADS * QK_NOPE), BF16), pltpu.VMEM((KV_LORA, nv), BF16)],
        compiler_params=_cparams(("arbitrary",), VMEM_LIMIT),
        name="mla_kv_up",
    )(ckv, kr, w_uk, w_uv)


def kernel(x_prompt, x_sample, cache_a_k, cache_a_v, cache_a_kidx, cache_b_ckv, cache_b_krope, norm_mix, norm_ffn, norm_final, rel_bias, a_w_in, a_w_out, b_w_in, b_norm_q, b_norm_kv, b_w_uq, b_w_uk, b_w_uv, b_w_out, moe_w_grp, moe_b_grp, moe_w_rtr, moe_b_rtr, moe_w_gate, moe_w_up, moe_w_down):
    n_p, s_p, d = x_prompt.shape
    n_s, s_s, _ = x_sample.shape
    past = cache_a_k.shape[2]
    tp = n_p * s_p
    ts = n_s * s_s
    t = tp + ts
    a_qd = A_HEADS * A_HEAD_DIM
    tm = _tile(math.gcd(tp, ts), 1024)
    tm2 = _tile(math.gcd(tp, ts), 512)

    x_parts = [x_prompt.reshape(tp, d), x_sample.reshape(ts, d)]

    def router_params(i):
        w_r = jnp.concatenate([moe_w_grp[i], moe_w_rtr[i],
                               jnp.zeros((d, LANES - N_GROUPS - N_EXPERTS), F32)], axis=1)
        b_r = jnp.concatenate([moe_b_grp[i], moe_b_rtr[i],
                               jnp.zeros((LANES - N_GROUPS - N_EXPERTS,), F32)]).reshape(1, LANES)
        return w_r, b_r

    u0 = rmsnorm(x_parts, norm_mix[0], tm)
    w_a = a_w_in[0]
    (q_b,) = matmul(u0, w_a, 0, a_qd, [BF16], tm, 512, "a_proj_q")
    tmp, tms = _tile(tp, 1024), _tile(ts, 1024)
    kp_f, kp_b = matmul(u0, w_a, a_qd, a_qd, [F32, BF16], tmp, 512, "a_proj_k", 0, tp)
    vp_f, vp_b = matmul(u0, w_a, 2 * a_qd, a_qd, [F32, BF16], tmp, 512, "a_proj_v", 0, tp)
    ks_f, ks_b = matmul(u0, w_a, a_qd, a_qd, [F32, BF16], tms, 512, "a_proj_k", tp, ts)
    vs_f, vs_b = matmul(u0, w_a, 2 * a_qd, a_qd, [F32, BF16], tms, 512, "a_proj_v", tp, ts)
    (qi_b,) = matmul(u0, w_a, 3 * a_qd, IDX_HEADS * IDX_DIM, [BF16], tm, 512, "a_proj_qi")
    c_ki = 3 * a_qd + IDX_HEADS * IDX_DIM
    w_ki = w_a[:, c_ki:c_ki + IDX_DIM]
    w_wi = w_a[:, c_ki + IDX_DIM:c_ki + IDX_DIM + IDX_HEADS]
    zk = jnp.zeros((d, IDX_DIM), F32)
    w_tail = jnp.concatenate([w_ki, zk, zk, w_ki, w_wi, jnp.zeros((d, LANES - IDX_HEADS), F32)], axis=1)
    (tail,) = matmul(u0, w_tail, 0, 3 * LANES, [F32], tm, 3 * LANES, "a_proj_tail")
    kidx = tail[:, :IDX_DIM]

    bias_tab = rel_bias_tables(rel_bias)
    a_scale = A_HEAD_DIM ** -0.5
    mask_p = dsa_select(qi_b, tail, 2, tail, n_seq=n_p, sq=s_p, bq=128, sk=s_p, sk_real=s_p, tk=512,
                        pos0=0, q_row0=0)
    kp_t = kp_b.reshape(n_p, s_p, a_qd).transpose(0, 2, 1).reshape(n_p * a_qd, s_p)
    att_p = attention(q_b, [(kp_t, vp_b, s_p, 0, 0)], bias_tab, mask_p, n_seq=n_p, sq=s_p, bq=ATT_BQ, q_row0=0,
                      q_col0=0, n_heads=A_HEADS, hg=ATT_HG, dq=A_HEAD_DIM, dv=A_HEAD_DIM, scale=a_scale, pos0=0,
                      dyn_sk=s_p, name="dsa_attention_prompt")
    sk_s = past + s_s
    sk_pad = (sk_s + ATT_TILE - 1) // ATT_TILE * ATT_TILE
    ki_past = cache_a_kidx[0]
    zp = jnp.zeros_like(ki_past)
    kc_past = jnp.concatenate([ki_past, zp, zp, ki_past], axis=-1)
    kc_new = tail[tp:, :2 * LANES].reshape(n_s, s_s, 2 * LANES)
    kc_s = jnp.concatenate([kc_past, kc_new, jnp.zeros((n_s, sk_pad - sk_s, 2 * LANES), F32)], axis=1)
    mask_s = dsa_select(qi_b, tail, 2, kc_s.reshape(n_s * sk_pad, 2 * LANES), n_seq=n_s, sq=s_s, bq=s_s,
                        sk=sk_pad, sk_real=sk_s, tk=ATT_TILE, pos0=past, q_row0=tp)
    tiles_s = [(0, r, r, ATT_TILE) for r in range(0, past, ATT_TILE)] + [(1, 0, past, s_s)]
    att_s = attention(q_b, [(cache_a_k[0].reshape(n_s * past * A_HEADS, A_HEAD_DIM),
                             cache_a_v[0].reshape(n_s * past * A_HEADS, A_HEAD_DIM), past, 0, 0),
                            (ks_b, vs_b, s_s, 0, 0)],
                      bias_tab, mask_s, n_seq=n_s, sq=s_s, bq=s_s, q_row0=tp, q_col0=0, n_heads=A_HEADS, hg=A_HEADS,
                      dq=A_HEAD_DIM, dv=A_HEAD_DIM, scale=a_scale, pos0=past, seg_tiles=tiles_s,
                      name="dsa_attention_sample")

    w_r0, b_r0 = router_params(0)
    h1, xn0, lg0 = outproj_norm_router([att_p, att_s], a_w_out, 0, x_parts, norm_ffn[0], w_r0, b_r0, tm2, 512)
    h2, u1 = hier_moe_layer(h1, xn0, lg0, moe_w_gate, moe_w_up, moe_w_down, 0, norm_mix[1], BF16)

    n_in = Q_LORA + KV_LORA + QK_ROPE
    w_b = jnp.concatenate([b_w_in[0], jnp.zeros((d, LANES - QK_ROPE), F32)], axis=1)
    (proj,) = matmul(u1, w_b, 0, n_in + LANES - QK_ROPE, [F32], tm, 384, "b_proj")
    half = QK_ROPE // 2
    inv = ROPE_THETA ** (-jnp.arange(half, dtype=F32) / half)
    pos_all = jnp.concatenate([jnp.tile(jnp.arange(s_p, dtype=I32), n_p),
                               jnp.tile(past + jnp.arange(s_s, dtype=I32), n_s)])
    ang = pos_all.astype(F32)[:, None] * inv[None, :]
    zl = jnp.zeros((t, LANES - QK_ROPE), F32)
    cos_t = jnp.concatenate([jnp.cos(ang), jnp.cos(ang), zl], axis=1)
    sin_t = jnp.concatenate([-jnp.sin(ang), jnp.sin(ang), zl], axis=1)
    cq_b, ckv_f, ckv_b, kr_f = mla_prep(proj, b_norm_q[0], b_norm_kv[0], cos_t, sin_t, tm)
    w_uq_pad = jnp.pad(b_w_uq[0].reshape(Q_LORA, B_HEADS, QK_NOPE + QK_ROPE),
                       ((0, 0), (0, 0), (0, MLA_QK_PAD - QK_NOPE - QK_ROPE))).reshape(Q_LORA, B_HEADS * MLA_QK_PAD)
    qm = mla_q_up(cq_b, w_uq_pad, cos_t, sin_t, tm2)
    km_p, vm_p = mla_kv_up(ckv_b[:tp], kr_f[:tp], b_w_uk, b_w_uv, 0, _tile(tp, 512))
    ctab = causal_tables()
    hgb = 8
    km_t = km_p.reshape(n_p, s_p, B_HEADS * MLA_QK_PAD).transpose(0, 2, 1).reshape(n_p * B_HEADS * MLA_QK_PAD, s_p)
    matt_p = attention(qm, [(km_t, vm_p, s_p, 0, 0)], ctab, None, n_seq=n_p, sq=s_p, bq=ATT_BQ, q_row0=0, q_col0=0,
                       n_heads=B_HEADS, hg=ATT_HG, dq=MLA_QK_PAD, dv=V_DIM, scale=B_SCALE, pos0=0, dyn_sk=s_p,
                       name="mla_attention_prompt")
    ckv_all = jnp.concatenate([cache_b_ckv[0], ckv_f[tp:].reshape(n_s, s_s, KV_LORA)], axis=1)
    kr_past = jnp.concatenate([cache_b_krope[0], jnp.zeros((n_s, past, LANES - QK_ROPE), F32)], axis=-1)
    kr_all = jnp.concatenate([kr_past, kr_f[tp:].reshape(n_s, s_s, LANES)], axis=1)
    km_s, vm_s = mla_kv_up(ckv_all.reshape(n_s * sk_s, KV_LORA), kr_all.reshape(n_s * sk_s, LANES),
                           b_w_uk, b_w_uv, 0, sk_s)
    tiles_m = [(0, r, r, ATT_TILE) for r in range(0, past, ATT_TILE)] + [(0, past, past, s_s)]
    matt_s = attention(qm, [(km_s, vm_s, sk_s, 0, 0)], ctab, None, n_seq=n_s, sq=s_s, bq=s_s, q_row0=tp, q_col0=0,
                       n_heads=B_HEADS, hg=hgb, dq=MLA_QK_PAD, dv=V_DIM, scale=B_SCALE, pos0=past,
                       seg_tiles=tiles_m, name="mla_attention_sample")

    w_r1, b_r1 = router_params(1)
    h3, xn1, lg1 = outproj_norm_router([matt_p, matt_s], b_w_out, 0, [h2], norm_ffn[1], w_r1, b_r1, tm2, 512)
    y_p, y_s = hier_moe_layer(h3, xn1, lg1, moe_w_gate, moe_w_up, moe_w_down, 1, norm_final, F32, split=tp)
    y_prompt = y_p.reshape(n_p, s_p, d)
    y_sample = y_s.reshape(n_s, s_s, d)

    def heads(a, n, s):
        return a.reshape(1, n, s, A_HEADS, A_HEAD_DIM)

    kr_out = kr_f[:, :QK_ROPE]
    return (y_prompt, y_sample,
            heads(kp_f, n_p, s_p), heads(vp_f, n_p, s_p), kidx[:tp].reshape(1, n_p, s_p, IDX_DIM),
            ckv_f[:tp].reshape(1, n_p, s_p, KV_LORA), kr_out[:tp].reshape(1, n_p, s_p, QK_ROPE),
            heads(ks_f, n_s, s_s), heads(vs_f, n_s, s_s), kidx[tp:].reshape(1, n_s, s_s, IDX_DIM),
            ckv_f[tp:].reshape(1, n_s, s_s, KV_LORA), kr_out[tp:].reshape(1, n_s, s_s, QK_ROPE))
```

```python
import functools
import math

import jax
import jax.numpy as jnp
from jax import lax
from jax.experimental import pallas as pl
from jax.experimental.pallas import tpu as pltpu

F32 = jnp.float32
BF16 = jnp.bfloat16
I32 = jnp.int32

RMS_EPS = 1e-6
CHUNK = 64
NEG = -1e30
INT_MIN = -2 ** 31
INT_MAX = 2 ** 31 - 1
LOG2E = math.log2(math.e)

LANES = 128
KEY_BLOCK = 128
ATT_TILE = 256
ATT_BQ = 128
ATT_HG = 8
VMEM_LIMIT = 56 * 1024 * 1024

A_HEADS = 16
A_HEAD_DIM = 128
IDX_HEADS = 16
IDX_DIM = 64
TOPK_MAX = 256
IDX_W_SCALE = float((IDX_HEADS * IDX_DIM) ** -0.5)
NUM_BUCKETS = 32
MAX_DISTANCE = 128
B_HEADS = 16
Q_LORA = 512
KV_LORA = 512
QK_NOPE = 128
QK_ROPE = 64
V_DIM = 128
ROPE_THETA = 10000.0
B_SCALE = float((QK_NOPE + QK_ROPE) ** -0.5)
MLA_QK_PAD = 256
N_GROUPS = 8
EXPERTS_PER_GROUP = 8
N_EXPERTS = 64
D_EXPERT = 512
MOE_BM = 128
ROW_CHUNKS = 16
X_AHEAD = 2
W_AHEAD = 2


def _tile(n, pref):
    for c in range(pref, 0, -LANES):
        if n % c == 0:
            return c
    raise ValueError(f"no 128-multiple tile divides {n}")


def _cparams(sem, vmem=None):
    return pltpu.CompilerParams(dimension_semantics=sem, vmem_limit_bytes=vmem)


def _dot(a, b):
    return jnp.dot(a, b, preferred_element_type=F32)


def _dot_nt(a, b):
    return lax.dot_general(a, b, (((1,), (1,)), ((), ())), preferred_element_type=F32)


def _rms(x, g):
    ms = jnp.mean(x * x, axis=-1, keepdims=True)
    return x * lax.rsqrt(ms + RMS_EPS) * g


def _pick_rows(i, n_first, refs):
    if len(refs) == 1:
        return refs[0][...]
    return jnp.where(i < n_first, refs[0][...], refs[1][...])


def _row_split_specs(parts, tm, width, col_of):
    n_first = parts[0].shape[0] // tm
    assert all(p.shape[0] % tm == 0 for p in parts)
    specs = [pl.BlockSpec((tm, width), lambda i, *r: (jnp.minimum(i, n_first - 1), col_of(i, *r)))]
    if len(parts) == 2:
        specs.append(pl.BlockSpec((tm, width), lambda i, *r: (jnp.maximum(i - n_first, 0), col_of(i, *r))))
    return specs, n_first


def _rmsnorm_body(*refs, n_first):
    x = _pick_rows(pl.program_id(0), n_first, refs[:-2])
    g_ref, o_ref = refs[-2:]
    o_ref[...] = _rms(x, g_ref[...]).astype(o_ref.dtype)


def rmsnorm(xs, g, tm, out_dtype=BF16):
    d = xs[0].shape[1]
    t = sum(x.shape[0] for x in xs)
    specs, n_first = _row_split_specs(xs, tm, d, lambda i: 0)
    return pl.pallas_call(
        functools.partial(_rmsnorm_body, n_first=n_first),
        grid=(t // tm,),
        in_specs=specs + [pl.BlockSpec((1, d), lambda i: (0, 0))],
        out_specs=pl.BlockSpec((tm, d), lambda i: (i, 0)),
        out_shape=jax.ShapeDtypeStruct((t, d), out_dtype),
        compiler_params=_cparams(("parallel",)),
        name="rmsnorm",
    )(*xs, g.reshape(1, d))


def _matmul_body(a_ref, w_ref, *rest):
    outs, wb = rest[:-1], rest[-1]

    @pl.when(pl.program_id(1) == 0)
    def _():
        wb[...] = w_ref[...].astype(BF16)

    r = _dot(a_ref[...], wb[...])
    for o in outs:
        o[...] = r.astype(o.dtype)


def matmul(a, w, col0, ncols, out_dtypes, tm, tn, name, row0=0, nrows=None):
    k = a.shape[1]
    t = a.shape[0] if nrows is None else nrows
    assert ncols % tn == 0 and col0 % tn == 0 and t % tm == 0 and row0 % tm == 0
    cb = col0 // tn
    rb = row0 // tm
    return pl.pallas_call(
        _matmul_body,
        grid=(ncols // tn, t // tm),
        in_specs=[pl.BlockSpec((tm, k), lambda j, i: (i + rb, 0)),
                  pl.BlockSpec((k, tn), lambda j, i: (0, j + cb))],
        out_specs=[pl.BlockSpec((tm, tn), lambda j, i: (i, j)) for _ in out_dtypes],
        out_shape=[jax.ShapeDtypeStruct((t, ncols), dt) for dt in out_dtypes],
        scratch_shapes=[pltpu.VMEM((k, tn), BF16)],
        compiler_params=_cparams(("arbitrary", "arbitrary"), VMEM_LIMIT),
        name=name,
    )(a, w)


def _float_sort_key(x):
    bits = lax.bitcast_convert_type(x, I32)
    return bits ^ ((bits >> 31) & INT_MAX)


def _select_body(qi_ref, wi_ref, kc_ref, mask_ref, key_scr, jm_scr, *, bq, sk, sk_real, tk, pos0, topk):
    b = pl.program_id(1)
    t0 = pos0 + b * bq
    kmax = jnp.minimum(sk_real, ((t0 + bq - 1) // CHUNK + 1) * CHUNK)
    nkt = (kmax + tk - 1) // tk
    nch = tk // LANES

    w = wi_ref[...] * IDX_W_SCALE
    wb = [jnp.broadcast_to(w[:, h:h + 1], (bq, LANES)) for h in range(IDX_HEADS)]
    qpos = t0 + lax.broadcasted_iota(I32, (LANES, LANES), 1)
    lim = jnp.minimum((qpos // CHUNK + 1) * CHUNK, sk_real)
    sub = lax.broadcasted_iota(I32, (LANES, LANES), 0)

    def to_lanes(x):
        if bq < LANES:
            x = jnp.concatenate([x, jnp.zeros((LANES - bq, LANES), x.dtype)], axis=0)
        return x.T

    def score_tile(j, carry):
        off = pl.multiple_of(j * tk, tk)
        kc = kc_ref[pl.ds(off, tk), :].astype(BF16)
        ka, kb = kc[:, :LANES], kc[:, LANES:]
        accs = [jnp.zeros((bq, LANES), F32) for _ in range(nch)]
        for g in range(IDX_HEADS // 2):
            qg = qi_ref[:, g * LANES:(g + 1) * LANES]
            sa = _dot_nt(qg, ka)
            sb = _dot_nt(qg, kb)
            for c in range(nch):
                sl = slice(c * LANES, (c + 1) * LANES)
                accs[c] = accs[c] + wb[2 * g] * jnp.maximum(sa[:, sl], 0.0) \
                    + wb[2 * g + 1] * jnp.maximum(sb[:, sl], 0.0)
        for c in range(nch):
            kpos = off + c * LANES + sub
            sc = jnp.where(kpos < lim, to_lanes(accs[c]), -jnp.inf)
            key_scr[pl.ds(off + c * LANES, LANES), :] = _float_sort_key(sc)
        return carry

    lax.fori_loop(0, nkt, score_tile, 0)

    def count(indicator):
        def tile(j, cnt):
            off = pl.multiple_of(j * tk, tk)
            for c in range(nch):
                kt = key_scr[pl.ds(off + c * LANES, LANES), :]
                cnt = cnt + indicator(kt, off + c * LANES + sub)
            return cnt
        cnt = lax.fori_loop(0, nkt, tile, jnp.zeros((LANES, LANES), F32))
        return jnp.sum(cnt, axis=0, keepdims=True)

    def bit_step(i, pfx_u):
        bit = lax.shift_left(jnp.int32(1), 31 - i)
        cand_u = pfx_u | bit
        cand_s = cand_u ^ INT_MIN
        total = count(lambda kt, kp: jnp.where(kt >= cand_s, 1.0, 0.0))
        return jnp.where(total >= topk, cand_u, pfx_u)

    pfx = lax.fori_loop(0, 32, bit_step, jnp.zeros((1, LANES), I32))
    thr = pfx ^ INT_MIN

    n_gt = count(lambda kt, kp: jnp.where(kt > thr, 1.0, 0.0))
    n_ge = count(lambda kt, kp: jnp.where(kt >= thr, 1.0, 0.0))
    quota = topk - n_gt
    jm_scr[...] = jnp.full((1, LANES), INT_MAX, I32)
    any_excess = jnp.max(jnp.where(n_ge > topk, 1.0, 0.0)) > 0.0

    @pl.when(any_excess)
    def _():
        nbits = max(1, int(sk - 1).bit_length())

        def idx_step(i, ans):
            cand = ans | lax.shift_left(jnp.int32(1), nbits - 1 - i)
            below = count(lambda kt, kp: jnp.where(kt == thr, jnp.where(kp < cand, 1.0, 0.0), 0.0))
            return jnp.where(below < quota, cand, ans)

        jm_scr[...] = lax.fori_loop(0, nbits, idx_step, jnp.zeros((1, LANES), I32))

    jm = jm_scr[...]

    def write_tile(j, carry):
        off = pl.multiple_of(j * tk, tk)
        for c in range(nch):
            kt = key_scr[pl.ds(off + c * LANES, LANES), :]
            kpos = off + c * LANES + sub
            v = jnp.where(kt > thr, 0.0, jnp.where(kt == thr, jnp.where(kpos <= jm, 0.0, NEG), NEG))
            v = jnp.where(kpos < lim, v, NEG)
            mask_ref[:, pl.ds(off + c * LANES, LANES)] = v.T[:bq].astype(mask_ref.dtype)
        return carry

    lax.fori_loop(0, nkt, write_tile, 0)

    def fill_tile(j, carry):
        off = pl.multiple_of(j * tk, tk)
        mask_ref[:, pl.ds(off, tk)] = jnp.full((bq, tk), NEG, mask_ref.dtype)
        return carry

    lax.fori_loop(nkt, sk // tk, fill_tile, 0)


def dsa_select(qi, wi_arr, wi_blk, kc, *, n_seq, sq, bq, sk, sk_real, tk, pos0, q_row0):
    nqb = sq // bq
    qb0 = q_row0 // bq
    assert q_row0 % bq == 0 and sk % tk == 0 and tk >= TOPK_MAX
    topk = min(TOPK_MAX, sk_real // 4)
    body = functools.partial(_select_body, bq=bq, sk=sk, sk_real=sk_real, tk=tk, pos0=pos0, topk=topk)
    return pl.pallas_call(
        body,
        grid=(n_seq, nqb),
        in_specs=[pl.BlockSpec((bq, IDX_HEADS * IDX_DIM), lambda s, b: (qb0 + s * nqb + b, 0)),
                  pl.BlockSpec((bq, LANES), lambda s, b: (qb0 + s * nqb + b, wi_blk)),
                  pl.BlockSpec((sk, 2 * LANES), lambda s, b: (s, 0))],
        out_specs=pl.BlockSpec((bq, sk), lambda s, b: (s * nqb + b, 0)),
        out_shape=jax.ShapeDtypeStruct((n_seq * sq, sk), BF16),
        scratch_shapes=[pltpu.VMEM((sk, LANES), I32), pltpu.VMEM((1, LANES), I32)],
        compiler_params=_cparams(("parallel", "arbitrary"), VMEM_LIMIT),
        name="dsa_select",
    )(qi, wi_arr, kc)


def _rel_tables_body(bias_ref, tab_ref):
    i = lax.broadcasted_iota(I32, (KEY_BLOCK, KEY_BLOCK), 0)
    j = lax.broadcasted_iota(I32, (KEY_BLOCK, KEY_BLOCK), 1)
    nb = NUM_BUCKETS // 2
    max_exact = nb // 2
    edges = [12, 16, 23, 32, 46, 64, 91]
    for d in range(4):
        rel = j - i + (d - 2) * KEY_BLOCK
        n = jnp.abs(rel)
        large = jnp.full_like(n, max_exact)
        for e in edges:
            large = large + jnp.where(n >= e, 1, 0)
        bucket = jnp.where(rel > 0, nb, 0) + jnp.where(n < max_exact, n, large)
        for h in range(A_HEADS):
            acc = jnp.zeros((KEY_BLOCK, KEY_BLOCK), F32)
            for k in range(NUM_BUCKETS):
                acc = jnp.where(bucket == k, bias_ref[k, h], acc)
            tab_ref[h, d] = acc * LOG2E


def rel_bias_tables(rel_bias):
    return pl.pallas_call(
        _rel_tables_body,
        in_specs=[pl.BlockSpec(memory_space=pltpu.SMEM)],
        out_shape=jax.ShapeDtypeStruct((A_HEADS, 4, KEY_BLOCK, KEY_BLOCK), F32),
        name="rel_bias_tables",
    )(rel_bias)


def _causal_tables_body(tab_ref):
    i = lax.broadcasted_iota(I32, (KEY_BLOCK, KEY_BLOCK), 0)
    j = lax.broadcasted_iota(I32, (KEY_BLOCK, KEY_BLOCK), 1)
    zero = jnp.zeros((KEY_BLOCK, KEY_BLOCK), F32)
    tab_ref[0, 0] = zero
    tab_ref[0, 1] = zero
    tab_ref[0, 2] = jnp.where(j // CHUNK <= i // CHUNK, 0.0, NEG)
    tab_ref[0, 3] = jnp.full((KEY_BLOCK, KEY_BLOCK), NEG, F32)


def causal_tables():
    return pl.pallas_call(
        _causal_tables_body,
        out_shape=jax.ShapeDtypeStruct((1, 4, KEY_BLOCK, KEY_BLOCK), F32),
        name="causal_tables",
    )()


def _attn_pipe_body(*refs, has_mask, per_head_tab, bq, hg, dq, dv, scale, pos0, sk):
    q_ref, k_ref, v_ref, tab_ref = refs[:4]
    mask_ref = refs[4] if has_mask else None
    out_ref = refs[5] if has_mask else refs[4]
    s_scr, p_scr, m_scr, l_scr, acc_scr = refs[-5:]

    b = pl.program_id(2)
    t0 = pos0 + b * bq
    qblk = t0 // KEY_BLOCK
    kmax = jnp.minimum(sk, ((t0 + bq - 1) // CHUNK + 1) * CHUNK)
    nkt = (kmax + ATT_TILE - 1) // ATT_TILE
    nch = ATT_TILE // LANES

    m_scr[...] = jnp.full(m_scr.shape, NEG, F32)
    l_scr[...] = jnp.zeros(l_scr.shape, F32)
    acc_scr[...] = jnp.zeros(acc_scr.shape, F32)

    def paired_loop(tile):
        def four(jj, carry):
            for u in range(4):
                tile(4 * jj + u, carry)
            return carry

        n4 = nkt // 4
        lax.fori_loop(0, n4, four, 0)

        @pl.when(nkt % 4 >= 2)
        def _():
            tile(4 * n4, 0)
            tile(4 * n4 + 1, 0)

        @pl.when(nkt % 2 == 1)
        def _():
            tile(nkt - 1, 0)

    def logits_tile(j, carry):
        off = pl.multiple_of(j * ATT_TILE, ATT_TILE)
        kb0 = off // KEY_BLOCK
        ds = [[jnp.clip(kb0 + c - (qblk + r) + 2, 0, 3) for c in range(nch)] for r in range(bq // KEY_BLOCK)]
        if has_mask:
            mk = mask_ref[:, pl.ds(off, ATT_TILE)].astype(F32)
        for h in range(hg):
            s = _dot(q_ref[:, h * dq:(h + 1) * dq], k_ref[h * dq:(h + 1) * dq, pl.ds(off, ATT_TILE)]) * scale
            hh = h if per_head_tab else 0
            s = s + jnp.concatenate(
                [jnp.concatenate([tab_ref[hh, d] for d in row], axis=1) for row in ds], axis=0)
            if has_mask:
                s = s + mk
            s_scr[h, :, pl.ds(off, ATT_TILE)] = s
            mvec = m_scr[h]
            for c in range(nch):
                mvec = jnp.maximum(mvec, s[:, c * LANES:(c + 1) * LANES])
            m_scr[h] = mvec
        return carry

    paired_loop(logits_tile)
    for h in range(hg):
        m_scr[h] = jnp.broadcast_to(jnp.max(m_scr[h], axis=1, keepdims=True), (bq, LANES))

    def exp_tile(j, carry):
        off = pl.multiple_of(j * ATT_TILE, ATT_TILE)
        for h in range(hg):
            s = s_scr[h, :, pl.ds(off, ATT_TILE)]
            m = m_scr[h]
            p = [jnp.exp2(s[:, c * LANES:(c + 1) * LANES] - m) for c in range(nch)]
            lvec = l_scr[h]
            for c in range(nch):
                lvec = lvec + p[c]
            l_scr[h] = lvec
            p_scr[h, :, pl.ds(off, ATT_TILE)] = jnp.concatenate(p, axis=1).astype(BF16)
        return carry

    lax.fori_loop(0, nkt, exp_tile, 0)

    def pv_tile(j, carry):
        off = pl.multiple_of(j * ATT_TILE, ATT_TILE)
        for h in range(hg):
            acc_scr[h] = acc_scr[h] + _dot(p_scr[h, :, pl.ds(off, ATT_TILE)],
                                           v_ref[pl.ds(off, ATT_TILE), h * dv:(h + 1) * dv])
        return carry

    paired_loop(pv_tile)
    for h in range(hg):
        l_row = jnp.sum(l_scr[h], axis=1, keepdims=True)
        out_ref[:, h * dv:(h + 1) * dv] = (acc_scr[h] / l_row).astype(out_ref.dtype)


def _attn_body(*refs, n_seg, has_mask, per_head_tab, bq, hg, dq, dv, scale, pos0, seg_tiles, by_head):
    q_ref = refs[0]
    kv = refs[1:1 + 2 * n_seg]
    pos = 1 + 2 * n_seg
    tab_ref = refs[pos]
    pos += 1
    mask_ref = refs[pos] if has_mask else None
    pos += 1 if has_mask else 0
    out_ref = refs[pos]
    s_scr, p_scr = refs[pos + 1:pos + 3]

    b = pl.program_id(2)
    t0 = pos0 + b * bq
    qblk = t0 // KEY_BLOCK

    def lane_pad(x, fill):
        w = x.shape[1]
        return x if w == LANES else jnp.concatenate([x, jnp.full((bq, LANES - w), fill, F32)], axis=1)

    mvec = [jnp.full((bq, LANES), NEG, F32) for _ in range(hg)]
    for (si, row0, key0, width) in seg_tiles:
        nch = max(1, width // KEY_BLOCK)
        cw = min(width, KEY_BLOCK)
        ds = [jnp.clip(key0 // KEY_BLOCK + c - qblk + 2, 0, 3) for c in range(nch)]
        k_t = None if by_head[si] else kv[2 * si][row0:row0 + width, :].astype(BF16)
        if has_mask:
            mk = mask_ref[:, key0:key0 + width].astype(F32)
        for h in range(hg):
            hh = h if per_head_tab else 0
            if by_head[si]:
                k_h = kv[2 * si][pl.ds(row0 * hg + h, width, stride=hg), :].astype(BF16)
            else:
                k_h = k_t[:, h * dq:(h + 1) * dq]
            s = _dot_nt(q_ref[:, h * dq:(h + 1) * dq], k_h) * scale
            bias = [tab_ref[hh, ds[c], :bq, :cw] for c in range(nch)]
            s = s + (jnp.concatenate(bias, axis=1) if nch > 1 else bias[0])
            if has_mask:
                s = s + mk
            s_scr[h, :, key0:key0 + width] = s
            for c in range(nch):
                mvec[h] = jnp.maximum(mvec[h], lane_pad(s[:, c * cw:(c + 1) * cw], NEG))

    m = [jnp.broadcast_to(jnp.max(mvec[h], axis=1, keepdims=True), (bq, LANES)) for h in range(hg)]
    lvec = [jnp.zeros((bq, LANES), F32) for _ in range(hg)]
    for (si, row0, key0, width) in seg_tiles:
        nch = max(1, width // KEY_BLOCK)
        cw = min(width, KEY_BLOCK)
        for h in range(hg):
            s = s_scr[h, :, key0:key0 + width]
            p = [jnp.exp2(s[:, c * cw:(c + 1) * cw] - m[h][:, :cw]) for c in range(nch)]
            for c in range(nch):
                lvec[h] = lvec[h] + lane_pad(p[c], 0.0)
            p_scr[h, :, key0:key0 + width] = (jnp.concatenate(p, axis=1) if nch > 1 else p[0]).astype(BF16)

    acc = [jnp.zeros((bq, dv), F32) for _ in range(hg)]
    for (si, row0, key0, width) in seg_tiles:
        v_t = None if by_head[si] else kv[2 * si + 1][row0:row0 + width, :].astype(BF16)
        for h in range(hg):
            if by_head[si]:
                v_h = kv[2 * si + 1][pl.ds(row0 * hg + h, width, stride=hg), :].astype(BF16)
            else:
                v_h = v_t[:, h * dv:(h + 1) * dv]
            acc[h] = acc[h] + _dot(p_scr[h, :, key0:key0 + width], v_h)

    for h in range(hg):
        l_row = jnp.sum(lvec[h], axis=1, keepdims=True)
        out_ref[:, h * dv:(h + 1) * dv] = (acc[h] / l_row).astype(out_ref.dtype)


def attention(q, segs, tab, mask, *, n_seq, sq, bq, q_row0, q_col0, n_heads, hg, dq, dv, scale, pos0,
              seg_tiles=None, dyn_sk=None, name="attention"):
    nqb = sq // bq
    ng = n_heads // hg
    qb0 = q_row0 // bq
    assert q_row0 % bq == 0 and q_col0 % (hg * dq) == 0 and pos0 % KEY_BLOCK == 0
    scale = scale * LOG2E
    assert bq % KEY_BLOCK == 0 or nqb == 1
    qc0 = q_col0 // (hg * dq)
    in_specs = [pl.BlockSpec((bq, hg * dq), lambda s, g, b: (qb0 + s * nqb + b, qc0 + g))]
    args = [q]
    by_head = []
    for (k, v, rows, kc0, vc0) in segs:
        assert kc0 % (hg * dq) == 0 and vc0 % (hg * dv) == 0
        by_head.append(k.shape[1] == dq and n_heads > 1)
        if by_head[-1]:
            assert hg == n_heads and dq == dv and k.shape[0] == n_seq * rows * n_heads and dyn_sk is None
            for _ in range(2):
                in_specs.append(pl.BlockSpec((rows * n_heads, dq), lambda s, g, b: (s, 0)))
            args += [k, v]
            continue
        mode = dict(pipeline_mode=pl.Buffered(1)) if dyn_sk is not None else {}
        if dyn_sk is not None:
            assert kc0 == 0 and k.shape == (n_seq * n_heads * dq, rows)
            in_specs.append(pl.BlockSpec((hg * dq, rows), lambda s, g, b: (s * ng + g, 0), **mode))
        else:
            in_specs.append(pl.BlockSpec((rows, hg * dq), lambda s, g, b, c=kc0 // (hg * dq): (s, c + g)))
        in_specs.append(pl.BlockSpec((rows, hg * dv), lambda s, g, b, c=vc0 // (hg * dv): (s, c + g), **mode))
        args += [k, v]
    per_head_tab = tab.shape[0] > 1
    if per_head_tab:
        in_specs.append(pl.BlockSpec((hg, 4, KEY_BLOCK, KEY_BLOCK), lambda s, g, b: (g, 0, 0, 0)))
    else:
        in_specs.append(pl.BlockSpec((1, 4, KEY_BLOCK, KEY_BLOCK), lambda s, g, b: (0, 0, 0, 0)))
    args.append(tab)
    if mask is not None:
        in_specs.append(pl.BlockSpec((bq, mask.shape[1]), lambda s, g, b: (s * nqb + b, 0)))
        args.append(mask)
    if dyn_sk is not None:
        assert len(segs) == 1 and dyn_sk % ATT_TILE == 0 and bq % KEY_BLOCK == 0
        body = functools.partial(_attn_pipe_body, has_mask=mask is not None, per_head_tab=per_head_tab,
                                 bq=bq, hg=hg, dq=dq, dv=dv, scale=scale, pos0=pos0, sk=dyn_sk)
        scratch = [pltpu.VMEM((hg, bq, dyn_sk), F32), pltpu.VMEM((hg, bq, dyn_sk), BF16),
                   pltpu.VMEM((hg, bq, LANES), F32), pltpu.VMEM((hg, bq, LANES), F32),
                   pltpu.VMEM((hg, bq, dv), F32)]
    else:
        body = functools.partial(_attn_body, n_seg=len(segs), has_mask=mask is not None,
                                 per_head_tab=per_head_tab, bq=bq, hg=hg, dq=dq, dv=dv, scale=scale, pos0=pos0,
                                 seg_tiles=seg_tiles, by_head=tuple(by_head))
        sk_tot = max(key0 + width for (_, _, key0, width) in seg_tiles)
        sk_tot = (sk_tot + LANES - 1) // LANES * LANES
        scratch = [pltpu.VMEM((hg, bq, sk_tot), F32), pltpu.VMEM((hg, bq, sk_tot), BF16)]
    return pl.pallas_call(
        body,
        grid=(n_seq, ng, nqb),
        in_specs=in_specs,
        out_specs=pl.BlockSpec((bq, hg * dv), lambda s, g, b: (s * nqb + b, g)),
        out_shape=jax.ShapeDtypeStruct((n_seq * sq, n_heads * dv), BF16),
        scratch_shapes=scratch,
        compiler_params=_cparams(("parallel", "parallel", "arbitrary"), VMEM_LIMIT),
        name=name,
    )(*args)


def _outproj_body(*refs, tn, na, nx, a_first, x_first):
    i = pl.program_id(0)
    j = pl.program_id(1)
    a = _pick_rows(i, a_first, refs[:na])
    w_ref = refs[na]
    x = _pick_rows(i, x_first, refs[na + 1:na + 1 + nx])
    g_ref, wr_ref, br_ref, h_ref, xn_ref, lg_ref, hrow, wb = refs[na + 1 + nx:]
    col = pl.ds(pl.multiple_of(j * tn, tn), tn)

    @pl.when(i == 0)
    def _():
        wb[:, col] = w_ref[0].astype(BF16)

    r = x + _dot(a, wb[:, col])
    h_ref[...] = r
    hrow[:, col] = r

    @pl.when(j == pl.num_programs(1) - 1)
    def _():
        xn = _rms(hrow[...], g_ref[...])
        tm = hrow.shape[0]
        for c in range(ROW_CHUNKS):
            xn_ref[pl.ds(c, tm, stride=ROW_CHUNKS), :] = xn[:, c * LANES:(c + 1) * LANES]
        lg_ref[...] = _dot(xn.astype(BF16), wr_ref[...].astype(BF16)) + br_ref[...]


def outproj_norm_router(a_parts, w_out, layer, x_parts, g_ffn, w_r, b_r, tm, tn):
    k = a_parts[0].shape[1]
    t = sum(p.shape[0] for p in a_parts)
    d = w_out.shape[-1]
    a_specs, a_first = _row_split_specs(a_parts, tm, k, lambda i, j: 0)
    x_specs, x_first = _row_split_specs(x_parts, tm, tn, lambda i, j: j)
    nj = d // tn
    return pl.pallas_call(
        functools.partial(_outproj_body, tn=tn, na=len(a_parts), nx=len(x_parts), a_first=a_first,
                          x_first=x_first),
        grid=(t // tm, d // tn),
        in_specs=a_specs + [pl.BlockSpec((1, k, tn), lambda i, j: (layer, 0, jnp.where(i == 0, j, nj - 1)))] + x_specs + [
            pl.BlockSpec((1, d), lambda i, j: (0, 0)),
            pl.BlockSpec((d, LANES), lambda i, j: (0, 0)),
            pl.BlockSpec((1, LANES), lambda i, j: (0, 0))],
        out_specs=[pl.BlockSpec((tm, tn), lambda i, j: (i, j)),
                   pl.BlockSpec((tm * ROW_CHUNKS, LANES), lambda i, j: (i, 0)),
                   pl.BlockSpec((tm, LANES), lambda i, j: (i, 0))],
        out_shape=[jax.ShapeDtypeStruct((t, d), F32), jax.ShapeDtypeStruct((t * ROW_CHUNKS, LANES), F32),
                   jax.ShapeDtypeStruct((t, LANES), F32)],
        scratch_shapes=[pltpu.VMEM((tm, d), F32), pltpu.VMEM((k, d), BF16)],
        compiler_params=_cparams(("arbitrary", "arbitrary"), VMEM_LIMIT),
        name="outproj_norm_router",
    )(*a_parts, w_out, *x_parts, g_ffn.reshape(1, d), w_r, b_r)


def _routing_body(lg_ref, out_ref, cnt_ref, carry, *, tm):
    i = pl.program_id(0)

    @pl.when(i == 0)
    def _():
        carry[...] = jnp.zeros(carry.shape, F32)

    x = lg_ref[...]
    lane = lax.broadcasted_iota(I32, (tm, LANES), 1)
    neg_inf = -jnp.inf

    def rmax(v):
        return jnp.max(v, axis=1, keepdims=True)

    def rmin(v):
        return jnp.min(v, axis=1, keepdims=True)

    def rsum(v):
        return jnp.sum(v, axis=1, keepdims=True)

    gm = lane < N_GROUPS
    gmax = rmax(jnp.where(gm, x, neg_inf))
    gsel = rmin(jnp.where(gm, jnp.where(x == gmax, lane, LANES), LANES))
    gsum = rsum(jnp.where(gm, jnp.exp(x - gmax), 0.0))
    g_w = 1.0 / gsum
    lo = N_GROUPS + gsel * EXPERTS_PER_GROUP
    em = jnp.logical_and(lane >= lo, lane < lo + EXPERTS_PER_GROUP)
    emax = rmax(jnp.where(em, x, neg_inf))
    ee = jnp.where(em, jnp.exp(x - emax), 0.0)
    p = ee / rsum(ee)
    p1 = rmax(jnp.where(em, p, -1.0))
    i1 = rmin(jnp.where(em, jnp.where(p == p1, lane, LANES), LANES))
    em2 = jnp.logical_and(em, lane != i1)
    p2 = rmax(jnp.where(em2, p, -1.0))
    i2 = rmin(jnp.where(em2, jnp.where(p == p2, lane, LANES), LANES))
    den = p1 + p2
    g1 = g_w * (p1 / den)
    g2 = g_w * (p2 / den)

    oh1 = jnp.where(lane == i1, 1.0, 0.0)
    oh2 = jnp.where(lane == i2, 1.0, 0.0)
    oh = oh1 + oh2
    r = lax.broadcasted_iota(I32, (tm, tm), 0)
    c = lax.broadcasted_iota(I32, (tm, tm), 1)
    lower = jnp.where(c < r, 1.0, 0.0).astype(BF16)
    before = _dot(lower, oh.astype(BF16)) + carry[...]
    rank1 = rsum(oh1 * before)
    rank2 = rsum(oh2 * before)
    carry[...] = carry[...] + jnp.sum(oh, axis=0, keepdims=True)

    e1 = (i1 - N_GROUPS).astype(F32)
    e2 = (i2 - N_GROUPS).astype(F32)
    vals = [e1, e2, rank1, rank2, g1, g2]
    out = jnp.zeros((tm, LANES), F32)
    for k, v in enumerate(vals):
        out = jnp.where(lane == k, v, out)
    out_ref[...] = out
    cnt_ref[...] = carry[...]


def moe_routing(logits, tm):
    t = logits.shape[0]
    return pl.pallas_call(
        functools.partial(_routing_body, tm=tm),
        grid=(t // tm,),
        in_specs=[pl.BlockSpec((tm, LANES), lambda i: (i, 0))],
        out_specs=[pl.BlockSpec((tm, LANES), lambda i: (i, 0)), pl.BlockSpec((1, LANES), lambda i: (0, 0))],
        out_shape=[jax.ShapeDtypeStruct((t, LANES), F32), jax.ShapeDtypeStruct((1, LANES), F32)],
        scratch_shapes=[pltpu.VMEM((1, LANES), F32)],
        compiler_params=_cparams(("arbitrary",)),
        name="moe_routing",
    )(logits)


def _slot_tokens_body(slots_ref, init_hbm, tok_ref, sem):
    init = pltpu.make_async_copy(init_hbm, tok_ref, sem)
    init.start()
    init.wait()

    def put(a, c):
        tok_ref[slots_ref[a]] = lax.shift_right_logical(a, 1)
        return c

    lax.fori_loop(0, slots_ref.shape[0], put, 0, unroll=8)


def moe_slot_tokens(slots, n_slots):
    init = jnp.arange(n_slots, dtype=I32) % (slots.shape[0] // 2)
    return pl.pallas_call(
        _slot_tokens_body,
        in_specs=[pl.BlockSpec(memory_space=pltpu.SMEM), pl.BlockSpec(memory_space=pl.ANY)],
        out_specs=pl.BlockSpec(memory_space=pltpu.SMEM),
        out_shape=jax.ShapeDtypeStruct((n_slots,), I32),
        scratch_shapes=[pltpu.SemaphoreType.DMA],
        name="moe_slot_tokens",
    )(slots, init)


def _dispatch_body(nslot_ref, tok_ref, x_hbm, xs_ref, buf, sem, *, rows):
    i = pl.program_id(0)
    ni = pl.num_programs(0)
    n_used = nslot_ref[0]

    def issue(blk, s):
        @pl.when(blk * rows < n_used)
        def _():
            def pair(rp, carry):
                for k in range(2):
                    r = 2 * rp + k
                    src = pl.multiple_of(tok_ref[blk * rows + r] * ROW_CHUNKS, ROW_CHUNKS)
                    dst = pl.multiple_of(r * ROW_CHUNKS, ROW_CHUNKS)
                    pltpu.make_async_copy(x_hbm.at[pl.ds(src, ROW_CHUNKS), :],
                                          buf.at[s, pl.ds(dst, ROW_CHUNKS), :], sem.at[s]).start(priority=k)
                return carry

            lax.fori_loop(0, rows // 2, pair, 0, unroll=4)

    @pl.when(i == 0)
    def _():
        issue(0, 0)

    @pl.when(i + 1 < ni)
    def _():
        issue(i + 1, (i + 1) % 2)

    slot = i % 2

    @pl.when(i * rows < n_used)
    def _():
        pltpu.make_async_copy(x_hbm.at[pl.ds(0, rows * ROW_CHUNKS), :], buf.at[slot], sem.at[slot]).wait()
        for j in range(ROW_CHUNKS):
            xs_ref[:, j * LANES:(j + 1) * LANES] = \
                buf[slot, pl.ds(j, rows, stride=ROW_CHUNKS), :].astype(xs_ref.dtype)

    @pl.when(i * rows >= n_used)
    def _():
        xs_ref[...] = jnp.zeros(xs_ref.shape, xs_ref.dtype)


def moe_dispatch(xn, slot_tok, n_used, rows):
    assert xn.shape[1] == LANES
    d = ROW_CHUNKS * LANES
    n_slots = slot_tok.shape[0]
    assert n_slots % rows == 0
    grid_spec = pltpu.PrefetchScalarGridSpec(
        num_scalar_prefetch=2,
        grid=(n_slots // rows,),
        in_specs=[pl.BlockSpec(memory_space=pl.ANY)],
        out_specs=pl.BlockSpec((rows, d), lambda i, ns, tk: (i, 0)),
        scratch_shapes=[pltpu.VMEM((2, rows * ROW_CHUNKS, LANES), F32), pltpu.SemaphoreType.DMA((2,))],
    )
    return pl.pallas_call(
        functools.partial(_dispatch_body, rows=rows),
        grid_spec=grid_spec,
        out_shape=jax.ShapeDtypeStruct((n_slots, d), BF16),
        compiler_params=_cparams(("arbitrary",), VMEM_LIMIT),
        name="moe_dispatch",
    )(n_used, slot_tok, xn)


def _experts_body(blk0_ref, nblk_ref, ntot_ref, xs_hbm, wg_hbm, wu_hbm, wd_hbm, ys_hbm,
                  xbuf, ybuf, zbuf, wgf, wuf, wdf, xsem, ysem, zsem, wsem, wgb, wub, wdb, *, bm, layer):
    e = pl.program_id(0)
    n_exp = pl.num_programs(0)
    n_total = ntot_ref[0]

    def w_copies(ex, s):
        return [pltpu.make_async_copy(w.at[layer, ex], buf.at[s], wsem.at[s])
                for w, buf in ((wg_hbm, wgf), (wu_hbm, wuf), (wd_hbm, wdf))]

    for e0 in range(W_AHEAD):
        @pl.when(e == 0)
        def _():
            for c in w_copies(e0, e0):
                c.start()

    @pl.when(e + W_AHEAD < n_exp)
    def _():
        for c in w_copies(e + W_AHEAD, (e + W_AHEAD) % (W_AHEAD + 1)):
            c.start()

    ws = e % (W_AHEAD + 1)
    for c in w_copies(e, ws):
        c.wait()

    def x_copy(g, s):
        return pltpu.make_async_copy(xs_hbm.at[pl.ds(g * bm, bm), :], xbuf.at[s], xsem.at[s])

    def y_copy(g, s):
        return pltpu.make_async_copy(ybuf.at[s], ys_hbm.at[pl.ds(g * bm, bm), :], ysem.at[s])

    for g0 in range(X_AHEAD):
        @pl.when(jnp.logical_and(e == 0, n_total > g0))
        def _():
            x_copy(g0, g0).start(priority=1)

    @pl.when(nblk_ref[e] > 0)
    def _():
        wgb[...] = wgf[ws].astype(BF16)
        wub[...] = wuf[ws].astype(BF16)
        wdb[...] = wdf[ws].astype(BF16)

    def block(i, carry):
        g = blk0_ref[e] + i
        s = g % 2
        sx = g % (X_AHEAD + 1)

        @pl.when(g + X_AHEAD < n_total)
        def _():
            x_copy(g + X_AHEAD, (g + X_AHEAD) % (X_AHEAD + 1)).start(priority=1)

        x_copy(g, sx).wait()
        x = xbuf[sx]
        gt = _dot(x, wgb[...])
        up = _dot(x, wub[...])
        a = (gt * (1.0 / (1.0 + jnp.exp(-gt)))) * up
        y = _dot(a.astype(BF16), wdb[...])

        @pl.when(g >= 2)
        def _():
            y_copy(g - 2, s).wait()

        ybuf[s] = y
        y_copy(g, s).start()
        return carry

    lax.fori_loop(0, nblk_ref[e], block, 0)

    @pl.when(e == pl.num_programs(0) - 1)
    def _():
        for back in (2, 1):
            @pl.when(n_total >= back)
            def _():
                y_copy(n_total - back, (n_total - back) % 2).wait()

        def z_copy(g):
            return pltpu.make_async_copy(zbuf, ys_hbm.at[pl.ds(g * bm, bm), :], zsem)

        zbuf[...] = jnp.zeros(zbuf.shape, F32)
        n_blocks = ys_hbm.shape[0] // bm
        lax.fori_loop(n_total, n_blocks, lambda g, c: (z_copy(g).start(), c)[1], 0)
        lax.fori_loop(n_total, n_blocks, lambda g, c: (z_copy(g).wait(), c)[1], 0)


def moe_experts(xs, blk0, nblk, n_total, w_gate, w_up, w_down, layer, bm):
    n_slots, d = xs.shape
    n_exp = blk0.shape[0]
    de = w_gate.shape[-1]
    grid_spec = pltpu.PrefetchScalarGridSpec(
        num_scalar_prefetch=3,
        grid=(n_exp,),
        in_specs=[pl.BlockSpec(memory_space=pl.ANY) for _ in range(4)],
        out_specs=pl.BlockSpec(memory_space=pl.ANY),
        scratch_shapes=[pltpu.VMEM((X_AHEAD + 1, bm, d), BF16), pltpu.VMEM((2, bm, d), F32),
                        pltpu.VMEM((bm, d), F32),
                        pltpu.VMEM((W_AHEAD + 1, d, de), F32), pltpu.VMEM((W_AHEAD + 1, d, de), F32),
                        pltpu.VMEM((W_AHEAD + 1, de, d), F32),
                        pltpu.SemaphoreType.DMA((X_AHEAD + 1,)), pltpu.SemaphoreType.DMA((2,)),
                        pltpu.SemaphoreType.DMA, pltpu.SemaphoreType.DMA((W_AHEAD + 1,)),
                        pltpu.VMEM((d, de), BF16), pltpu.VMEM((d, de), BF16), pltpu.VMEM((de, d), BF16)],
    )
    return pl.pallas_call(
        functools.partial(_experts_body, bm=bm, layer=layer),
        grid_spec=grid_spec,
        out_shape=jax.ShapeDtypeStruct((n_slots, d), F32),
        compiler_params=_cparams(("arbitrary",), VMEM_LIMIT),
        name="moe_experts",
    )(blk0, nblk, n_total, xs, w_gate, w_up, w_down)


def _combine_body(slot_ref, ys_hbm, h_ref, rt_ref, g_ref, *rest, tm, tile0, want_h):
    outs, (buf, sem) = rest[:-2], rest[-2:]
    i = pl.program_id(0)
    ni = pl.num_programs(0)

    def row_copy(src_row, k, r, s):
        return pltpu.make_async_copy(ys_hbm.at[pl.ds(src_row, 1), :], buf.at[s, k, pl.ds(r, 1), :], sem.at[s])

    def issue(tile, s):
        def one(r, carry):
            base = ((tile0 + tile) * tm + r) * 2
            row_copy(slot_ref[base], 0, r, s).start(priority=0)
            row_copy(slot_ref[base + 1], 1, r, s).start(priority=1)
            return carry

        lax.fori_loop(0, tm, one, 0, unroll=8)

    @pl.when(i == 0)
    def _():
        issue(0, 0)

    @pl.when(i + 1 < ni)
    def _():
        issue(i + 1, (i + 1) % 2)

    slot = i % 2
    for k in range(2):
        pltpu.make_async_copy(ys_hbm.at[pl.ds(0, tm), :], buf.at[slot, k], sem.at[slot]).wait()

    rt = rt_ref[...]
    g1 = rt[:, 4:5]
    g2 = rt[:, 5:6]
    h2 = h_ref[...] + (buf[slot, 0] * g1 + buf[slot, 1] * g2)
    if want_h:
        outs[0][...] = h2
    outs[-1][...] = _rms(h2, g_ref[...]).astype(outs[-1].dtype)


def moe_combine(ys, slots, h1, route, g_next, tm, u_dtype, row0=0, nrows=None, want_h=True):
    d = h1.shape[1]
    t = h1.shape[0] if nrows is None else nrows
    assert row0 % tm == 0 and t % tm == 0
    tile0 = row0 // tm
    row_spec = pl.BlockSpec((tm, d), lambda i, sl: (i, 0))
    grid_spec = pltpu.PrefetchScalarGridSpec(
        num_scalar_prefetch=1,
        grid=(t // tm,),
        in_specs=[pl.BlockSpec(memory_space=pl.ANY),
                  pl.BlockSpec((tm, d), lambda i, sl: (i + tile0, 0)),
                  pl.BlockSpec((tm, LANES), lambda i, sl: (i + tile0, 0)),
                  pl.BlockSpec((1, d), lambda i, sl: (0, 0))],
        out_specs=[row_spec, row_spec] if want_h else [row_spec],
        scratch_shapes=[pltpu.VMEM((2, 2, tm, d), F32), pltpu.SemaphoreType.DMA((2,))],
    )
    shapes = [jax.ShapeDtypeStruct((t, d), F32)] if want_h else []
    return pl.pallas_call(
        functools.partial(_combine_body, tm=tm, tile0=tile0, want_h=want_h),
        grid_spec=grid_spec,
        out_shape=shapes + [jax.ShapeDtypeStruct((t, d), u_dtype)],
        compiler_params=_cparams(("arbitrary",), VMEM_LIMIT),
        name="moe_combine",
    )(slots, ys, h1, route, g_next.reshape(1, d))


def hier_moe_layer(h1, xn, logits, w_gate, w_up, w_down, layer, g_next, u_dtype, split=None):
    t = h1.shape[0]
    bm = MOE_BM
    route, counts = moe_routing(logits, _tile(t, 256))
    e = route[:, 0:2].astype(I32)
    rank = route[:, 2:4].astype(I32)
    counts = counts[0, N_GROUPS:N_GROUPS + N_EXPERTS].astype(I32)
    padded = (counts + bm - 1) // bm * bm
    pad_end = jnp.cumsum(padded)
    pad_start = pad_end - padded
    start_of = jnp.sum(jnp.where(e[..., None] == jnp.arange(N_EXPERTS, dtype=I32), pad_start, 0), axis=-1)
    slots = (start_of + rank).reshape(-1)
    nb = (2 * t) // bm + N_EXPERTS
    slot_tok = moe_slot_tokens(slots, nb * bm)
    n_used = pad_end[-1:]
    xs = moe_dispatch(xn, slot_tok, n_used, bm * max(f for f in (4, 2, 1) if nb % f == 0))
    ys = moe_experts(xs, pad_start // bm, padded // bm, n_used // bm, w_gate, w_up, w_down, layer, bm)
    tmc = _tile(t, 256)
    if split is None:
        return moe_combine(ys, slots, h1, route, g_next, tmc, u_dtype)
    (u_a,) = moe_combine(ys, slots, h1, route, g_next, tmc, u_dtype, 0, split, want_h=False)
    (u_b,) = moe_combine(ys, slots, h1, route, g_next, tmc, u_dtype, split, t - split, want_h=False)
    return u_a, u_b


def _swap_halves(x):
    lane = lax.broadcasted_iota(I32, x.shape, 1)
    return jnp.where(lane < QK_ROPE // 2, pltpu.roll(x, LANES - QK_ROPE // 2, 1), pltpu.roll(x, QK_ROPE // 2, 1))


def _mla_prep_body(p_ref, gq_ref, gkv_ref, cos_ref, sin_ref, cq_ref, ckv_ref, ckvb_ref, kr_ref):
    p = p_ref[...]
    cq_ref[...] = _rms(p[:, :Q_LORA], gq_ref[...]).astype(cq_ref.dtype)
    ckv = _rms(p[:, Q_LORA:Q_LORA + KV_LORA], gkv_ref[...])
    ckv_ref[...] = ckv
    ckvb_ref[...] = ckv.astype(ckvb_ref.dtype)
    kr = p[:, Q_LORA + KV_LORA:]
    kr_ref[...] = kr * cos_ref[...] + _swap_halves(kr) * sin_ref[...]


def mla_prep(proj, g_q, g_kv, cos, sin, tm):
    t = proj.shape[0]
    return pl.pallas_call(
        _mla_prep_body,
        grid=(t // tm,),
        in_specs=[pl.BlockSpec((tm, proj.shape[1]), lambda i: (i, 0)),
                  pl.BlockSpec((1, Q_LORA), lambda i: (0, 0)),
                  pl.BlockSpec((1, KV_LORA), lambda i: (0, 0)),
                  pl.BlockSpec((tm, LANES), lambda i: (i, 0)),
                  pl.BlockSpec((tm, LANES), lambda i: (i, 0))],
        out_specs=[pl.BlockSpec((tm, Q_LORA), lambda i: (i, 0)),
                   pl.BlockSpec((tm, KV_LORA), lambda i: (i, 0)),
                   pl.BlockSpec((tm, KV_LORA), lambda i: (i, 0)),
                   pl.BlockSpec((tm, LANES), lambda i: (i, 0))],
        out_shape=[jax.ShapeDtypeStruct((t, Q_LORA), BF16), jax.ShapeDtypeStruct((t, KV_LORA), F32),
                   jax.ShapeDtypeStruct((t, KV_LORA), BF16), jax.ShapeDtypeStruct((t, LANES), F32)],
        compiler_params=_cparams(("parallel",)),
        name="mla_prep",
    )(proj, g_q.reshape(1, -1), g_kv.reshape(1, -1), cos, sin)


def _mla_q_body(cq_ref, w_ref, cos_ref, sin_ref, q_ref, wb):
    @pl.when(pl.program_id(0) == 0)
    def _():
        wb[...] = w_ref[...].astype(BF16)

    cq = cq_ref[...]
    cos = cos_ref[...]
    sin = sin_ref[...]
    for h in range(B_HEADS):
        y = _dot(cq, wb[:, h * MLA_QK_PAD:(h + 1) * MLA_QK_PAD])
        xr = y[:, LANES:]
        q_ref[:, h * MLA_QK_PAD:h * MLA_QK_PAD + LANES] = y[:, :LANES].astype(q_ref.dtype)
        q_ref[:, h * MLA_QK_PAD + LANES:(h + 1) * MLA_QK_PAD] = \
            (xr * cos + _swap_halves(xr) * sin).astype(q_ref.dtype)


def mla_q_up(cq, w_uq_pad, cos, sin, tm):
    t = cq.shape[0]
    n = B_HEADS * MLA_QK_PAD
    return pl.pallas_call(
        _mla_q_body,
        grid=(t // tm,),
        in_specs=[pl.BlockSpec((tm, Q_LORA), lambda i: (i, 0)),
                  pl.BlockSpec((Q_LORA, n), lambda i: (0, 0)),
                  pl.BlockSpec((tm, LANES), lambda i: (i, 0)),
                  pl.BlockSpec((tm, LANES), lambda i: (i, 0))],
        out_specs=pl.BlockSpec((tm, n), lambda i: (i, 0)),
        out_shape=jax.ShapeDtypeStruct((t, n), BF16),
        scratch_shapes=[pltpu.VMEM((Q_LORA, n), BF16)],
        compiler_params=_cparams(("arbitrary",), VMEM_LIMIT),
        name="mla_q_up",
    )(cq, w_uq_pad, cos, sin)


def _mla_kv_body(c_ref, kr_ref, wk_ref, wv_ref, k_ref, v_ref, wkb, wvb):
    @pl.when(pl.program_id(0) == 0)
    def _():
        wkb[...] = wk_ref[0].astype(BF16)
        wvb[...] = wv_ref[0].astype(BF16)

    c = c_ref[...].astype(BF16)
    kr = kr_ref[...].astype(k_ref.dtype)
    kn = _dot(c, wkb[...])
    for h in range(B_HEADS):
        k_ref[:, h * MLA_QK_PAD:h * MLA_QK_PAD + LANES] = kn[:, h * QK_NOPE:(h + 1) * QK_NOPE].astype(k_ref.dtype)
        k_ref[:, h * MLA_QK_PAD + LANES:(h + 1) * MLA_QK_PAD] = kr
    v_ref[...] = _dot(c, wvb[...]).astype(v_ref.dtype)


def mla_kv_up(ckv, kr, w_uk, w_uv, layer, tm):
    t = ckv.shape[0]
    nk = B_HEADS * MLA_QK_PAD
    nv = B_HEADS * V_DIM
    return pl.pallas_call(
        _mla_kv_body,
        grid=(t // tm,),
        in_specs=[pl.BlockSpec((tm, KV_LORA), lambda i: (i, 0)),
                  pl.BlockSpec((tm, LANES), lambda i: (i, 0)),
                  pl.BlockSpec((1, KV_LORA, B_HEADS * QK_NOPE), lambda i: (layer, 0, 0)),
                  pl.BlockSpec((1, KV_LORA, nv), lambda i: (layer, 0, 0))],
        out_specs=[pl.BlockSpec((tm, nk), lambda i: (i, 0)), pl.BlockSpec((tm, nv), lambda i: (i, 0))],
        out_shape=[jax.ShapeDtypeStruct((t, nk), BF16), jax.ShapeDtypeStruct((t, nv), BF16)],
        scratch_shapes=[pltpu.VMEM((KV_LORA, B_HEADS * QK_NOPE), BF16), pltpu.VMEM((KV_LORA, nv), BF16)],
        compiler_params=_cparams(("arbitrary",), VMEM_LIMIT),
        name="mla_kv_up",
    )(ckv, kr, w_uk, w_uv)


def kernel(x_prompt, x_sample, cache_a_k, cache_a_v, cache_a_kidx, cache_b_ckv, cache_b_krope, norm_mix, norm_ffn, norm_final, rel_bias, a_w_in, a_w_out, b_w_in, b_norm_q, b_norm_kv, b_w_uq, b_w_uk, b_w_uv, b_w_out, moe_w_grp, moe_b_grp, moe_w_rtr, moe_b_rtr, moe_w_gate, moe_w_up, moe_w_down):
    n_p, s_p, d = x_prompt.shape
    n_s, s_s, _ = x_sample.shape
    past = cache_a_k.shape[2]
    tp = n_p * s_p
    ts = n_s * s_s
    t = tp + ts
    a_qd = A_HEADS * A_HEAD_DIM
    tm = _tile(math.gcd(tp, ts), 1024)
    tm2 = _tile(math.gcd(tp, ts), 512)

    x_parts = [x_prompt.reshape(tp, d), x_sample.reshape(ts, d)]

    def router_params(i):
        w_r = jnp.concatenate([moe_w_grp[i], moe_w_rtr[i],
                               jnp.zeros((d, LANES - N_GROUPS - N_EXPERTS), F32)], axis=1)
        b_r = jnp.concatenate([moe_b_grp[i], moe_b_rtr[i],
                               jnp.zeros((LANES - N_GROUPS - N_EXPERTS,), F32)]).reshape(1, LANES)
        return w_r, b_r

    u0 = rmsnorm(x_parts, norm_mix[0], tm)
    w_a = a_w_in[0]
    (q_b,) = matmul(u0, w_a, 0, a_qd, [BF16], tm, 512, "a_proj_q")
    tmp, tms = _tile(tp, 1024), _tile(ts, 1024)
    kp_f, kp_b = matmul(u0, w_a, a_qd, a_qd, [F32, BF16], tmp, 512, "a_proj_k", 0, tp)
    vp_f, vp_b = matmul(u0, w_a, 2 * a_qd, a_qd, [F32, BF16], tmp, 512, "a_proj_v", 0, tp)
    ks_f, ks_b = matmul(u0, w_a, a_qd, a_qd, [F32, BF16], tms, 512, "a_proj_k", tp, ts)
    vs_f, vs_b = matmul(u0, w_a, 2 * a_qd, a_qd, [F32, BF16], tms, 512, "a_proj_v", tp, ts)
    (qi_b,) = matmul(u0, w_a, 3 * a_qd, IDX_HEADS * IDX_DIM, [BF16], tm, 512, "a_proj_qi")
    c_ki = 3 * a_qd + IDX_HEADS * IDX_DIM
    w_ki = w_a[:, c_ki:c_ki + IDX_DIM]
    w_wi = w_a[:, c_ki + IDX_DIM:c_ki + IDX_DIM + IDX_HEADS]
    zk = jnp.zeros((d, IDX_DIM), F32)
    w_tail = jnp.concatenate([w_ki, zk, zk, w_ki, w_wi, jnp.zeros((d, LANES - IDX_HEADS), F32)], axis=1)
    (tail,) = matmul(u0, w_tail, 0, 3 * LANES, [F32], tm, 3 * LANES, "a_proj_tail")
    kidx = tail[:, :IDX_DIM]

    bias_tab = rel_bias_tables(rel_bias)
    a_scale = A_HEAD_DIM ** -0.5
    mask_p = dsa_select(qi_b, tail, 2, tail, n_seq=n_p, sq=s_p, bq=128, sk=s_p, sk_real=s_p, tk=512,
                        pos0=0, q_row0=0)
    kp_t = kp_b.reshape(n_p, s_p, a_qd).transpose(0, 2, 1).reshape(n_p * a_qd, s_p)
    att_p = attention(q_b, [(kp_t, vp_b, s_p, 0, 0)], bias_tab, mask_p, n_seq=n_p, sq=s_p, bq=ATT_BQ, q_row0=0,
                      q_col0=0, n_heads=A_HEADS, hg=ATT_HG, dq=A_HEAD_DIM, dv=A_HEAD_DIM, scale=a_scale, pos0=0,
                      dyn_sk=s_p, name="dsa_attention_prompt")
    sk_s = past + s_s
    sk_pad = (sk_s + ATT_TILE - 1) // ATT_TILE * ATT_TILE
    ki_past = cache_a_kidx[0]
    zp = jnp.zeros_like(ki_past)
    kc_past = jnp.concatenate([ki_past, zp, zp, ki_past], axis=-1)
    kc_new = tail[tp:, :2 * LANES].reshape(n_s, s_s, 2 * LANES)
    kc_s = jnp.concatenate([kc_past, kc_new, jnp.zeros((n_s, sk_pad - sk_s, 2 * LANES), F32)], axis=1)
    mask_s = dsa_select(qi_b, tail, 2, kc_s.reshape(n_s * sk_pad, 2 * LANES), n_seq=n_s, sq=s_s, bq=s_s,
                        sk=sk_pad, sk_real=sk_s, tk=ATT_TILE, pos0=past, q_row0=tp)
    tiles_s = [(0, r, r, ATT_TILE) for r in range(0, past, ATT_TILE)] + [(1, 0, past, s_s)]
    att_s = attention(q_b, [(cache_a_k[0].reshape(n_s * past * A_HEADS, A_HEAD_DIM),
                             cache_a_v[0].reshape(n_s * past * A_HEADS, A_HEAD_DIM), past, 0, 0),
                            (ks_b, vs_b, s_s, 0, 0)],
                      bias_tab, mask_s, n_seq=n_s, sq=s_s, bq=s_s, q_row0=tp, q_col0=0, n_heads=A_HEADS, hg=A_HEADS,
                      dq=A_HEAD_DIM, dv=A_HEAD_DIM, scale=a_scale, pos0=past, seg_tiles=tiles_s,
                      name="dsa_attention_sample")

    w_r0, b_r0 = router_params(0)
    h1, xn0, lg0 = outproj_norm_router([att_p, att_s], a_w_out, 0, x_parts, norm_ffn[0], w_r0, b_r0, tm2, 512)
    h2, u1 = hier_moe_layer(h1, xn0, lg0, moe_w_gate, moe_w_up, moe_w_down, 0, norm_mix[1], BF16)

    n_in = Q_LORA + KV_LORA + QK_ROPE
    w_b = jnp.concatenate([b_w_in[0], jnp.zeros((d, LANES - QK_ROPE), F32)], axis=1)
    (proj,) = matmul(u1, w_b, 0, n_in + LANES - QK_ROPE, [F32], tm, 384, "b_proj")
    half = QK_ROPE // 2
    inv = ROPE_THETA ** (-jnp.arange(half, dtype=F32) / half)
    pos_all = jnp.concatenate([jnp.tile(jnp.arange(s_p, dtype=I32), n_p),
                               jnp.tile(past + jnp.arange(s_s, dtype=I32), n_s)])
    ang = pos_all.astype(F32)[:, None] * inv[None, :]
    zl = jnp.zeros((t, LANES - QK_ROPE), F32)
    cos_t = jnp.concatenate([jnp.cos(ang), jnp.cos(ang), zl], axis=1)
    sin_t = jnp.concatenate([-jnp.sin(ang), jnp.sin(ang), zl], axis=1)
    cq_b, ckv_f, ckv_b, kr_f = mla_prep(proj, b_norm_q[0], b_norm_kv[0], cos_t, sin_t, tm)
    w_uq_pad = jnp.pad(b_w_uq[0].reshape(Q_LORA, B_HEADS, QK_NOPE + QK_ROPE),
                       ((0, 0), (0, 0), (0, MLA_QK_PAD - QK_NOPE - QK_ROPE))).reshape(Q_LORA, B_HEADS * MLA_QK_PAD)
    qm = mla_q_up(cq_b, w_uq_pad, cos_t, sin_t, tm2)
    km_p, vm_p = mla_kv_up(ckv_b[:tp], kr_f[:tp], b_w_uk, b_w_uv, 0, _tile(tp, 512))
    ctab = causal_tables()
    hgb = 8
    km_t = km_p.reshape(n_p, s_p, B_HEADS * MLA_QK_PAD).transpose(0, 2, 1).reshape(n_p * B_HEADS * MLA_QK_PAD, s_p)
    matt_p = attention(qm, [(km_t, vm_p, s_p, 0, 0)], ctab, None, n_seq=n_p, sq=s_p, bq=ATT_BQ, q_row0=0, q_col0=0,
                       n_heads=B_HEADS, hg=ATT_HG, dq=MLA_QK_PAD, dv=V_DIM, scale=B_SCALE, pos0=0, dyn_sk=s_p,
                       name="mla_attention_prompt")
    ckv_all = jnp.concatenate([cache_b_ckv[0], ckv_f[tp:].reshape(n_s, s_s, KV_LORA)], axis=1)
    kr_past = jnp.concatenate([cache_b_krope[0], jnp.zeros((n_s, past, LANES - QK_ROPE), F32)], axis=-1)
    kr_all = jnp.concatenate([kr_past, kr_f[tp:].reshape(n_s, s_s, LANES)], axis=1)
    km_s, vm_s = mla_kv_up(ckv_all.reshape(n_s * sk_s, KV_LORA), kr_all.reshape(n_s * sk_s, LANES),
                           b_w_uk, b_w_uv, 0, sk_s)
    tiles_m = [(0, r, r, ATT_TILE) for r in range(0, past, ATT_TILE)] + [(0, past, past, s_s)]
    matt_s = attention(qm, [(km_s, vm_s, sk_s, 0, 0)], ctab, None, n_seq=n_s, sq=s_s, bq=s_s, q_row0=tp, q_col0=0,
                       n_heads=B_HEADS, hg=hgb, dq=MLA_QK_PAD, dv=V_DIM, scale=B_SCALE, pos0=past,
                       seg_tiles=tiles_m, name="mla_attention_sample")

    w_r1, b_r1 = router_params(1)
    h3, xn1, lg1 = outproj_norm_router([matt_p, matt_s], b_w_out, 0, [h2], norm_ffn[1], w_r1, b_r1, tm2, 512)
    y_p, y_s = hier_moe_layer(h3, xn1, lg1, moe_w_gate, moe_w_up, moe_w_down, 1, norm_final, F32, split=tp)
    y_prompt = y_p.reshape(n_p, s_p, d)
    y_sample = y_s.reshape(n_s, s_s, d)

    def heads(a, n, s):
        return a.reshape(1, n, s, A_HEADS, A_HEAD_DIM)

    kr_out = kr_f[:, :QK_ROPE]
    return (y_prompt, y_sample,
            heads(kp_f, n_p, s_p), heads(vp_f, n_p, s_p), kidx[:tp].reshape(1, n_p, s_p, IDX_DIM),
            ckv_f[:tp].reshape(1, n_p, s_p, KV_LORA), kr_out[:tp].reshape(1, n_p, s_p, QK_ROPE),
            heads(ks_f, n_s, s_s), heads(vs_f, n_s, s_s), kidx[tp:].reshape(1, n_s, s_s, IDX_DIM),
            ckv_f[tp:].reshape(1, n_s, s_s, KV_LORA), kr_out[tp:].reshape(1, n_s, s_s, QK_ROPE))
```

```python
import functools
import math

import jax
import jax.numpy as jnp
from jax import lax
from jax.experimental import pallas as pl
from jax.experimental.pallas import tpu as pltpu

F32 = jnp.float32
BF16 = jnp.bfloat16
I32 = jnp.int32

RMS_EPS = 1e-6
CHUNK = 64
NEG = -1e30
INT_MIN = -2 ** 31
INT_MAX = 2 ** 31 - 1
LOG2E = math.log2(math.e)

LANES = 128
KEY_BLOCK = 128
ATT_TILE = 256
ATT_BQ = 128
ATT_HG = 8
VMEM_LIMIT = 56 * 1024 * 1024

A_HEADS = 16
A_HEAD_DIM = 128
IDX_HEADS = 16
IDX_DIM = 64
TOPK_MAX = 256
IDX_W_SCALE = float((IDX_HEADS * IDX_DIM) ** -0.5)
NUM_BUCKETS = 32
MAX_DISTANCE = 128
B_HEADS = 16
Q_LORA = 512
KV_LORA = 512
QK_NOPE = 128
QK_ROPE = 64
V_DIM = 128
ROPE_THETA = 10000.0
B_SCALE = float((QK_NOPE + QK_ROPE) ** -0.5)
MLA_QK_PAD = 256
N_GROUPS = 8
EXPERTS_PER_GROUP = 8
N_EXPERTS = 64
D_EXPERT = 512
MOE_BM = 128
ROW_CHUNKS = 16
X_AHEAD = 2


def _tile(n, pref):
    for c in range(pref, 0, -LANES):
        if n % c == 0:
            return c
    raise ValueError(f"no 128-multiple tile divides {n}")


def _cparams(sem, vmem=None):
    return pltpu.CompilerParams(dimension_semantics=sem, vmem_limit_bytes=vmem)


def _dot(a, b):
    return jnp.dot(a, b, preferred_element_type=F32)


def _dot_nt(a, b):
    return lax.dot_general(a, b, (((1,), (1,)), ((), ())), preferred_element_type=F32)


def _rms(x, g):
    ms = jnp.mean(x * x, axis=-1, keepdims=True)
    return x * lax.rsqrt(ms + RMS_EPS) * g


def _pick_rows(i, n_first, refs):
    if len(refs) == 1:
        return refs[0][...]
    return jnp.where(i < n_first, refs[0][...], refs[1][...])


def _row_split_specs(parts, tm, width, col_of):
    n_first = parts[0].shape[0] // tm
    assert all(p.shape[0] % tm == 0 for p in parts)
    specs = [pl.BlockSpec((tm, width), lambda i, *r: (jnp.minimum(i, n_first - 1), col_of(i, *r)))]
    if len(parts) == 2:
        specs.append(pl.BlockSpec((tm, width), lambda i, *r: (jnp.maximum(i - n_first, 0), col_of(i, *r))))
    return specs, n_first


def _rmsnorm_body(*refs, n_first):
    x = _pick_rows(pl.program_id(0), n_first, refs[:-2])
    g_ref, o_ref = refs[-2:]
    o_ref[...] = _rms(x, g_ref[...]).astype(o_ref.dtype)


def rmsnorm(xs, g, tm, out_dtype=BF16):
    d = xs[0].shape[1]
    t = sum(x.shape[0] for x in xs)
    specs, n_first = _row_split_specs(xs, tm, d, lambda i: 0)
    return pl.pallas_call(
        functools.partial(_rmsnorm_body, n_first=n_first),
        grid=(t // tm,),
        in_specs=specs + [pl.BlockSpec((1, d), lambda i: (0, 0))],
        out_specs=pl.BlockSpec((tm, d), lambda i: (i, 0)),
        out_shape=jax.ShapeDtypeStruct((t, d), out_dtype),
        compiler_params=_cparams(("parallel",)),
        name="rmsnorm",
    )(*xs, g.reshape(1, d))


def _matmul_body(a_ref, w_ref, *rest):
    outs, wb = rest[:-1], rest[-1]

    @pl.when(pl.program_id(1) == 0)
    def _():
        wb[...] = w_ref[...].astype(BF16)

    r = _dot(a_ref[...], wb[...])
    for o in outs:
        o[...] = r.astype(o.dtype)


def matmul(a, w, col0, ncols, out_dtypes, tm, tn, name, row0=0, nrows=None):
    k = a.shape[1]
    t = a.shape[0] if nrows is None else nrows
    assert ncols % tn == 0 and col0 % tn == 0 and t % tm == 0 and row0 % tm == 0
    cb = col0 // tn
    rb = row0 // tm
    return pl.pallas_call(
        _matmul_body,
        grid=(ncols // tn, t // tm),
        in_specs=[pl.BlockSpec((tm, k), lambda j, i: (i + rb, 0)),
                  pl.BlockSpec((k, tn), lambda j, i: (0, j + cb))],
        out_specs=[pl.BlockSpec((tm, tn), lambda j, i: (i, j)) for _ in out_dtypes],
        out_shape=[jax.ShapeDtypeStruct((t, ncols), dt) for dt in out_dtypes],
        scratch_shapes=[pltpu.VMEM((k, tn), BF16)],
        compiler_params=_cparams(("arbitrary", "arbitrary"), VMEM_LIMIT),
        name=name,
    )(a, w)


def _float_sort_key(x):
    bits = lax.bitcast_convert_type(x, I32)
    return bits ^ ((bits >> 31) & INT_MAX)


def _select_body(qi_ref, wi_ref, kc_ref, mask_ref, key_scr, jm_scr, *, bq, sk, sk_real, tk, pos0, topk):
    b = pl.program_id(1)
    t0 = pos0 + b * bq
    kmax = jnp.minimum(sk_real, ((t0 + bq - 1) // CHUNK + 1) * CHUNK)
    nkt = (kmax + tk - 1) // tk
    nch = tk // LANES

    w = wi_ref[...] * IDX_W_SCALE
    wb = [jnp.broadcast_to(w[:, h:h + 1], (bq, LANES)) for h in range(IDX_HEADS)]
    qpos = t0 + lax.broadcasted_iota(I32, (LANES, LANES), 1)
    lim = jnp.minimum((qpos // CHUNK + 1) * CHUNK, sk_real)
    sub = lax.broadcasted_iota(I32, (LANES, LANES), 0)

    def to_lanes(x):
        if bq < LANES:
            x = jnp.concatenate([x, jnp.zeros((LANES - bq, LANES), x.dtype)], axis=0)
        return x.T

    def score_tile(j, carry):
        off = pl.multiple_of(j * tk, tk)
        kc = kc_ref[pl.ds(off, tk), :].astype(BF16)
        ka, kb = kc[:, :LANES], kc[:, LANES:]
        accs = [jnp.zeros((bq, LANES), F32) for _ in range(nch)]
        for g in range(IDX_HEADS // 2):
            qg = qi_ref[:, g * LANES:(g + 1) * LANES]
            sa = _dot_nt(qg, ka)
            sb = _dot_nt(qg, kb)
            for c in range(nch):
                sl = slice(c * LANES, (c + 1) * LANES)
                accs[c] = accs[c] + wb[2 * g] * jnp.maximum(sa[:, sl], 0.0) \
                    + wb[2 * g + 1] * jnp.maximum(sb[:, sl], 0.0)
        for c in range(nch):
            kpos = off + c * LANES + sub
            sc = jnp.where(kpos < lim, to_lanes(accs[c]), -jnp.inf)
            key_scr[pl.ds(off + c * LANES, LANES), :] = _float_sort_key(sc)
        return carry

    lax.fori_loop(0, nkt, score_tile, 0)

    def count(indicator):
        def tile(j, cnt):
            off = pl.multiple_of(j * tk, tk)
            for c in range(nch):
                kt = key_scr[pl.ds(off + c * LANES, LANES), :]
                cnt = cnt + indicator(kt, off + c * LANES + sub)
            return cnt
        cnt = lax.fori_loop(0, nkt, tile, jnp.zeros((LANES, LANES), F32))
        return jnp.sum(cnt, axis=0, keepdims=True)

    def bit_step(i, pfx_u):
        bit = lax.shift_left(jnp.int32(1), 31 - i)
        cand_u = pfx_u | bit
        cand_s = cand_u ^ INT_MIN
        total = count(lambda kt, kp: jnp.where(kt >= cand_s, 1.0, 0.0))
        return jnp.where(total >= topk, cand_u, pfx_u)

    pfx = lax.fori_loop(0, 32, bit_step, jnp.zeros((1, LANES), I32))
    thr = pfx ^ INT_MIN

    n_gt = count(lambda kt, kp: jnp.where(kt > thr, 1.0, 0.0))
    n_ge = count(lambda kt, kp: jnp.where(kt >= thr, 1.0, 0.0))
    quota = topk - n_gt
    jm_scr[...] = jnp.full((1, LANES), INT_MAX, I32)
    any_excess = jnp.max(jnp.where(n_ge > topk, 1.0, 0.0)) > 0.0

    @pl.when(any_excess)
    def _():
        nbits = max(1, int(sk - 1).bit_length())

        def idx_step(i, ans):
            cand = ans | lax.shift_left(jnp.int32(1), nbits - 1 - i)
            below = count(lambda kt, kp: jnp.where(kt == thr, jnp.where(kp < cand, 1.0, 0.0), 0.0))
            return jnp.where(below < quota, cand, ans)

        jm_scr[...] = lax.fori_loop(0, nbits, idx_step, jnp.zeros((1, LANES), I32))

    jm = jm_scr[...]

    def write_tile(j, carry):
        off = pl.multiple_of(j * tk, tk)
        for c in range(nch):
            kt = key_scr[pl.ds(off + c * LANES, LANES), :]
            kpos = off + c * LANES + sub
            v = jnp.where(kt > thr, 0.0, jnp.where(kt == thr, jnp.where(kpos <= jm, 0.0, NEG), NEG))
            v = jnp.where(kpos < lim, v, NEG)
            mask_ref[:, pl.ds(off + c * LANES, LANES)] = v.T[:bq].astype(mask_ref.dtype)
        return carry

    lax.fori_loop(0, nkt, write_tile, 0)

    def fill_tile(j, carry):
        off = pl.multiple_of(j * tk, tk)
        mask_ref[:, pl.ds(off, tk)] = jnp.full((bq, tk), NEG, mask_ref.dtype)
        return carry

    lax.fori_loop(nkt, sk // tk, fill_tile, 0)


def dsa_select(qi, wi_arr, wi_blk, kc, *, n_seq, sq, bq, sk, sk_real, tk, pos0, q_row0):
    nqb = sq // bq
    qb0 = q_row0 // bq
    assert q_row0 % bq == 0 and sk % tk == 0 and tk >= TOPK_MAX
    topk = min(TOPK_MAX, sk_real // 4)
    body = functools.partial(_select_body, bq=bq, sk=sk, sk_real=sk_real, tk=tk, pos0=pos0, topk=topk)
    return pl.pallas_call(
        body,
        grid=(n_seq, nqb),
        in_specs=[pl.BlockSpec((bq, IDX_HEADS * IDX_DIM), lambda s, b: (qb0 + s * nqb + b, 0)),
                  pl.BlockSpec((bq, LANES), lambda s, b: (qb0 + s * nqb + b, wi_blk)),
                  pl.BlockSpec((sk, 2 * LANES), lambda s, b: (s, 0))],
        out_specs=pl.BlockSpec((bq, sk), lambda s, b: (s * nqb + b, 0)),
        out_shape=jax.ShapeDtypeStruct((n_seq * sq, sk), BF16),
        scratch_shapes=[pltpu.VMEM((sk, LANES), I32), pltpu.VMEM((1, LANES), I32)],
        compiler_params=_cparams(("parallel", "arbitrary"), VMEM_LIMIT),
        name="dsa_select",
    )(qi, wi_arr, kc)


def _rel_tables_body(bias_ref, tab_ref):
    i = lax.broadcasted_iota(I32, (KEY_BLOCK, KEY_BLOCK), 0)
    j = lax.broadcasted_iota(I32, (KEY_BLOCK, KEY_BLOCK), 1)
    nb = NUM_BUCKETS // 2
    max_exact = nb // 2
    edges = [12, 16, 23, 32, 46, 64, 91]
    for d in range(4):
        rel = j - i + (d - 2) * KEY_BLOCK
        n = jnp.abs(rel)
        large = jnp.full_like(n, max_exact)
        for e in edges:
            large = large + jnp.where(n >= e, 1, 0)
        bucket = jnp.where(rel > 0, nb, 0) + jnp.where(n < max_exact, n, large)
        for h in range(A_HEADS):
            acc = jnp.zeros((KEY_BLOCK, KEY_BLOCK), F32)
            for k in range(NUM_BUCKETS):
                acc = jnp.where(bucket == k, bias_ref[k, h], acc)
            tab_ref[h, d] = acc * LOG2E


def rel_bias_tables(rel_bias):
    return pl.pallas_call(
        _rel_tables_body,
        in_specs=[pl.BlockSpec(memory_space=pltpu.SMEM)],
        out_shape=jax.ShapeDtypeStruct((A_HEADS, 4, KEY_BLOCK, KEY_BLOCK), F32),
        name="rel_bias_tables",
    )(rel_bias)


def _causal_tables_body(tab_ref):
    i = lax.broadcasted_iota(I32, (KEY_BLOCK, KEY_BLOCK), 0)
    j = lax.broadcasted_iota(I32, (KEY_BLOCK, KEY_BLOCK), 1)
    zero = jnp.zeros((KEY_BLOCK, KEY_BLOCK), F32)
    tab_ref[0, 0] = zero
    tab_ref[0, 1] = zero
    tab_ref[0, 2] = jnp.where(j // CHUNK <= i // CHUNK, 0.0, NEG)
    tab_ref[0, 3] = jnp.full((KEY_BLOCK, KEY_BLOCK), NEG, F32)


def causal_tables():
    return pl.pallas_call(
        _causal_tables_body,
        out_shape=jax.ShapeDtypeStruct((1, 4, KEY_BLOCK, KEY_BLOCK), F32),
        name="causal_tables",
    )()


def _attn_pipe_body(*refs, has_mask, per_head_tab, bq, hg, dq, dv, scale, pos0, sk):
    q_ref, k_ref, v_ref, tab_ref = refs[:4]
    mask_ref = refs[4] if has_mask else None
    out_ref = refs[5] if has_mask else refs[4]
    s_scr, p_scr, m_scr, l_scr, acc_scr = refs[-5:]

    b = pl.program_id(2)
    t0 = pos0 + b * bq
    qblk = t0 // KEY_BLOCK
    kmax = jnp.minimum(sk, ((t0 + bq - 1) // CHUNK + 1) * CHUNK)
    nkt = (kmax + ATT_TILE - 1) // ATT_TILE
    nch = ATT_TILE // LANES

    m_scr[...] = jnp.full(m_scr.shape, NEG, F32)
    l_scr[...] = jnp.zeros(l_scr.shape, F32)
    acc_scr[...] = jnp.zeros(acc_scr.shape, F32)

    def paired_loop(tile):
        def four(jj, carry):
            for u in range(4):
                tile(4 * jj + u, carry)
            return carry

        n4 = nkt // 4
        lax.fori_loop(0, n4, four, 0)

        @pl.when(nkt % 4 >= 2)
        def _():
            tile(4 * n4, 0)
            tile(4 * n4 + 1, 0)

        @pl.when(nkt % 2 == 1)
        def _():
            tile(nkt - 1, 0)

    def logits_tile(j, carry):
        off = pl.multiple_of(j * ATT_TILE, ATT_TILE)
        kb0 = off // KEY_BLOCK
        ds = [[jnp.clip(kb0 + c - (qblk + r) + 2, 0, 3) for c in range(nch)] for r in range(bq // KEY_BLOCK)]
        if has_mask:
            mk = mask_ref[:, pl.ds(off, ATT_TILE)].astype(F32)
        for h in range(hg):
            s = _dot(q_ref[:, h * dq:(h + 1) * dq], k_ref[h * dq:(h + 1) * dq, pl.ds(off, ATT_TILE)]) * scale
            hh = h if per_head_tab else 0
            s = s + jnp.concatenate(
                [jnp.concatenate([tab_ref[hh, d] for d in row], axis=1) for row in ds], axis=0)
            if has_mask:
                s = s + mk
            s_scr[h, :, pl.ds(off, ATT_TILE)] = s
            mvec = m_scr[h]
            for c in range(nch):
                mvec = jnp.maximum(mvec, s[:, c * LANES:(c + 1) * LANES])
            m_scr[h] = mvec
        return carry

    paired_loop(logits_tile)
    for h in range(hg):
        m_scr[h] = jnp.broadcast_to(jnp.max(m_scr[h], axis=1, keepdims=True), (bq, LANES))

    def exp_tile(j, carry):
        off = pl.multiple_of(j * ATT_TILE, ATT_TILE)
        for h in range(hg):
            s = s_scr[h, :, pl.ds(off, ATT_TILE)]
            m = m_scr[h]
            p = [jnp.exp2(s[:, c * LANES:(c + 1) * LANES] - m) for c in range(nch)]
            lvec = l_scr[h]
            for c in range(nch):
                lvec = lvec + p[c]
            l_scr[h] = lvec
            p_scr[h, :, pl.ds(off, ATT_TILE)] = jnp.concatenate(p, axis=1).astype(BF16)
        return carry

    lax.fori_loop(0, nkt, exp_tile, 0)

    def pv_tile(j, carry):
        off = pl.multiple_of(j * ATT_TILE, ATT_TILE)
        for h in range(hg):
            acc_scr[h] = acc_scr[h] + _dot(p_scr[h, :, pl.ds(off, ATT_TILE)],
                                           v_ref[pl.ds(off, ATT_TILE), h * dv:(h + 1) * dv])
        return carry

    paired_loop(pv_tile)
    for h in range(hg):
        l_row = jnp.sum(l_scr[h], axis=1, keepdims=True)
        out_ref[:, h * dv:(h + 1) * dv] = (acc_scr[h] / l_row).astype(out_ref.dtype)


def _attn_body(*refs, n_seg, has_mask, per_head_tab, bq, hg, dq, dv, scale, pos0, seg_tiles, by_head):
    q_ref = refs[0]
    kv = refs[1:1 + 2 * n_seg]
    pos = 1 + 2 * n_seg
    tab_ref = refs[pos]
    pos += 1
    mask_ref = refs[pos] if has_mask else None
    pos += 1 if has_mask else 0
    out_ref = refs[pos]
    s_scr, p_scr = refs[pos + 1:pos + 3]

    b = pl.program_id(2)
    t0 = pos0 + b * bq
    qblk = t0 // KEY_BLOCK

    def lane_pad(x, fill):
        w = x.shape[1]
        return x if w == LANES else jnp.concatenate([x, jnp.full((bq, LANES - w), fill, F32)], axis=1)

    mvec = [jnp.full((bq, LANES), NEG, F32) for _ in range(hg)]
    for (si, row0, key0, width) in seg_tiles:
        nch = max(1, width // KEY_BLOCK)
        cw = min(width, KEY_BLOCK)
        ds = [jnp.clip(key0 // KEY_BLOCK + c - qblk + 2, 0, 3) for c in range(nch)]
        k_t = None if by_head[si] else kv[2 * si][row0:row0 + width, :].astype(BF16)
        if has_mask:
            mk = mask_ref[:, key0:key0 + width].astype(F32)
        for h in range(hg):
            hh = h if per_head_tab else 0
            if by_head[si]:
                k_h = kv[2 * si][pl.ds(row0 * hg + h, width, stride=hg), :].astype(BF16)
            else:
                k_h = k_t[:, h * dq:(h + 1) * dq]
            s = _dot_nt(q_ref[:, h * dq:(h + 1) * dq], k_h) * scale
            bias = [tab_ref[hh, ds[c], :bq, :cw] for c in range(nch)]
            s = s + (jnp.concatenate(bias, axis=1) if nch > 1 else bias[0])
            if has_mask:
                s = s + mk
            s_scr[h, :, key0:key0 + width] = s
            for c in range(nch):
                mvec[h] = jnp.maximum(mvec[h], lane_pad(s[:, c * cw:(c + 1) * cw], NEG))

    m = [jnp.broadcast_to(jnp.max(mvec[h], axis=1, keepdims=True), (bq, LANES)) for h in range(hg)]
    lvec = [jnp.zeros((bq, LANES), F32) for _ in range(hg)]
    for (si, row0, key0, width) in seg_tiles:
        nch = max(1, width // KEY_BLOCK)
        cw = min(width, KEY_BLOCK)
        for h in range(hg):
            s = s_scr[h, :, key0:key0 + width]
            p = [jnp.exp2(s[:, c * cw:(c + 1) * cw] - m[h][:, :cw]) for c in range(nch)]
            for c in range(nch):
                lvec[h] = lvec[h] + lane_pad(p[c], 0.0)
            p_scr[h, :, key0:key0 + width] = (jnp.concatenate(p, axis=1) if nch > 1 else p[0]).astype(BF16)

    acc = [jnp.zeros((bq, dv), F32) for _ in range(hg)]
    for (si, row0, key0, width) in seg_tiles:
        v_t = None if by_head[si] else kv[2 * si + 1][row0:row0 + width, :].astype(BF16)
        for h in range(hg):
            if by_head[si]:
                v_h = kv[2 * si + 1][pl.ds(row0 * hg + h, width, stride=hg), :].astype(BF16)
            else:
                v_h = v_t[:, h * dv:(h + 1) * dv]
            acc[h] = acc[h] + _dot(p_scr[h, :, key0:key0 + width], v_h)

    for h in range(hg):
        l_row = jnp.sum(lvec[h], axis=1, keepdims=True)
        out_ref[:, h * dv:(h + 1) * dv] = (acc[h] / l_row).astype(out_ref.dtype)


def attention(q, segs, tab, mask, *, n_seq, sq, bq, q_row0, q_col0, n_heads, hg, dq, dv, scale, pos0,
              seg_tiles=None, dyn_sk=None, name="attention"):
    nqb = sq // bq
    ng = n_heads // hg
    qb0 = q_row0 // bq
    assert q_row0 % bq == 0 and q_col0 % (hg * dq) == 0 and pos0 % KEY_BLOCK == 0
    scale = scale * LOG2E
    assert bq % KEY_BLOCK == 0 or nqb == 1
    qc0 = q_col0 // (hg * dq)
    in_specs = [pl.BlockSpec((bq, hg * dq), lambda s, g, b: (qb0 + s * nqb + b, qc0 + g))]
    args = [q]
    by_head = []
    for (k, v, rows, kc0, vc0) in segs:
        assert kc0 % (hg * dq) == 0 and vc0 % (hg * dv) == 0
        by_head.append(k.shape[1] == dq and n_heads > 1)
        if by_head[-1]:
            assert hg == n_heads and dq == dv and k.shape[0] == n_seq * rows * n_heads and dyn_sk is None
            for _ in range(2):
                in_specs.append(pl.BlockSpec((rows * n_heads, dq), lambda s, g, b: (s, 0)))
            args += [k, v]
            continue
        mode = dict(pipeline_mode=pl.Buffered(1)) if dyn_sk is not None else {}
        if dyn_sk is not None:
            assert kc0 == 0 and k.shape == (n_seq * n_heads * dq, rows)
            in_specs.append(pl.BlockSpec((hg * dq, rows), lambda s, g, b: (s * ng + g, 0), **mode))
        else:
            in_specs.append(pl.BlockSpec((rows, hg * dq), lambda s, g, b, c=kc0 // (hg * dq): (s, c + g)))
        in_specs.append(pl.BlockSpec((rows, hg * dv), lambda s, g, b, c=vc0 // (hg * dv): (s, c + g), **mode))
        args += [k, v]
    per_head_tab = tab.shape[0] > 1
    if per_head_tab:
        in_specs.append(pl.BlockSpec((hg, 4, KEY_BLOCK, KEY_BLOCK), lambda s, g, b: (g, 0, 0, 0)))
    else:
        in_specs.append(pl.BlockSpec((1, 4, KEY_BLOCK, KEY_BLOCK), lambda s, g, b: (0, 0, 0, 0)))
    args.append(tab)
    if mask is not None:
        in_specs.append(pl.BlockSpec((bq, mask.shape[1]), lambda s, g, b: (s * nqb + b, 0)))
        args.append(mask)
    if dyn_sk is not None:
        assert len(segs) == 1 and dyn_sk % ATT_TILE == 0 and bq % KEY_BLOCK == 0
        body = functools.partial(_attn_pipe_body, has_mask=mask is not None, per_head_tab=per_head_tab,
                                 bq=bq, hg=hg, dq=dq, dv=dv, scale=scale, pos0=pos0, sk=dyn_sk)
        scratch = [pltpu.VMEM((hg, bq, dyn_sk), F32), pltpu.VMEM((hg, bq, dyn_sk), BF16),
                   pltpu.VMEM((hg, bq, LANES), F32), pltpu.VMEM((hg, bq, LANES), F32),
                   pltpu.VMEM((hg, bq, dv), F32)]
    else:
        body = functools.partial(_attn_body, n_seg=len(segs), has_mask=mask is not None,
                                 per_head_tab=per_head_tab, bq=bq, hg=hg, dq=dq, dv=dv, scale=scale, pos0=pos0,
                                 seg_tiles=seg_tiles, by_head=tuple(by_head))
        sk_tot = max(key0 + width for (_, _, key0, width) in seg_tiles)
        sk_tot = (sk_tot + LANES - 1) // LANES * LANES
        scratch = [pltpu.VMEM((hg, bq, sk_tot), F32), pltpu.VMEM((hg, bq, sk_tot), BF16)]
    return pl.pallas_call(
        body,
        grid=(n_seq, ng, nqb),
        in_specs=in_specs,
        out_specs=pl.BlockSpec((bq, hg * dv), lambda s, g, b: (s * nqb + b, g)),
        out_shape=jax.ShapeDtypeStruct((n_seq * sq, n_heads * dv), BF16),
        scratch_shapes=scratch,
        compiler_params=_cparams(("parallel", "parallel", "arbitrary"), VMEM_LIMIT),
        name=name,
    )(*args)


def _outproj_body(*refs, tn, na, nx, a_first, x_first):
    i = pl.program_id(0)
    j = pl.program_id(1)
    a = _pick_rows(i, a_first, refs[:na])
    w_ref = refs[na]
    x = _pick_rows(i, x_first, refs[na + 1:na + 1 + nx])
    g_ref, wr_ref, br_ref, h_ref, xn_ref, lg_ref, hrow, wb = refs[na + 1 + nx:]
    col = pl.ds(pl.multiple_of(j * tn, tn), tn)

    @pl.when(i == 0)
    def _():
        wb[:, col] = w_ref[0].astype(BF16)

    r = x + _dot(a, wb[:, col])
    h_ref[...] = r
    hrow[:, col] = r

    @pl.when(j == pl.num_programs(1) - 1)
    def _():
        xn = _rms(hrow[...], g_ref[...])
        tm = hrow.shape[0]
        for c in range(ROW_CHUNKS):
            xn_ref[pl.ds(c, tm, stride=ROW_CHUNKS), :] = xn[:, c * LANES:(c + 1) * LANES]
        lg_ref[...] = _dot(xn.astype(BF16), wr_ref[...].astype(BF16)) + br_ref[...]


def outproj_norm_router(a_parts, w_out, layer, x_parts, g_ffn, w_r, b_r, tm, tn):
    k = a_parts[0].shape[1]
    t = sum(p.shape[0] for p in a_parts)
    d = w_out.shape[-1]
    a_specs, a_first = _row_split_specs(a_parts, tm, k, lambda i, j: 0)
    x_specs, x_first = _row_split_specs(x_parts, tm, tn, lambda i, j: j)
    nj = d // tn
    return pl.pallas_call(
        functools.partial(_outproj_body, tn=tn, na=len(a_parts), nx=len(x_parts), a_first=a_first,
                          x_first=x_first),
        grid=(t // tm, d // tn),
        in_specs=a_specs + [pl.BlockSpec((1, k, tn), lambda i, j: (layer, 0, jnp.where(i == 0, j, nj - 1)))] + x_specs + [
            pl.BlockSpec((1, d), lambda i, j: (0, 0)),
            pl.BlockSpec((d, LANES), lambda i, j: (0, 0)),
            pl.BlockSpec((1, LANES), lambda i, j: (0, 0))],
        out_specs=[pl.BlockSpec((tm, tn), lambda i, j: (i, j)),
                   pl.BlockSpec((tm * ROW_CHUNKS, LANES), lambda i, j: (i, 0)),
                   pl.BlockSpec((tm, LANES), lambda i, j: (i, 0))],
        out_shape=[jax.ShapeDtypeStruct((t, d), F32), jax.ShapeDtypeStruct((t * ROW_CHUNKS, LANES), F32),
                   jax.ShapeDtypeStruct((t, LANES), F32)],
        scratch_shapes=[pltpu.VMEM((tm, d), F32), pltpu.VMEM((k, d), BF16)],
        compiler_params=_cparams(("arbitrary", "arbitrary"), VMEM_LIMIT),
        name="outproj_norm_router",
    )(*a_parts, w_out, *x_parts, g_ffn.reshape(1, d), w_r, b_r)


def _routing_body(lg_ref, out_ref, cnt_ref, carry, *, tm):
    i = pl.program_id(0)

    @pl.when(i == 0)
    def _():
        carry[...] = jnp.zeros(carry.shape, F32)

    x = lg_ref[...]
    lane = lax.broadcasted_iota(I32, (tm, LANES), 1)
    neg_inf = -jnp.inf

    def rmax(v):
        return jnp.max(v, axis=1, keepdims=True)

    def rmin(v):
        return jnp.min(v, axis=1, keepdims=True)

    def rsum(v):
        return jnp.sum(v, axis=1, keepdims=True)

    gm = lane < N_GROUPS
    gmax = rmax(jnp.where(gm, x, neg_inf))
    gsel = rmin(jnp.where(gm, jnp.where(x == gmax, lane, LANES), LANES))
    gsum = rsum(jnp.where(gm, jnp.exp(x - gmax), 0.0))
    g_w = 1.0 / gsum
    lo = N_GROUPS + gsel * EXPERTS_PER_GROUP
    em = jnp.logical_and(lane >= lo, lane < lo + EXPERTS_PER_GROUP)
    emax = rmax(jnp.where(em, x, neg_inf))
    ee = jnp.where(em, jnp.exp(x - emax), 0.0)
    p = ee / rsum(ee)
    p1 = rmax(jnp.where(em, p, -1.0))
    i1 = rmin(jnp.where(em, jnp.where(p == p1, lane, LANES), LANES))
    em2 = jnp.logical_and(em, lane != i1)
    p2 = rmax(jnp.where(em2, p, -1.0))
    i2 = rmin(jnp.where(em2, jnp.where(p == p2, lane, LANES), LANES))
    den = p1 + p2
    g1 = g_w * (p1 / den)
    g2 = g_w * (p2 / den)

    oh1 = jnp.where(lane == i1, 1.0, 0.0)
    oh2 = jnp.where(lane == i2, 1.0, 0.0)
    oh = oh1 + oh2
    r = lax.broadcasted_iota(I32, (tm, tm), 0)
    c = lax.broadcasted_iota(I32, (tm, tm), 1)
    lower = jnp.where(c < r, 1.0, 0.0).astype(BF16)
    before = _dot(lower, oh.astype(BF16)) + carry[...]
    rank1 = rsum(oh1 * before)
    rank2 = rsum(oh2 * before)
    carry[...] = carry[...] + jnp.sum(oh, axis=0, keepdims=True)

    e1 = (i1 - N_GROUPS).astype(F32)
    e2 = (i2 - N_GROUPS).astype(F32)
    vals = [e1, e2, rank1, rank2, g1, g2]
    out = jnp.zeros((tm, LANES), F32)
    for k, v in enumerate(vals):
        out = jnp.where(lane == k, v, out)
    out_ref[...] = out
    cnt_ref[...] = carry[...]


def moe_routing(logits, tm):
    t = logits.shape[0]
    return pl.pallas_call(
        functools.partial(_routing_body, tm=tm),
        grid=(t // tm,),
        in_specs=[pl.BlockSpec((tm, LANES), lambda i: (i, 0))],
        out_specs=[pl.BlockSpec((tm, LANES), lambda i: (i, 0)), pl.BlockSpec((1, LANES), lambda i: (0, 0))],
        out_shape=[jax.ShapeDtypeStruct((t, LANES), F32), jax.ShapeDtypeStruct((1, LANES), F32)],
        scratch_shapes=[pltpu.VMEM((1, LANES), F32)],
        compiler_params=_cparams(("arbitrary",)),
        name="moe_routing",
    )(logits)


def _slot_tokens_body(slots_ref, init_hbm, tok_ref, sem):
    init = pltpu.make_async_copy(init_hbm, tok_ref, sem)
    init.start()
    init.wait()

    def put(a, c):
        tok_ref[slots_ref[a]] = lax.shift_right_logical(a, 1)
        return c

    lax.fori_loop(0, slots_ref.shape[0], put, 0, unroll=8)


def moe_slot_tokens(slots, n_slots):
    init = jnp.arange(n_slots, dtype=I32) % (slots.shape[0] // 2)
    return pl.pallas_call(
        _slot_tokens_body,
        in_specs=[pl.BlockSpec(memory_space=pltpu.SMEM), pl.BlockSpec(memory_space=pl.ANY)],
        out_specs=pl.BlockSpec(memory_space=pltpu.SMEM),
        out_shape=jax.ShapeDtypeStruct((n_slots,), I32),
        scratch_shapes=[pltpu.SemaphoreType.DMA],
        name="moe_slot_tokens",
    )(slots, init)


def _dispatch_body(nslot_ref, tok_ref, x_hbm, xs_ref, buf, sem, *, rows):
    i = pl.program_id(0)
    ni = pl.num_programs(0)
    n_used = nslot_ref[0]

    def issue(blk, s):
        @pl.when(blk * rows < n_used)
        def _():
            def pair(rp, carry):
                for k in range(2):
                    r = 2 * rp + k
                    src = pl.multiple_of(tok_ref[blk * rows + r] * ROW_CHUNKS, ROW_CHUNKS)
                    dst = pl.multiple_of(r * ROW_CHUNKS, ROW_CHUNKS)
                    pltpu.make_async_copy(x_hbm.at[pl.ds(src, ROW_CHUNKS), :],
                                          buf.at[s, pl.ds(dst, ROW_CHUNKS), :], sem.at[s]).start(priority=k)
                return carry

            lax.fori_loop(0, rows // 2, pair, 0, unroll=4)

    @pl.when(i == 0)
    def _():
        issue(0, 0)

    @pl.when(i + 1 < ni)
    def _():
        issue(i + 1, (i + 1) % 2)

    slot = i % 2

    @pl.when(i * rows < n_used)
    def _():
        pltpu.make_async_copy(x_hbm.at[pl.ds(0, rows * ROW_CHUNKS), :], buf.at[slot], sem.at[slot]).wait()
        for j in range(ROW_CHUNKS):
            xs_ref[:, j * LANES:(j + 1) * LANES] = \
                buf[slot, pl.ds(j, rows, stride=ROW_CHUNKS), :].astype(xs_ref.dtype)

    @pl.when(i * rows >= n_used)
    def _():
        xs_ref[...] = jnp.zeros(xs_ref.shape, xs_ref.dtype)


def moe_dispatch(xn, slot_tok, n_used, rows):
    assert xn.shape[1] == LANES
    d = ROW_CHUNKS * LANES
    n_slots = slot_tok.shape[0]
    assert n_slots % rows == 0
    grid_spec = pltpu.PrefetchScalarGridSpec(
        num_scalar_prefetch=2,
        grid=(n_slots // rows,),
        in_specs=[pl.BlockSpec(memory_space=pl.ANY)],
        out_specs=pl.BlockSpec((rows, d), lambda i, ns, tk: (i, 0)),
        scratch_shapes=[pltpu.VMEM((2, rows * ROW_CHUNKS, LANES), F32), pltpu.SemaphoreType.DMA((2,))],
    )
    return pl.pallas_call(
        functools.partial(_dispatch_body, rows=rows),
        grid_spec=grid_spec,
        out_shape=jax.ShapeDtypeStruct((n_slots, d), BF16),
        compiler_params=_cparams(("arbitrary",), VMEM_LIMIT),
        name="moe_dispatch",
    )(n_used, slot_tok, xn)


def _experts_body(blk0_ref, nblk_ref, ntot_ref, xs_hbm, wg_ref, wu_ref, wd_ref, ys_hbm,
                  xbuf, ybuf, zbuf, xsem, ysem, zsem, wgb, wub, wdb, *, bm):
    e = pl.program_id(0)
    n_total = ntot_ref[0]

    def x_copy(g, s):
        return pltpu.make_async_copy(xs_hbm.at[pl.ds(g * bm, bm), :], xbuf.at[s], xsem.at[s])

    def y_copy(g, s):
        return pltpu.make_async_copy(ybuf.at[s], ys_hbm.at[pl.ds(g * bm, bm), :], ysem.at[s])

    for g0 in range(X_AHEAD):
        @pl.when(jnp.logical_and(e == 0, n_total > g0))
        def _():
            x_copy(g0, g0).start(priority=1)

    @pl.when(nblk_ref[e] > 0)
    def _():
        wgb[...] = wg_ref[0, 0].astype(BF16)
        wub[...] = wu_ref[0, 0].astype(BF16)
        wdb[...] = wd_ref[0, 0].astype(BF16)

    def block(i, carry):
        g = blk0_ref[e] + i
        s = g % 2
        sx = g % (X_AHEAD + 1)

        @pl.when(g + X_AHEAD < n_total)
        def _():
            x_copy(g + X_AHEAD, (g + X_AHEAD) % (X_AHEAD + 1)).start(priority=1)

        x_copy(g, sx).wait()
        x = xbuf[sx]
        gt = _dot(x, wgb[...])
        up = _dot(x, wub[...])
        a = (gt * (1.0 / (1.0 + jnp.exp(-gt)))) * up
        y = _dot(a.astype(BF16), wdb[...])

        @pl.when(g >= 2)
        def _():
            y_copy(g - 2, s).wait()

        ybuf[s] = y
        y_copy(g, s).start()
        return carry

    lax.fori_loop(0, nblk_ref[e], block, 0)

    @pl.when(e == pl.num_programs(0) - 1)
    def _():
        for back in (2, 1):
            @pl.when(n_total >= back)
            def _():
                y_copy(n_total - back, (n_total - back) % 2).wait()

        def z_copy(g):
            return pltpu.make_async_copy(zbuf, ys_hbm.at[pl.ds(g * bm, bm), :], zsem)

        zbuf[...] = jnp.zeros(zbuf.shape, F32)
        n_blocks = ys_hbm.shape[0] // bm
        lax.fori_loop(n_total, n_blocks, lambda g, c: (z_copy(g).start(), c)[1], 0)
        lax.fori_loop(n_total, n_blocks, lambda g, c: (z_copy(g).wait(), c)[1], 0)


def moe_experts(xs, blk0, nblk, n_total, w_gate, w_up, w_down, layer, bm):
    n_slots, d = xs.shape
    n_exp = blk0.shape[0]
    de = w_gate.shape[-1]
    grid_spec = pltpu.PrefetchScalarGridSpec(
        num_scalar_prefetch=3,
        grid=(n_exp,),
        in_specs=[pl.BlockSpec(memory_space=pl.ANY),
                  pl.BlockSpec((1, 1, d, de), lambda e, b0, nb, nt: (layer, e, 0, 0)),
                  pl.BlockSpec((1, 1, d, de), lambda e, b0, nb, nt: (layer, e, 0, 0)),
                  pl.BlockSpec((1, 1, de, d), lambda e, b0, nb, nt: (layer, e, 0, 0))],
        out_specs=pl.BlockSpec(memory_space=pl.ANY),
        scratch_shapes=[pltpu.VMEM((X_AHEAD + 1, bm, d), BF16), pltpu.VMEM((2, bm, d), F32),
                        pltpu.VMEM((bm, d), F32),
                        pltpu.SemaphoreType.DMA((X_AHEAD + 1,)), pltpu.SemaphoreType.DMA((2,)),
                        pltpu.SemaphoreType.DMA,
                        pltpu.VMEM((d, de), BF16), pltpu.VMEM((d, de), BF16), pltpu.VMEM((de, d), BF16)],
    )
    return pl.pallas_call(
        functools.partial(_experts_body, bm=bm),
        grid_spec=grid_spec,
        out_shape=jax.ShapeDtypeStruct((n_slots, d), F32),
        compiler_params=_cparams(("arbitrary",), VMEM_LIMIT),
        name="moe_experts",
    )(blk0, nblk, n_total, xs, w_gate, w_up, w_down)


def _combine_body(slot_ref, ys_hbm, h_ref, rt_ref, g_ref, *rest, tm, tile0, want_h):
    outs, (buf, sem) = rest[:-2], rest[-2:]
    i = pl.program_id(0)
    ni = pl.num_programs(0)

    def row_copy(src_row, k, r, s):
        return pltpu.make_async_copy(ys_hbm.at[pl.ds(src_row, 1), :], buf.at[s, k, pl.ds(r, 1), :], sem.at[s])

    def issue(tile, s):
        def one(r, carry):
            base = ((tile0 + tile) * tm + r) * 2
            row_copy(slot_ref[base], 0, r, s).start(priority=0)
            row_copy(slot_ref[base + 1], 1, r, s).start(priority=1)
            return carry

        lax.fori_loop(0, tm, one, 0, unroll=8)

    @pl.when(i == 0)
    def _():
        issue(0, 0)

    @pl.when(i + 1 < ni)
    def _():
        issue(i + 1, (i + 1) % 2)

    slot = i % 2
    for k in range(2):
        pltpu.make_async_copy(ys_hbm.at[pl.ds(0, tm), :], buf.at[slot, k], sem.at[slot]).wait()

    rt = rt_ref[...]
    g1 = rt[:, 4:5]
    g2 = rt[:, 5:6]
    h2 = h_ref[...] + (buf[slot, 0] * g1 + buf[slot, 1] * g2)
    if want_h:
        outs[0][...] = h2
    outs[-1][...] = _rms(h2, g_ref[...]).astype(outs[-1].dtype)


def moe_combine(ys, slots, h1, route, g_next, tm, u_dtype, row0=0, nrows=None, want_h=True):
    d = h1.shape[1]
    t = h1.shape[0] if nrows is None else nrows
    assert row0 % tm == 0 and t % tm == 0
    tile0 = row0 // tm
    row_spec = pl.BlockSpec((tm, d), lambda i, sl: (i, 0))
    grid_spec = pltpu.PrefetchScalarGridSpec(
        num_scalar_prefetch=1,
        grid=(t // tm,),
        in_specs=[pl.BlockSpec(memory_space=pl.ANY),
                  pl.BlockSpec((tm, d), lambda i, sl: (i + tile0, 0)),
                  pl.BlockSpec((tm, LANES), lambda i, sl: (i + tile0, 0)),
                  pl.BlockSpec((1, d), lambda i, sl: (0, 0))],
        out_specs=[row_spec, row_spec] if want_h else [row_spec],
        scratch_shapes=[pltpu.VMEM((2, 2, tm, d), F32), pltpu.SemaphoreType.DMA((2,))],
    )
    shapes = [jax.ShapeDtypeStruct((t, d), F32)] if want_h else []
    return pl.pallas_call(
        functools.partial(_combine_body, tm=tm, tile0=tile0, want_h=want_h),
        grid_spec=grid_spec,
        out_shape=shapes + [jax.ShapeDtypeStruct((t, d), u_dtype)],
        compiler_params=_cparams(("arbitrary",), VMEM_LIMIT),
        name="moe_combine",
    )(slots, ys, h1, route, g_next.reshape(1, d))


def hier_moe_layer(h1, xn, logits, w_gate, w_up, w_down, layer, g_next, u_dtype, split=None):
    t = h1.shape[0]
    bm = MOE_BM
    route, counts = moe_routing(logits, _tile(t, 512))
    e = route[:, 0:2].astype(I32)
    rank = route[:, 2:4].astype(I32)
    counts = counts[0, N_GROUPS:N_GROUPS + N_EXPERTS].astype(I32)
    padded = (counts + bm - 1) // bm * bm
    pad_end = jnp.cumsum(padded)
    pad_start = pad_end - padded
    start_of = jnp.sum(jnp.where(e[..., None] == jnp.arange(N_EXPERTS, dtype=I32), pad_start, 0), axis=-1)
    slots = (start_of + rank).reshape(-1)
    nb = (2 * t) // bm + N_EXPERTS
    slot_tok = moe_slot_tokens(slots, nb * bm)
    n_used = pad_end[-1:]
    xs = moe_dispatch(xn, slot_tok, n_used, bm * max(f for f in (8, 4, 2, 1) if nb % f == 0))
    ys = moe_experts(xs, pad_start // bm, padded // bm, n_used // bm, w_gate, w_up, w_down, layer, bm)
    tmc = _tile(math.gcd(t, split or t), 512)
    if split is None:
        return moe_combine(ys, slots, h1, route, g_next, tmc, u_dtype)
    (u_a,) = moe_combine(ys, slots, h1, route, g_next, tmc, u_dtype, 0, split, want_h=False)
    (u_b,) = moe_combine(ys, slots, h1, route, g_next, tmc, u_dtype, split, t - split, want_h=False)
    return u_a, u_b


def _swap_halves(x):
    lane = lax.broadcasted_iota(I32, x.shape, 1)
    return jnp.where(lane < QK_ROPE // 2, pltpu.roll(x, LANES - QK_ROPE // 2, 1), pltpu.roll(x, QK_ROPE // 2, 1))


def _mla_prep_body(p_ref, gq_ref, gkv_ref, cos_ref, sin_ref, cq_ref, ckv_ref, ckvb_ref, kr_ref):
    p = p_ref[...]
    cq_ref[...] = _rms(p[:, :Q_LORA], gq_ref[...]).astype(cq_ref.dtype)
    ckv = _rms(p[:, Q_LORA:Q_LORA + KV_LORA], gkv_ref[...])
    ckv_ref[...] = ckv
    ckvb_ref[...] = ckv.astype(ckvb_ref.dtype)
    kr = p[:, Q_LORA + KV_LORA:]
    kr_ref[...] = kr * cos_ref[...] + _swap_halves(kr) * sin_ref[...]


def mla_prep(proj, g_q, g_kv, cos, sin, tm):
    t = proj.shape[0]
    return pl.pallas_call(
        _mla_prep_body,
        grid=(t // tm,),
        in_specs=[pl.BlockSpec((tm, proj.shape[1]), lambda i: (i, 0)),
                  pl.BlockSpec((1, Q_LORA), lambda i: (0, 0)),
                  pl.BlockSpec((1, KV_LORA), lambda i: (0, 0)),
                  pl.BlockSpec((tm, LANES), lambda i: (i, 0)),
                  pl.BlockSpec((tm, LANES), lambda i: (i, 0))],
        out_specs=[pl.BlockSpec((tm, Q_LORA), lambda i: (i, 0)),
                   pl.BlockSpec((tm, KV_LORA), lambda i: (i, 0)),
                   pl.BlockSpec((tm, KV_LORA), lambda i: (i, 0)),
                   pl.BlockSpec((tm, LANES), lambda i: (i, 0))],
        out_shape=[jax.ShapeDtypeStruct((t, Q_LORA), BF16), jax.ShapeDtypeStruct((t, KV_LORA), F32),
                   jax.ShapeDtypeStruct((t, KV_LORA), BF16), jax.ShapeDtypeStruct((t, LANES), F32)],
        compiler_params=_cparams(("parallel",)),
        name="mla_prep",
    )(proj, g_q.reshape(1, -1), g_kv.reshape(1, -1), cos, sin)


def _mla_q_body(cq_ref, w_ref, cos_ref, sin_ref, q_ref, wb):
    @pl.when(pl.program_id(0) == 0)
    def _():
        wb[...] = w_ref[...].astype(BF16)

    cq = cq_ref[...]
    cos = cos_ref[...]
    sin = sin_ref[...]
    for h in range(B_HEADS):
        y = _dot(cq, wb[:, h * MLA_QK_PAD:(h + 1) * MLA_QK_PAD])
        xr = y[:, LANES:]
        q_ref[:, h * MLA_QK_PAD:h * MLA_QK_PAD + LANES] = y[:, :LANES].astype(q_ref.dtype)
        q_ref[:, h * MLA_QK_PAD + LANES:(h + 1) * MLA_QK_PAD] = \
            (xr * cos + _swap_halves(xr) * sin).astype(q_ref.dtype)


def mla_q_up(cq, w_uq_pad, cos, sin, tm):
    t = cq.shape[0]
    n = B_HEADS * MLA_QK_PAD
    return pl.pallas_call(
        _mla_q_body,
        grid=(t // tm,),
        in_specs=[pl.BlockSpec((tm, Q_LORA), lambda i: (i, 0)),
                  pl.BlockSpec((Q_LORA, n), lambda i: (0, 0)),
                  pl.BlockSpec((tm, LANES), lambda i: (i, 0)),
                  pl.BlockSpec((tm, LANES), lambda i: (i, 0))],
        out_specs=pl.BlockSpec((tm, n), lambda i: (i, 0)),
        out_shape=jax.ShapeDtypeStruct((t, n), BF16),
        scratch_shapes=[pltpu.VMEM((Q_LORA, n), BF16)],
        compiler_params=_cparams(("arbitrary",), VMEM_LIMIT),
        name="mla_q_up",
    )(cq, w_uq_pad, cos, sin)


def _mla_kv_body(c_ref, kr_ref, wk_ref, wv_ref, k_ref, v_ref, wkb, wvb):
    @pl.when(pl.program_id(0) == 0)
    def _():
        wkb[...] = wk_ref[0].astype(BF16)
        wvb[...] = wv_ref[0].astype(BF16)

    c = c_ref[...].astype(BF16)
    kr = kr_ref[...].astype(k_ref.dtype)
    kn = _dot(c, wkb[...])
    for h in range(B_HEADS):
        k_ref[:, h * MLA_QK_PAD:h * MLA_QK_PAD + LANES] = kn[:, h * QK_NOPE:(h + 1) * QK_NOPE].astype(k_ref.dtype)
        k_ref[:, h * MLA_QK_PAD + LANES:(h + 1) * MLA_QK_PAD] = kr
    v_ref[...] = _dot(c, wvb[...]).astype(v_ref.dtype)


def mla_kv_up(ckv, kr, w_uk, w_uv, layer, tm):
    t = ckv.shape[0]
    nk = B_HEADS * MLA_QK_PAD
    nv = B_HEADS * V_DIM
    return pl.pallas_call(
        _mla_kv_body,
        grid=(t // tm,),
        in_specs=[pl.BlockSpec((tm, KV_LORA), lambda i: (i, 0)),
                  pl.BlockSpec((tm, LANES), lambda i: (i, 0)),
                  pl.BlockSpec((1, KV_LORA, B_HEADS * QK_NOPE), lambda i: (layer, 0, 0)),
                  pl.BlockSpec((1, KV_LORA, nv), lambda i: (layer, 0, 0))],
        out_specs=[pl.BlockSpec((tm, nk), lambda i: (i, 0)), pl.BlockSpec((tm, nv), lambda i: (i, 0))],
        out_shape=[jax.ShapeDtypeStruct((t, nk), BF16), jax.ShapeDtypeStruct((t, nv), BF16)],
        scratch_shapes=[pltpu.VMEM((KV_LORA, B_HEADS * QK_NOPE), BF16), pltpu.VMEM((KV_LORA, nv), BF16)],
        compiler_params=_cparams(("arbitrary",), VMEM_LIMIT),
        name="mla_kv_up",
    )(ckv, kr, w_uk, w_uv)


def kernel(x_prompt, x_sample, cache_a_k, cache_a_v, cache_a_kidx, cache_b_ckv, cache_b_krope, norm_mix, norm_ffn, norm_final, rel_bias, a_w_in, a_w_out, b_w_in, b_norm_q, b_norm_kv, b_w_uq, b_w_uk, b_w_uv, b_w_out, moe_w_grp, moe_b_grp, moe_w_rtr, moe_b_rtr, moe_w_gate, moe_w_up, moe_w_down):
    n_p, s_p, d = x_prompt.shape
    n_s, s_s, _ = x_sample.shape
    past = cache_a_k.shape[2]
    tp = n_p * s_p
    ts = n_s * s_s
    t = tp + ts
    a_qd = A_HEADS * A_HEAD_DIM
    tm = _tile(math.gcd(tp, ts), 1024)
    tm2 = _tile(math.gcd(tp, ts), 512)

    x_parts = [x_prompt.reshape(tp, d), x_sample.reshape(ts, d)]

    def router_params(i):
        w_r = jnp.concatenate([moe_w_grp[i], moe_w_rtr[i],
                               jnp.zeros((d, LANES - N_GROUPS - N_EXPERTS), F32)], axis=1)
        b_r = jnp.concatenate([moe_b_grp[i], moe_b_rtr[i],
                               jnp.zeros((LANES - N_GROUPS - N_EXPERTS,), F32)]).reshape(1, LANES)
        return w_r, b_r

    u0 = rmsnorm(x_parts, norm_mix[0], tm)
    w_a = a_w_in[0]
    (q_b,) = matmul(u0, w_a, 0, a_qd, [BF16], tm, 512, "a_proj_q")
    tmp, tms = _tile(tp, 1024), _tile(ts, 1024)
    kp_f, kp_b = matmul(u0, w_a, a_qd, a_qd, [F32, BF16], tmp, 512, "a_proj_k", 0, tp)
    vp_f, vp_b = matmul(u0, w_a, 2 * a_qd, a_qd, [F32, BF16], tmp, 512, "a_proj_v", 0, tp)
    ks_f, ks_b = matmul(u0, w_a, a_qd, a_qd, [F32, BF16], tms, 512, "a_proj_k", tp, ts)
    vs_f, vs_b = matmul(u0, w_a, 2 * a_qd, a_qd, [F32, BF16], tms, 512, "a_proj_v", tp, ts)
    (qi_b,) = matmul(u0, w_a, 3 * a_qd, IDX_HEADS * IDX_DIM, [BF16], tm, 512, "a_proj_qi")
    c_ki = 3 * a_qd + IDX_HEADS * IDX_DIM
    w_ki = w_a[:, c_ki:c_ki + IDX_DIM]
    w_wi = w_a[:, c_ki + IDX_DIM:c_ki + IDX_DIM + IDX_HEADS]
    zk = jnp.zeros((d, IDX_DIM), F32)
    w_tail = jnp.concatenate([w_ki, zk, zk, w_ki, w_wi, jnp.zeros((d, LANES - IDX_HEADS), F32)], axis=1)
    (tail,) = matmul(u0, w_tail, 0, 3 * LANES, [F32], tm, 3 * LANES, "a_proj_tail")
    kidx = tail[:, :IDX_DIM]

    bias_tab = rel_bias_tables(rel_bias)
    a_scale = A_HEAD_DIM ** -0.5
    mask_p = dsa_select(qi_b, tail, 2, tail, n_seq=n_p, sq=s_p, bq=128, sk=s_p, sk_real=s_p, tk=512,
                        pos0=0, q_row0=0)
    kp_t = kp_b.reshape(n_p, s_p, a_qd).transpose(0, 2, 1).reshape(n_p * a_qd, s_p)
    att_p = attention(q_b, [(kp_t, vp_b, s_p, 0, 0)], bias_tab, mask_p, n_seq=n_p, sq=s_p, bq=ATT_BQ, q_row0=0,
                      q_col0=0, n_heads=A_HEADS, hg=ATT_HG, dq=A_HEAD_DIM, dv=A_HEAD_DIM, scale=a_scale, pos0=0,
                      dyn_sk=s_p, name="dsa_attention_prompt")
    sk_s = past + s_s
    sk_pad = (sk_s + ATT_TILE - 1) // ATT_TILE * ATT_TILE
    ki_past = cache_a_kidx[0]
    zp = jnp.zeros_like(ki_past)
    kc_past = jnp.concatenate([ki_past, zp, zp, ki_past], axis=-1)
    kc_new = tail[tp:, :2 * LANES].reshape(n_s, s_s, 2 * LANES)
    kc_s = jnp.concatenate([kc_past, kc_new, jnp.zeros((n_s, sk_pad - sk_s, 2 * LANES), F32)], axis=1)
    mask_s = dsa_select(qi_b, tail, 2, kc_s.reshape(n_s * sk_pad, 2 * LANES), n_seq=n_s, sq=s_s, bq=s_s,
                        sk=sk_pad, sk_real=sk_s, tk=sk_pad, pos0=past, q_row0=tp)
    tiles_s = [(0, r, r, ATT_TILE) for r in range(0, past, ATT_TILE)] + [(1, 0, past, s_s)]
    att_s = attention(q_b, [(cache_a_k[0].reshape(n_s * past * A_HEADS, A_HEAD_DIM),
                             cache_a_v[0].reshape(n_s * past * A_HEADS, A_HEAD_DIM), past, 0, 0),
                            (ks_b, vs_b, s_s, 0, 0)],
                      bias_tab, mask_s, n_seq=n_s, sq=s_s, bq=s_s, q_row0=tp, q_col0=0, n_heads=A_HEADS, hg=A_HEADS,
                      dq=A_HEAD_DIM, dv=A_HEAD_DIM, scale=a_scale, pos0=past, seg_tiles=tiles_s,
                      name="dsa_attention_sample")

    w_r0, b_r0 = router_params(0)
    h1, xn0, lg0 = outproj_norm_router([att_p, att_s], a_w_out, 0, x_parts, norm_ffn[0], w_r0, b_r0, tm2, 512)
    h2, u1 = hier_moe_layer(h1, xn0, lg0, moe_w_gate, moe_w_up, moe_w_down, 0, norm_mix[1], BF16)

    n_in = Q_LORA + KV_LORA + QK_ROPE
    w_b = jnp.concatenate([b_w_in[0], jnp.zeros((d, LANES - QK_ROPE), F32)], axis=1)
    (proj,) = matmul(u1, w_b, 0, n_in + LANES - QK_ROPE, [F32], tm, 384, "b_proj")
    half = QK_ROPE // 2
    inv = ROPE_THETA ** (-jnp.arange(half, dtype=F32) / half)
    pos_all = jnp.concatenate([jnp.tile(jnp.arange(s_p, dtype=I32), n_p),
                               jnp.tile(past + jnp.arange(s_s, dtype=I32), n_s)])
    ang = pos_all.astype(F32)[:, None] * inv[None, :]
    zl = jnp.zeros((t, LANES - QK_ROPE), F32)
    cos_t = jnp.concatenate([jnp.cos(ang), jnp.cos(ang), zl], axis=1)
    sin_t = jnp.concatenate([-jnp.sin(ang), jnp.sin(ang), zl], axis=1)
    cq_b, ckv_f, ckv_b, kr_f = mla_prep(proj, b_norm_q[0], b_norm_kv[0], cos_t, sin_t, tm)
    w_uq_pad = jnp.pad(b_w_uq[0].reshape(Q_LORA, B_HEADS, QK_NOPE + QK_ROPE),
                       ((0, 0), (0, 0), (0, MLA_QK_PAD - QK_NOPE - QK_ROPE))).reshape(Q_LORA, B_HEADS * MLA_QK_PAD)
    qm = mla_q_up(cq_b, w_uq_pad, cos_t, sin_t, tm)
    km_p, vm_p = mla_kv_up(ckv_b[:tp], kr_f[:tp], b_w_uk, b_w_uv, 0, _tile(tp, 512))
    ctab = causal_tables()
    hgb = 8
    km_t = km_p.reshape(n_p, s_p, B_HEADS * MLA_QK_PAD).transpose(0, 2, 1).reshape(n_p * B_HEADS * MLA_QK_PAD, s_p)
    matt_p = attention(qm, [(km_t, vm_p, s_p, 0, 0)], ctab, None, n_seq=n_p, sq=s_p, bq=ATT_BQ, q_row0=0, q_col0=0,
                       n_heads=B_HEADS, hg=ATT_HG, dq=MLA_QK_PAD, dv=V_DIM, scale=B_SCALE, pos0=0, dyn_sk=s_p,
                       name="mla_attention_prompt")
    ckv_all = jnp.concatenate([cache_b_ckv[0], ckv_f[tp:].reshape(n_s, s_s, KV_LORA)], axis=1)
    kr_past = jnp.concatenate([cache_b_krope[0], jnp.zeros((n_s, past, LANES - QK_ROPE), F32)], axis=-1)
    kr_all = jnp.concatenate([kr_past, kr_f[tp:].reshape(n_s, s_s, LANES)], axis=1)
    km_s, vm_s = mla_kv_up(ckv_all.reshape(n_s * sk_s, KV_LORA), kr_all.reshape(n_s * sk_s, LANES),
                           b_w_uk, b_w_uv, 0, sk_s)
    tiles_m = [(0, r, r, ATT_TILE) for r in range(0, past, ATT_TILE)] + [(0, past, past, s_s)]
    matt_s = attention(qm, [(km_s, vm_s, sk_s, 0, 0)], ctab, None, n_seq=n_s, sq=s_s, bq=s_s, q_row0=tp, q_col0=0,
                       n_heads=B_HEADS, hg=hgb, dq=MLA_QK_PAD, dv=V_DIM, scale=B_SCALE, pos0=past,
                       seg_tiles=tiles_m, name="mla_attention_sample")

    w_r1, b_r1 = router_params(1)
    h3, xn1, lg1 = outproj_norm_router([matt_p, matt_s], b_w_out, 0, [h2], norm_ffn[1], w_r1, b_r1, tm2, 512)
    y_p, y_s = hier_moe_layer(h3, xn1, lg1, moe_w_gate, moe_w_up, moe_w_down, 1, norm_final, F32, split=tp)
    y_prompt = y_p.reshape(n_p, s_p, d)
    y_sample = y_s.reshape(n_s, s_s, d)

    def heads(a, n, s):
        return a.reshape(1, n, s, A_HEADS, A_HEAD_DIM)

    kr_out = kr_f[:, :QK_ROPE]
    return (y_prompt, y_sample,
            heads(kp_f, n_p, s_p), heads(vp_f, n_p, s_p), kidx[:tp].reshape(1, n_p, s_p, IDX_DIM),
            ckv_f[:tp].reshape(1, n_p, s_p, KV_LORA), kr_out[:tp].reshape(1, n_p, s_p, QK_ROPE),
            heads(ks_f, n_s, s_s), heads(vs_f, n_s, s_s), kidx[tp:].reshape(1, n_s, s_s, IDX_DIM),
            ckv_f[tp:].reshape(1, n_s, s_s, KV_LORA), kr_out[tp:].reshape(1, n_s, s_s, QK_ROPE))
```

```python
import functools
import math

import jax
import jax.numpy as jnp
from jax import lax
from jax.experimental import pallas as pl
from jax.experimental.pallas import tpu as pltpu

F32 = jnp.float32
BF16 = jnp.bfloat16
I32 = jnp.int32

RMS_EPS = 1e-6
CHUNK = 64
NEG = -1e30
INT_MIN = -2 ** 31
INT_MAX = 2 ** 31 - 1
LOG2E = math.log2(math.e)

LANES = 128
KEY_BLOCK = 128
ATT_TILE = 256
ATT_BQ = 128
ATT_HG = 8
VMEM_LIMIT = 56 * 1024 * 1024

A_HEADS = 16
A_HEAD_DIM = 128
IDX_HEADS = 16
IDX_DIM = 64
TOPK_MAX = 256
IDX_W_SCALE = float((IDX_HEADS * IDX_DIM) ** -0.5)
NUM_BUCKETS = 32
MAX_DISTANCE = 128
B_HEADS = 16
Q_LORA = 512
KV_LORA = 512
QK_NOPE = 128
QK_ROPE = 64
V_DIM = 128
ROPE_THETA = 10000.0
B_SCALE = float((QK_NOPE + QK_ROPE) ** -0.5)
MLA_QK_PAD = 256
N_GROUPS = 8
EXPERTS_PER_GROUP = 8
N_EXPERTS = 64
D_EXPERT = 512
MOE_BM = 128
ROW_CHUNKS = 16
X_AHEAD = 2


def _tile(n, pref):
    for c in range(pref, 0, -LANES):
        if n % c == 0:
            return c
    raise ValueError(f"no 128-multiple tile divides {n}")


def _cparams(sem, vmem=None):
    return pltpu.CompilerParams(dimension_semantics=sem, vmem_limit_bytes=vmem)


def _dot(a, b):
    return jnp.dot(a, b, preferred_element_type=F32)


def _dot_nt(a, b):
    return lax.dot_general(a, b, (((1,), (1,)), ((), ())), preferred_element_type=F32)


def _rms(x, g):
    ms = jnp.mean(x * x, axis=-1, keepdims=True)
    return x * lax.rsqrt(ms + RMS_EPS) * g


def _pick_rows(i, n_first, refs):
    if len(refs) == 1:
        return refs[0][...]
    return jnp.where(i < n_first, refs[0][...], refs[1][...])


def _row_split_specs(parts, tm, width, col_of):
    n_first = parts[0].shape[0] // tm
    assert all(p.shape[0] % tm == 0 for p in parts)
    specs = [pl.BlockSpec((tm, width), lambda i, *r: (jnp.minimum(i, n_first - 1), col_of(i, *r)))]
    if len(parts) == 2:
        specs.append(pl.BlockSpec((tm, width), lambda i, *r: (jnp.maximum(i - n_first, 0), col_of(i, *r))))
    return specs, n_first


def _rmsnorm_body(*refs, n_first):
    x = _pick_rows(pl.program_id(0), n_first, refs[:-2])
    g_ref, o_ref = refs[-2:]
    o_ref[...] = _rms(x, g_ref[...]).astype(o_ref.dtype)


def rmsnorm(xs, g, tm, out_dtype=BF16):
    d = xs[0].shape[1]
    t = sum(x.shape[0] for x in xs)
    specs, n_first = _row_split_specs(xs, tm, d, lambda i: 0)
    return pl.pallas_call(
        functools.partial(_rmsnorm_body, n_first=n_first),
        grid=(t // tm,),
        in_specs=specs + [pl.BlockSpec((1, d), lambda i: (0, 0))],
        out_specs=pl.BlockSpec((tm, d), lambda i: (i, 0)),
        out_shape=jax.ShapeDtypeStruct((t, d), out_dtype),
        compiler_params=_cparams(("parallel",)),
        name="rmsnorm",
    )(*xs, g.reshape(1, d))


def _matmul_body(a_ref, w_ref, *rest):
    outs, wb = rest[:-1], rest[-1]

    @pl.when(pl.program_id(1) == 0)
    def _():
        wb[...] = w_ref[...].astype(BF16)

    r = _dot(a_ref[...], wb[...])
    for o in outs:
        o[...] = r.astype(o.dtype)


def matmul(a, w, col0, ncols, out_dtypes, tm, tn, name, row0=0, nrows=None):
    k = a.shape[1]
    t = a.shape[0] if nrows is None else nrows
    assert ncols % tn == 0 and col0 % tn == 0 and t % tm == 0 and row0 % tm == 0
    cb = col0 // tn
    rb = row0 // tm
    return pl.pallas_call(
        _matmul_body,
        grid=(ncols // tn, t // tm),
        in_specs=[pl.BlockSpec((tm, k), lambda j, i: (i + rb, 0)),
                  pl.BlockSpec((k, tn), lambda j, i: (0, j + cb))],
        out_specs=[pl.BlockSpec((tm, tn), lambda j, i: (i, j)) for _ in out_dtypes],
        out_shape=[jax.ShapeDtypeStruct((t, ncols), dt) for dt in out_dtypes],
        scratch_shapes=[pltpu.VMEM((k, tn), BF16)],
        compiler_params=_cparams(("arbitrary", "arbitrary"), VMEM_LIMIT),
        name=name,
    )(a, w)


def _float_sort_key(x):
    bits = lax.bitcast_convert_type(x, I32)
    return bits ^ ((bits >> 31) & INT_MAX)


def _select_body(qi_ref, wi_ref, kc_ref, mask_ref, key_scr, jm_scr, *, bq, sk, sk_real, tk, pos0, topk):
    b = pl.program_id(1)
    t0 = pos0 + b * bq
    kmax = jnp.minimum(sk_real, ((t0 + bq - 1) // CHUNK + 1) * CHUNK)
    nkt = (kmax + tk - 1) // tk
    nch = tk // LANES

    w = wi_ref[...] * IDX_W_SCALE
    wb = [jnp.broadcast_to(w[:, h:h + 1], (bq, LANES)) for h in range(IDX_HEADS)]
    qpos = t0 + lax.broadcasted_iota(I32, (LANES, LANES), 1)
    lim = jnp.minimum((qpos // CHUNK + 1) * CHUNK, sk_real)
    sub = lax.broadcasted_iota(I32, (LANES, LANES), 0)

    def to_lanes(x):
        if bq < LANES:
            x = jnp.concatenate([x, jnp.zeros((LANES - bq, LANES), x.dtype)], axis=0)
        return x.T

    def score_tile(j, carry):
        off = pl.multiple_of(j * tk, tk)
        kc = kc_ref[pl.ds(off, tk), :].astype(BF16)
        ka, kb = kc[:, :LANES], kc[:, LANES:]
        accs = [jnp.zeros((bq, LANES), F32) for _ in range(nch)]
        for g in range(IDX_HEADS // 2):
            qg = qi_ref[:, g * LANES:(g + 1) * LANES]
            sa = _dot_nt(qg, ka)
            sb = _dot_nt(qg, kb)
            for c in range(nch):
                sl = slice(c * LANES, (c + 1) * LANES)
                accs[c] = accs[c] + wb[2 * g] * jnp.maximum(sa[:, sl], 0.0) \
                    + wb[2 * g + 1] * jnp.maximum(sb[:, sl], 0.0)
        for c in range(nch):
            kpos = off + c * LANES + sub
            sc = jnp.where(kpos < lim, to_lanes(accs[c]), -jnp.inf)
            key_scr[pl.ds(off + c * LANES, LANES), :] = _float_sort_key(sc)
        return carry

    lax.fori_loop(0, nkt, score_tile, 0)

    def count(indicator):
        def tile(j, cnt):
            off = pl.multiple_of(j * tk, tk)
            for c in range(nch):
                kt = key_scr[pl.ds(off + c * LANES, LANES), :]
                cnt = cnt + indicator(kt, off + c * LANES + sub)
            return cnt
        cnt = lax.fori_loop(0, nkt, tile, jnp.zeros((LANES, LANES), F32))
        return jnp.sum(cnt, axis=0, keepdims=True)

    def bit_step(i, pfx_u):
        bit = lax.shift_left(jnp.int32(1), 31 - i)
        cand_u = pfx_u | bit
        cand_s = cand_u ^ INT_MIN
        total = count(lambda kt, kp: jnp.where(kt >= cand_s, 1.0, 0.0))
        return jnp.where(total >= topk, cand_u, pfx_u)

    pfx = lax.fori_loop(0, 32, bit_step, jnp.zeros((1, LANES), I32))
    thr = pfx ^ INT_MIN

    n_gt = count(lambda kt, kp: jnp.where(kt > thr, 1.0, 0.0))
    n_ge = count(lambda kt, kp: jnp.where(kt >= thr, 1.0, 0.0))
    quota = topk - n_gt
    jm_scr[...] = jnp.full((1, LANES), INT_MAX, I32)
    any_excess = jnp.max(jnp.where(n_ge > topk, 1.0, 0.0)) > 0.0

    @pl.when(any_excess)
    def _():
        nbits = max(1, int(sk - 1).bit_length())

        def idx_step(i, ans):
            cand = ans | lax.shift_left(jnp.int32(1), nbits - 1 - i)
            below = count(lambda kt, kp: jnp.where(kt == thr, jnp.where(kp < cand, 1.0, 0.0), 0.0))
            return jnp.where(below < quota, cand, ans)

        jm_scr[...] = lax.fori_loop(0, nbits, idx_step, jnp.zeros((1, LANES), I32))

    jm = jm_scr[...]

    def write_tile(j, carry):
        off = pl.multiple_of(j * tk, tk)
        for c in range(nch):
            kt = key_scr[pl.ds(off + c * LANES, LANES), :]
            kpos = off + c * LANES + sub
            v = jnp.where(kt > thr, 0.0, jnp.where(kt == thr, jnp.where(kpos <= jm, 0.0, NEG), NEG))
            v = jnp.where(kpos < lim, v, NEG)
            mask_ref[:, pl.ds(off + c * LANES, LANES)] = v.T[:bq].astype(mask_ref.dtype)
        return carry

    lax.fori_loop(0, nkt, write_tile, 0)

    def fill_tile(j, carry):
        off = pl.multiple_of(j * tk, tk)
        mask_ref[:, pl.ds(off, tk)] = jnp.full((bq, tk), NEG, mask_ref.dtype)
        return carry

    lax.fori_loop(nkt, sk // tk, fill_tile, 0)


def dsa_select(qi, wi_arr, wi_blk, kc, *, n_seq, sq, bq, sk, sk_real, tk, pos0, q_row0):
    nqb = sq // bq
    qb0 = q_row0 // bq
    assert q_row0 % bq == 0 and sk % tk == 0 and tk >= TOPK_MAX
    topk = min(TOPK_MAX, sk_real // 4)
    body = functools.partial(_select_body, bq=bq, sk=sk, sk_real=sk_real, tk=tk, pos0=pos0, topk=topk)
    return pl.pallas_call(
        body,
        grid=(n_seq, nqb),
        in_specs=[pl.BlockSpec((bq, IDX_HEADS * IDX_DIM), lambda s, b: (qb0 + s * nqb + b, 0)),
                  pl.BlockSpec((bq, LANES), lambda s, b: (qb0 + s * nqb + b, wi_blk)),
                  pl.BlockSpec((sk, 2 * LANES), lambda s, b: (s, 0))],
        out_specs=pl.BlockSpec((bq, sk), lambda s, b: (s * nqb + b, 0)),
        out_shape=jax.ShapeDtypeStruct((n_seq * sq, sk), BF16),
        scratch_shapes=[pltpu.VMEM((sk, LANES), I32), pltpu.VMEM((1, LANES), I32)],
        compiler_params=_cparams(("parallel", "arbitrary"), VMEM_LIMIT),
        name="dsa_select",
    )(qi, wi_arr, kc)


def _rel_tables_body(bias_ref, tab_ref):
    i = lax.broadcasted_iota(I32, (KEY_BLOCK, KEY_BLOCK), 0)
    j = lax.broadcasted_iota(I32, (KEY_BLOCK, KEY_BLOCK), 1)
    nb = NUM_BUCKETS // 2
    max_exact = nb // 2
    edges = [12, 16, 23, 32, 46, 64, 91]
    for d in range(4):
        rel = j - i + (d - 2) * KEY_BLOCK
        n = jnp.abs(rel)
        large = jnp.full_like(n, max_exact)
        for e in edges:
            large = large + jnp.where(n >= e, 1, 0)
        bucket = jnp.where(rel > 0, nb, 0) + jnp.where(n < max_exact, n, large)
        for h in range(A_HEADS):
            acc = jnp.zeros((KEY_BLOCK, KEY_BLOCK), F32)
            for k in range(NUM_BUCKETS):
                acc = jnp.where(bucket == k, bias_ref[k, h], acc)
            tab_ref[h, d] = acc * LOG2E


def rel_bias_tables(rel_bias):
    return pl.pallas_call(
        _rel_tables_body,
        in_specs=[pl.BlockSpec(memory_space=pltpu.SMEM)],
        out_shape=jax.ShapeDtypeStruct((A_HEADS, 4, KEY_BLOCK, KEY_BLOCK), F32),
        name="rel_bias_tables",
    )(rel_bias)


def _causal_tables_body(tab_ref):
    i = lax.broadcasted_iota(I32, (KEY_BLOCK, KEY_BLOCK), 0)
    j = lax.broadcasted_iota(I32, (KEY_BLOCK, KEY_BLOCK), 1)
    zero = jnp.zeros((KEY_BLOCK, KEY_BLOCK), F32)
    tab_ref[0, 0] = zero
    tab_ref[0, 1] = zero
    tab_ref[0, 2] = jnp.where(j // CHUNK <= i // CHUNK, 0.0, NEG)
    tab_ref[0, 3] = jnp.full((KEY_BLOCK, KEY_BLOCK), NEG, F32)


def causal_tables():
    return pl.pallas_call(
        _causal_tables_body,
        out_shape=jax.ShapeDtypeStruct((1, 4, KEY_BLOCK, KEY_BLOCK), F32),
        name="causal_tables",
    )()


def _attn_pipe_body(*refs, has_mask, per_head_tab, bq, hg, dq, dv, scale, pos0, sk):
    q_ref, k_ref, v_ref, tab_ref = refs[:4]
    mask_ref = refs[4] if has_mask else None
    out_ref = refs[5] if has_mask else refs[4]
    s_scr, p_scr, m_scr, l_scr, acc_scr = refs[-5:]

    b = pl.program_id(2)
    t0 = pos0 + b * bq
    qblk = t0 // KEY_BLOCK
    kmax = jnp.minimum(sk, ((t0 + bq - 1) // CHUNK + 1) * CHUNK)
    nkt = (kmax + ATT_TILE - 1) // ATT_TILE
    nch = ATT_TILE // LANES

    m_scr[...] = jnp.full(m_scr.shape, NEG, F32)
    l_scr[...] = jnp.zeros(l_scr.shape, F32)
    acc_scr[...] = jnp.zeros(acc_scr.shape, F32)

    def paired_loop(tile):
        def four(jj, carry):
            for u in range(4):
                tile(4 * jj + u, carry)
            return carry

        n4 = nkt // 4
        lax.fori_loop(0, n4, four, 0)

        @pl.when(nkt % 4 >= 2)
        def _():
            tile(4 * n4, 0)
            tile(4 * n4 + 1, 0)

        @pl.when(nkt % 2 == 1)
        def _():
            tile(nkt - 1, 0)

    def logits_tile(j, carry):
        off = pl.multiple_of(j * ATT_TILE, ATT_TILE)
        kb0 = off // KEY_BLOCK
        ds = [[jnp.clip(kb0 + c - (qblk + r) + 2, 0, 3) for c in range(nch)] for r in range(bq // KEY_BLOCK)]
        if has_mask:
            mk = mask_ref[:, pl.ds(off, ATT_TILE)].astype(F32)
        for h in range(hg):
            s = _dot(q_ref[:, h * dq:(h + 1) * dq], k_ref[h * dq:(h + 1) * dq, pl.ds(off, ATT_TILE)]) * scale
            hh = h if per_head_tab else 0
            s = s + jnp.concatenate(
                [jnp.concatenate([tab_ref[hh, d] for d in row], axis=1) for row in ds], axis=0)
            if has_mask:
                s = s + mk
            s_scr[h, :, pl.ds(off, ATT_TILE)] = s
            mvec = m_scr[h]
            for c in range(nch):
                mvec = jnp.maximum(mvec, s[:, c * LANES:(c + 1) * LANES])
            m_scr[h] = mvec
        return carry

    paired_loop(logits_tile)
    for h in range(hg):
        m_scr[h] = jnp.broadcast_to(jnp.max(m_scr[h], axis=1, keepdims=True), (bq, LANES))

    def exp_tile(j, carry):
        off = pl.multiple_of(j * ATT_TILE, ATT_TILE)
        for h in range(hg):
            s = s_scr[h, :, pl.ds(off, ATT_TILE)]
            m = m_scr[h]
            p = [jnp.exp2(s[:, c * LANES:(c + 1) * LANES] - m) for c in range(nch)]
            lvec = l_scr[h]
            for c in range(nch):
                lvec = lvec + p[c]
            l_scr[h] = lvec
            p_scr[h, :, pl.ds(off, ATT_TILE)] = jnp.concatenate(p, axis=1).astype(BF16)
        return carry

    lax.fori_loop(0, nkt, exp_tile, 0)

    def pv_tile(j, carry):
        off = pl.multiple_of(j * ATT_TILE, ATT_TILE)
        for h in range(hg):
            acc_scr[h] = acc_scr[h] + _dot(p_scr[h, :, pl.ds(off, ATT_TILE)],
                                           v_ref[pl.ds(off, ATT_TILE), h * dv:(h + 1) * dv])
        return carry

    paired_loop(pv_tile)
    for h in range(hg):
        l_row = jnp.sum(l_scr[h], axis=1, keepdims=True)
        out_ref[:, h * dv:(h + 1) * dv] = (acc_scr[h] / l_row).astype(out_ref.dtype)


def _attn_body(*refs, n_seg, has_mask, per_head_tab, bq, hg, dq, dv, scale, pos0, seg_tiles, by_head):
    q_ref = refs[0]
    kv = refs[1:1 + 2 * n_seg]
    pos = 1 + 2 * n_seg
    tab_ref = refs[pos]
    pos += 1
    mask_ref = refs[pos] if has_mask else None
    pos += 1 if has_mask else 0
    out_ref = refs[pos]
    s_scr, p_scr = refs[pos + 1:pos + 3]

    b = pl.program_id(2)
    t0 = pos0 + b * bq
    qblk = t0 // KEY_BLOCK

    def lane_pad(x, fill):
        w = x.shape[1]
        return x if w == LANES else jnp.concatenate([x, jnp.full((bq, LANES - w), fill, F32)], axis=1)

    mvec = [jnp.full((bq, LANES), NEG, F32) for _ in range(hg)]
    for (si, row0, key0, width) in seg_tiles:
        nch = max(1, width // KEY_BLOCK)
        cw = min(width, KEY_BLOCK)
        ds = [jnp.clip(key0 // KEY_BLOCK + c - qblk + 2, 0, 3) for c in range(nch)]
        k_t = None if by_head[si] else kv[2 * si][row0:row0 + width, :].astype(BF16)
        if has_mask:
            mk = mask_ref[:, key0:key0 + width].astype(F32)
        for h in range(hg):
            hh = h if per_head_tab else 0
            if by_head[si]:
                k_h = kv[2 * si][pl.ds(row0 * hg + h, width, stride=hg), :].astype(BF16)
            else:
                k_h = k_t[:, h * dq:(h + 1) * dq]
            s = _dot_nt(q_ref[:, h * dq:(h + 1) * dq], k_h) * scale
            bias = [tab_ref[hh, ds[c], :bq, :cw] for c in range(nch)]
            s = s + (jnp.concatenate(bias, axis=1) if nch > 1 else bias[0])
            if has_mask:
                s = s + mk
            s_scr[h, :, key0:key0 + width] = s
            for c in range(nch):
                mvec[h] = jnp.maximum(mvec[h], lane_pad(s[:, c * cw:(c + 1) * cw], NEG))

    m = [jnp.broadcast_to(jnp.max(mvec[h], axis=1, keepdims=True), (bq, LANES)) for h in range(hg)]
    lvec = [jnp.zeros((bq, LANES), F32) for _ in range(hg)]
    for (si, row0, key0, width) in seg_tiles:
        nch = max(1, width // KEY_BLOCK)
        cw = min(width, KEY_BLOCK)
        for h in range(hg):
            s = s_scr[h, :, key0:key0 + width]
            p = [jnp.exp2(s[:, c * cw:(c + 1) * cw] - m[h][:, :cw]) for c in range(nch)]
            for c in range(nch):
                lvec[h] = lvec[h] + lane_pad(p[c], 0.0)
            p_scr[h, :, key0:key0 + width] = (jnp.concatenate(p, axis=1) if nch > 1 else p[0]).astype(BF16)

    acc = [jnp.zeros((bq, dv), F32) for _ in range(hg)]
    for (si, row0, key0, width) in seg_tiles:
        v_t = None if by_head[si] else kv[2 * si + 1][row0:row0 + width, :].astype(BF16)
        for h in range(hg):
            if by_head[si]:
                v_h = kv[2 * si + 1][pl.ds(row0 * hg + h, width, stride=hg), :].astype(BF16)
            else:
                v_h = v_t[:, h * dv:(h + 1) * dv]
            acc[h] = acc[h] + _dot(p_scr[h, :, key0:key0 + width], v_h)

    for h in range(hg):
        l_row = jnp.sum(lvec[h], axis=1, keepdims=True)
        out_ref[:, h * dv:(h + 1) * dv] = (acc[h] / l_row).astype(out_ref.dtype)


def attention(q, segs, tab, mask, *, n_seq, sq, bq, q_row0, q_col0, n_heads, hg, dq, dv, scale, pos0,
              seg_tiles=None, dyn_sk=None, name="attention"):
    nqb = sq // bq
    ng = n_heads // hg
    qb0 = q_row0 // bq
    assert q_row0 % bq == 0 and q_col0 % (hg * dq) == 0 and pos0 % KEY_BLOCK == 0
    scale = scale * LOG2E
    assert bq % KEY_BLOCK == 0 or nqb == 1
    qc0 = q_col0 // (hg * dq)
    in_specs = [pl.BlockSpec((bq, hg * dq), lambda s, g, b: (qb0 + s * nqb + b, qc0 + g))]
    args = [q]
    by_head = []
    for (k, v, rows, kc0, vc0) in segs:
        assert kc0 % (hg * dq) == 0 and vc0 % (hg * dv) == 0
        by_head.append(k.shape[1] == dq and n_heads > 1)
        if by_head[-1]:
            assert hg == n_heads and dq == dv and k.shape[0] == n_seq * rows * n_heads and dyn_sk is None
            for _ in range(2):
                in_specs.append(pl.BlockSpec((rows * n_heads, dq), lambda s, g, b: (s, 0)))
            args += [k, v]
            continue
        mode = dict(pipeline_mode=pl.Buffered(1)) if dyn_sk is not None else {}
        if dyn_sk is not None:
            assert kc0 == 0 and k.shape == (n_seq * n_heads * dq, rows)
            in_specs.append(pl.BlockSpec((hg * dq, rows), lambda s, g, b: (s * ng + g, 0), **mode))
        else:
            in_specs.append(pl.BlockSpec((rows, hg * dq), lambda s, g, b, c=kc0 // (hg * dq): (s, c + g)))
        in_specs.append(pl.BlockSpec((rows, hg * dv), lambda s, g, b, c=vc0 // (hg * dv): (s, c + g), **mode))
        args += [k, v]
    per_head_tab = tab.shape[0] > 1
    if per_head_tab:
        in_specs.append(pl.BlockSpec((hg, 4, KEY_BLOCK, KEY_BLOCK), lambda s, g, b: (g, 0, 0, 0)))
    else:
        in_specs.append(pl.BlockSpec((1, 4, KEY_BLOCK, KEY_BLOCK), lambda s, g, b: (0, 0, 0, 0)))
    args.append(tab)
    if mask is not None:
        in_specs.append(pl.BlockSpec((bq, mask.shape[1]), lambda s, g, b: (s * nqb + b, 0)))
        args.append(mask)
    if dyn_sk is not None:
        assert len(segs) == 1 and dyn_sk % ATT_TILE == 0 and bq % KEY_BLOCK == 0
        body = functools.partial(_attn_pipe_body, has_mask=mask is not None, per_head_tab=per_head_tab,
                                 bq=bq, hg=hg, dq=dq, dv=dv, scale=scale, pos0=pos0, sk=dyn_sk)
        scratch = [pltpu.VMEM((hg, bq, dyn_sk), F32), pltpu.VMEM((hg, bq, dyn_sk), BF16),
                   pltpu.VMEM((hg, bq, LANES), F32), pltpu.VMEM((hg, bq, LANES), F32),
                   pltpu.VMEM((hg, bq, dv), F32)]
    else:
        body = functools.partial(_attn_body, n_seg=len(segs), has_mask=mask is not None,
                                 per_head_tab=per_head_tab, bq=bq, hg=hg, dq=dq, dv=dv, scale=scale, pos0=pos0,
                                 seg_tiles=seg_tiles, by_head=tuple(by_head))
        sk_tot = max(key0 + width for (_, _, key0, width) in seg_tiles)
        sk_tot = (sk_tot + LANES - 1) // LANES * LANES
        scratch = [pltpu.VMEM((hg, bq, sk_tot), F32), pltpu.VMEM((hg, bq, sk_tot), BF16)]
    return pl.pallas_call(
        body,
        grid=(n_seq, ng, nqb),
        in_specs=in_specs,
        out_specs=pl.BlockSpec((bq, hg * dv), lambda s, g, b: (s * nqb + b, g)),
        out_shape=jax.ShapeDtypeStruct((n_seq * sq, n_heads * dv), BF16),
        scratch_shapes=scratch,
        compiler_params=_cparams(("parallel", "parallel", "arbitrary"), VMEM_LIMIT),
        name=name,
    )(*args)


def _outproj_body(*refs, tn, na, nx, a_first, x_first):
    i = pl.program_id(0)
    j = pl.program_id(1)
    a = _pick_rows(i, a_first, refs[:na])
    w_ref = refs[na]
    x = _pick_rows(i, x_first, refs[na + 1:na + 1 + nx])
    g_ref, wr_ref, br_ref, h_ref, xn_ref, lg_ref, hrow, wb = refs[na + 1 + nx:]
    col = pl.ds(pl.multiple_of(j * tn, tn), tn)

    @pl.when(i == 0)
    def _():
        wb[:, col] = w_ref[0].astype(BF16)

    r = x + _dot(a, wb[:, col])
    h_ref[...] = r
    hrow[:, col] = r

    @pl.when(j == pl.num_programs(1) - 1)
    def _():
        xn = _rms(hrow[...], g_ref[...])
        tm = hrow.shape[0]
        for c in range(ROW_CHUNKS):
            xn_ref[pl.ds(c, tm, stride=ROW_CHUNKS), :] = xn[:, c * LANES:(c + 1) * LANES]
        lg_ref[...] = _dot(xn.astype(BF16), wr_ref[...].astype(BF16)) + br_ref[...]


def outproj_norm_router(a_parts, w_out, layer, x_parts, g_ffn, w_r, b_r, tm, tn):
    k = a_parts[0].shape[1]
    t = sum(p.shape[0] for p in a_parts)
    d = w_out.shape[-1]
    a_specs, a_first = _row_split_specs(a_parts, tm, k, lambda i, j: 0)
    x_specs, x_first = _row_split_specs(x_parts, tm, tn, lambda i, j: j)
    nj = d // tn
    return pl.pallas_call(
        functools.partial(_outproj_body, tn=tn, na=len(a_parts), nx=len(x_parts), a_first=a_first,
                          x_first=x_first),
        grid=(t // tm, d // tn),
        in_specs=a_specs + [pl.BlockSpec((1, k, tn), lambda i, j: (layer, 0, jnp.where(i == 0, j, nj - 1)))] + x_specs + [
            pl.BlockSpec((1, d), lambda i, j: (0, 0)),
            pl.BlockSpec((d, LANES), lambda i, j: (0, 0)),
            pl.BlockSpec((1, LANES), lambda i, j: (0, 0))],
        out_specs=[pl.BlockSpec((tm, tn), lambda i, j: (i, j)),
                   pl.BlockSpec((tm * ROW_CHUNKS, LANES), lambda i, j: (i, 0)),
                   pl.BlockSpec((tm, LANES), lambda i, j: (i, 0))],
        out_shape=[jax.ShapeDtypeStruct((t, d), F32), jax.ShapeDtypeStruct((t * ROW_CHUNKS, LANES), F32),
                   jax.ShapeDtypeStruct((t, LANES), F32)],
        scratch_shapes=[pltpu.VMEM((tm, d), F32), pltpu.VMEM((k, d), BF16)],
        compiler_params=_cparams(("arbitrary", "arbitrary"), VMEM_LIMIT),
        name="outproj_norm_router",
    )(*a_parts, w_out, *x_parts, g_ffn.reshape(1, d), w_r, b_r)


def _routing_body(lg_ref, out_ref, cnt_ref, carry, *, tm):
    i = pl.program_id(0)

    @pl.when(i == 0)
    def _():
        carry[...] = jnp.zeros(carry.shape, F32)

    x = lg_ref[...]
    lane = lax.broadcasted_iota(I32, (tm, LANES), 1)
    neg_inf = -jnp.inf

    def rmax(v):
        return jnp.max(v, axis=1, keepdims=True)

    def rmin(v):
        return jnp.min(v, axis=1, keepdims=True)

    def rsum(v):
        return jnp.sum(v, axis=1, keepdims=True)

    gm = lane < N_GROUPS
    gmax = rmax(jnp.where(gm, x, neg_inf))
    gsel = rmin(jnp.where(gm, jnp.where(x == gmax, lane, LANES), LANES))
    gsum = rsum(jnp.where(gm, jnp.exp(x - gmax), 0.0))
    g_w = 1.0 / gsum
    lo = N_GROUPS + gsel * EXPERTS_PER_GROUP
    em = jnp.logical_and(lane >= lo, lane < lo + EXPERTS_PER_GROUP)
    emax = rmax(jnp.where(em, x, neg_inf))
    ee = jnp.where(em, jnp.exp(x - emax), 0.0)
    p = ee / rsum(ee)
    p1 = rmax(jnp.where(em, p, -1.0))
    i1 = rmin(jnp.where(em, jnp.where(p == p1, lane, LANES), LANES))
    em2 = jnp.logical_and(em, lane != i1)
    p2 = rmax(jnp.where(em2, p, -1.0))
    i2 = rmin(jnp.where(em2, jnp.where(p == p2, lane, LANES), LANES))
    den = p1 + p2
    g1 = g_w * (p1 / den)
    g2 = g_w * (p2 / den)

    oh1 = jnp.where(lane == i1, 1.0, 0.0)
    oh2 = jnp.where(lane == i2, 1.0, 0.0)
    oh = oh1 + oh2
    r = lax.broadcasted_iota(I32, (tm, tm), 0)
    c = lax.broadcasted_iota(I32, (tm, tm), 1)
    lower = jnp.where(c < r, 1.0, 0.0).astype(BF16)
    before = _dot(lower, oh.astype(BF16)) + carry[...]
    rank1 = rsum(oh1 * before)
    rank2 = rsum(oh2 * before)
    carry[...] = carry[...] + jnp.sum(oh, axis=0, keepdims=True)

    e1 = (i1 - N_GROUPS).astype(F32)
    e2 = (i2 - N_GROUPS).astype(F32)
    vals = [e1, e2, rank1, rank2, g1, g2]
    out = jnp.zeros((tm, LANES), F32)
    for k, v in enumerate(vals):
        out = jnp.where(lane == k, v, out)
    out_ref[...] = out
    cnt_ref[...] = carry[...]


def moe_routing(logits, tm):
    t = logits.shape[0]
    return pl.pallas_call(
        functools.partial(_routing_body, tm=tm),
        grid=(t // tm,),
        in_specs=[pl.BlockSpec((tm, LANES), lambda i: (i, 0))],
        out_specs=[pl.BlockSpec((tm, LANES), lambda i: (i, 0)), pl.BlockSpec((1, LANES), lambda i: (0, 0))],
        out_shape=[jax.ShapeDtypeStruct((t, LANES), F32), jax.ShapeDtypeStruct((1, LANES), F32)],
        scratch_shapes=[pltpu.VMEM((1, LANES), F32)],
        compiler_params=_cparams(("arbitrary",)),
        name="moe_routing",
    )(logits)


def _slot_tokens_body(slots_ref, init_hbm, tok_ref, sem):
    init = pltpu.make_async_copy(init_hbm, tok_ref, sem)
    init.start()
    init.wait()

    def put(a, c):
        tok_ref[slots_ref[a]] = lax.shift_right_logical(a, 1)
        return c

    lax.fori_loop(0, slots_ref.shape[0], put, 0, unroll=8)


def moe_slot_tokens(slots, n_slots):
    init = jnp.arange(n_slots, dtype=I32) % (slots.shape[0] // 2)
    return pl.pallas_call(
        _slot_tokens_body,
        in_specs=[pl.BlockSpec(memory_space=pltpu.SMEM), pl.BlockSpec(memory_space=pl.ANY)],
        out_specs=pl.BlockSpec(memory_space=pltpu.SMEM),
        out_shape=jax.ShapeDtypeStruct((n_slots,), I32),
        scratch_shapes=[pltpu.SemaphoreType.DMA],
        name="moe_slot_tokens",
    )(slots, init)


def _dispatch_body(nslot_ref, tok_ref, x_hbm, xs_ref, buf, sem, *, rows):
    i = pl.program_id(0)
    ni = pl.num_programs(0)
    n_used = nslot_ref[0]

    def issue(blk, s):
        @pl.when(blk * rows < n_used)
        def _():
            def pair(rp, carry):
                for k in range(2):
                    r = 2 * rp + k
                    src = pl.multiple_of(tok_ref[blk * rows + r] * ROW_CHUNKS, ROW_CHUNKS)
                    dst = pl.multiple_of(r * ROW_CHUNKS, ROW_CHUNKS)
                    pltpu.make_async_copy(x_hbm.at[pl.ds(src, ROW_CHUNKS), :],
                                          buf.at[s, pl.ds(dst, ROW_CHUNKS), :], sem.at[s]).start(priority=k)
                return carry

            lax.fori_loop(0, rows // 2, pair, 0, unroll=4)

    @pl.when(i == 0)
    def _():
        issue(0, 0)

    @pl.when(i + 1 < ni)
    def _():
        issue(i + 1, (i + 1) % 2)

    slot = i % 2

    @pl.when(i * rows < n_used)
    def _():
        pltpu.make_async_copy(x_hbm.at[pl.ds(0, rows * ROW_CHUNKS), :], buf.at[slot], sem.at[slot]).wait()
        for j in range(ROW_CHUNKS):
            xs_ref[:, j * LANES:(j + 1) * LANES] = \
                buf[slot, pl.ds(j, rows, stride=ROW_CHUNKS), :].astype(xs_ref.dtype)

    @pl.when(i * rows >= n_used)
    def _():
        xs_ref[...] = jnp.zeros(xs_ref.shape, xs_ref.dtype)


def moe_dispatch(xn, slot_tok, n_used, rows):
    assert xn.shape[1] == LANES
    d = ROW_CHUNKS * LANES
    n_slots = slot_tok.shape[0]
    assert n_slots % rows == 0
    grid_spec = pltpu.PrefetchScalarGridSpec(
        num_scalar_prefetch=2,
        grid=(n_slots // rows,),
        in_specs=[pl.BlockSpec(memory_space=pl.ANY)],
        out_specs=pl.BlockSpec((rows, d), lambda i, ns, tk: (i, 0)),
        scratch_shapes=[pltpu.VMEM((2, rows * ROW_CHUNKS, LANES), F32), pltpu.SemaphoreType.DMA((2,))],
    )
    return pl.pallas_call(
        functools.partial(_dispatch_body, rows=rows),
        grid_spec=grid_spec,
        out_shape=jax.ShapeDtypeStruct((n_slots, d), BF16),
        compiler_params=_cparams(("arbitrary",), VMEM_LIMIT),
        name="moe_dispatch",
    )(n_used, slot_tok, xn)


def _experts_body(blk0_ref, nblk_ref, ntot_ref, xs_hbm, wg_ref, wu_ref, wd_ref, ys_hbm,
                  xbuf, ybuf, zbuf, xsem, ysem, zsem, wgb, wub, wdb, *, bm):
    e = pl.program_id(0)
    n_total = ntot_ref[0]

    def x_copy(g, s):
        return pltpu.make_async_copy(xs_hbm.at[pl.ds(g * bm, bm), :], xbuf.at[s], xsem.at[s])

    def y_copy(g, s):
        return pltpu.make_async_copy(ybuf.at[s], ys_hbm.at[pl.ds(g * bm, bm), :], ysem.at[s])

    for g0 in range(X_AHEAD):
        @pl.when(jnp.logical_and(e == 0, n_total > g0))
        def _():
            x_copy(g0, g0).start(priority=1)

    @pl.when(nblk_ref[e] > 0)
    def _():
        wgb[...] = wg_ref[0, 0].astype(BF16)
        wub[...] = wu_ref[0, 0].astype(BF16)
        wdb[...] = wd_ref[0, 0].astype(BF16)

    def block(i, carry):
        g = blk0_ref[e] + i
        s = g % 2
        sx = g % (X_AHEAD + 1)

        @pl.when(g + X_AHEAD < n_total)
        def _():
            x_copy(g + X_AHEAD, (g + X_AHEAD) % (X_AHEAD + 1)).start(priority=1)

        x_copy(g, sx).wait()
        x = xbuf[sx]
        gt = _dot(x, wgb[...])
        up = _dot(x, wub[...])
        a = (gt * (1.0 / (1.0 + jnp.exp(-gt)))) * up
        y = _dot(a.astype(BF16), wdb[...])

        @pl.when(g >= 2)
        def _():
            y_copy(g - 2, s).wait()

        ybuf[s] = y
        y_copy(g, s).start()
        return carry

    lax.fori_loop(0, nblk_ref[e], block, 0)

    @pl.when(e == pl.num_programs(0) - 1)
    def _():
        for back in (2, 1):
            @pl.when(n_total >= back)
            def _():
                y_copy(n_total - back, (n_total - back) % 2).wait()

        def z_copy(g):
            return pltpu.make_async_copy(zbuf, ys_hbm.at[pl.ds(g * bm, bm), :], zsem)

        zbuf[...] = jnp.zeros(zbuf.shape, F32)
        n_blocks = ys_hbm.shape[0] // bm
        lax.fori_loop(n_total, n_blocks, lambda g, c: (z_copy(g).start(), c)[1], 0)
        lax.fori_loop(n_total, n_blocks, lambda g, c: (z_copy(g).wait(), c)[1], 0)


def moe_experts(xs, blk0, nblk, n_total, w_gate, w_up, w_down, layer, bm):
    n_slots, d = xs.shape
    n_exp = blk0.shape[0]
    de = w_gate.shape[-1]
    grid_spec = pltpu.PrefetchScalarGridSpec(
        num_scalar_prefetch=3,
        grid=(n_exp,),
        in_specs=[pl.BlockSpec(memory_space=pl.ANY),
                  pl.BlockSpec((1, 1, d, de), lambda e, b0, nb, nt: (layer, e, 0, 0)),
                  pl.BlockSpec((1, 1, d, de), lambda e, b0, nb, nt: (layer, e, 0, 0)),
                  pl.BlockSpec((1, 1, de, d), lambda e, b0, nb, nt: (layer, e, 0, 0))],
        out_specs=pl.BlockSpec(memory_space=pl.ANY),
        scratch_shapes=[pltpu.VMEM((X_AHEAD + 1, bm, d), BF16), pltpu.VMEM((2, bm, d), F32),
                        pltpu.VMEM((bm, d), F32),
                        pltpu.SemaphoreType.DMA((X_AHEAD + 1,)), pltpu.SemaphoreType.DMA((2,)),
                        pltpu.SemaphoreType.DMA,
                        pltpu.VMEM((d, de), BF16), pltpu.VMEM((d, de), BF16), pltpu.VMEM((de, d), BF16)],
    )
    return pl.pallas_call(
        functools.partial(_experts_body, bm=bm),
        grid_spec=grid_spec,
        out_shape=jax.ShapeDtypeStruct((n_slots, d), F32),
        compiler_params=_cparams(("arbitrary",), VMEM_LIMIT),
        name="moe_experts",
    )(blk0, nblk, n_total, xs, w_gate, w_up, w_down)


def _combine_body(slot_ref, ys_hbm, h_ref, rt_ref, g_ref, *rest, tm, tile0, want_h):
    outs, (buf, sem) = rest[:-2], rest[-2:]
    i = pl.program_id(0)
    ni = pl.num_programs(0)

    def row_copy(src_row, k, r, s):
        return pltpu.make_async_copy(ys_hbm.at[pl.ds(src_row, 1), :], buf.at[s, k, pl.ds(r, 1), :], sem.at[s])

    def issue(tile, s):
        def one(r, carry):
            base = ((tile0 + tile) * tm + r) * 2
            row_copy(slot_ref[base], 0, r, s).start(priority=0)
            row_copy(slot_ref[base + 1], 1, r, s).start(priority=1)
            return carry

        lax.fori_loop(0, tm, one, 0, unroll=8)

    @pl.when(i == 0)
    def _():
        issue(0, 0)

    @pl.when(i + 1 < ni)
    def _():
        issue(i + 1, (i + 1) % 2)

    slot = i % 2
    for k in range(2):
        pltpu.make_async_copy(ys_hbm.at[pl.ds(0, tm), :], buf.at[slot, k], sem.at[slot]).wait()

    rt = rt_ref[...]
    g1 = rt[:, 4:5]
    g2 = rt[:, 5:6]
    h2 = h_ref[...] + (buf[slot, 0] * g1 + buf[slot, 1] * g2)
    if want_h:
        outs[0][...] = h2
    outs[-1][...] = _rms(h2, g_ref[...]).astype(outs[-1].dtype)


def moe_combine(ys, slots, h1, route, g_next, tm, u_dtype, row0=0, nrows=None, want_h=True):
    d = h1.shape[1]
    t = h1.shape[0] if nrows is None else nrows
    assert row0 % tm == 0 and t % tm == 0
    tile0 = row0 // tm
    row_spec = pl.BlockSpec((tm, d), lambda i, sl: (i, 0))
    grid_spec = pltpu.PrefetchScalarGridSpec(
        num_scalar_prefetch=1,
        grid=(t // tm,),
        in_specs=[pl.BlockSpec(memory_space=pl.ANY),
                  pl.BlockSpec((tm, d), lambda i, sl: (i + tile0, 0)),
                  pl.BlockSpec((tm, LANES), lambda i, sl: (i + tile0, 0)),
                  pl.BlockSpec((1, d), lambda i, sl: (0, 0))],
        out_specs=[row_spec, row_spec] if want_h else [row_spec],
        scratch_shapes=[pltpu.VMEM((2, 2, tm, d), F32), pltpu.SemaphoreType.DMA((2,))],
    )
    shapes = [jax.ShapeDtypeStruct((t, d), F32)] if want_h else []
    return pl.pallas_call(
        functools.partial(_combine_body, tm=tm, tile0=tile0, want_h=want_h),
        grid_spec=grid_spec,
        out_shape=shapes + [jax.ShapeDtypeStruct((t, d), u_dtype)],
        compiler_params=_cparams(("arbitrary",), VMEM_LIMIT),
        name="moe_combine",
    )(slots, ys, h1, route, g_next.reshape(1, d))


def hier_moe_layer(h1, xn, logits, w_gate, w_up, w_down, layer, g_next, u_dtype, split=None):
    t = h1.shape[0]
    bm = MOE_BM
    route, counts = moe_routing(logits, _tile(t, 512))
    e = route[:, 0:2].astype(I32)
    rank = route[:, 2:4].astype(I32)
    counts = counts[0, N_GROUPS:N_GROUPS + N_EXPERTS].astype(I32)
    padded = (counts + bm - 1) // bm * bm
    pad_end = jnp.cumsum(padded)
    pad_start = pad_end - padded
    start_of = jnp.sum(jnp.where(e[..., None] == jnp.arange(N_EXPERTS, dtype=I32), pad_start, 0), axis=-1)
    slots = (start_of + rank).reshape(-1)
    nb = (2 * t) // bm + N_EXPERTS
    slot_tok = moe_slot_tokens(slots, nb * bm)
    n_used = pad_end[-1:]
    xs = moe_dispatch(xn, slot_tok, n_used, bm * max(f for f in (4, 2, 1) if nb % f == 0))
    ys = moe_experts(xs, pad_start // bm, padded // bm, n_used // bm, w_gate, w_up, w_down, layer, bm)
    tmc = _tile(math.gcd(t, split or t), 256)
    if split is None:
        return moe_combine(ys, slots, h1, route, g_next, tmc, u_dtype)
    (u_a,) = moe_combine(ys, slots, h1, route, g_next, tmc, u_dtype, 0, split, want_h=False)
    (u_b,) = moe_combine(ys, slots, h1, route, g_next, tmc, u_dtype, split, t - split, want_h=False)
    return u_a, u_b


def _swap_halves(x):
    lane = lax.broadcasted_iota(I32, x.shape, 1)
    return jnp.where(lane < QK_ROPE // 2, pltpu.roll(x, LANES - QK_ROPE // 2, 1), pltpu.roll(x, QK_ROPE // 2, 1))


def _mla_prep_body(p_ref, gq_ref, gkv_ref, cos_ref, sin_ref, cq_ref, ckv_ref, ckvb_ref, kr_ref):
    p = p_ref[...]
    cq_ref[...] = _rms(p[:, :Q_LORA], gq_ref[...]).astype(cq_ref.dtype)
    ckv = _rms(p[:, Q_LORA:Q_LORA + KV_LORA], gkv_ref[...])
    ckv_ref[...] = ckv
    ckvb_ref[...] = ckv.astype(ckvb_ref.dtype)
    kr = p[:, Q_LORA + KV_LORA:]
    kr_ref[...] = kr * cos_ref[...] + _swap_halves(kr) * sin_ref[...]


def mla_prep(proj, g_q, g_kv, cos, sin, tm):
    t = proj.shape[0]
    return pl.pallas_call(
        _mla_prep_body,
        grid=(t // tm,),
        in_specs=[pl.BlockSpec((tm, proj.shape[1]), lambda i: (i, 0)),
                  pl.BlockSpec((1, Q_LORA), lambda i: (0, 0)),
                  pl.BlockSpec((1, KV_LORA), lambda i: (0, 0)),
                  pl.BlockSpec((tm, LANES), lambda i: (i, 0)),
                  pl.BlockSpec((tm, LANES), lambda i: (i, 0))],
        out_specs=[pl.BlockSpec((tm, Q_LORA), lambda i: (i, 0)),
                   pl.BlockSpec((tm, KV_LORA), lambda i: (i, 0)),
                   pl.BlockSpec((tm, KV_LORA), lambda i: (i, 0)),
                   pl.BlockSpec((tm, LANES), lambda i: (i, 0))],
        out_shape=[jax.ShapeDtypeStruct((t, Q_LORA), BF16), jax.ShapeDtypeStruct((t, KV_LORA), F32),
                   jax.ShapeDtypeStruct((t, KV_LORA), BF16), jax.ShapeDtypeStruct((t, LANES), F32)],
        compiler_params=_cparams(("parallel",)),
        name="mla_prep",
    )(proj, g_q.reshape(1, -1), g_kv.reshape(1, -1), cos, sin)


def _mla_q_body(cq_ref, w_ref, cos_ref, sin_ref, q_ref, wb):
    @pl.when(pl.program_id(0) == 0)
    def _():
        wb[...] = w_ref[...].astype(BF16)

    cq = cq_ref[...]
    cos = cos_ref[...]
    sin = sin_ref[...]
    for h in range(B_HEADS):
        y = _dot(cq, wb[:, h * MLA_QK_PAD:(h + 1) * MLA_QK_PAD])
        xr = y[:, LANES:]
        q_ref[:, h * MLA_QK_PAD:h * MLA_QK_PAD + LANES] = y[:, :LANES].astype(q_ref.dtype)
        q_ref[:, h * MLA_QK_PAD + LANES:(h + 1) * MLA_QK_PAD] = \
            (xr * cos + _swap_halves(xr) * sin).astype(q_ref.dtype)


def mla_q_up(cq, w_uq_pad, cos, sin, tm):
    t = cq.shape[0]
    n = B_HEADS * MLA_QK_PAD
    return pl.pallas_call(
        _mla_q_body,
        grid=(t // tm,),
        in_specs=[pl.BlockSpec((tm, Q_LORA), lambda i: (i, 0)),
                  pl.BlockSpec((Q_LORA, n), lambda i: (0, 0)),
                  pl.BlockSpec((tm, LANES), lambda i: (i, 0)),
                  pl.BlockSpec((tm, LANES), lambda i: (i, 0))],
        out_specs=pl.BlockSpec((tm, n), lambda i: (i, 0)),
        out_shape=jax.ShapeDtypeStruct((t, n), BF16),
        scratch_shapes=[pltpu.VMEM((Q_LORA, n), BF16)],
        compiler_params=_cparams(("arbitrary",), VMEM_LIMIT),
        name="mla_q_up",
    )(cq, w_uq_pad, cos, sin)


def _mla_kv_body(c_ref, kr_ref, wk_ref, wv_ref, k_ref, v_ref, wkb, wvb):
    @pl.when(pl.program_id(0) == 0)
    def _():
        wkb[...] = wk_ref[0].astype(BF16)
        wvb[...] = wv_ref[0].astype(BF16)

    c = c_ref[...].astype(BF16)
    kr = kr_ref[...].astype(k_ref.dtype)
    kn = _dot(c, wkb[...])
    for h in range(B_HEADS):
        k_ref[:, h * MLA_QK_PAD:h * MLA_QK_PAD + LANES] = kn[:, h * QK_NOPE:(h + 1) * QK_NOPE].astype(k_ref.dtype)
        k_ref[:, h * MLA_QK_PAD + LANES:(h + 1) * MLA_QK_PAD] = kr
    v_ref[...] = _dot(c, wvb[...]).astype(v_ref.dtype)


def mla_kv_up(ckv, kr, w_uk, w_uv, layer, tm):
    t = ckv.shape[0]
    nk = B_HEADS * MLA_QK_PAD
    nv = B_HEADS * V_DIM
    return pl.pallas_call(
        _mla_kv_body,
        grid=(t // tm,),
        in_specs=[pl.BlockSpec((tm, KV_LORA), lambda i: (i, 0)),
                  pl.BlockSpec((tm, LANES), lambda i: (i, 0)),
                  pl.BlockSpec((1, KV_LORA, B_HEADS * QK_NOPE), lambda i: (layer, 0, 0)),
                  pl.BlockSpec((1, KV_LORA, nv), lambda i: (layer, 0, 0))],
        out_specs=[pl.BlockSpec((tm, nk), lambda i: (i, 0)), pl.BlockSpec((tm, nv), lambda i: (i, 0))],
        out_shape=[jax.ShapeDtypeStruct((t, nk), BF16), jax.ShapeDtypeStruct((t, nv), BF16)],
        scratch_shapes=[pltpu.VMEM((KV_LORA, B_HEADS * QK_NOPE), BF16), pltpu.VMEM((KV_LORA, nv), BF16)],
        compiler_params=_cparams(("arbitrary",), VMEM_LIMIT),
        name="mla_kv_up",
    )(ckv, kr, w_uk, w_uv)


def kernel(x_prompt, x_sample, cache_a_k, cache_a_v, cache_a_kidx, cache_b_ckv, cache_b_krope, norm_mix, norm_ffn, norm_final, rel_bias, a_w_in, a_w_out, b_w_in, b_norm_q, b_norm_kv, b_w_uq, b_w_uk, b_w_uv, b_w_out, moe_w_grp, moe_b_grp, moe_w_rtr, moe_b_rtr, moe_w_gate, moe_w_up, moe_w_down):
    n_p, s_p, d = x_prompt.shape
    n_s, s_s, _ = x_sample.shape
    past = cache_a_k.shape[2]
    tp = n_p * s_p
    ts = n_s * s_s
    t = tp + ts
    a_qd = A_HEADS * A_HEAD_DIM
    tm = _tile(math.gcd(tp, ts), 1024)
    tm2 = _tile(math.gcd(tp, ts), 512)

    x_parts = [x_prompt.reshape(tp, d), x_sample.reshape(ts, d)]

    def router_params(i):
        w_r = jnp.concatenate([moe_w_grp[i], moe_w_rtr[i],
                               jnp.zeros((d, LANES - N_GROUPS - N_EXPERTS), F32)], axis=1)
        b_r = jnp.concatenate([moe_b_grp[i], moe_b_rtr[i],
                               jnp.zeros((LANES - N_GROUPS - N_EXPERTS,), F32)]).reshape(1, LANES)
        return w_r, b_r

    u0 = rmsnorm(x_parts, norm_mix[0], tm)
    w_a = a_w_in[0]
    (q_b,) = matmul(u0, w_a, 0, a_qd, [BF16], tm, 512, "a_proj_q")
    tmp, tms = _tile(tp, 1024), _tile(ts, 1024)
    kp_f, kp_b = matmul(u0, w_a, a_qd, a_qd, [F32, BF16], tmp, 512, "a_proj_k", 0, tp)
    vp_f, vp_b = matmul(u0, w_a, 2 * a_qd, a_qd, [F32, BF16], tmp, 512, "a_proj_v", 0, tp)
    ks_f, ks_b = matmul(u0, w_a, a_qd, a_qd, [F32, BF16], tms, 512, "a_proj_k", tp, ts)
    vs_f, vs_b = matmul(u0, w_a, 2 * a_qd, a_qd, [F32, BF16], tms, 512, "a_proj_v", tp, ts)
    (qi_b,) = matmul(u0, w_a, 3 * a_qd, IDX_HEADS * IDX_DIM, [BF16], tm, 512, "a_proj_qi")
    c_ki = 3 * a_qd + IDX_HEADS * IDX_DIM
    w_ki = w_a[:, c_ki:c_ki + IDX_DIM]
    w_wi = w_a[:, c_ki + IDX_DIM:c_ki + IDX_DIM + IDX_HEADS]
    zk = jnp.zeros((d, IDX_DIM), F32)
    w_tail = jnp.concatenate([w_ki, zk, zk, w_ki, w_wi, jnp.zeros((d, LANES - IDX_HEADS), F32)], axis=1)
    (tail,) = matmul(u0, w_tail, 0, 3 * LANES, [F32], tm, 3 * LANES, "a_proj_tail")
    kidx = tail[:, :IDX_DIM]

    bias_tab = rel_bias_tables(rel_bias)
    a_scale = A_HEAD_DIM ** -0.5
    mask_p = dsa_select(qi_b, tail, 2, tail, n_seq=n_p, sq=s_p, bq=128, sk=s_p, sk_real=s_p, tk=512,
                        pos0=0, q_row0=0)
    kp_t = kp_b.reshape(n_p, s_p, a_qd).transpose(0, 2, 1).reshape(n_p * a_qd, s_p)
    att_p = attention(q_b, [(kp_t, vp_b, s_p, 0, 0)], bias_tab, mask_p, n_seq=n_p, sq=s_p, bq=ATT_BQ, q_row0=0,
                      q_col0=0, n_heads=A_HEADS, hg=ATT_HG, dq=A_HEAD_DIM, dv=A_HEAD_DIM, scale=a_scale, pos0=0,
                      dyn_sk=s_p, name="dsa_attention_prompt")
    sk_s = past + s_s
    sk_pad = (sk_s + ATT_TILE - 1) // ATT_TILE * ATT_TILE
    ki_past = cache_a_kidx[0]
    zp = jnp.zeros_like(ki_past)
    kc_past = jnp.concatenate([ki_past, zp, zp, ki_past], axis=-1)
    kc_new = tail[tp:, :2 * LANES].reshape(n_s, s_s, 2 * LANES)
    kc_s = jnp.concatenate([kc_past, kc_new, jnp.zeros((n_s, sk_pad - sk_s, 2 * LANES), F32)], axis=1)
    mask_s = dsa_select(qi_b, tail, 2, kc_s.reshape(n_s * sk_pad, 2 * LANES), n_seq=n_s, sq=s_s, bq=s_s,
                        sk=sk_pad, sk_real=sk_s, tk=sk_pad, pos0=past, q_row0=tp)
    tiles_s = [(0, r, r, ATT_TILE) for r in range(0, past, ATT_TILE)] + [(1, 0, past, s_s)]
    att_s = attention(q_b, [(cache_a_k[0].reshape(n_s * past * A_HEADS, A_HEAD_DIM),
                             cache_a_v[0].reshape(n_s * past * A_HEADS, A_HEAD_DIM), past, 0, 0),
                            (ks_b, vs_b, s_s, 0, 0)],
                      bias_tab, mask_s, n_seq=n_s, sq=s_s, bq=s_s, q_row0=tp, q_col0=0, n_heads=A_HEADS, hg=A_HEADS,
                      dq=A_HEAD_DIM, dv=A_HEAD_DIM, scale=a_scale, pos0=past, seg_tiles=tiles_s,
                      name="dsa_attention_sample")

    w_r0, b_r0 = router_params(0)
    h1, xn0, lg0 = outproj_norm_router([att_p, att_s], a_w_out, 0, x_parts, norm_ffn[0], w_r0, b_r0, tm2, 512)
    h2, u1 = hier_moe_layer(h1, xn0, lg0, moe_w_gate, moe_w_up, moe_w_down, 0, norm_mix[1], BF16)

    n_in = Q_LORA + KV_LORA + QK_ROPE
    w_b = jnp.concatenate([b_w_in[0], jnp.zeros((d, LANES - QK_ROPE), F32)], axis=1)
    (proj,) = matmul(u1, w_b, 0, n_in + LANES - QK_ROPE, [F32], tm, 384, "b_proj")
    half = QK_ROPE // 2
    inv = ROPE_THETA ** (-jnp.arange(half, dtype=F32) / half)
    pos_all = jnp.concatenate([jnp.tile(jnp.arange(s_p, dtype=I32), n_p),
                               jnp.tile(past + jnp.arange(s_s, dtype=I32), n_s)])
    ang = pos_all.astype(F32)[:, None] * inv[None, :]
    zl = jnp.zeros((t, LANES - QK_ROPE), F32)
    cos_t = jnp.concatenate([jnp.cos(ang), jnp.cos(ang), zl], axis=1)
    sin_t = jnp.concatenate([-jnp.sin(ang), jnp.sin(ang), zl], axis=1)
    cq_b, ckv_f, ckv_b, kr_f = mla_prep(proj, b_norm_q[0], b_norm_kv[0], cos_t, sin_t, tm)
    w_uq_pad = jnp.pad(b_w_uq[0].reshape(Q_LORA, B_HEADS, QK_NOPE + QK_ROPE),
                       ((0, 0), (0, 0), (0, MLA_QK_PAD - QK_NOPE - QK_ROPE))).reshape(Q_LORA, B_HEADS * MLA_QK_PAD)
    qm = mla_q_up(cq_b, w_uq_pad, cos_t, sin_t, tm2)
    km_p, vm_p = mla_kv_up(ckv_b[:tp], kr_f[:tp], b_w_uk, b_w_uv, 0, _tile(tp, 512))
    ctab = causal_tables()
    hgb = 8
    km_t = km_p.reshape(n_p, s_p, B_HEADS * MLA_QK_PAD).transpose(0, 2, 1).reshape(n_p * B_HEADS * MLA_QK_PAD, s_p)
    matt_p = attention(qm, [(km_t, vm_p, s_p, 0, 0)], ctab, None, n_seq=n_p, sq=s_p, bq=ATT_BQ, q_row0=0, q_col0=0,
                       n_heads=B_HEADS, hg=ATT_HG, dq=MLA_QK_PAD, dv=V_DIM, scale=B_SCALE, pos0=0, dyn_sk=s_p,
                       name="mla_attention_prompt")
    ckv_all = jnp.concatenate([cache_b_ckv[0], ckv_f[tp:].reshape(n_s, s_s, KV_LORA)], axis=1)
    kr_past = jnp.concatenate([cache_b_krope[0], jnp.zeros((n_s, past, LANES - QK_ROPE), F32)], axis=-1)
    kr_all = jnp.concatenate([kr_past, kr_f[tp:].reshape(n_s, s_s, LANES)], axis=1)
    km_s, vm_s = mla_kv_up(ckv_all.reshape(n_s * sk_s, KV_LORA), kr_all.reshape(n_s * sk_s, LANES),
                           b_w_uk, b_w_uv, 0, sk_s)
    tiles_m = [(0, r, r, ATT_TILE) for r in range(0, past, ATT_TILE)] + [(0, past, past, s_s)]
    matt_s = attention(qm, [(km_s, vm_s, sk_s, 0, 0)], ctab, None, n_seq=n_s, sq=s_s, bq=s_s, q_row0=tp, q_col0=0,
                       n_heads=B_HEADS, hg=hgb, dq=MLA_QK_PAD, dv=V_DIM, scale=B_SCALE, pos0=past,
                       seg_tiles=tiles_m, name="mla_attention_sample")

    w_r1, b_r1 = router_params(1)
    h3, xn1, lg1 = outproj_norm_router([matt_p, matt_s], b_w_out, 0, [h2], norm_ffn[1], w_r1, b_r1, tm2, 512)
    y_p, y_s = hier_moe_layer(h3, xn1, lg1, moe_w_gate, moe_w_up, moe_w_down, 1, norm_final, F32, split=tp)
    y_prompt = y_p.reshape(n_p, s_p, d)
    y_sample = y_s.reshape(n_s, s_s, d)

    def heads(a, n, s):
        return a.reshape(1, n, s, A_HEADS, A_HEAD_DIM)

    kr_out = kr_f[:, :QK_ROPE]
    return (y_prompt, y_sample,
            heads(kp_f, n_p, s_p), heads(vp_f, n_p, s_p), kidx[:tp].reshape(1, n_p, s_p, IDX_DIM),
            ckv_f[:tp].reshape(1, n_p, s_p, KV_LORA), kr_out[:tp].reshape(1, n_p, s_p, QK_ROPE),
            heads(ks_f, n_s, s_s), heads(vs_f, n_s, s_s), kidx[tp:].reshape(1, n_s, s_s, IDX_DIM),
            ckv_f[tp:].reshape(1, n_s, s_s, KV_LORA), kr_out[tp:].reshape(1, n_s, s_s, QK_ROPE))
```

```python
import functools
import math

import jax
import jax.numpy as jnp
from jax import lax
from jax.experimental import pallas as pl
from jax.experimental.pallas import tpu as pltpu

F32 = jnp.float32
BF16 = jnp.bfloat16
I32 = jnp.int32

RMS_EPS = 1e-6
CHUNK = 64
NEG = -1e30
INT_MIN = -2 ** 31
INT_MAX = 2 ** 31 - 1
LOG2E = math.log2(math.e)

LANES = 128
KEY_BLOCK = 128
ATT_TILE = 256
ATT_BQ = 128
ATT_HG = 8
VMEM_LIMIT = 56 * 1024 * 1024

A_HEADS = 16
A_HEAD_DIM = 128
IDX_HEADS = 16
IDX_DIM = 64
TOPK_MAX = 256
IDX_W_SCALE = float((IDX_HEADS * IDX_DIM) ** -0.5)
NUM_BUCKETS = 32
MAX_DISTANCE = 128
B_HEADS = 16
Q_LORA = 512
KV_LORA = 512
QK_NOPE = 128
QK_ROPE = 64
V_DIM = 128
ROPE_THETA = 10000.0
B_SCALE = float((QK_NOPE + QK_ROPE) ** -0.5)
MLA_QK_PAD = 256
N_GROUPS = 8
EXPERTS_PER_GROUP = 8
N_EXPERTS = 64
D_EXPERT = 512
MOE_BM = 128
ROW_CHUNKS = 16
X_AHEAD = 2


def _tile(n, pref):
    for c in range(pref, 0, -LANES):
        if n % c == 0:
            return c
    raise ValueError(f"no 128-multiple tile divides {n}")


def _cparams(sem, vmem=None):
    return pltpu.CompilerParams(dimension_semantics=sem, vmem_limit_bytes=vmem)


def _dot(a, b):
    return jnp.dot(a, b, preferred_element_type=F32)


def _dot_nt(a, b):
    return lax.dot_general(a, b, (((1,), (1,)), ((), ())), preferred_element_type=F32)


def _rms(x, g):
    ms = jnp.mean(x * x, axis=-1, keepdims=True)
    return x * lax.rsqrt(ms + RMS_EPS) * g


def _pick_rows(i, n_first, refs):
    if len(refs) == 1:
        return refs[0][...]
    return jnp.where(i < n_first, refs[0][...], refs[1][...])


def _row_split_specs(parts, tm, width, col_of):
    n_first = parts[0].shape[0] // tm
    assert all(p.shape[0] % tm == 0 for p in parts)
    specs = [pl.BlockSpec((tm, width), lambda i, *r: (jnp.minimum(i, n_first - 1), col_of(i, *r)))]
    if len(parts) == 2:
        specs.append(pl.BlockSpec((tm, width), lambda i, *r: (jnp.maximum(i - n_first, 0), col_of(i, *r))))
    return specs, n_first


def _rmsnorm_body(*refs, n_first):
    x = _pick_rows(pl.program_id(0), n_first, refs[:-2])
    g_ref, o_ref = refs[-2:]
    o_ref[...] = _rms(x, g_ref[...]).astype(o_ref.dtype)


def rmsnorm(xs, g, tm, out_dtype=BF16):
    d = xs[0].shape[1]
    t = sum(x.shape[0] for x in xs)
    specs, n_first = _row_split_specs(xs, tm, d, lambda i: 0)
    return pl.pallas_call(
        functools.partial(_rmsnorm_body, n_first=n_first),
        grid=(t // tm,),
        in_specs=specs + [pl.BlockSpec((1, d), lambda i: (0, 0))],
        out_specs=pl.BlockSpec((tm, d), lambda i: (i, 0)),
        out_shape=jax.ShapeDtypeStruct((t, d), out_dtype),
        compiler_params=_cparams(("parallel",)),
        name="rmsnorm",
    )(*xs, g.reshape(1, d))


def _matmul_body(a_ref, w_ref, *rest):
    outs, wb = rest[:-1], rest[-1]

    @pl.when(pl.program_id(1) == 0)
    def _():
        wb[...] = w_ref[...].astype(BF16)

    r = _dot(a_ref[...], wb[...])
    for o in outs:
        o[...] = r.astype(o.dtype)


def matmul(a, w, col0, ncols, out_dtypes, tm, tn, name, row0=0, nrows=None):
    k = a.shape[1]
    t = a.shape[0] if nrows is None else nrows
    assert ncols % tn == 0 and col0 % tn == 0 and t % tm == 0 and row0 % tm == 0
    cb = col0 // tn
    rb = row0 // tm
    return pl.pallas_call(
        _matmul_body,
        grid=(ncols // tn, t // tm),
        in_specs=[pl.BlockSpec((tm, k), lambda j, i: (i + rb, 0)),
                  pl.BlockSpec((k, tn), lambda j, i: (0, j + cb))],
        out_specs=[pl.BlockSpec((tm, tn), lambda j, i: (i, j)) for _ in out_dtypes],
        out_shape=[jax.ShapeDtypeStruct((t, ncols), dt) for dt in out_dtypes],
        scratch_shapes=[pltpu.VMEM((k, tn), BF16)],
        compiler_params=_cparams(("arbitrary", "arbitrary"), VMEM_LIMIT),
        name=name,
    )(a, w)


def _float_sort_key(x):
    bits = lax.bitcast_convert_type(x, I32)
    return bits ^ ((bits >> 31) & INT_MAX)


def _select_body(qi_ref, wi_ref, kc_ref, mask_ref, key_scr, jm_scr, *, bq, sk, sk_real, tk, pos0, topk):
    b = pl.program_id(1)
    t0 = pos0 + b * bq
    kmax = jnp.minimum(sk_real, ((t0 + bq - 1) // CHUNK + 1) * CHUNK)
    nkt = (kmax + tk - 1) // tk
    nch = tk // LANES

    w = wi_ref[...] * IDX_W_SCALE
    wb = [jnp.broadcast_to(w[:, h:h + 1], (bq, LANES)) for h in range(IDX_HEADS)]
    qpos = t0 + lax.broadcasted_iota(I32, (LANES, LANES), 1)
    lim = jnp.minimum((qpos // CHUNK + 1) * CHUNK, sk_real)
    sub = lax.broadcasted_iota(I32, (LANES, LANES), 0)

    def to_lanes(x):
        if bq < LANES:
            x = jnp.concatenate([x, jnp.zeros((LANES - bq, LANES), x.dtype)], axis=0)
        return x.T

    def score_tile(j, carry):
        off = pl.multiple_of(j * tk, tk)
        kc = kc_ref[pl.ds(off, tk), :].astype(BF16)
        ka, kb = kc[:, :LANES], kc[:, LANES:]
        accs = [jnp.zeros((bq, LANES), F32) for _ in range(nch)]
        for g in range(IDX_HEADS // 2):
            qg = qi_ref[:, g * LANES:(g + 1) * LANES]
            sa = _dot_nt(qg, ka)
            sb = _dot_nt(qg, kb)
            for c in range(nch):
                sl = slice(c * LANES, (c + 1) * LANES)
                accs[c] = accs[c] + wb[2 * g] * jnp.maximum(sa[:, sl], 0.0) \
                    + wb[2 * g + 1] * jnp.maximum(sb[:, sl], 0.0)
        for c in range(nch):
            kpos = off + c * LANES + sub
            sc = jnp.where(kpos < lim, to_lanes(accs[c]), -jnp.inf)
            key_scr[pl.ds(off + c * LANES, LANES), :] = _float_sort_key(sc)
        return carry

    lax.fori_loop(0, nkt, score_tile, 0)

    def count(indicator):
        def tile(j, cnt):
            off = pl.multiple_of(j * tk, tk)
            for c in range(nch):
                kt = key_scr[pl.ds(off + c * LANES, LANES), :]
                cnt = cnt + indicator(kt, off + c * LANES + sub)
            return cnt
        cnt = lax.fori_loop(0, nkt, tile, jnp.zeros((LANES, LANES), F32))
        return jnp.sum(cnt, axis=0, keepdims=True)

    def bit_step(i, pfx_u):
        bit = lax.shift_left(jnp.int32(1), 31 - i)
        cand_u = pfx_u | bit
        cand_s = cand_u ^ INT_MIN
        total = count(lambda kt, kp: jnp.where(kt >= cand_s, 1.0, 0.0))
        return jnp.where(total >= topk, cand_u, pfx_u)

    pfx = lax.fori_loop(0, 32, bit_step, jnp.zeros((1, LANES), I32))
    thr = pfx ^ INT_MIN

    n_gt = count(lambda kt, kp: jnp.where(kt > thr, 1.0, 0.0))
    n_ge = count(lambda kt, kp: jnp.where(kt >= thr, 1.0, 0.0))
    quota = topk - n_gt
    jm_scr[...] = jnp.full((1, LANES), INT_MAX, I32)
    any_excess = jnp.max(jnp.where(n_ge > topk, 1.0, 0.0)) > 0.0

    @pl.when(any_excess)
    def _():
        nbits = max(1, int(sk - 1).bit_length())

        def idx_step(i, ans):
            cand = ans | lax.shift_left(jnp.int32(1), nbits - 1 - i)
            below = count(lambda kt, kp: jnp.where(kt == thr, jnp.where(kp < cand, 1.0, 0.0), 0.0))
            return jnp.where(below < quota, cand, ans)

        jm_scr[...] = lax.fori_loop(0, nbits, idx_step, jnp.zeros((1, LANES), I32))

    jm = jm_scr[...]

    def write_tile(j, carry):
        off = pl.multiple_of(j * tk, tk)
        for c in range(nch):
            kt = key_scr[pl.ds(off + c * LANES, LANES), :]
            kpos = off + c * LANES + sub
            v = jnp.where(kt > thr, 0.0, jnp.where(kt == thr, jnp.where(kpos <= jm, 0.0, NEG), NEG))
            v = jnp.where(kpos < lim, v, NEG)
            mask_ref[:, pl.ds(off + c * LANES, LANES)] = v.T[:bq].astype(mask_ref.dtype)
        return carry

    lax.fori_loop(0, nkt, write_tile, 0)

    def fill_tile(j, carry):
        off = pl.multiple_of(j * tk, tk)
        mask_ref[:, pl.ds(off, tk)] = jnp.full((bq, tk), NEG, mask_ref.dtype)
        return carry

    lax.fori_loop(nkt, sk // tk, fill_tile, 0)


def dsa_select(qi, wi_arr, wi_blk, kc, *, n_seq, sq, bq, sk, sk_real, tk, pos0, q_row0):
    nqb = sq // bq
    qb0 = q_row0 // bq
    assert q_row0 % bq == 0 and sk % tk == 0 and tk >= TOPK_MAX
    topk = min(TOPK_MAX, sk_real // 4)
    body = functools.partial(_select_body, bq=bq, sk=sk, sk_real=sk_real, tk=tk, pos0=pos0, topk=topk)
    return pl.pallas_call(
        body,
        grid=(n_seq, nqb),
        in_specs=[pl.BlockSpec((bq, IDX_HEADS * IDX_DIM), lambda s, b: (qb0 + s * nqb + b, 0)),
                  pl.BlockSpec((bq, LANES), lambda s, b: (qb0 + s * nqb + b, wi_blk)),
                  pl.BlockSpec((sk, 2 * LANES), lambda s, b: (s, 0))],
        out_specs=pl.BlockSpec((bq, sk), lambda s, b: (s * nqb + b, 0)),
        out_shape=jax.ShapeDtypeStruct((n_seq * sq, sk), BF16),
        scratch_shapes=[pltpu.VMEM((sk, LANES), I32), pltpu.VMEM((1, LANES), I32)],
        compiler_params=_cparams(("parallel", "arbitrary"), VMEM_LIMIT),
        name="dsa_select",
    )(qi, wi_arr, kc)


def _rel_tables_body(bias_ref, tab_ref):
    i = lax.broadcasted_iota(I32, (KEY_BLOCK, KEY_BLOCK), 0)
    j = lax.broadcasted_iota(I32, (KEY_BLOCK, KEY_BLOCK), 1)
    nb = NUM_BUCKETS // 2
    max_exact = nb // 2
    edges = [12, 16, 23, 32, 46, 64, 91]
    for d in range(4):
        rel = j - i + (d - 2) * KEY_BLOCK
        n = jnp.abs(rel)
        large = jnp.full_like(n, max_exact)
        for e in edges:
            large = large + jnp.where(n >= e, 1, 0)
        bucket = jnp.where(rel > 0, nb, 0) + jnp.where(n < max_exact, n, large)
        for h in range(A_HEADS):
            acc = jnp.zeros((KEY_BLOCK, KEY_BLOCK), F32)
            for k in range(NUM_BUCKETS):
                acc = jnp.where(bucket == k, bias_ref[k, h], acc)
            tab_ref[h, d] = acc * LOG2E


def rel_bias_tables(rel_bias):
    return pl.pallas_call(
        _rel_tables_body,
        in_specs=[pl.BlockSpec(memory_space=pltpu.SMEM)],
        out_shape=jax.ShapeDtypeStruct((A_HEADS, 4, KEY_BLOCK, KEY_BLOCK), F32),
        name="rel_bias_tables",
    )(rel_bias)


def _causal_tables_body(tab_ref):
    i = lax.broadcasted_iota(I32, (KEY_BLOCK, KEY_BLOCK), 0)
    j = lax.broadcasted_iota(I32, (KEY_BLOCK, KEY_BLOCK), 1)
    zero = jnp.zeros((KEY_BLOCK, KEY_BLOCK), F32)
    tab_ref[0, 0] = zero
    tab_ref[0, 1] = zero
    tab_ref[0, 2] = jnp.where(j // CHUNK <= i // CHUNK, 0.0, NEG)
    tab_ref[0, 3] = jnp.full((KEY_BLOCK, KEY_BLOCK), NEG, F32)


def causal_tables():
    return pl.pallas_call(
        _causal_tables_body,
        out_shape=jax.ShapeDtypeStruct((1, 4, KEY_BLOCK, KEY_BLOCK), F32),
        name="causal_tables",
    )()


def _attn_pipe_body(*refs, has_mask, per_head_tab, bq, hg, dq, dv, scale, pos0, sk):
    q_ref, k_ref, v_ref, tab_ref = refs[:4]
    mask_ref = refs[4] if has_mask else None
    out_ref = refs[5] if has_mask else refs[4]
    s_scr, p_scr, m_scr, l_scr, acc_scr = refs[-5:]

    b = pl.program_id(2)
    t0 = pos0 + b * bq
    qblk = t0 // KEY_BLOCK
    kmax = jnp.minimum(sk, ((t0 + bq - 1) // CHUNK + 1) * CHUNK)
    nkt = (kmax + ATT_TILE - 1) // ATT_TILE
    nch = ATT_TILE // LANES

    m_scr[...] = jnp.full(m_scr.shape, NEG, F32)
    l_scr[...] = jnp.zeros(l_scr.shape, F32)
    acc_scr[...] = jnp.zeros(acc_scr.shape, F32)

    def paired_loop(tile):
        def four(jj, carry):
            for u in range(4):
                tile(4 * jj + u, carry)
            return carry

        n4 = nkt // 4
        lax.fori_loop(0, n4, four, 0)

        @pl.when(nkt % 4 >= 2)
        def _():
            tile(4 * n4, 0)
            tile(4 * n4 + 1, 0)

        @pl.when(nkt % 2 == 1)
        def _():
            tile(nkt - 1, 0)

    def logits_tile(j, carry):
        off = pl.multiple_of(j * ATT_TILE, ATT_TILE)
        kb0 = off // KEY_BLOCK
        ds = [[jnp.clip(kb0 + c - (qblk + r) + 2, 0, 3) for c in range(nch)] for r in range(bq // KEY_BLOCK)]
        if has_mask:
            mk = mask_ref[:, pl.ds(off, ATT_TILE)].astype(F32)
        for h in range(hg):
            s = _dot(q_ref[:, h * dq:(h + 1) * dq], k_ref[h * dq:(h + 1) * dq, pl.ds(off, ATT_TILE)]) * scale
            hh = h if per_head_tab else 0
            s = s + jnp.concatenate(
                [jnp.concatenate([tab_ref[hh, d] for d in row], axis=1) for row in ds], axis=0)
            if has_mask:
                s = s + mk
            s_scr[h, :, pl.ds(off, ATT_TILE)] = s
            mvec = m_scr[h]
            for c in range(nch):
                mvec = jnp.maximum(mvec, s[:, c * LANES:(c + 1) * LANES])
            m_scr[h] = mvec
        return carry

    paired_loop(logits_tile)
    for h in range(hg):
        m_scr[h] = jnp.broadcast_to(jnp.max(m_scr[h], axis=1, keepdims=True), (bq, LANES))

    def exp_tile(j, carry):
        off = pl.multiple_of(j * ATT_TILE, ATT_TILE)
        for h in range(hg):
            s = s_scr[h, :, pl.ds(off, ATT_TILE)]
            m = m_scr[h]
            p = [jnp.exp2(s[:, c * LANES:(c + 1) * LANES] - m) for c in range(nch)]
            lvec = l_scr[h]
            for c in range(nch):
                lvec = lvec + p[c]
            l_scr[h] = lvec
            p_scr[h, :, pl.ds(off, ATT_TILE)] = jnp.concatenate(p, axis=1).astype(BF16)
        return carry

    paired_loop(exp_tile)

    def pv_tile(j, carry):
        off = pl.multiple_of(j * ATT_TILE, ATT_TILE)
        for h in range(hg):
            acc_scr[h] = acc_scr[h] + _dot(p_scr[h, :, pl.ds(off, ATT_TILE)],
                                           v_ref[pl.ds(off, ATT_TILE), h * dv:(h + 1) * dv])
        return carry

    paired_loop(pv_tile)
    for h in range(hg):
        l_row = jnp.sum(l_scr[h], axis=1, keepdims=True)
        out_ref[:, h * dv:(h + 1) * dv] = (acc_scr[h] / l_row).astype(out_ref.dtype)


def _attn_body(*refs, n_seg, has_mask, per_head_tab, bq, hg, dq, dv, scale, pos0, seg_tiles, by_head):
    q_ref = refs[0]
    kv = refs[1:1 + 2 * n_seg]
    pos = 1 + 2 * n_seg
    tab_ref = refs[pos]
    pos += 1
    mask_ref = refs[pos] if has_mask else None
    pos += 1 if has_mask else 0
    out_ref = refs[pos]
    s_scr, p_scr = refs[pos + 1:pos + 3]

    b = pl.program_id(2)
    t0 = pos0 + b * bq
    qblk = t0 // KEY_BLOCK

    def lane_pad(x, fill):
        w = x.shape[1]
        return x if w == LANES else jnp.concatenate([x, jnp.full((bq, LANES - w), fill, F32)], axis=1)

    mvec = [jnp.full((bq, LANES), NEG, F32) for _ in range(hg)]
    for (si, row0, key0, width) in seg_tiles:
        nch = max(1, width // KEY_BLOCK)
        cw = min(width, KEY_BLOCK)
        ds = [jnp.clip(key0 // KEY_BLOCK + c - qblk + 2, 0, 3) for c in range(nch)]
        k_t = None if by_head[si] else kv[2 * si][row0:row0 + width, :].astype(BF16)
        if has_mask:
            mk = mask_ref[:, key0:key0 + width].astype(F32)
        for h in range(hg):
            hh = h if per_head_tab else 0
            if by_head[si]:
                k_h = kv[2 * si][pl.ds(row0 * hg + h, width, stride=hg), :].astype(BF16)
            else:
                k_h = k_t[:, h * dq:(h + 1) * dq]
            s = _dot_nt(q_ref[:, h * dq:(h + 1) * dq], k_h) * scale
            bias = [tab_ref[hh, ds[c], :bq, :cw] for c in range(nch)]
            s = s + (jnp.concatenate(bias, axis=1) if nch > 1 else bias[0])
            if has_mask:
                s = s + mk
            s_scr[h, :, key0:key0 + width] = s
            for c in range(nch):
                mvec[h] = jnp.maximum(mvec[h], lane_pad(s[:, c * cw:(c + 1) * cw], NEG))

    m = [jnp.broadcast_to(jnp.max(mvec[h], axis=1, keepdims=True), (bq, LANES)) for h in range(hg)]
    lvec = [jnp.zeros((bq, LANES), F32) for _ in range(hg)]
    for (si, row0, key0, width) in seg_tiles:
        nch = max(1, width // KEY_BLOCK)
        cw = min(width, KEY_BLOCK)
        for h in range(hg):
            s = s_scr[h, :, key0:key0 + width]
            p = [jnp.exp2(s[:, c * cw:(c + 1) * cw] - m[h][:, :cw]) for c in range(nch)]
            for c in range(nch):
                lvec[h] = lvec[h] + lane_pad(p[c], 0.0)
            p_scr[h, :, key0:key0 + width] = (jnp.concatenate(p, axis=1) if nch > 1 else p[0]).astype(BF16)

    acc = [jnp.zeros((bq, dv), F32) for _ in range(hg)]
    for (si, row0, key0, width) in seg_tiles:
        v_t = None if by_head[si] else kv[2 * si + 1][row0:row0 + width, :].astype(BF16)
        for h in range(hg):
            if by_head[si]:
                v_h = kv[2 * si + 1][pl.ds(row0 * hg + h, width, stride=hg), :].astype(BF16)
            else:
                v_h = v_t[:, h * dv:(h + 1) * dv]
            acc[h] = acc[h] + _dot(p_scr[h, :, key0:key0 + width], v_h)

    for h in range(hg):
        l_row = jnp.sum(lvec[h], axis=1, keepdims=True)
        out_ref[:, h * dv:(h + 1) * dv] = (acc[h] / l_row).astype(out_ref.dtype)


def attention(q, segs, tab, mask, *, n_seq, sq, bq, q_row0, q_col0, n_heads, hg, dq, dv, scale, pos0,
              seg_tiles=None, dyn_sk=None, name="attention"):
    nqb = sq // bq
    ng = n_heads // hg
    qb0 = q_row0 // bq
    assert q_row0 % bq == 0 and q_col0 % (hg * dq) == 0 and pos0 % KEY_BLOCK == 0
    scale = scale * LOG2E
    assert bq % KEY_BLOCK == 0 or nqb == 1
    qc0 = q_col0 // (hg * dq)
    in_specs = [pl.BlockSpec((bq, hg * dq), lambda s, g, b: (qb0 + s * nqb + b, qc0 + g))]
    args = [q]
    by_head = []
    for (k, v, rows, kc0, vc0) in segs:
        assert kc0 % (hg * dq) == 0 and vc0 % (hg * dv) == 0
        by_head.append(k.shape[1] == dq and n_heads > 1)
        if by_head[-1]:
            assert hg == n_heads and dq == dv and k.shape[0] == n_seq * rows * n_heads and dyn_sk is None
            for _ in range(2):
                in_specs.append(pl.BlockSpec((rows * n_heads, dq), lambda s, g, b: (s, 0)))
            args += [k, v]
            continue
        mode = dict(pipeline_mode=pl.Buffered(1)) if dyn_sk is not None else {}
        if dyn_sk is not None:
            assert kc0 == 0 and k.shape == (n_seq * n_heads * dq, rows)
            in_specs.append(pl.BlockSpec((hg * dq, rows), lambda s, g, b: (s * ng + g, 0), **mode))
        else:
            in_specs.append(pl.BlockSpec((rows, hg * dq), lambda s, g, b, c=kc0 // (hg * dq): (s, c + g)))
        in_specs.append(pl.BlockSpec((rows, hg * dv), lambda s, g, b, c=vc0 // (hg * dv): (s, c + g), **mode))
        args += [k, v]
    per_head_tab = tab.shape[0] > 1
    if per_head_tab:
        in_specs.append(pl.BlockSpec((hg, 4, KEY_BLOCK, KEY_BLOCK), lambda s, g, b: (g, 0, 0, 0)))
    else:
        in_specs.append(pl.BlockSpec((1, 4, KEY_BLOCK, KEY_BLOCK), lambda s, g, b: (0, 0, 0, 0)))
    args.append(tab)
    if mask is not None:
        in_specs.append(pl.BlockSpec((bq, mask.shape[1]), lambda s, g, b: (s * nqb + b, 0)))
        args.append(mask)
    if dyn_sk is not None:
        assert len(segs) == 1 and dyn_sk % ATT_TILE == 0 and bq % KEY_BLOCK == 0
        body = functools.partial(_attn_pipe_body, has_mask=mask is not None, per_head_tab=per_head_tab,
                                 bq=bq, hg=hg, dq=dq, dv=dv, scale=scale, pos0=pos0, sk=dyn_sk)
        scratch = [pltpu.VMEM((hg, bq, dyn_sk), F32), pltpu.VMEM((hg, bq, dyn_sk), BF16),
                   pltpu.VMEM((hg, bq, LANES), F32), pltpu.VMEM((hg, bq, LANES), F32),
                   pltpu.VMEM((hg, bq, dv), F32)]
    else:
        body = functools.partial(_attn_body, n_seg=len(segs), has_mask=mask is not None,
                                 per_head_tab=per_head_tab, bq=bq, hg=hg, dq=dq, dv=dv, scale=scale, pos0=pos0,
                                 seg_tiles=seg_tiles, by_head=tuple(by_head))
        sk_tot = max(key0 + width for (_, _, key0, width) in seg_tiles)
        sk_tot = (sk_tot + LANES - 1) // LANES * LANES
        scratch = [pltpu.VMEM((hg, bq, sk_tot), F32), pltpu.VMEM((hg, bq, sk_tot), BF16)]
    return pl.pallas_call(
        body,
        grid=(n_seq, ng, nqb),
        in_specs=in_specs,
        out_specs=pl.BlockSpec((bq, hg * dv), lambda s, g, b: (s * nqb + b, g)),
        out_shape=jax.ShapeDtypeStruct((n_seq * sq, n_heads * dv), BF16),
        scratch_shapes=scratch,
        compiler_params=_cparams(("parallel", "parallel", "arbitrary"), VMEM_LIMIT),
        name=name,
    )(*args)


def _outproj_body(*refs, tn, na, nx, a_first, x_first):
    i = pl.program_id(0)
    j = pl.program_id(1)
    a = _pick_rows(i, a_first, refs[:na])
    w_ref = refs[na]
    x = _pick_rows(i, x_first, refs[na + 1:na + 1 + nx])
    g_ref, wr_ref, br_ref, h_ref, xn_ref, lg_ref, hrow, wb = refs[na + 1 + nx:]
    col = pl.ds(pl.multiple_of(j * tn, tn), tn)

    @pl.when(i == 0)
    def _():
        wb[:, col] = w_ref[0].astype(BF16)

    r = x + _dot(a, wb[:, col])
    h_ref[...] = r
    hrow[:, col] = r

    @pl.when(j == pl.num_programs(1) - 1)
    def _():
        xn = _rms(hrow[...], g_ref[...])
        tm = hrow.shape[0]
        for c in range(ROW_CHUNKS):
            xn_ref[pl.ds(c, tm, stride=ROW_CHUNKS), :] = xn[:, c * LANES:(c + 1) * LANES]
        lg_ref[...] = _dot(xn.astype(BF16), wr_ref[...].astype(BF16)) + br_ref[...]


def outproj_norm_router(a_parts, w_out, layer, x_parts, g_ffn, w_r, b_r, tm, tn):
    k = a_parts[0].shape[1]
    t = sum(p.shape[0] for p in a_parts)
    d = w_out.shape[-1]
    a_specs, a_first = _row_split_specs(a_parts, tm, k, lambda i, j: 0)
    x_specs, x_first = _row_split_specs(x_parts, tm, tn, lambda i, j: j)
    nj = d // tn
    return pl.pallas_call(
        functools.partial(_outproj_body, tn=tn, na=len(a_parts), nx=len(x_parts), a_first=a_first,
                          x_first=x_first),
        grid=(t // tm, d // tn),
        in_specs=a_specs + [pl.BlockSpec((1, k, tn), lambda i, j: (layer, 0, jnp.where(i == 0, j, nj - 1)))] + x_specs + [
            pl.BlockSpec((1, d), lambda i, j: (0, 0)),
            pl.BlockSpec((d, LANES), lambda i, j: (0, 0)),
            pl.BlockSpec((1, LANES), lambda i, j: (0, 0))],
        out_specs=[pl.BlockSpec((tm, tn), lambda i, j: (i, j)),
                   pl.BlockSpec((tm * ROW_CHUNKS, LANES), lambda i, j: (i, 0)),
                   pl.BlockSpec((tm, LANES), lambda i, j: (i, 0))],
        out_shape=[jax.ShapeDtypeStruct((t, d), F32), jax.ShapeDtypeStruct((t * ROW_CHUNKS, LANES), F32),
                   jax.ShapeDtypeStruct((t, LANES), F32)],
        scratch_shapes=[pltpu.VMEM((tm, d), F32), pltpu.VMEM((k, d), BF16)],
        compiler_params=_cparams(("arbitrary", "arbitrary"), VMEM_LIMIT),
        name="outproj_norm_router",
    )(*a_parts, w_out, *x_parts, g_ffn.reshape(1, d), w_r, b_r)


def _routing_body(lg_ref, out_ref, cnt_ref, carry, *, tm):
    i = pl.program_id(0)

    @pl.when(i == 0)
    def _():
        carry[...] = jnp.zeros(carry.shape, F32)

    x = lg_ref[...]
    lane = lax.broadcasted_iota(I32, (tm, LANES), 1)
    neg_inf = -jnp.inf

    def rmax(v):
        return jnp.max(v, axis=1, keepdims=True)

    def rmin(v):
        return jnp.min(v, axis=1, keepdims=True)

    def rsum(v):
        return jnp.sum(v, axis=1, keepdims=True)

    gm = lane < N_GROUPS
    gmax = rmax(jnp.where(gm, x, neg_inf))
    gsel = rmin(jnp.where(gm, jnp.where(x == gmax, lane, LANES), LANES))
    gsum = rsum(jnp.where(gm, jnp.exp(x - gmax), 0.0))
    g_w = 1.0 / gsum
    lo = N_GROUPS + gsel * EXPERTS_PER_GROUP
    em = jnp.logical_and(lane >= lo, lane < lo + EXPERTS_PER_GROUP)
    emax = rmax(jnp.where(em, x, neg_inf))
    ee = jnp.where(em, jnp.exp(x - emax), 0.0)
    p = ee / rsum(ee)
    p1 = rmax(jnp.where(em, p, -1.0))
    i1 = rmin(jnp.where(em, jnp.where(p == p1, lane, LANES), LANES))
    em2 = jnp.logical_and(em, lane != i1)
    p2 = rmax(jnp.where(em2, p, -1.0))
    i2 = rmin(jnp.where(em2, jnp.where(p == p2, lane, LANES), LANES))
    den = p1 + p2
    g1 = g_w * (p1 / den)
    g2 = g_w * (p2 / den)

    oh1 = jnp.where(lane == i1, 1.0, 0.0)
    oh2 = jnp.where(lane == i2, 1.0, 0.0)
    oh = oh1 + oh2
    r = lax.broadcasted_iota(I32, (tm, tm), 0)
    c = lax.broadcasted_iota(I32, (tm, tm), 1)
    lower = jnp.where(c < r, 1.0, 0.0).astype(BF16)
    before = _dot(lower, oh.astype(BF16)) + carry[...]
    rank1 = rsum(oh1 * before)
    rank2 = rsum(oh2 * before)
    carry[...] = carry[...] + jnp.sum(oh, axis=0, keepdims=True)

    e1 = (i1 - N_GROUPS).astype(F32)
    e2 = (i2 - N_GROUPS).astype(F32)
    vals = [e1, e2, rank1, rank2, g1, g2]
    out = jnp.zeros((tm, LANES), F32)
    for k, v in enumerate(vals):
        out = jnp.where(lane == k, v, out)
    out_ref[...] = out
    cnt_ref[...] = carry[...]


def moe_routing(logits, tm):
    t = logits.shape[0]
    return pl.pallas_call(
        functools.partial(_routing_body, tm=tm),
        grid=(t // tm,),
        in_specs=[pl.BlockSpec((tm, LANES), lambda i: (i, 0))],
        out_specs=[pl.BlockSpec((tm, LANES), lambda i: (i, 0)), pl.BlockSpec((1, LANES), lambda i: (0, 0))],
        out_shape=[jax.ShapeDtypeStruct((t, LANES), F32), jax.ShapeDtypeStruct((1, LANES), F32)],
        scratch_shapes=[pltpu.VMEM((1, LANES), F32)],
        compiler_params=_cparams(("arbitrary",)),
        name="moe_routing",
    )(logits)


def _slot_tokens_body(slots_ref, init_hbm, tok_ref, sem):
    init = pltpu.make_async_copy(init_hbm, tok_ref, sem)
    init.start()
    init.wait()

    def put(a, c):
        tok_ref[slots_ref[a]] = lax.shift_right_logical(a, 1)
        return c

    lax.fori_loop(0, slots_ref.shape[0], put, 0, unroll=8)


def moe_slot_tokens(slots, n_slots):
    init = jnp.arange(n_slots, dtype=I32) % (slots.shape[0] // 2)
    return pl.pallas_call(
        _slot_tokens_body,
        in_specs=[pl.BlockSpec(memory_space=pltpu.SMEM), pl.BlockSpec(memory_space=pl.ANY)],
        out_specs=pl.BlockSpec(memory_space=pltpu.SMEM),
        out_shape=jax.ShapeDtypeStruct((n_slots,), I32),
        scratch_shapes=[pltpu.SemaphoreType.DMA],
        name="moe_slot_tokens",
    )(slots, init)


def _dispatch_body(nslot_ref, tok_ref, x_hbm, xs_ref, buf, sem, *, rows):
    i = pl.program_id(0)
    ni = pl.num_programs(0)
    n_used = nslot_ref[0]

    def issue(blk, s):
        @pl.when(blk * rows < n_used)
        def _():
            def pair(rp, carry):
                for k in range(2):
                    r = 2 * rp + k
                    src = pl.multiple_of(tok_ref[blk * rows + r] * ROW_CHUNKS, ROW_CHUNKS)
                    dst = pl.multiple_of(r * ROW_CHUNKS, ROW_CHUNKS)
                    pltpu.make_async_copy(x_hbm.at[pl.ds(src, ROW_CHUNKS), :],
                                          buf.at[s, pl.ds(dst, ROW_CHUNKS), :], sem.at[s]).start(priority=k)
                return carry

            lax.fori_loop(0, rows // 2, pair, 0, unroll=4)

    @pl.when(i == 0)
    def _():
        issue(0, 0)

    @pl.when(i + 1 < ni)
    def _():
        issue(i + 1, (i + 1) % 2)

    slot = i % 2

    @pl.when(i * rows < n_used)
    def _():
        pltpu.make_async_copy(x_hbm.at[pl.ds(0, rows * ROW_CHUNKS), :], buf.at[slot], sem.at[slot]).wait()
        for j in range(ROW_CHUNKS):
            xs_ref[:, j * LANES:(j + 1) * LANES] = \
                buf[slot, pl.ds(j, rows, stride=ROW_CHUNKS), :].astype(xs_ref.dtype)

    @pl.when(i * rows >= n_used)
    def _():
        xs_ref[...] = jnp.zeros(xs_ref.shape, xs_ref.dtype)


def moe_dispatch(xn, slot_tok, n_used, rows):
    assert xn.shape[1] == LANES
    d = ROW_CHUNKS * LANES
    n_slots = slot_tok.shape[0]
    assert n_slots % rows == 0
    grid_spec = pltpu.PrefetchScalarGridSpec(
        num_scalar_prefetch=2,
        grid=(n_slots // rows,),
        in_specs=[pl.BlockSpec(memory_space=pl.ANY)],
        out_specs=pl.BlockSpec((rows, d), lambda i, ns, tk: (i, 0)),
        scratch_shapes=[pltpu.VMEM((2, rows * ROW_CHUNKS, LANES), F32), pltpu.SemaphoreType.DMA((2,))],
    )
    return pl.pallas_call(
        functools.partial(_dispatch_body, rows=rows),
        grid_spec=grid_spec,
        out_shape=jax.ShapeDtypeStruct((n_slots, d), BF16),
        compiler_params=_cparams(("arbitrary",), VMEM_LIMIT),
        name="moe_dispatch",
    )(n_used, slot_tok, xn)


def _experts_body(blk0_ref, nblk_ref, ntot_ref, xs_hbm, wg_ref, wu_ref, wd_ref, ys_hbm,
                  xbuf, ybuf, zbuf, xsem, ysem, zsem, wgb, wub, wdb, *, bm):
    e = pl.program_id(0)
    n_total = ntot_ref[0]

    def x_copy(g, s):
        return pltpu.make_async_copy(xs_hbm.at[pl.ds(g * bm, bm), :], xbuf.at[s], xsem.at[s])

    def y_copy(g, s):
        return pltpu.make_async_copy(ybuf.at[s], ys_hbm.at[pl.ds(g * bm, bm), :], ysem.at[s])

    for g0 in range(X_AHEAD):
        @pl.when(jnp.logical_and(e == 0, n_total > g0))
        def _():
            x_copy(g0, g0).start(priority=1)

    @pl.when(nblk_ref[e] > 0)
    def _():
        wgb[...] = wg_ref[0, 0].astype(BF16)
        wub[...] = wu_ref[0, 0].astype(BF16)
        wdb[...] = wd_ref[0, 0].astype(BF16)

    def block(i, carry):
        g = blk0_ref[e] + i
        s = g % 2
        sx = g % (X_AHEAD + 1)

        @pl.when(g + X_AHEAD < n_total)
        def _():
            x_copy(g + X_AHEAD, (g + X_AHEAD) % (X_AHEAD + 1)).start(priority=1)

        x_copy(g, sx).wait()
        x = xbuf[sx]
        gt = _dot(x, wgb[...])
        up = _dot(x, wub[...])
        a = (gt * (1.0 / (1.0 + jnp.exp(-gt)))) * up
        y = _dot(a.astype(BF16), wdb[...])

        @pl.when(g >= 2)
        def _():
            y_copy(g - 2, s).wait()

        ybuf[s] = y
        y_copy(g, s).start()
        return carry

    lax.fori_loop(0, nblk_ref[e], block, 0)

    @pl.when(e == pl.num_programs(0) - 1)
    def _():
        for back in (2, 1):
            @pl.when(n_total >= back)
            def _():
                y_copy(n_total - back, (n_total - back) % 2).wait()

        def z_copy(g):
            return pltpu.make_async_copy(zbuf, ys_hbm.at[pl.ds(g * bm, bm), :], zsem)

        zbuf[...] = jnp.zeros(zbuf.shape, F32)
        n_blocks = ys_hbm.shape[0] // bm
        lax.fori_loop(n_total, n_blocks, lambda g, c: (z_copy(g).start(), c)[1], 0)
        lax.fori_loop(n_total, n_blocks, lambda g, c: (z_copy(g).wait(), c)[1], 0)


def moe_experts(xs, blk0, nblk, n_total, w_gate, w_up, w_down, layer, bm):
    n_slots, d = xs.shape
    n_exp = blk0.shape[0]
    de = w_gate.shape[-1]
    grid_spec = pltpu.PrefetchScalarGridSpec(
        num_scalar_prefetch=3,
        grid=(n_exp,),
        in_specs=[pl.BlockSpec(memory_space=pl.ANY),
                  pl.BlockSpec((1, 1, d, de), lambda e, b0, nb, nt: (layer, e, 0, 0)),
                  pl.BlockSpec((1, 1, d, de), lambda e, b0, nb, nt: (layer, e, 0, 0)),
                  pl.BlockSpec((1, 1, de, d), lambda e, b0, nb, nt: (layer, e, 0, 0))],
        out_specs=pl.BlockSpec(memory_space=pl.ANY),
        scratch_shapes=[pltpu.VMEM((X_AHEAD + 1, bm, d), BF16), pltpu.VMEM((2, bm, d), F32),
                        pltpu.VMEM((bm, d), F32),
                        pltpu.SemaphoreType.DMA((X_AHEAD + 1,)), pltpu.SemaphoreType.DMA((2,)),
                        pltpu.SemaphoreType.DMA,
                        pltpu.VMEM((d, de), BF16), pltpu.VMEM((d, de), BF16), pltpu.VMEM((de, d), BF16)],
    )
    return pl.pallas_call(
        functools.partial(_experts_body, bm=bm),
        grid_spec=grid_spec,
        out_shape=jax.ShapeDtypeStruct((n_slots, d), F32),
        compiler_params=_cparams(("arbitrary",), VMEM_LIMIT),
        name="moe_experts",
    )(blk0, nblk, n_total, xs, w_gate, w_up, w_down)


def _combine_body(slot_ref, ys_hbm, h_ref, rt_ref, g_ref, *rest, tm, tile0, want_h):
    outs, (buf, sem) = rest[:-2], rest[-2:]
    i = pl.program_id(0)
    ni = pl.num_programs(0)

    def row_copy(src_row, k, r, s):
        return pltpu.make_async_copy(ys_hbm.at[pl.ds(src_row, 1), :], buf.at[s, k, pl.ds(r, 1), :], sem.at[s])

    def issue(tile, s):
        def one(r, carry):
            base = ((tile0 + tile) * tm + r) * 2
            row_copy(slot_ref[base], 0, r, s).start(priority=0)
            row_copy(slot_ref[base + 1], 1, r, s).start(priority=1)
            return carry

        lax.fori_loop(0, tm, one, 0, unroll=8)

    @pl.when(i == 0)
    def _():
        issue(0, 0)

    @pl.when(i + 1 < ni)
    def _():
        issue(i + 1, (i + 1) % 2)

    slot = i % 2
    for k in range(2):
        pltpu.make_async_copy(ys_hbm.at[pl.ds(0, tm), :], buf.at[slot, k], sem.at[slot]).wait()

    rt = rt_ref[...]
    g1 = rt[:, 4:5]
    g2 = rt[:, 5:6]
    h2 = h_ref[...] + (buf[slot, 0] * g1 + buf[slot, 1] * g2)
    if want_h:
        outs[0][...] = h2
    outs[-1][...] = _rms(h2, g_ref[...]).astype(outs[-1].dtype)


def moe_combine(ys, slots, h1, route, g_next, tm, u_dtype, row0=0, nrows=None, want_h=True):
    d = h1.shape[1]
    t = h1.shape[0] if nrows is None else nrows
    assert row0 % tm == 0 and t % tm == 0
    tile0 = row0 // tm
    row_spec = pl.BlockSpec((tm, d), lambda i, sl: (i, 0))
    grid_spec = pltpu.PrefetchScalarGridSpec(
        num_scalar_prefetch=1,
        grid=(t // tm,),
        in_specs=[pl.BlockSpec(memory_space=pl.ANY),
                  pl.BlockSpec((tm, d), lambda i, sl: (i + tile0, 0)),
                  pl.BlockSpec((tm, LANES), lambda i, sl: (i + tile0, 0)),
                  pl.BlockSpec((1, d), lambda i, sl: (0, 0))],
        out_specs=[row_spec, row_spec] if want_h else [row_spec],
        scratch_shapes=[pltpu.VMEM((2, 2, tm, d), F32), pltpu.SemaphoreType.DMA((2,))],
    )
    shapes = [jax.ShapeDtypeStruct((t, d), F32)] if want_h else []
    return pl.pallas_call(
        functools.partial(_combine_body, tm=tm, tile0=tile0, want_h=want_h),
        grid_spec=grid_spec,
        out_shape=shapes + [jax.ShapeDtypeStruct((t, d), u_dtype)],
        compiler_params=_cparams(("arbitrary",), VMEM_LIMIT),
        name="moe_combine",
    )(slots, ys, h1, route, g_next.reshape(1, d))


def hier_moe_layer(h1, xn, logits, w_gate, w_up, w_down, layer, g_next, u_dtype, split=None):
    t = h1.shape[0]
    bm = MOE_BM
    route, counts = moe_routing(logits, _tile(t, 512))
    e = route[:, 0:2].astype(I32)
    rank = route[:, 2:4].astype(I32)
    counts = counts[0, N_GROUPS:N_GROUPS + N_EXPERTS].astype(I32)
    padded = (counts + bm - 1) // bm * bm
    pad_end = jnp.cumsum(padded)
    pad_start = pad_end - padded
    start_of = jnp.sum(jnp.where(e[..., None] == jnp.arange(N_EXPERTS, dtype=I32), pad_start, 0), axis=-1)
    slots = (start_of + rank).reshape(-1)
    nb = (2 * t) // bm + N_EXPERTS
    slot_tok = moe_slot_tokens(slots, nb * bm)
    n_used = pad_end[-1:]
    xs = moe_dispatch(xn, slot_tok, n_used, bm * max(f for f in (4, 2, 1) if nb % f == 0))
    ys = moe_experts(xs, pad_start // bm, padded // bm, n_used // bm, w_gate, w_up, w_down, layer, bm)
    tmc = _tile(math.gcd(t, split or t), 256)
    if split is None:
        return moe_combine(ys, slots, h1, route, g_next, tmc, u_dtype)
    (u_a,) = moe_combine(ys, slots, h1, route, g_next, tmc, u_dtype, 0, split, want_h=False)
    (u_b,) = moe_combine(ys, slots, h1, route, g_next, tmc, u_dtype, split, t - split, want_h=False)
    return u_a, u_b


def _swap_halves(x):
    lane = lax.broadcasted_iota(I32, x.shape, 1)
    return jnp.where(lane < QK_ROPE // 2, pltpu.roll(x, LANES - QK_ROPE // 2, 1), pltpu.roll(x, QK_ROPE // 2, 1))


def _mla_prep_body(p_ref, gq_ref, gkv_ref, cos_ref, sin_ref, cq_ref, ckv_ref, ckvb_ref, kr_ref):
    p = p_ref[...]
    cq_ref[...] = _rms(p[:, :Q_LORA], gq_ref[...]).astype(cq_ref.dtype)
    ckv = _rms(p[:, Q_LORA:Q_LORA + KV_LORA], gkv_ref[...])
    ckv_ref[...] = ckv
    ckvb_ref[...] = ckv.astype(ckvb_ref.dtype)
    kr = p[:, Q_LORA + KV_LORA:]
    kr_ref[...] = kr * cos_ref[...] + _swap_halves(kr) * sin_ref[...]


def mla_prep(proj, g_q, g_kv, cos, sin, tm):
    t = proj.shape[0]
    return pl.pallas_call(
        _mla_prep_body,
        grid=(t // tm,),
        in_specs=[pl.BlockSpec((tm, proj.shape[1]), lambda i: (i, 0)),
                  pl.BlockSpec((1, Q_LORA), lambda i: (0, 0)),
                  pl.BlockSpec((1, KV_LORA), lambda i: (0, 0)),
                  pl.BlockSpec((tm, LANES), lambda i: (i, 0)),
                  pl.BlockSpec((tm, LANES), lambda i: (i, 0))],
        out_specs=[pl.BlockSpec((tm, Q_LORA), lambda i: (i, 0)),
                   pl.BlockSpec((tm, KV_LORA), lambda i: (i, 0)),
                   pl.BlockSpec((tm, KV_LORA), lambda i: (i, 0)),
                   pl.BlockSpec((tm, LANES), lambda i: (i, 0))],
        out_shape=[jax.ShapeDtypeStruct((t, Q_LORA), BF16), jax.ShapeDtypeStruct((t, KV_LORA), F32),
                   jax.ShapeDtypeStruct((t, KV_LORA), BF16), jax.ShapeDtypeStruct((t, LANES), F32)],
        compiler_params=_cparams(("parallel",)),
        name="mla_prep",
    )(proj, g_q.reshape(1, -1), g_kv.reshape(1, -1), cos, sin)


def _mla_q_body(cq_ref, w_ref, cos_ref, sin_ref, q_ref, wb):
    @pl.when(pl.program_id(0) == 0)
    def _():
        wb[...] = w_ref[...].astype(BF16)

    cq = cq_ref[...]
    cos = cos_ref[...]
    sin = sin_ref[...]
    for h in range(B_HEADS):
        y = _dot(cq, wb[:, h * MLA_QK_PAD:(h + 1) * MLA_QK_PAD])
        xr = y[:, LANES:]
        q_ref[:, h * MLA_QK_PAD:h * MLA_QK_PAD + LANES] = y[:, :LANES].astype(q_ref.dtype)
        q_ref[:, h * MLA_QK_PAD + LANES:(h + 1) * MLA_QK_PAD] = \
            (xr * cos + _swap_halves(xr) * sin).astype(q_ref.dtype)


def mla_q_up(cq, w_uq_pad, cos, sin, tm):
    t = cq.shape[0]
    n = B_HEADS * MLA_QK_PAD
    return pl.pallas_call(
        _mla_q_body,
        grid=(t // tm,),
        in_specs=[pl.BlockSpec((tm, Q_LORA), lambda i: (i, 0)),
                  pl.BlockSpec((Q_LORA, n), lambda i: (0, 0)),
                  pl.BlockSpec((tm, LANES), lambda i: (i, 0)),
                  pl.BlockSpec((tm, LANES), lambda i: (i, 0))],
        out_specs=pl.BlockSpec((tm, n), lambda i: (i, 0)),
        out_shape=jax.ShapeDtypeStruct((t, n), BF16),
        scratch_shapes=[pltpu.VMEM((Q_LORA, n), BF16)],
        compiler_params=_cparams(("arbitrary",), VMEM_LIMIT),
        name="mla_q_up",
    )(cq, w_uq_pad, cos, sin)


def _mla_kv_body(c_ref, kr_ref, wk_ref, wv_ref, k_ref, v_ref, wkb, wvb):
    @pl.when(pl.program_id(0) == 0)
    def _():
        wkb[...] = wk_ref[0].astype(BF16)
        wvb[...] = wv_ref[0].astype(BF16)

    c = c_ref[...].astype(BF16)
    kr = kr_ref[...].astype(k_ref.dtype)
    kn = _dot(c, wkb[...])
    for h in range(B_HEADS):
        k_ref[:, h * MLA_QK_PAD:h * MLA_QK_PAD + LANES] = kn[:, h * QK_NOPE:(h + 1) * QK_NOPE].astype(k_ref.dtype)
        k_ref[:, h * MLA_QK_PAD + LANES:(h + 1) * MLA_QK_PAD] = kr
    v_ref[...] = _dot(c, wvb[...]).astype(v_ref.dtype)


def mla_kv_up(ckv, kr, w_uk, w_uv, layer, tm):
    t = ckv.shape[0]
    nk = B_HEADS * MLA_QK_PAD
    nv = B_HEADS * V_DIM
    return pl.pallas_call(
        _mla_kv_body,
        grid=(t // tm,),
        in_specs=[pl.BlockSpec((tm, KV_LORA), lambda i: (i, 0)),
                  pl.BlockSpec((tm, LANES), lambda i: (i, 0)),
                  pl.BlockSpec((1, KV_LORA, B_HEADS * QK_NOPE), lambda i: (layer, 0, 0)),
                  pl.BlockSpec((1, KV_LORA, nv), lambda i: (layer, 0, 0))],
        out_specs=[pl.BlockSpec((tm, nk), lambda i: (i, 0)), pl.BlockSpec((tm, nv), lambda i: (i, 0))],
        out_shape=[jax.ShapeDtypeStruct((t, nk), BF16), jax.ShapeDtypeStruct((t, nv), BF16)],
        scratch_shapes=[pltpu.VMEM((KV_LORA, B_HEADS * QK_NOPE), BF16), pltpu.VMEM((KV_LORA, nv), BF16)],
        compiler_params=_cparams(("arbitrary",), VMEM_LIMIT),
        name="mla_kv_up",
    )(ckv, kr, w_uk, w_uv)


def kernel(x_prompt, x_sample, cache_a_k, cache_a_v, cache_a_kidx, cache_b_ckv, cache_b_krope, norm_mix, norm_ffn, norm_final, rel_bias, a_w_in, a_w_out, b_w_in, b_norm_q, b_norm_kv, b_w_uq, b_w_uk, b_w_uv, b_w_out, moe_w_grp, moe_b_grp, moe_w_rtr, moe_b_rtr, moe_w_gate, moe_w_up, moe_w_down):
    n_p, s_p, d = x_prompt.shape
    n_s, s_s, _ = x_sample.shape
    past = cache_a_k.shape[2]
    tp = n_p * s_p
    ts = n_s * s_s
    t = tp + ts
    a_qd = A_HEADS * A_HEAD_DIM
    tm = _tile(math.gcd(tp, ts), 1024)
    tm2 = _tile(math.gcd(tp, ts), 512)

    x_parts = [x_prompt.reshape(tp, d), x_sample.reshape(ts, d)]

    def router_params(i):
        w_r = jnp.concatenate([moe_w_grp[i], moe_w_rtr[i],
                               jnp.zeros((d, LANES - N_GROUPS - N_EXPERTS), F32)], axis=1)
        b_r = jnp.concatenate([moe_b_grp[i], moe_b_rtr[i],
                               jnp.zeros((LANES - N_GROUPS - N_EXPERTS,), F32)]).reshape(1, LANES)
        return w_r, b_r

    u0 = rmsnorm(x_parts, norm_mix[0], tm)
    w_a = a_w_in[0]
    (q_b,) = matmul(u0, w_a, 0, a_qd, [BF16], tm, 512, "a_proj_q")
    tmp, tms = _tile(tp, 1024), _tile(ts, 1024)
    kp_f, kp_b = matmul(u0, w_a, a_qd, a_qd, [F32, BF16], tmp, 512, "a_proj_k", 0, tp)
    vp_f, vp_b = matmul(u0, w_a, 2 * a_qd, a_qd, [F32, BF16], tmp, 512, "a_proj_v", 0, tp)
    ks_f, ks_b = matmul(u0, w_a, a_qd, a_qd, [F32, BF16], tms, 512, "a_proj_k", tp, ts)
    vs_f, vs_b = matmul(u0, w_a, 2 * a_qd, a_qd, [F32, BF16], tms, 512, "a_proj_v", tp, ts)
    (qi_b,) = matmul(u0, w_a, 3 * a_qd, IDX_HEADS * IDX_DIM, [BF16], tm, 512, "a_proj_qi")
    c_ki = 3 * a_qd + IDX_HEADS * IDX_DIM
    w_ki = w_a[:, c_ki:c_ki + IDX_DIM]
    w_wi = w_a[:, c_ki + IDX_DIM:c_ki + IDX_DIM + IDX_HEADS]
    zk = jnp.zeros((d, IDX_DIM), F32)
    w_tail = jnp.concatenate([w_ki, zk, zk, w_ki, w_wi, jnp.zeros((d, LANES - IDX_HEADS), F32)], axis=1)
    (tail,) = matmul(u0, w_tail, 0, 3 * LANES, [F32], tm, 3 * LANES, "a_proj_tail")
    kidx = tail[:, :IDX_DIM]

    bias_tab = rel_bias_tables(rel_bias)
    a_scale = A_HEAD_DIM ** -0.5
    mask_p = dsa_select(qi_b, tail, 2, tail, n_seq=n_p, sq=s_p, bq=128, sk=s_p, sk_real=s_p, tk=512,
                        pos0=0, q_row0=0)
    kp_t = kp_b.reshape(n_p, s_p, a_qd).transpose(0, 2, 1).reshape(n_p * a_qd, s_p)
    att_p = attention(q_b, [(kp_t, vp_b, s_p, 0, 0)], bias_tab, mask_p, n_seq=n_p, sq=s_p, bq=ATT_BQ, q_row0=0,
                      q_col0=0, n_heads=A_HEADS, hg=ATT_HG, dq=A_HEAD_DIM, dv=A_HEAD_DIM, scale=a_scale, pos0=0,
                      dyn_sk=s_p, name="dsa_attention_prompt")
    sk_s = past + s_s
    sk_pad = (sk_s + ATT_TILE - 1) // ATT_TILE * ATT_TILE
    ki_past = cache_a_kidx[0]
    zp = jnp.zeros_like(ki_past)
    kc_past = jnp.concatenate([ki_past, zp, zp, ki_past], axis=-1)
    kc_new = tail[tp:, :2 * LANES].reshape(n_s, s_s, 2 * LANES)
    kc_s = jnp.concatenate([kc_past, kc_new, jnp.zeros((n_s, sk_pad - sk_s, 2 * LANES), F32)], axis=1)
    mask_s = dsa_select(qi_b, tail, 2, kc_s.reshape(n_s * sk_pad, 2 * LANES), n_seq=n_s, sq=s_s, bq=s_s,
                        sk=sk_pad, sk_real=sk_s, tk=sk_pad, pos0=past, q_row0=tp)
    tiles_s = [(0, r, r, ATT_TILE) for r in range(0, past, ATT_TILE)] + [(1, 0, past, s_s)]
    att_s = attention(q_b, [(cache_a_k[0].reshape(n_s * past * A_HEADS, A_HEAD_DIM),
                             cache_a_v[0].reshape(n_s * past * A_HEADS, A_HEAD_DIM), past, 0, 0),
                            (ks_b, vs_b, s_s, 0, 0)],
                      bias_tab, mask_s, n_seq=n_s, sq=s_s, bq=s_s, q_row0=tp, q_col0=0, n_heads=A_HEADS, hg=A_HEADS,
                      dq=A_HEAD_DIM, dv=A_HEAD_DIM, scale=a_scale, pos0=past, seg_tiles=tiles_s,
                      name="dsa_attention_sample")

    w_r0, b_r0 = router_params(0)
    h1, xn0, lg0 = outproj_norm_router([att_p, att_s], a_w_out, 0, x_parts, norm_ffn[0], w_r0, b_r0, tm2, 512)
    h2, u1 = hier_moe_layer(h1, xn0, lg0, moe_w_gate, moe_w_up, moe_w_down, 0, norm_mix[1], BF16)

    n_in = Q_LORA + KV_LORA + QK_ROPE
    w_b = jnp.concatenate([b_w_in[0], jnp.zeros((d, LANES - QK_ROPE), F32)], axis=1)
    (proj,) = matmul(u1, w_b, 0, n_in + LANES - QK_ROPE, [F32], tm, 384, "b_proj")
    half = QK_ROPE // 2
    inv = ROPE_THETA ** (-jnp.arange(half, dtype=F32) / half)
    pos_all = jnp.concatenate([jnp.tile(jnp.arange(s_p, dtype=I32), n_p),
                               jnp.tile(past + jnp.arange(s_s, dtype=I32), n_s)])
    ang = pos_all.astype(F32)[:, None] * inv[None, :]
    zl = jnp.zeros((t, LANES - QK_ROPE), F32)
    cos_t = jnp.concatenate([jnp.cos(ang), jnp.cos(ang), zl], axis=1)
    sin_t = jnp.concatenate([-jnp.sin(ang), jnp.sin(ang), zl], axis=1)
    cq_b, ckv_f, ckv_b, kr_f = mla_prep(proj, b_norm_q[0], b_norm_kv[0], cos_t, sin_t, tm)
    w_uq_pad = jnp.pad(b_w_uq[0].reshape(Q_LORA, B_HEADS, QK_NOPE + QK_ROPE),
                       ((0, 0), (0, 0), (0, MLA_QK_PAD - QK_NOPE - QK_ROPE))).reshape(Q_LORA, B_HEADS * MLA_QK_PAD)
    qm = mla_q_up(cq_b, w_uq_pad, cos_t, sin_t, tm2)
    km_p, vm_p = mla_kv_up(ckv_b[:tp], kr_f[:tp], b_w_uk, b_w_uv, 0, _tile(tp, 512))
    ctab = causal_tables()
    hgb = 8
    km_t = km_p.reshape(n_p, s_p, B_HEADS * MLA_QK_PAD).transpose(0, 2, 1).reshape(n_p * B_HEADS * MLA_QK_PAD, s_p)
    matt_p = attention(qm, [(km_t, vm_p, s_p, 0, 0)], ctab, None, n_seq=n_p, sq=s_p, bq=ATT_BQ, q_row0=0, q_col0=0,
                       n_heads=B_HEADS, hg=ATT_HG, dq=MLA_QK_PAD, dv=V_DIM, scale=B_SCALE, pos0=0, dyn_sk=s_p,
                       name="mla_attention_prompt")
    ckv_all = jnp.concatenate([cache_b_ckv[0], ckv_f[tp:].reshape(n_s, s_s, KV_LORA)], axis=1)
    kr_past = jnp.concatenate([cache_b_krope[0], jnp.zeros((n_s, past, LANES - QK_ROPE), F32)], axis=-1)
    kr_all = jnp.concatenate([kr_past, kr_f[tp:].reshape(n_s, s_s, LANES)], axis=1)
    km_s, vm_s = mla_kv_up(ckv_all.reshape(n_s * sk_s, KV_LORA), kr_all.reshape(n_s * sk_s, LANES),
                           b_w_uk, b_w_uv, 0, sk_s)
    tiles_m = [(0, r, r, ATT_TILE) for r in range(0, past, ATT_TILE)] + [(0, past, past, s_s)]
    matt_s = attention(qm, [(km_s, vm_s, sk_s, 0, 0)], ctab, None, n_seq=n_s, sq=s_s, bq=s_s, q_row0=tp, q_col0=0,
                       n_heads=B_HEADS, hg=hgb, dq=MLA_QK_PAD, dv=V_DIM, scale=B_SCALE, pos0=past,
                       seg_tiles=tiles_m, name="mla_attention_sample")

    w_r1, b_r1 = router_params(1)
    h3, xn1, lg1 = outproj_norm_router([matt_p, matt_s], b_w_out, 0, [h2], norm_ffn[1], w_r1, b_r1, tm2, 512)
    y_p, y_s = hier_moe_layer(h3, xn1, lg1, moe_w_gate, moe_w_up, moe_w_down, 1, norm_final, F32, split=tp)
    y_prompt = y_p.reshape(n_p, s_p, d)
    y_sample = y_s.reshape(n_s, s_s, d)

    def heads(a, n, s):
        return a.reshape(1, n, s, A_HEADS, A_HEAD_DIM)

    kr_out = kr_f[:, :QK_ROPE]
    return (y_prompt, y_sample,
            heads(kp_f, n_p, s_p), heads(vp_f, n_p, s_p), kidx[:tp].reshape(1, n_p, s_p, IDX_DIM),
            ckv_f[:tp].reshape(1, n_p, s_p, KV_LORA), kr_out[:tp].reshape(1, n_p, s_p, QK_ROPE),
            heads(ks_f, n_s, s_s), heads(vs_f, n_s, s_s), kidx[tp:].reshape(1, n_s, s_s, IDX_DIM),
            ckv_f[tp:].reshape(1, n_s, s_s, KV_LORA), kr_out[tp:].reshape(1, n_s, s_s, QK_ROPE))
```
